```python
import math
import jax, jax.numpy as jnp
from jax import lax
import numpy as np

D_MODEL = 1024
BATCH = 8
SEQ = 2048
DEPTH = 1
DEC_BATCH = 128
DEC_SEQ = 4
PAST_LEN = 16384
PAGE_SIZE = 128

N_META = 16
D_RNN = D_MODEL
RNN_BLOCKS = 16
RNN_BLOCK = D_RNN // RNN_BLOCKS
CONV_W = 4
RG_C = 8.0
GLA_HEADS = 4
GLA_DK = D_MODEL // 2 // GLA_HEADS
GLA_DV = D_MODEL // GLA_HEADS
GLA_QK = GLA_HEADS * GLA_DK
GLA_V = GLA_HEADS * GLA_DV
GLA_LOWRANK = 16
GLA_TAU = 16.0
GLA_CHUNK = 64
N_EXPERTS = 256
TOP_K = 8
N_GROUPS = 8
TOPK_GROUPS = 4
D_EXPERT = D_MODEL // 4
ROUTED_SCALE = 2.5
MOE_BLOCK = 128
DN_ALPHA = (2.0 * DEPTH) ** 0.25
DN_BETA = (8.0 * DEPTH) ** -0.25
LN_EPS = 1e-5
RMS_EPS = 1e-6
PROJ_SIZES = (D_RNN, D_RNN, GLA_QK, GLA_QK, GLA_V, GLA_V, GLA_LOWRANK, D_MODEL, D_MODEL)
PROJ_TOTAL = D_RNN + D_RNN + GLA_QK + GLA_QK + GLA_V + GLA_V + GLA_LOWRANK + D_MODEL + D_MODEL

kernel_name = "hawk_gla_moe_deepnorm_meta_step"


def _layer_norm(x, g, b):
    xf = x.astype(jnp.float32)
    mu = jnp.mean(xf, -1, keepdims=True)
    var = jnp.mean(jnp.square(xf - mu), -1, keepdims=True)
    return ((xf - mu) * lax.rsqrt(var + LN_EPS) * g + b).astype(x.dtype)


def _gla_chunk_len(L):
    return GLA_CHUNK if L % GLA_CHUNK == 0 else L


def _causal_conv(u, buf, conv_w, conv_b):
    L = u.shape[1]
    full = jnp.concatenate([buf.astype(u.dtype), u], axis=1)
    y = conv_b + sum(full[:, j:j + L] * conv_w[j] for j in range(CONV_W))
    return y, full[:, L:]


def _rg_lru(x, h0, rg_wa, rg_ba, rg_wi, rg_bi, rg_lambda):
    B, L, _ = x.shape
    xb = x.reshape(B, L, RNN_BLOCKS, RNN_BLOCK)
    r = jax.nn.sigmoid((jnp.einsum('blnd,nde->blne', xb, rg_wa).reshape(B, L, D_RNN) + rg_ba).astype(jnp.float32))
    i = jax.nn.sigmoid((jnp.einsum('blnd,nde->blne', xb, rg_wi).reshape(B, L, D_RNN) + rg_bi).astype(jnp.float32))
    log_a = -RG_C * r * jax.nn.softplus(-rg_lambda.astype(jnp.float32))
    a = jnp.exp(log_a)
    mult = jnp.sqrt(-jnp.expm1(2.0 * log_a))
    bterm = mult * (i * x.astype(jnp.float32))
    bterm = bterm.at[:, 0].add(a[:, 0] * h0)

    def combine(c1, c2):
        a1, b1 = c1
        a2, b2 = c2
        return a1 * a2, a2 * b1 + b2

    _, h = lax.associative_scan(combine, (a, bterm), axis=1)
    return h, h[:, -1]


def _gla_chunks(q, k, v, log_f, S0, chunk):
    B, L, H, _ = q.shape
    n = L // chunk

    def to_chunks(t):
        return jnp.moveaxis(t.reshape(B, n, chunk, *t.shape[2:]), 1, 0)

    mask = jnp.tril(jnp.ones((chunk, chunk), dtype=bool))
    mid = chunk // 2

    def step(S, inp):
        qc, kc, vc, fc = inp
        bcum = jnp.cumsum(fc, axis=1)
        ref = bcum[:, mid:mid + 1]
        o_inter = jnp.einsum('bchk,bhkv->bchv', qc * jnp.exp(bcum), S)
        att = jnp.einsum('bthk,bshk->bhts', qc * jnp.exp(bcum - ref), kc * jnp.exp(ref - bcum))
        att = jnp.where(mask, att, 0.0)
        o_intra = jnp.einsum('bhts,bshv->bthv', att, vc)
        b_last = bcum[:, -1]
        k_dec = kc * jnp.exp(b_last[:, None] - bcum)
        S_new = jnp.exp(b_last)[..., None] * S + jnp.einsum('bshk,bshv->bhkv', k_dec, vc)
        return S_new, o_inter + o_intra

    S_fin, o = lax.scan(step, S0, (to_chunks(q), to_chunks(k), to_chunks(v), to_chunks(log_f)))
    o = jnp.moveaxis(o, 0, 1).reshape(B, L, H, v.shape[-1])
    return o, S_fin


def _token_mixers(x, conv_buf, h0, S0, segments, w_in, conv_w, conv_b, rg_wa, rg_ba, rg_wi, rg_bi,
                  rg_lambda, gla_wf2, gla_bf, gla_norm_g, w_out):
    B, L, _ = x.shape
    offsets = np.cumsum(PROJ_SIZES)[:-1].tolist()
    xr, gr, q, k, v, go, fl, ma, mb = jnp.split(x @ w_in, offsets, axis=-1)

    xc, conv_new = _causal_conv(xr, conv_buf, conv_w, conv_b)
    h, h_last = _rg_lru(xc, h0, rg_wa, rg_ba, rg_wi, rg_bi, rg_lambda)
    y_a = (h * jax.nn.gelu(gr.astype(jnp.float32))).astype(x.dtype)

    qh = q.reshape(B, L, GLA_HEADS, GLA_DK).astype(jnp.float32) * (GLA_DK ** -0.5)
    kh = k.reshape(B, L, GLA_HEADS, GLA_DK).astype(jnp.float32)
    vh = v.reshape(B, L, GLA_HEADS, GLA_DV).astype(jnp.float32)
    log_f = jax.nn.log_sigmoid((fl @ gla_wf2 + gla_bf).astype(jnp.float32)) / GLA_TAU
    log_f = log_f.reshape(B, L, GLA_HEADS, GLA_DK)
    outs = []
    S = S0
    start = 0
    for seg_len, chunk in segments:
        sl = slice(start, start + seg_len)
        o_seg, S = _gla_chunks(qh[:, sl], kh[:, sl], vh[:, sl], log_f[:, sl], S, chunk)
        outs.append(o_seg)
        start += seg_len
    o = jnp.concatenate(outs, axis=1)
    o = o * lax.rsqrt(jnp.mean(o * o, -1, keepdims=True) + RMS_EPS) * gla_norm_g.astype(jnp.float32)
    y_b = (o.reshape(B, L, GLA_V) * jax.nn.silu(go.astype(jnp.float32))).astype(x.dtype)

    merged = jax.nn.sigmoid(ma) * y_a + jax.nn.sigmoid(mb) * y_b
    return merged @ w_out, conv_new, h_last, S


def _route(x2, w_router, router_bias):
    T = x2.shape[0]
    s = jax.nn.sigmoid((x2 @ w_router).astype(jnp.float32))
    sel = s + router_bias.astype(jnp.float32)
    grp_score = lax.top_k(sel.reshape(T, N_GROUPS, N_EXPERTS // N_GROUPS), 2)[0].sum(-1)
    _, gidx = lax.top_k(grp_score, TOPK_GROUPS)
    gmask = jnp.sum(jax.nn.one_hot(gidx, N_GROUPS, dtype=jnp.float32), axis=1) > 0
    gmask = jnp.repeat(gmask, N_EXPERTS // N_GROUPS, axis=1)
    _, eidx = lax.top_k(jnp.where(gmask, sel, -jnp.inf), TOP_K)
    w = jnp.take_along_axis(s, eidx, axis=1)
    w = w / jnp.sum(w, -1, keepdims=True) * ROUTED_SCALE
    return eidx, w


def _routed_experts(x2, eidx, gate, w_gate, w_up, w_down):
    T, D = x2.shape
    A = T * TOP_K
    e_flat = eidx.reshape(-1)
    tok_flat = jnp.repeat(jnp.arange(T, dtype=jnp.int32), TOP_K)
    g_flat = gate.reshape(-1)
    order = jnp.argsort(e_flat, stable=True)
    e_sorted = e_flat[order]
    counts = jnp.bincount(e_flat, length=N_EXPERTS)
    starts = jnp.cumsum(counts) - counts
    padded = (counts + MOE_BLOCK - 1) // MOE_BLOCK * MOE_BLOCK
    pad_ends = jnp.cumsum(padded)
    pad_starts = pad_ends - padded
    dest = pad_starts[e_sorted] + (jnp.arange(A) - starts[e_sorted])
    n_blocks = -(-A // MOE_BLOCK) + N_EXPERTS
    P = n_blocks * MOE_BLOCK
    slot_tok = jnp.full((P,), T, dtype=jnp.int32).at[dest].set(tok_flat[order])
    slot_gate = jnp.zeros((P,), gate.dtype).at[dest].set(g_flat[order])
    block_start = jnp.arange(n_blocks, dtype=pad_ends.dtype) * MOE_BLOCK
    block_expert = jnp.minimum(jnp.searchsorted(pad_ends, block_start, side='right'), N_EXPERTS - 1)
    x_pad = jnp.concatenate([x2, jnp.zeros((1, D), x2.dtype)], axis=0)

    def block_fn(args):
        toks, e = args
        xb = x_pad[toks]
        hb = jax.nn.silu(xb @ w_gate[e]) * (xb @ w_up[e])
        return hb @ w_down[e]

    y = lax.map(block_fn, (slot_tok.reshape(n_blocks, MOE_BLOCK), block_expert))
    y = y.reshape(P, D) * slot_gate[:, None]
    return jax.ops.segment_sum(y, slot_tok, num_segments=T + 1)[:T]


def _moe(x, w_router, router_bias, w_gate, w_up, w_down, ws_gate, ws_up, ws_down):
    B, L, D = x.shape
    x2 = x.reshape(B * L, D)
    eidx, gate = _route(x2, w_router, router_bias)
    routed = _routed_experts(x2, eidx, gate.astype(x.dtype), w_gate, w_up, w_down)
    shared = (jax.nn.silu(x2 @ ws_gate) * (x2 @ ws_up)) @ ws_down
    return (routed + shared).reshape(B, L, D)


def _layer(x, conv_buf, h0, S0, segments, drop_meta, w_in, conv_w, conv_b, rg_wa, rg_ba, rg_wi, rg_bi,
           rg_lambda, gla_wf2, gla_bf, gla_norm_g, w_out, ln1_g, ln1_b, w_router, router_bias,
           w_gate, w_up, w_down, ws_gate, ws_up, ws_down, ln2_g, ln2_b):
    mix, conv_new, h_last, S_last = _token_mixers(x, conv_buf, h0, S0, segments, w_in, conv_w, conv_b,
                                                  rg_wa, rg_ba, rg_wi, rg_bi, rg_lambda, gla_wf2, gla_bf,
                                                  gla_norm_g, w_out)
    x = _layer_norm(DN_ALPHA * x + mix, ln1_g, ln1_b)
    if drop_meta:
        x = x[:, N_META:]
    x = _layer_norm(DN_ALPHA * x + _moe(x, w_router, router_bias, w_gate, w_up, w_down, ws_gate, ws_up, ws_down),
                    ln2_g, ln2_b)
    return x, conv_new, h_last, S_last


def setup_inputs(seed: int = 0) -> dict:
    key = jax.random.key(seed)
    ks = iter(jax.random.split(key, 48))
    f32 = jnp.float32

    def nrm(shape, scale):
        return jax.random.normal(next(ks), shape, f32) * scale

    u = jax.random.uniform(next(ks), (DEPTH, D_RNN), f32, 0.9, 0.999)
    a0 = u ** (1.0 / RG_C)
    rg_lambda = jnp.log(a0) - jnp.log1p(-a0)
    return {
        "x_prompt": nrm((BATCH, SEQ, D_MODEL), 1.0),
        "x_sample": nrm((DEC_BATCH, DEC_SEQ, D_MODEL), 1.0),
        "state_rglru_h": nrm((DEPTH, DEC_BATCH, D_RNN), 0.5),
        "state_conv": nrm((DEPTH, DEC_BATCH, CONV_W - 1, D_RNN), 1.0),
        "state_gla": nrm((DEPTH, DEC_BATCH, GLA_HEADS, GLA_DK, GLA_DV), 1.0),
        "meta_tokens": nrm((N_META, D_MODEL), 1.0),
        "w_in": nrm((DEPTH, D_MODEL, PROJ_TOTAL), D_MODEL ** -0.5),
        "conv_w": nrm((DEPTH, CONV_W, D_RNN), CONV_W ** -0.5),
        "conv_b": nrm((DEPTH, D_RNN), 0.02),
        "rg_wa": nrm((DEPTH, RNN_BLOCKS, RNN_BLOCK, RNN_BLOCK), RNN_BLOCK ** -0.5),
        "rg_ba": nrm((DEPTH, D_RNN), 0.02),
        "rg_wi": nrm((DEPTH, RNN_BLOCKS, RNN_BLOCK, RNN_BLOCK), RNN_BLOCK ** -0.5),
        "rg_bi": nrm((DEPTH, D_RNN), 0.02),
        "rg_lambda": rg_lambda,
        "gla_wf2": nrm((DEPTH, GLA_LOWRANK, GLA_QK), GLA_LOWRANK ** -0.5),
        "gla_bf": 1.0 + nrm((DEPTH, GLA_QK), 0.1),
        "gla_norm_g": 1.0 + nrm((DEPTH, GLA_DV), 0.02),
        "w_out": nrm((DEPTH, D_MODEL, D_MODEL), DN_BETA * D_MODEL ** -0.5),
        "ln1_g": 1.0 + nrm((DEPTH, D_MODEL), 0.02),
        "ln1_b": nrm((DEPTH, D_MODEL), 0.02),
        "w_router": nrm((DEPTH, D_MODEL, N_EXPERTS), D_MODEL ** -0.5),
        "router_bias": nrm((DEPTH, N_EXPERTS), 0.01),
        "w_gate": nrm((DEPTH, N_EXPERTS, D_MODEL, D_EXPERT), D_MODEL ** -0.5),
        "w_up": nrm((DEPTH, N_EXPERTS, D_MODEL, D_EXPERT), D_MODEL ** -0.5),
        "w_down": nrm((DEPTH, N_EXPERTS, D_EXPERT, D_MODEL), DN_BETA * D_EXPERT ** -0.5),
        "ws_gate": nrm((DEPTH, D_MODEL, D_EXPERT), D_MODEL ** -0.5),
        "ws_up": nrm((DEPTH, D_MODEL, D_EXPERT), D_MODEL ** -0.5),
        "ws_down": nrm((DEPTH, D_EXPERT, D_MODEL), DN_BETA * D_EXPERT ** -0.5),
        "ln2_g": 1.0 + nrm((DEPTH, D_MODEL), 0.02),
        "ln2_b": nrm((DEPTH, D_MODEL), 0.02),
    }


def reference(x_prompt, x_sample, state_rglru_h, state_conv, state_gla, meta_tokens, w_in, conv_w, conv_b,
              rg_wa, rg_ba, rg_wi, rg_bi, rg_lambda, gla_wf2, gla_bf, gla_norm_g, w_out, ln1_g, ln1_b,
              w_router, router_bias, w_gate, w_up, w_down, ws_gate, ws_up, ws_down, ln2_g, ln2_b):
    Bp, Lp, D = x_prompt.shape
    Bs, Ls, _ = x_sample.shape
    xp = jnp.concatenate([jnp.broadcast_to(meta_tokens.astype(x_prompt.dtype), (Bp, N_META, D)), x_prompt], axis=1)
    xs = x_sample
    seg_p = ((N_META, N_META), (Lp, _gla_chunk_len(Lp)))
    seg_s = ((Ls, _gla_chunk_len(Ls)),)
    conv0_p = jnp.zeros((Bp, CONV_W - 1, D_RNN), x_prompt.dtype)
    h0_p = jnp.zeros((Bp, D_RNN), jnp.float32)
    S0_p = jnp.zeros((Bp, GLA_HEADS, GLA_DK, GLA_DV), jnp.float32)

    hp_l, cp_l, sp_l, hs_l, cs_l, ss_l = [], [], [], [], [], []
    for l in range(DEPTH):
        lw = (w_in[l], conv_w[l], conv_b[l], rg_wa[l], rg_ba[l], rg_wi[l], rg_bi[l], rg_lambda[l],
              gla_wf2[l], gla_bf[l], gla_norm_g[l], w_out[l], ln1_g[l], ln1_b[l], w_router[l], router_bias[l],
              w_gate[l], w_up[l], w_down[l], ws_gate[l], ws_up[l], ws_down[l], ln2_g[l], ln2_b[l])
        xp, cp, hp, sp = _layer(xp, conv0_p, h0_p, S0_p, seg_p, l == DEPTH - 1, *lw)
        xs, cs, hs, ss = _layer(xs, state_conv[l], state_rglru_h[l].astype(jnp.float32),
                                state_gla[l].astype(jnp.float32), seg_s, False, *lw)
        hp_l.append(hp); cp_l.append(cp); sp_l.append(sp)
        hs_l.append(hs); cs_l.append(cs); ss_l.append(ss)

    y_prompt = xp
    y_sample = xs
    h_p = jnp.stack(hp_l).astype(state_rglru_h.dtype)
    conv_p = jnp.stack(cp_l).astype(state_conv.dtype)
    gla_p = jnp.stack(sp_l).astype(state_gla.dtype)
    h_s = jnp.stack(hs_l).astype(state_rglru_h.dtype)
    conv_s = jnp.stack(cs_l).astype(state_conv.dtype)
    gla_s = jnp.stack(ss_l).astype(state_gla.dtype)
    return (y_prompt, y_sample, h_p, conv_p, gla_p, h_s, conv_s, gla_s)
```

```python
import functools

import jax
import jax.numpy as jnp
from jax import lax
from jax.experimental import pallas as pl
from jax.experimental.pallas import tpu as pltpu

F32 = jnp.float32
BF16 = jnp.bfloat16
HIGHEST = lax.Precision.HIGHEST

N_META = 16
CONV_W = 4
RG_C = 8.0
RNN_BLOCKS = 16
GLA_HEADS = 4
GLA_TAU = 16.0
GLA_CHUNK = 64
N_GROUPS = 8
TOPK_GROUPS = 4
TOP_K = 8
ROUTED_SCALE = 2.5
LN_EPS = 1e-5
RMS_EPS = 1e-6

LANES = 128
SUBLANES = 8
MXU_DIM = 256
VMEM_LIMIT_BYTES = 56 * 1024 * 1024

MOE_ROWS = 128


def _params(*sem):
    return pltpu.CompilerParams(dimension_semantics=sem, vmem_limit_bytes=VMEM_LIMIT_BYTES)


def _sigmoid(x):
    return jax.nn.sigmoid(x)


def _softplus(z):
    return jnp.maximum(z, 0.0) + jnp.log1p(jnp.exp(-jnp.abs(z)))


def _layer_norm(z, g, b):
    mu = jnp.mean(z, axis=-1, keepdims=True)
    zc = z - mu
    var = jnp.mean(zc * zc, axis=-1, keepdims=True)
    return zc * lax.rsqrt(var + LN_EPS) * g + b


def _const_spec(shape):
    nd = len(shape)
    return pl.BlockSpec(shape, lambda *_: (0,) * nd)


def _inproj_kernel(x_ref, wa_ref, wb_ref, wm_ref, ua_ref, ub_ref, um_ref):
    xb = x_ref[...].astype(BF16)
    ua_ref[...] = jnp.dot(xb, wa_ref[...], preferred_element_type=F32)
    ub_ref[...] = jnp.dot(xb, wb_ref[...], preferred_element_type=F32)
    um_ref[...] = jnp.dot(xb, wm_ref[...], preferred_element_type=F32)


def _inproj(x2d, wa, wb, wm, tm):
    m, d = x2d.shape
    na, nb, nm = wa.shape[1], wb.shape[1], wm.shape[1]
    return pl.pallas_call(
        _inproj_kernel,
        grid=(m // tm,),
        in_specs=[pl.BlockSpec((tm, d), lambda i: (i, 0)),
                  _const_spec(wa.shape), _const_spec(wb.shape), _const_spec(wm.shape)],
        out_specs=[pl.BlockSpec((tm, na), lambda i: (i, 0)),
                   pl.BlockSpec((tm, nb), lambda i: (i, 0)),
                   pl.BlockSpec((tm, nm), lambda i: (i, 0))],
        out_shape=[jax.ShapeDtypeStruct((m, na), F32),
                   jax.ShapeDtypeStruct((m, nb), F32),
                   jax.ShapeDtypeStruct((m, nm), F32)],
        compiler_params=_params("arbitrary"),
        name="inproj",
    )(x2d, wa, wb, wm)


def _rglru_kernel(ua_ref, cbuf_ref, h0_ref, cw_ref, cb_ref, wa_ref, ba_ref, wi_ref, bi_ref, lam_ref,
                  ya_ref, hlast_ref, cnew_ref, ext_scr, h_scr, *, tb, l_last, nblk, d):
    j = pl.program_id(1)
    ntail = CONV_W - 1

    @pl.when(j == 0)
    def _():
        h_scr[...] = h0_ref[0]
        ext_scr[SUBLANES - ntail:SUBLANES, :] = cbuf_ref[0]

    xr = ua_ref[0, :, :d]
    gr = ua_ref[0, :, d:]
    ext_scr[SUBLANES:, :] = xr
    xc = cb_ref[...] + cw_ref[ntail:ntail + 1, :] * xr
    for s in range(1, CONV_W):
        xc = xc + cw_ref[ntail - s:ntail - s + 1, :] * ext_scr[SUBLANES - s:SUBLANES - s + tb, :]

    xcb = xc.astype(BF16)
    nt = d // MXU_DIM
    ra = jnp.concatenate(
        [jnp.dot(xcb[:, q * MXU_DIM:(q + 1) * MXU_DIM], wa_ref[q], preferred_element_type=F32)
         for q in range(nt)], axis=-1)
    ia = jnp.concatenate(
        [jnp.dot(xcb[:, q * MXU_DIM:(q + 1) * MXU_DIM], wi_ref[q], preferred_element_type=F32)
         for q in range(nt)], axis=-1)
    r = _sigmoid(ra + ba_ref[...])
    ig = _sigmoid(ia + bi_ref[...])
    log_a = (-RG_C) * r * _softplus(-lam_ref[...])
    a = jnp.exp(log_a)
    mult = jnp.sqrt(-jnp.tanh(log_a) * (a * a + 1.0))
    b = mult * (ig * xc)

    row = lax.broadcasted_iota(jnp.int32, (tb, d), 0)
    b = b + jnp.where(row == 0, a * h_scr[...], 0.0)
    s = 1
    while s < tb:
        a_sh = jnp.where(row >= s, pltpu.roll(a, s, 0), 1.0)
        b_sh = jnp.where(row >= s, pltpu.roll(b, s, 0), 0.0)
        b = b + a * b_sh
        a = a * a_sh
        s *= 2
    h = b
    h_scr[...] = h[tb - 1:tb, :]
    ya_ref[0] = h * jax.nn.gelu(gr, approximate=True)

    @pl.when(j == nblk - 1)
    def _():
        hlast_ref[0] = h[l_last - 1:l_last, :]
        cnew_ref[0] = ext_scr[SUBLANES + l_last - ntail:SUBLANES + l_last, :]

    tail = ext_scr[tb + SUBLANES - ntail:tb + SUBLANES, :]
    ext_scr[SUBLANES - ntail:SUBLANES, :] = tail


def _rglru(ua, cbuf, h0, cw, cb, wa4, ba, wi4, bi, lam, *, tb, l_real):
    bsz, lp, n2 = ua.shape
    d = n2 // 2
    nblk = lp // tb
    l_last = l_real - (nblk - 1) * tb
    kern = functools.partial(_rglru_kernel, tb=tb, l_last=l_last, nblk=nblk, d=d)
    return pl.pallas_call(
        kern,
        grid=(bsz, nblk),
        in_specs=[pl.BlockSpec((1, tb, n2), lambda b, j: (b, j, 0)),
                  pl.BlockSpec((1, CONV_W - 1, d), lambda b, j: (b, 0, 0)),
                  pl.BlockSpec((1, 1, d), lambda b, j: (b, 0, 0)),
                  _const_spec(cw.shape), _const_spec(cb.shape),
                  _const_spec(wa4.shape), _const_spec(ba.shape),
                  _const_spec(wi4.shape), _const_spec(bi.shape), _const_spec(lam.shape)],
        out_specs=[pl.BlockSpec((1, tb, d), lambda b, j: (b, j, 0)),
                   pl.BlockSpec((1, 1, d), lambda b, j: (b, 0, 0)),
                   pl.BlockSpec((1, CONV_W - 1, d), lambda b, j: (b, 0, 0))],
        out_shape=[jax.ShapeDtypeStruct((bsz, lp, d), F32),
                   jax.ShapeDtypeStruct((bsz, 1, d), F32),
                   jax.ShapeDtypeStruct((bsz, CONV_W - 1, d), F32)],
        scratch_shapes=[pltpu.VMEM((tb + SUBLANES, d), F32), pltpu.VMEM((1, d), F32)],
        compiler_params=_params("arbitrary", "arbitrary"),
        name="rglru",
    )(ua, cbuf, h0, cw, cb, wa4, ba, wi4, bi, lam)


def _gla_kernel(ub_ref, s0_ref, wf_ref, bf_ref, g_ref, yb_ref, sout_ref, s_scr,
                *, tb, chunk, l_real, nblk, dk, dv):
    j = pl.program_id(1)
    nh = GLA_HEADS
    qk = nh * dk
    vw = nh * dv

    @pl.when(j == 0)
    def _():
        s_scr[...] = s0_ref[0]

    off_k, off_v, off_g, off_f = qk, 2 * qk, 2 * qk + vw, 2 * qk + 2 * vw
    fl = ub_ref[0, :, off_f:off_f + LANES].astype(BF16)
    logit = jnp.dot(fl, wf_ref[...], preferred_element_type=F32) + bf_ref[...]
    log_f = (jnp.minimum(logit, 0.0) - jnp.log1p(jnp.exp(-jnp.abs(logit)))) * (1.0 / GLA_TAU)
    if l_real < tb * nblk:
        rows = lax.broadcasted_iota(jnp.int32, log_f.shape, 0) + j * tb
        log_f = jnp.where(rows < l_real, log_f, 0.0)

    ri = lax.broadcasted_iota(jnp.int32, (chunk, chunk), 0)
    ci = lax.broadcasted_iota(jnp.int32, (chunk, chunk), 1)
    causal = ri >= ci
    tril = causal.astype(F32)
    ones_cv = jnp.ones((chunk, dv), F32)
    mid = chunk // 2
    scale = dk ** -0.5
    tdims = (((0,), (0,)), ((), ()))

    for c in range(tb // chunk):
        r0 = c * chunk
        for h in range(nh):
            q = ub_ref[0, r0:r0 + chunk, h * dk:(h + 1) * dk] * scale
            k = ub_ref[0, r0:r0 + chunk, off_k + h * dk:off_k + (h + 1) * dk]
            v = ub_ref[0, r0:r0 + chunk, off_v + h * dv:off_v + (h + 1) * dv]
            go = ub_ref[0, r0:r0 + chunk, off_g + h * dv:off_g + (h + 1) * dv]
            f = log_f[r0:r0 + chunk, h * dk:(h + 1) * dk]
            bcum = jnp.dot(tril, f, precision=HIGHEST, preferred_element_type=F32)
            ref = bcum[mid:mid + 1, :]
            blast = bcum[chunk - 1:chunk, :]
            s_old = s_scr[h]
            vb = v.astype(BF16)
            o_inter = jnp.dot((q * jnp.exp(bcum)).astype(BF16), s_old.astype(BF16),
                              preferred_element_type=F32)
            att = lax.dot_general((q * jnp.exp(bcum - ref)).astype(BF16),
                                  (k * jnp.exp(ref - bcum)).astype(BF16),
                                  (((1,), (1,)), ((), ())), preferred_element_type=F32)
            att = jnp.where(causal, att, 0.0)
            o = o_inter + jnp.dot(att.astype(BF16), vb, preferred_element_type=F32)
            k_dec = (k * jnp.exp(blast - bcum)).astype(BF16)
            dec = lax.dot_general(f, ones_cv, tdims, precision=HIGHEST, preferred_element_type=F32)
            s_scr[h] = jnp.exp(dec) * s_old + lax.dot_general(k_dec, vb, tdims,
                                                              preferred_element_type=F32)
            o = o * lax.rsqrt(jnp.mean(o * o, axis=-1, keepdims=True) + RMS_EPS) * g_ref[...]
            yb_ref[0, r0:r0 + chunk, h * dv:(h + 1) * dv] = o * (go * _sigmoid(go))

    @pl.when(j == nblk - 1)
    def _():
        sout_ref[0] = s_scr[...]


def _gla(ub, s0, wf, bf, g, *, tb, chunk, l_real):
    bsz, lp, nb = ub.shape
    _, nh, dk, dv = s0.shape
    nblk = lp // tb
    kern = functools.partial(_gla_kernel, tb=tb, chunk=chunk, l_real=l_real, nblk=nblk, dk=dk, dv=dv)
    return pl.pallas_call(
        kern,
        grid=(bsz, nblk),
        in_specs=[pl.BlockSpec((1, tb, nb), lambda b, j: (b, j, 0)),
                  pl.BlockSpec((1, nh, dk, dv), lambda b, j: (b, 0, 0, 0)),
                  _const_spec(wf.shape), _const_spec(bf.shape), _const_spec(g.shape)],
        out_specs=[pl.BlockSpec((1, tb, nh * dv), lambda b, j: (b, j, 0)),
                   pl.BlockSpec((1, nh, dk, dv), lambda b, j: (b, 0, 0, 0))],
        out_shape=[jax.ShapeDtypeStruct((bsz, lp, nh * dv), F32),
                   jax.ShapeDtypeStruct((bsz, nh, dk, dv), F32)],
        scratch_shapes=[pltpu.VMEM((nh, dk, dv), F32)],
        compiler_params=_params("arbitrary", "arbitrary"),
        name="gla",
    )(ub, s0, wf, bf, g)


def _mix_kernel(ya_ref, yb_ref, um_ref, x_ref, wo_ref, g_ref, b_ref, o_ref, *, alpha, d):
    merged = _sigmoid(um_ref[:, :d]) * ya_ref[...] + _sigmoid(um_ref[:, d:]) * yb_ref[...]
    mix = jnp.dot(merged.astype(BF16), wo_ref[...], preferred_element_type=F32)
    o_ref[...] = _layer_norm(alpha * x_ref[...] + mix, g_ref[...], b_ref[...])


def _mix(ya, yb, um, x, wo, g, b, *, alpha, tm):
    m, d = x.shape
    row = lambda w: pl.BlockSpec((tm, w), lambda i: (i, 0))
    return pl.pallas_call(
        functools.partial(_mix_kernel, alpha=alpha, d=d),
        grid=(m // tm,),
        in_specs=[row(d), row(d), row(2 * d), row(d),
                  _const_spec(wo.shape), _const_spec(g.shape), _const_spec(b.shape)],
        out_specs=row(d),
        out_shape=jax.ShapeDtypeStruct((m, d), F32),
        compiler_params=_params("arbitrary"),
        name="mix_ln1",
    )(ya, yb, um, x, wo, g, b)


def _router_kernel(x_ref, wr_ref, bias_ref, eidx_ref, gate_ref, rank_ref, cnt_ref, run_scr,
                   *, tm, ne):
    i = pl.program_id(0)

    @pl.when(i == 0)
    def _():
        run_scr[...] = jnp.zeros_like(run_scr)

    logits = jnp.dot(x_ref[...], wr_ref[...], precision=HIGHEST, preferred_element_type=F32)
    s = _sigmoid(logits)
    sel = s + bias_ref[...]
    lane = lax.broadcasted_iota(jnp.int32, (tm, ne), 1)
    lane_f = lane.astype(F32)
    gsz = ne // N_GROUPS
    neg = -jnp.inf

    gscores = []
    for g in range(N_GROUPS):
        ing = (lane >= g * gsz) & (lane < (g + 1) * gsz)
        v = jnp.where(ing, sel, neg)
        m1 = jnp.max(v, axis=1, keepdims=True)
        i1 = jnp.min(jnp.where(v == m1, lane_f, float(ne)), axis=1, keepdims=True)
        m2 = jnp.max(jnp.where(lane_f == i1, neg, v), axis=1, keepdims=True)
        gscores.append(m1 + m2)
    cur = jnp.full((tm, ne), neg, F32)
    for g in range(N_GROUPS):
        ahead = jnp.zeros((tm, 1), F32)
        for g2 in range(N_GROUPS):
            if g2 == g:
                continue
            beats = (gscores[g2] >= gscores[g]) if g2 < g else (gscores[g2] > gscores[g])
            ahead = ahead + beats.astype(F32)
        ing = (lane >= g * gsz) & (lane < (g + 1) * gsz)
        cur = jnp.where(ing & (ahead < float(TOPK_GROUPS)), sel, cur)

    col = lax.broadcasted_iota(jnp.int32, (tm, TOP_K), 1)
    eidx = jnp.zeros((tm, TOP_K), F32)
    wsel = jnp.zeros((tm, TOP_K), F32)
    onehot_sum = jnp.zeros((tm, ne), F32)
    picks = []
    for k in range(TOP_K):
        m = jnp.max(cur, axis=1, keepdims=True)
        ik = jnp.min(jnp.where(cur == m, lane_f, float(ne)), axis=1, keepdims=True)
        hit = lane_f == ik
        wk = jnp.sum(jnp.where(hit, s, 0.0), axis=1, keepdims=True)
        cur = jnp.where(hit, neg, cur)
        onehot_sum = onehot_sum + hit.astype(F32)
        picks.append(hit)
        eidx = jnp.where(col == k, ik, eidx)
        wsel = jnp.where(col == k, wk, wsel)
    gate_ref[...] = wsel / jnp.sum(wsel, axis=1, keepdims=True) * ROUTED_SCALE
    eidx_ref[...] = eidx.astype(jnp.int32)

    ri = lax.broadcasted_iota(jnp.int32, (tm, tm), 0)
    ci = lax.broadcasted_iota(jnp.int32, (tm, tm), 1)
    before = (ci < ri).astype(BF16)
    counts = jnp.dot(before, onehot_sum.astype(BF16), preferred_element_type=F32) + run_scr[...]
    rank = jnp.zeros((tm, TOP_K), F32)
    for k in range(TOP_K):
        rk = jnp.sum(jnp.where(picks[k], counts, 0.0), axis=1, keepdims=True)
        rank = jnp.where(col == k, rk, rank)
    rank_ref[...] = rank.astype(jnp.int32)
    run_scr[...] = run_scr[...] + jnp.sum(onehot_sum, axis=0, keepdims=True)
    cnt_ref[...] = run_scr[...].astype(jnp.int32)


def _router(x, wr, bias, *, tm):
    t, d = x.shape
    ne = wr.shape[1]
    out_row = lambda: pl.BlockSpec((tm, TOP_K), lambda i: (i, 0))
    return pl.pallas_call(
        functools.partial(_router_kernel, tm=tm, ne=ne),
        grid=(t // tm,),
        in_specs=[pl.BlockSpec((tm, d), lambda i: (i, 0)), _const_spec(wr.shape), _const_spec(bias.shape)],
        out_specs=[out_row(), out_row(), out_row(), _const_spec((1, ne))],
        out_shape=[jax.ShapeDtypeStruct((t, TOP_K), jnp.int32),
                   jax.ShapeDtypeStruct((t, TOP_K), F32),
                   jax.ShapeDtypeStruct((t, TOP_K), jnp.int32),
                   jax.ShapeDtypeStruct((1, ne), jnp.int32)],
        scratch_shapes=[pltpu.VMEM((1, ne), F32)],
        compiler_params=_params("arbitrary"),
        name="router",
    )(x, wr, bias)


def _dest_kernel(eidx_ref, rank_ref, start_ref, dest_ref, *, tm, ne):
    lane = lax.broadcasted_iota(jnp.int32, (tm, ne), 1)
    col = lax.broadcasted_iota(jnp.int32, (tm, TOP_K), 1)
    starts = start_ref[...].astype(F32)
    eidx = eidx_ref[...]
    dest = jnp.zeros((tm, TOP_K), F32)
    for k in range(TOP_K):
        hit = lane == eidx[:, k:k + 1]
        sk = jnp.sum(jnp.where(hit, starts, 0.0), axis=1, keepdims=True)
        dest = jnp.where(col == k, sk, dest)
    dest_ref[...] = dest.astype(jnp.int32) + rank_ref[...]


def _dest(eidx, rank, starts, *, tm):
    t = eidx.shape[0]
    ne = starts.shape[1]
    row = lambda: pl.BlockSpec((tm, TOP_K), lambda i: (i, 0))
    return pl.pallas_call(
        functools.partial(_dest_kernel, tm=tm, ne=ne),
        grid=(t // tm,),
        in_specs=[row(), row(), _const_spec((1, ne))],
        out_specs=row(),
        out_shape=jax.ShapeDtypeStruct((t, TOP_K), jnp.int32),
        compiler_params=_params("arbitrary"),
        name="dest",
    )(eidx, rank, starts)


def _row_copy(src_ref, src_row, dst_ref, dst_row, sem):
    return pltpu.make_async_copy(src_ref.at[pl.ds(src_row, 1), :], dst_ref.at[pl.ds(dst_row, 1), :], sem)


def _dispatch_kernel(dest_ref, x_ref, xs_ref, sem, *, tt):
    n = tt * TOP_K

    def start(j, c):
        _row_copy(x_ref, lax.shift_right_logical(j, 3), xs_ref, dest_ref[j], sem).start()
        return c

    def wait(j, c):
        _row_copy(x_ref, 0, xs_ref, 0, sem).wait()
        return c

    lax.fori_loop(0, n, start, 0)
    lax.fori_loop(0, n, wait, 0)


def _dispatch(dest_flat, x, n_slots, *, tt):
    t, d = x.shape
    assert TOP_K == 8
    return pl.pallas_call(
        functools.partial(_dispatch_kernel, tt=tt),
        grid=(t // tt,),
        in_specs=[pl.BlockSpec((tt * TOP_K,), lambda i: (i,), memory_space=pltpu.SMEM),
                  pl.BlockSpec((tt, d), lambda i: (i, 0))],
        out_specs=pl.BlockSpec(memory_space=pl.ANY),
        out_shape=jax.ShapeDtypeStruct((n_slots, d), F32),
        scratch_shapes=[pltpu.SemaphoreType.DMA(())],
        compiler_params=_params("arbitrary"),
        name="dispatch",
    )(dest_flat, x)


def _expert_kernel(be_ref, br_ref, nu_ref, xs_ref, wg_ref, wu_ref, wd_ref, y_ref, wgb, wub, wdb):
    i = pl.program_id(0)

    @pl.when(i < nu_ref[0])
    def _():
        prev = be_ref[jnp.maximum(i - 1, 0)]

        @pl.when((i == 0) | (be_ref[i] != prev))
        def _():
            wgb[...] = wg_ref[0].astype(BF16)
            wub[...] = wu_ref[0].astype(BF16)
            wdb[...] = wd_ref[0].astype(BF16)

        x = xs_ref[...]
        rows = lax.broadcasted_iota(jnp.int32, x.shape, 0)
        xb = jnp.where(rows < br_ref[i], x, 0.0).astype(BF16)
        g = jnp.dot(xb, wgb[...], preferred_element_type=F32)
        u = jnp.dot(xb, wub[...], preferred_element_type=F32)
        hmid = (g * _sigmoid(g)) * u
        y_ref[...] = jnp.dot(hmid.astype(BF16), wdb[...], preferred_element_type=F32)


def _experts(block_expert, block_rows, n_used, xs, wg, wu, wd):
    p, d = xs.shape
    ne, _, de = wg.shape
    nblocks = p // MOE_ROWS

    def blk(i, be, br, nu):
        return jnp.minimum(i, nu[0] - 1)

    grid_spec = pltpu.PrefetchScalarGridSpec(
        num_scalar_prefetch=3,
        grid=(nblocks,),
        in_specs=[pl.BlockSpec((MOE_ROWS, d), lambda i, be, br, nu: (blk(i, be, br, nu), 0)),
                  pl.BlockSpec((1, d, de), lambda i, be, br, nu: (be[blk(i, be, br, nu)], 0, 0)),
                  pl.BlockSpec((1, d, de), lambda i, be, br, nu: (be[blk(i, be, br, nu)], 0, 0)),
                  pl.BlockSpec((1, de, d), lambda i, be, br, nu: (be[blk(i, be, br, nu)], 0, 0))],
        out_specs=pl.BlockSpec((MOE_ROWS, d), lambda i, be, br, nu: (blk(i, be, br, nu), 0)),
        scratch_shapes=[pltpu.VMEM((d, de), BF16), pltpu.VMEM((d, de), BF16), pltpu.VMEM((de, d), BF16)],
    )
    return pl.pallas_call(
        _expert_kernel,
        grid_spec=grid_spec,
        out_shape=jax.ShapeDtypeStruct((p, d), F32),
        compiler_params=_params("arbitrary"),
        name="experts",
    )(block_expert, block_rows, n_used, xs, wg, wu, wd)


def _combine_kernel(dest_ref, x_ref, gate_ref, ys_ref, wsg_ref, wsu_ref, wsd_ref, g_ref, b_ref,
                    o_ref, ybuf, sem, *, tt, alpha):
    n = tt * TOP_K

    def start(j, c):
        t = lax.shift_right_logical(j, 3)
        k = jnp.bitwise_and(j, TOP_K - 1)
        _row_copy(ys_ref, dest_ref[j], ybuf, k * tt + t, sem).start()
        return c

    def wait(j, c):
        _row_copy(ys_ref, 0, ybuf, 0, sem).wait()
        return c

    lax.fori_loop(0, n, start, 0)
    x = x_ref[...]
    xb = x.astype(BF16)
    sg = jnp.dot(xb, wsg_ref[...], preferred_element_type=F32)
    su = jnp.dot(xb, wsu_ref[...], preferred_element_type=F32)
    acc = jnp.dot(((sg * _sigmoid(sg)) * su).astype(BF16), wsd_ref[...], preferred_element_type=F32)
    lax.fori_loop(0, n, wait, 0)
    gate = gate_ref[...]
    for k in range(TOP_K):
        acc = acc + gate[:, k:k + 1] * ybuf[k * tt:(k + 1) * tt, :]
    o_ref[...] = _layer_norm(alpha * x + acc, g_ref[...], b_ref[...])


def _combine(dest_flat, x, gate, ys, wsg, wsu, wsd, g, b, *, tt, alpha):
    t, d = x.shape
    assert TOP_K == 8
    return pl.pallas_call(
        functools.partial(_combine_kernel, tt=tt, alpha=alpha),
        grid=(t // tt,),
        in_specs=[pl.BlockSpec((tt * TOP_K,), lambda i: (i,), memory_space=pltpu.SMEM),
                  pl.BlockSpec((tt, d), lambda i: (i, 0)),
                  pl.BlockSpec((tt, TOP_K), lambda i: (i, 0)),
                  pl.BlockSpec(memory_space=pl.ANY),
                  _const_spec(wsg.shape), _const_spec(wsu.shape), _const_spec(wsd.shape),
                  _const_spec(g.shape), _const_spec(b.shape)],
        out_specs=pl.BlockSpec((tt, d), lambda i: (i, 0)),
        out_shape=jax.ShapeDtypeStruct((t, d), F32),
        scratch_shapes=[pltpu.VMEM((tt * TOP_K, d), F32), pltpu.SemaphoreType.DMA(())],
        compiler_params=_params("arbitrary"),
        name="combine_ln2",
    )(dest_flat, x, gate, ys, wsg, wsu, wsd, g, b)


def _block_diag_tiles(w):
    n, s, _ = w.shape
    per = MXU_DIM // s
    w = w.reshape(n // per, per, s, s)
    eye = jnp.eye(per, dtype=w.dtype)
    return jnp.einsum('tpij,pq->tpiqj', w, eye).reshape(n // per, MXU_DIM, MXU_DIM)


def _row(v):
    return v.reshape(1, -1)


def _moe(x1, w_router, router_bias, w_gate, w_up, w_down, ws_gate, ws_up, ws_down, ln_g, ln_b, alpha):
    t, d = x1.shape
    ne = w_router.shape[1]
    eidx, gate, rank, counts = _router(x1, w_router, _row(router_bias), tm=256)
    counts = counts[0]
    padded = (counts + MOE_ROWS - 1) // MOE_ROWS * MOE_ROWS
    pad_ends = jnp.cumsum(padded)
    pad_starts = pad_ends - padded
    nblocks = (t * TOP_K) // MOE_ROWS + ne
    block_start = jnp.arange(nblocks, dtype=jnp.int32) * MOE_ROWS
    block_expert = jnp.minimum(jnp.searchsorted(pad_ends, block_start, side='right'), ne - 1).astype(jnp.int32)
    block_rows = jnp.clip(pad_starts[block_expert] + counts[block_expert] - block_start, 0, MOE_ROWS).astype(jnp.int32)
    n_used = jnp.maximum(pad_ends[-1:] // MOE_ROWS, 1).astype(jnp.int32)

    dest = _dest(eidx, rank, _row(pad_starts.astype(jnp.int32)), tm=256)
    dest_flat = dest.reshape(-1)
    xs = _dispatch(dest_flat, x1, nblocks * MOE_ROWS, tt=256)
    ys = _experts(block_expert, block_rows, n_used, xs, w_gate, w_up, w_down)
    return _combine(dest_flat, x1, gate, ys, ws_gate.astype(BF16), ws_up.astype(BF16),
                    ws_down.astype(BF16), _row(ln_g), _row(ln_b), tt=128, alpha=alpha)


def _pad_time(a, lp):
    return jnp.pad(a, ((0, 0), (0, lp - a.shape[1]), (0, 0)))


def kernel(x_prompt, x_sample, state_rglru_h, state_conv, state_gla, meta_tokens, w_in, conv_w, conv_b,
           rg_wa, rg_ba, rg_wi, rg_bi, rg_lambda, gla_wf2, gla_bf, gla_norm_g, w_out, ln1_g, ln1_b,
           w_router, router_bias, w_gate, w_up, w_down, ws_gate, ws_up, ws_down, ln2_g, ln2_b):
    bp, lp, d = x_prompt.shape
    bs, ls, _ = x_sample.shape
    depth = w_in.shape[0]
    nh, dk, dv = state_gla.shape[2:]
    qk, vw = nh * dk, nh * dv
    lowrank = gla_wf2.shape[1]
    alpha = (2.0 * depth) ** 0.25
    ls_pad = SUBLANES
    assert depth == 1 and ls <= ls_pad and lp % 256 == 0 and N_META % SUBLANES == 0

    xp = x_prompt.reshape(bp * lp, d)
    xs = x_sample.reshape(bs * ls, d)
    xm = meta_tokens.astype(F32)

    hp_l, cp_l, sp_l, hs_l, cs_l, ss_l = [], [], [], [], [], []
    for l in range(depth):
        o1, o2 = 2 * d, 2 * d + 2 * qk + 2 * vw
        wa = w_in[l][:, :o1].astype(BF16)
        wb = jnp.pad(w_in[l][:, o1:o2 + lowrank], ((0, 0), (0, LANES - lowrank))).astype(BF16)
        wm = w_in[l][:, o2 + lowrank:].astype(BF16)
        wf = jnp.pad(gla_wf2[l], ((0, LANES - lowrank), (0, 0))).astype(BF16)
        wa4 = _block_diag_tiles(rg_wa[l]).astype(BF16)
        wi4 = _block_diag_tiles(rg_wi[l]).astype(BF16)
        mixer_w = (conv_w[l], _row(conv_b[l]), wa4, _row(rg_ba[l]), wi4, _row(rg_bi[l]), _row(rg_lambda[l]))
        gla_w = (wf, _row(gla_bf[l]), _row(gla_norm_g[l]))

        ua, ub, _ = _inproj(xm, wa, wb, wm, tm=N_META)
        _, h_m, c_m = _rglru(ua[None], jnp.zeros((1, CONV_W - 1, d), F32), jnp.zeros((1, 1, d), F32),
                             *mixer_w, tb=N_META, l_real=N_META)
        _, s_m = _gla(ub[None], jnp.zeros((1, nh, dk, dv), F32), *gla_w,
                      tb=N_META, chunk=N_META, l_real=N_META)

        ua, ub, um = _inproj(xp, wa, wb, wm, tm=256)
        ya, h_p, c_p = _rglru(ua.reshape(bp, lp, -1), jnp.broadcast_to(c_m, (bp,) + c_m.shape[1:]),
                              jnp.broadcast_to(h_m, (bp,) + h_m.shape[1:]), *mixer_w, tb=256, l_real=lp)
        yb, s_p = _gla(ub.reshape(bp, lp, -1), jnp.broadcast_to(s_m, (bp,) + s_m.shape[1:]), *gla_w,
                       tb=256, chunk=GLA_CHUNK, l_real=lp)
        x1p = _mix(ya.reshape(bp * lp, d), yb.reshape(bp * lp, d), um, xp, w_out[l].astype(BF16),
                   _row(ln1_g[l]), _row(ln1_b[l]), alpha=alpha, tm=512)

        xs_pad = _pad_time(xs.reshape(bs, ls, d), ls_pad).reshape(bs * ls_pad, d)
        ua, ub, um = _inproj(xs_pad, wa, wb, wm, tm=256)
        ya, h_s, c_s = _rglru(ua.reshape(bs, ls_pad, -1), state_conv[l].astype(F32),
                              state_rglru_h[l].astype(F32)[:, None, :], *mixer_w, tb=ls_pad, l_real=ls)
        yb, s_s = _gla(ub.reshape(bs, ls_pad, -1), state_gla[l].astype(F32), *gla_w,
                       tb=ls_pad, chunk=ls_pad, l_real=ls)
        unpad = lambda a: a.reshape(bs, ls_pad, -1)[:, :ls].reshape(bs * ls, -1)
        x1s = _mix(unpad(ya), unpad(yb), unpad(um), xs, w_out[l].astype(BF16),
                   _row(ln1_g[l]), _row(ln1_b[l]), alpha=alpha, tm=512)

        x1 = jnp.concatenate([x1p, x1s], axis=0)
        x2 = _moe(x1, w_router[l], router_bias[l], w_gate[l], w_up[l], w_down[l],
                  ws_gate[l], ws_up[l], ws_down[l], ln2_g[l], ln2_b[l], alpha)
        xp, xs = x2[:bp * lp], x2[bp * lp:]

        hp_l.append(h_p[:, 0]); cp_l.append(c_p); sp_l.append(s_p)
        hs_l.append(h_s[:, 0]); cs_l.append(c_s); ss_l.append(s_s)

    y_prompt = xp.reshape(bp, lp, d)
    y_sample = xs.reshape(bs, ls, d)
    return (y_prompt, y_sample,
            jnp.stack(hp_l).astype(state_rglru_h.dtype), jnp.stack(cp_l).astype(state_conv.dtype),
            jnp.stack(sp_l).astype(state_gla.dtype),
            jnp.stack(hs_l).astype(state_rglru_h.dtype), jnp.stack(cs_l).astype(state_conv.dtype),
            jnp.stack(ss_l).astype(state_gla.dtype))
```

```python
import functools

import jax
import jax.numpy as jnp
from jax import lax
from jax.experimental import pallas as pl
from jax.experimental.pallas import tpu as pltpu

F32 = jnp.float32
BF16 = jnp.bfloat16
HIGHEST = lax.Precision.HIGHEST

N_META = 16
CONV_W = 4
RG_C = 8.0
RNN_BLOCKS = 16
GLA_HEADS = 4
GLA_TAU = 16.0
GLA_CHUNK = 64
N_GROUPS = 8
TOPK_GROUPS = 4
TOP_K = 8
ROUTED_SCALE = 2.5
LN_EPS = 1e-5
RMS_EPS = 1e-6

LANES = 128
SUBLANES = 8
MXU_DIM = 256
VMEM_LIMIT_BYTES = 56 * 1024 * 1024

MOE_ROWS = 128


def _params(*sem):
    return pltpu.CompilerParams(dimension_semantics=sem, vmem_limit_bytes=VMEM_LIMIT_BYTES)


def _sigmoid(x):
    return jax.nn.sigmoid(x)


def _softplus(z):
    return jnp.maximum(z, 0.0) + jnp.log1p(jnp.exp(-jnp.abs(z)))


def _layer_norm(z, g, b):
    mu = jnp.mean(z, axis=-1, keepdims=True)
    zc = z - mu
    var = jnp.mean(zc * zc, axis=-1, keepdims=True)
    return zc * lax.rsqrt(var + LN_EPS) * g + b


def _const_spec(shape):
    nd = len(shape)
    return pl.BlockSpec(shape, lambda *_: (0,) * nd)


def _inproj_kernel(x_ref, wa_ref, wb_ref, wm_ref, ua_ref, ub_ref, um_ref):
    xb = x_ref[...].astype(BF16)
    ua_ref[...] = jnp.dot(xb, wa_ref[...], preferred_element_type=F32)
    ub_ref[...] = jnp.dot(xb, wb_ref[...], preferred_element_type=F32)
    um_ref[...] = jnp.dot(xb, wm_ref[...], preferred_element_type=F32)


def _inproj(x2d, wa, wb, wm, tm):
    m, d = x2d.shape
    na, nb, nm = wa.shape[1], wb.shape[1], wm.shape[1]
    return pl.pallas_call(
        _inproj_kernel,
        grid=(m // tm,),
        in_specs=[pl.BlockSpec((tm, d), lambda i: (i, 0)),
                  _const_spec(wa.shape), _const_spec(wb.shape), _const_spec(wm.shape)],
        out_specs=[pl.BlockSpec((tm, na), lambda i: (i, 0)),
                   pl.BlockSpec((tm, nb), lambda i: (i, 0)),
                   pl.BlockSpec((tm, nm), lambda i: (i, 0))],
        out_shape=[jax.ShapeDtypeStruct((m, na), F32),
                   jax.ShapeDtypeStruct((m, nb), F32),
                   jax.ShapeDtypeStruct((m, nm), F32)],
        compiler_params=_params("arbitrary"),
        name="inproj",
    )(x2d, wa, wb, wm)


def _rglru_kernel(ua_ref, cbuf_ref, h0_ref, cw_ref, cb_ref, wa_ref, ba_ref, wi_ref, bi_ref, lam_ref,
                  ya_ref, hlast_ref, cnew_ref, ext_scr, h_scr, *, tb, l_last, nblk, d):
    j = pl.program_id(1)
    ntail = CONV_W - 1

    @pl.when(j == 0)
    def _():
        h_scr[...] = h0_ref[0]
        ext_scr[SUBLANES - ntail:SUBLANES, :] = cbuf_ref[0]

    xr = ua_ref[0, :, :d]
    gr = ua_ref[0, :, d:]
    ext_scr[SUBLANES:, :] = xr
    xc = cb_ref[...] + cw_ref[ntail:ntail + 1, :] * xr
    for s in range(1, CONV_W):
        xc = xc + cw_ref[ntail - s:ntail - s + 1, :] * ext_scr[SUBLANES - s:SUBLANES - s + tb, :]

    xcb = xc.astype(BF16)
    nt = d // MXU_DIM
    ra = jnp.concatenate(
        [jnp.dot(xcb[:, q * MXU_DIM:(q + 1) * MXU_DIM], wa_ref[q], preferred_element_type=F32)
         for q in range(nt)], axis=-1)
    ia = jnp.concatenate(
        [jnp.dot(xcb[:, q * MXU_DIM:(q + 1) * MXU_DIM], wi_ref[q], preferred_element_type=F32)
         for q in range(nt)], axis=-1)
    r = _sigmoid(ra + ba_ref[...])
    ig = _sigmoid(ia + bi_ref[...])
    log_a = (-RG_C) * r * _softplus(-lam_ref[...])
    a = jnp.exp(log_a)
    mult = jnp.sqrt(-jnp.tanh(log_a) * (a * a + 1.0))
    b = mult * (ig * xc)

    row = lax.broadcasted_iota(jnp.int32, (tb, d), 0)
    b = b + jnp.where(row == 0, a * h_scr[...], 0.0)
    s = 1
    while s < tb:
        a_sh = jnp.where(row >= s, pltpu.roll(a, s, 0), 1.0)
        b_sh = jnp.where(row >= s, pltpu.roll(b, s, 0), 0.0)
        b = b + a * b_sh
        a = a * a_sh
        s *= 2
    h = b
    h_scr[...] = h[tb - 1:tb, :]
    ya_ref[0] = h * jax.nn.gelu(gr, approximate=True)

    @pl.when(j == nblk - 1)
    def _():
        hlast_ref[0] = h[l_last - 1:l_last, :]
        cnew_ref[0] = ext_scr[SUBLANES + l_last - ntail:SUBLANES + l_last, :]

    tail = ext_scr[tb + SUBLANES - ntail:tb + SUBLANES, :]
    ext_scr[SUBLANES - ntail:SUBLANES, :] = tail


def _rglru(ua, cbuf, h0, cw, cb, wa4, ba, wi4, bi, lam, *, tb, l_real):
    bsz, lp, n2 = ua.shape
    d = n2 // 2
    nblk = lp // tb
    l_last = l_real - (nblk - 1) * tb
    kern = functools.partial(_rglru_kernel, tb=tb, l_last=l_last, nblk=nblk, d=d)
    return pl.pallas_call(
        kern,
        grid=(bsz, nblk),
        in_specs=[pl.BlockSpec((1, tb, n2), lambda b, j: (b, j, 0)),
                  pl.BlockSpec((1, CONV_W - 1, d), lambda b, j: (b, 0, 0)),
                  pl.BlockSpec((1, 1, d), lambda b, j: (b, 0, 0)),
                  _const_spec(cw.shape), _const_spec(cb.shape),
                  _const_spec(wa4.shape), _const_spec(ba.shape),
                  _const_spec(wi4.shape), _const_spec(bi.shape), _const_spec(lam.shape)],
        out_specs=[pl.BlockSpec((1, tb, d), lambda b, j: (b, j, 0)),
                   pl.BlockSpec((1, 1, d), lambda b, j: (b, 0, 0)),
                   pl.BlockSpec((1, CONV_W - 1, d), lambda b, j: (b, 0, 0))],
        out_shape=[jax.ShapeDtypeStruct((bsz, lp, d), F32),
                   jax.ShapeDtypeStruct((bsz, 1, d), F32),
                   jax.ShapeDtypeStruct((bsz, CONV_W - 1, d), F32)],
        scratch_shapes=[pltpu.VMEM((tb + SUBLANES, d), F32), pltpu.VMEM((1, d), F32)],
        compiler_params=_params("arbitrary", "arbitrary"),
        name="rglru",
    )(ua, cbuf, h0, cw, cb, wa4, ba, wi4, bi, lam)


def _gla_kernel(ub_ref, s0_ref, wf_ref, bf_ref, g_ref, yb_ref, sout_ref, s_scr,
                *, tb, chunk, l_real, nblk, dk, dv):
    j = pl.program_id(1)
    nh = GLA_HEADS
    qk = nh * dk
    vw = nh * dv

    @pl.when(j == 0)
    def _():
        s_scr[...] = s0_ref[0]

    off_k, off_v, off_g, off_f = qk, 2 * qk, 2 * qk + vw, 2 * qk + 2 * vw
    fl = ub_ref[0, :, off_f:off_f + LANES].astype(BF16)
    logit = jnp.dot(fl, wf_ref[...], preferred_element_type=F32) + bf_ref[...]
    log_f = (jnp.minimum(logit, 0.0) - jnp.log1p(jnp.exp(-jnp.abs(logit)))) * (1.0 / GLA_TAU)
    if l_real < tb * nblk:
        rows = lax.broadcasted_iota(jnp.int32, log_f.shape, 0) + j * tb
        log_f = jnp.where(rows < l_real, log_f, 0.0)

    ri = lax.broadcasted_iota(jnp.int32, (chunk, chunk), 0)
    ci = lax.broadcasted_iota(jnp.int32, (chunk, chunk), 1)
    causal = ri >= ci
    tril = causal.astype(F32)
    ones_cv = jnp.ones((chunk, dv), F32)
    mid = chunk // 2
    scale = dk ** -0.5
    tdims = (((0,), (0,)), ((), ()))

    for c in range(tb // chunk):
        r0 = c * chunk
        for h in range(nh):
            q = ub_ref[0, r0:r0 + chunk, h * dk:(h + 1) * dk] * scale
            k = ub_ref[0, r0:r0 + chunk, off_k + h * dk:off_k + (h + 1) * dk]
            v = ub_ref[0, r0:r0 + chunk, off_v + h * dv:off_v + (h + 1) * dv]
            go = ub_ref[0, r0:r0 + chunk, off_g + h * dv:off_g + (h + 1) * dv]
            f = log_f[r0:r0 + chunk, h * dk:(h + 1) * dk]
            bcum = jnp.dot(tril, f, precision=HIGHEST, preferred_element_type=F32)
            ref = bcum[mid:mid + 1, :]
            blast = bcum[chunk - 1:chunk, :]
            s_old = s_scr[h]
            vb = v.astype(BF16)
            o_inter = jnp.dot((q * jnp.exp(bcum)).astype(BF16), s_old.astype(BF16),
                              preferred_element_type=F32)
            att = lax.dot_general((q * jnp.exp(bcum - ref)).astype(BF16),
                                  (k * jnp.exp(ref - bcum)).astype(BF16),
                                  (((1,), (1,)), ((), ())), preferred_element_type=F32)
            att = jnp.where(causal, att, 0.0)
            o = o_inter + jnp.dot(att.astype(BF16), vb, preferred_element_type=F32)
            k_dec = (k * jnp.exp(blast - bcum)).astype(BF16)
            dec = lax.dot_general(f, ones_cv, tdims, precision=HIGHEST, preferred_element_type=F32)
            s_scr[h] = jnp.exp(dec) * s_old + lax.dot_general(k_dec, vb, tdims,
                                                              preferred_element_type=F32)
            o = o * lax.rsqrt(jnp.mean(o * o, axis=-1, keepdims=True) + RMS_EPS) * g_ref[...]
            yb_ref[0, r0:r0 + chunk, h * dv:(h + 1) * dv] = o * (go * _sigmoid(go))

    @pl.when(j == nblk - 1)
    def _():
        sout_ref[0] = s_scr[...]


def _gla(ub, s0, wf, bf, g, *, tb, chunk, l_real):
    bsz, lp, nb = ub.shape
    _, nh, dk, dv = s0.shape
    nblk = lp // tb
    kern = functools.partial(_gla_kernel, tb=tb, chunk=chunk, l_real=l_real, nblk=nblk, dk=dk, dv=dv)
    return pl.pallas_call(
        kern,
        grid=(bsz, nblk),
        in_specs=[pl.BlockSpec((1, tb, nb), lambda b, j: (b, j, 0)),
                  pl.BlockSpec((1, nh, dk, dv), lambda b, j: (b, 0, 0, 0)),
                  _const_spec(wf.shape), _const_spec(bf.shape), _const_spec(g.shape)],
        out_specs=[pl.BlockSpec((1, tb, nh * dv), lambda b, j: (b, j, 0)),
                   pl.BlockSpec((1, nh, dk, dv), lambda b, j: (b, 0, 0, 0))],
        out_shape=[jax.ShapeDtypeStruct((bsz, lp, nh * dv), F32),
                   jax.ShapeDtypeStruct((bsz, nh, dk, dv), F32)],
        scratch_shapes=[pltpu.VMEM((nh, dk, dv), F32)],
        compiler_params=_params("arbitrary", "arbitrary"),
        name="gla",
    )(ub, s0, wf, bf, g)


def _mix_kernel(ya_ref, yb_ref, um_ref, x_ref, wo_ref, g_ref, b_ref, o_ref, *, alpha, d):
    merged = _sigmoid(um_ref[:, :d]) * ya_ref[...] + _sigmoid(um_ref[:, d:]) * yb_ref[...]
    mix = jnp.dot(merged.astype(BF16), wo_ref[...], preferred_element_type=F32)
    o_ref[...] = _layer_norm(alpha * x_ref[...] + mix, g_ref[...], b_ref[...])


def _mix(ya, yb, um, x, wo, g, b, *, alpha, tm):
    m, d = x.shape
    row = lambda w: pl.BlockSpec((tm, w), lambda i: (i, 0))
    return pl.pallas_call(
        functools.partial(_mix_kernel, alpha=alpha, d=d),
        grid=(m // tm,),
        in_specs=[row(d), row(d), row(2 * d), row(d),
                  _const_spec(wo.shape), _const_spec(g.shape), _const_spec(b.shape)],
        out_specs=row(d),
        out_shape=jax.ShapeDtypeStruct((m, d), F32),
        compiler_params=_params("arbitrary"),
        name="mix_ln1",
    )(ya, yb, um, x, wo, g, b)


def _router_kernel(x_ref, wr_ref, bias_ref, eidx_ref, gate_ref, rank_ref, cnt_ref, run_scr,
                   *, tm, ne):
    i = pl.program_id(0)

    @pl.when(i == 0)
    def _():
        run_scr[...] = jnp.zeros_like(run_scr)

    logits = jnp.dot(x_ref[...], wr_ref[...], precision=HIGHEST, preferred_element_type=F32)
    s = _sigmoid(logits)
    sel = s + bias_ref[...]
    lane = lax.broadcasted_iota(jnp.int32, (tm, ne), 1)
    lane_f = lane.astype(F32)
    gsz = ne // N_GROUPS
    neg = -jnp.inf

    gscores = []
    for g in range(N_GROUPS):
        ing = (lane >= g * gsz) & (lane < (g + 1) * gsz)
        v = jnp.where(ing, sel, neg)
        m1 = jnp.max(v, axis=1, keepdims=True)
        i1 = jnp.min(jnp.where(v == m1, lane_f, float(ne)), axis=1, keepdims=True)
        m2 = jnp.max(jnp.where(lane_f == i1, neg, v), axis=1, keepdims=True)
        gscores.append(m1 + m2)
    cur = jnp.full((tm, ne), neg, F32)
    for g in range(N_GROUPS):
        ahead = jnp.zeros((tm, 1), F32)
        for g2 in range(N_GROUPS):
            if g2 == g:
                continue
            beats = (gscores[g2] >= gscores[g]) if g2 < g else (gscores[g2] > gscores[g])
            ahead = ahead + beats.astype(F32)
        ing = (lane >= g * gsz) & (lane < (g + 1) * gsz)
        cur = jnp.where(ing & (ahead < float(TOPK_GROUPS)), sel, cur)

    col = lax.broadcasted_iota(jnp.int32, (tm, TOP_K), 1)
    eidx = jnp.zeros((tm, TOP_K), F32)
    wsel = jnp.zeros((tm, TOP_K), F32)
    onehot_sum = jnp.zeros((tm, ne), F32)
    picks = []
    for k in range(TOP_K):
        m = jnp.max(cur, axis=1, keepdims=True)
        ik = jnp.min(jnp.where(cur == m, lane_f, float(ne)), axis=1, keepdims=True)
        hit = lane_f == ik
        wk = jnp.sum(jnp.where(hit, s, 0.0), axis=1, keepdims=True)
        cur = jnp.where(hit, neg, cur)
        onehot_sum = onehot_sum + hit.astype(F32)
        picks.append(hit)
        eidx = jnp.where(col == k, ik, eidx)
        wsel = jnp.where(col == k, wk, wsel)
    gate_ref[...] = wsel / jnp.sum(wsel, axis=1, keepdims=True) * ROUTED_SCALE
    eidx_ref[...] = eidx.astype(jnp.int32)

    ri = lax.broadcasted_iota(jnp.int32, (tm, tm), 0)
    ci = lax.broadcasted_iota(jnp.int32, (tm, tm), 1)
    before = (ci < ri).astype(BF16)
    counts = jnp.dot(before, onehot_sum.astype(BF16), preferred_element_type=F32) + run_scr[...]
    rank = jnp.zeros((tm, TOP_K), F32)
    for k in range(TOP_K):
        rk = jnp.sum(jnp.where(picks[k], counts, 0.0), axis=1, keepdims=True)
        rank = jnp.where(col == k, rk, rank)
    rank_ref[...] = rank.astype(jnp.int32)
    run_scr[...] = run_scr[...] + jnp.sum(onehot_sum, axis=0, keepdims=True)
    cnt_ref[...] = run_scr[...].astype(jnp.int32)


def _router(x, wr, bias, *, tm):
    t, d = x.shape
    ne = wr.shape[1]
    out_row = lambda: pl.BlockSpec((tm, TOP_K), lambda i: (i, 0))
    return pl.pallas_call(
        functools.partial(_router_kernel, tm=tm, ne=ne),
        grid=(t // tm,),
        in_specs=[pl.BlockSpec((tm, d), lambda i: (i, 0)), _const_spec(wr.shape), _const_spec(bias.shape)],
        out_specs=[out_row(), out_row(), out_row(), _const_spec((1, ne))],
        out_shape=[jax.ShapeDtypeStruct((t, TOP_K), jnp.int32),
                   jax.ShapeDtypeStruct((t, TOP_K), F32),
                   jax.ShapeDtypeStruct((t, TOP_K), jnp.int32),
                   jax.ShapeDtypeStruct((1, ne), jnp.int32)],
        scratch_shapes=[pltpu.VMEM((1, ne), F32)],
        compiler_params=_params("arbitrary"),
        name="router",
    )(x, wr, bias)


def _dest_kernel(eidx_ref, rank_ref, start_ref, dest_ref, *, tm, ne):
    lane = lax.broadcasted_iota(jnp.int32, (tm, ne), 1)
    col = lax.broadcasted_iota(jnp.int32, (tm, TOP_K), 1)
    starts = start_ref[...].astype(F32)
    eidx = eidx_ref[...]
    dest = jnp.zeros((tm, TOP_K), F32)
    for k in range(TOP_K):
        hit = lane == eidx[:, k:k + 1]
        sk = jnp.sum(jnp.where(hit, starts, 0.0), axis=1, keepdims=True)
        dest = jnp.where(col == k, sk, dest)
    dest_ref[...] = dest.astype(jnp.int32) + rank_ref[...]


def _dest(eidx, rank, starts, *, tm):
    t = eidx.shape[0]
    ne = starts.shape[1]
    row = lambda: pl.BlockSpec((tm, TOP_K), lambda i: (i, 0))
    return pl.pallas_call(
        functools.partial(_dest_kernel, tm=tm, ne=ne),
        grid=(t // tm,),
        in_specs=[row(), row(), _const_spec((1, ne))],
        out_specs=row(),
        out_shape=jax.ShapeDtypeStruct((t, TOP_K), jnp.int32),
        compiler_params=_params("arbitrary"),
        name="dest",
    )(eidx, rank, starts)


def _row_copy(src_ref, src_row, dst_ref, dst_row, sem):
    return pltpu.make_async_copy(src_ref.at[pl.ds(src_row, 1), :], dst_ref.at[pl.ds(dst_row, 1), :], sem)


def _dispatch_kernel(dest_ref, x_ref, xs_ref, sem, *, tt):
    def start(t, c):
        for k in range(TOP_K):
            _row_copy(x_ref, t, xs_ref, dest_ref[t * TOP_K + k], sem).start(priority=k % 2)
        return c

    lax.fori_loop(0, tt, start, 0)
    for _ in range(TOP_K):
        pltpu.make_async_copy(x_ref, xs_ref.at[pl.ds(0, tt), :], sem).wait()


def _dispatch(dest_flat, x, n_slots, *, tt):
    t, d = x.shape
    assert TOP_K == 8
    return pl.pallas_call(
        functools.partial(_dispatch_kernel, tt=tt),
        grid=(t // tt,),
        in_specs=[pl.BlockSpec((tt * TOP_K,), lambda i: (i,), memory_space=pltpu.SMEM),
                  pl.BlockSpec((tt, d), lambda i: (i, 0))],
        out_specs=pl.BlockSpec(memory_space=pl.ANY),
        out_shape=jax.ShapeDtypeStruct((n_slots, d), F32),
        scratch_shapes=[pltpu.SemaphoreType.DMA(())],
        compiler_params=_params("arbitrary"),
        name="dispatch",
    )(dest_flat, x)


def _expert_kernel(be_ref, br_ref, nu_ref, xs_ref, wg_ref, wu_ref, wd_ref, y_ref, wgb, wub, wdb):
    i = pl.program_id(0)

    @pl.when(i < nu_ref[0])
    def _():
        prev = be_ref[jnp.maximum(i - 1, 0)]

        @pl.when((i == 0) | (be_ref[i] != prev))
        def _():
            wgb[...] = wg_ref[0].astype(BF16)
            wub[...] = wu_ref[0].astype(BF16)
            wdb[...] = wd_ref[0].astype(BF16)

        x = xs_ref[...]
        rows = lax.broadcasted_iota(jnp.int32, x.shape, 0)
        xb = jnp.where(rows < br_ref[i], x, 0.0).astype(BF16)
        g = jnp.dot(xb, wgb[...], preferred_element_type=F32)
        u = jnp.dot(xb, wub[...], preferred_element_type=F32)
        hmid = (g * _sigmoid(g)) * u
        y_ref[...] = jnp.dot(hmid.astype(BF16), wdb[...], preferred_element_type=F32)


def _experts(block_expert, block_rows, n_used, xs, wg, wu, wd):
    p, d = xs.shape
    ne, _, de = wg.shape
    nblocks = p // MOE_ROWS

    def blk(i, be, br, nu):
        return jnp.minimum(i, nu[0] - 1)

    grid_spec = pltpu.PrefetchScalarGridSpec(
        num_scalar_prefetch=3,
        grid=(nblocks,),
        in_specs=[pl.BlockSpec((MOE_ROWS, d), lambda i, be, br, nu: (blk(i, be, br, nu), 0)),
                  pl.BlockSpec((1, d, de), lambda i, be, br, nu: (be[blk(i, be, br, nu)], 0, 0)),
                  pl.BlockSpec((1, d, de), lambda i, be, br, nu: (be[blk(i, be, br, nu)], 0, 0)),
                  pl.BlockSpec((1, de, d), lambda i, be, br, nu: (be[blk(i, be, br, nu)], 0, 0))],
        out_specs=pl.BlockSpec((MOE_ROWS, d), lambda i, be, br, nu: (blk(i, be, br, nu), 0)),
        scratch_shapes=[pltpu.VMEM((d, de), BF16), pltpu.VMEM((d, de), BF16), pltpu.VMEM((de, d), BF16)],
    )
    return pl.pallas_call(
        _expert_kernel,
        grid_spec=grid_spec,
        out_shape=jax.ShapeDtypeStruct((p, d), F32),
        compiler_params=_params("arbitrary"),
        name="experts",
    )(block_expert, block_rows, n_used, xs, wg, wu, wd)


def _combine_kernel(dest_ref, x_ref, gate_ref, ys_ref, wsg_ref, wsu_ref, wsd_ref, g_ref, b_ref,
                    o_ref, ybuf, sem, *, tt, alpha):
    def start(t, c):
        for k in range(TOP_K):
            _row_copy(ys_ref, dest_ref[t * TOP_K + k], ybuf, k * tt + t, sem).start(priority=k % 2)
        return c

    lax.fori_loop(0, tt, start, 0)
    x = x_ref[...]
    xb = x.astype(BF16)
    sg = jnp.dot(xb, wsg_ref[...], preferred_element_type=F32)
    su = jnp.dot(xb, wsu_ref[...], preferred_element_type=F32)
    acc = jnp.dot(((sg * _sigmoid(sg)) * su).astype(BF16), wsd_ref[...], preferred_element_type=F32)
    pltpu.make_async_copy(ys_ref.at[pl.ds(0, tt * TOP_K), :], ybuf, sem).wait()
    gate = gate_ref[...]
    for k in range(TOP_K):
        acc = acc + gate[:, k:k + 1] * ybuf[k * tt:(k + 1) * tt, :]
    o_ref[...] = _layer_norm(alpha * x + acc, g_ref[...], b_ref[...])


def _combine(dest_flat, x, gate, ys, wsg, wsu, wsd, g, b, *, tt, alpha):
    t, d = x.shape
    assert TOP_K == 8
    return pl.pallas_call(
        functools.partial(_combine_kernel, tt=tt, alpha=alpha),
        grid=(t // tt,),
        in_specs=[pl.BlockSpec((tt * TOP_K,), lambda i: (i,), memory_space=pltpu.SMEM),
                  pl.BlockSpec((tt, d), lambda i: (i, 0)),
                  pl.BlockSpec((tt, TOP_K), lambda i: (i, 0)),
                  pl.BlockSpec(memory_space=pl.ANY),
                  _const_spec(wsg.shape), _const_spec(wsu.shape), _const_spec(wsd.shape),
                  _const_spec(g.shape), _const_spec(b.shape)],
        out_specs=pl.BlockSpec((tt, d), lambda i: (i, 0)),
        out_shape=jax.ShapeDtypeStruct((t, d), F32),
        scratch_shapes=[pltpu.VMEM((tt * TOP_K, d), F32), pltpu.SemaphoreType.DMA(())],
        compiler_params=_params("arbitrary"),
        name="combine_ln2",
    )(dest_flat, x, gate, ys, wsg, wsu, wsd, g, b)


def _block_diag_tiles(w):
    n, s, _ = w.shape
    per = MXU_DIM // s
    w = w.reshape(n // per, per, s, s)
    eye = jnp.eye(per, dtype=w.dtype)
    return jnp.einsum('tpij,pq->tpiqj', w, eye).reshape(n // per, MXU_DIM, MXU_DIM)


def _row(v):
    return v.reshape(1, -1)


def _moe(x1, w_router, router_bias, w_gate, w_up, w_down, ws_gate, ws_up, ws_down, ln_g, ln_b, alpha):
    t, d = x1.shape
    ne = w_router.shape[1]
    eidx, gate, rank, counts = _router(x1, w_router, _row(router_bias), tm=256)
    counts = counts[0]
    padded = (counts + MOE_ROWS - 1) // MOE_ROWS * MOE_ROWS
    pad_ends = jnp.cumsum(padded)
    pad_starts = pad_ends - padded
    nblocks = (t * TOP_K) // MOE_ROWS + ne
    block_start = jnp.arange(nblocks, dtype=jnp.int32) * MOE_ROWS
    block_expert = jnp.minimum(jnp.sum(pad_ends[None, :] <= block_start[:, None], axis=1), ne - 1).astype(jnp.int32)
    block_rows = jnp.clip(pad_starts[block_expert] + counts[block_expert] - block_start, 0, MOE_ROWS).astype(jnp.int32)
    n_used = jnp.maximum(pad_ends[-1:] // MOE_ROWS, 1).astype(jnp.int32)

    dest = _dest(eidx, rank, _row(pad_starts.astype(jnp.int32)), tm=256)
    dest_flat = dest.reshape(-1)
    xs = _dispatch(dest_flat, x1, nblocks * MOE_ROWS, tt=256)
    ys = _experts(block_expert, block_rows, n_used, xs, w_gate, w_up, w_down)
    return _combine(dest_flat, x1, gate, ys, ws_gate.astype(BF16), ws_up.astype(BF16),
                    ws_down.astype(BF16), _row(ln_g), _row(ln_b), tt=128, alpha=alpha)


def _pad_time(a, lp):
    return jnp.pad(a, ((0, 0), (0, lp - a.shape[1]), (0, 0)))


def kernel(x_prompt, x_sample, state_rglru_h, state_conv, state_gla, meta_tokens, w_in, conv_w, conv_b,
           rg_wa, rg_ba, rg_wi, rg_bi, rg_lambda, gla_wf2, gla_bf, gla_norm_g, w_out, ln1_g, ln1_b,
           w_router, router_bias, w_gate, w_up, w_down, ws_gate, ws_up, ws_down, ln2_g, ln2_b):
    bp, lp, d = x_prompt.shape
    bs, ls, _ = x_sample.shape
    depth = w_in.shape[0]
    nh, dk, dv = state_gla.shape[2:]
    qk, vw = nh * dk, nh * dv
    lowrank = gla_wf2.shape[1]
    alpha = (2.0 * depth) ** 0.25
    ls_pad = SUBLANES
    assert depth == 1 and ls <= ls_pad and lp % 256 == 0 and N_META % SUBLANES == 0

    xp = x_prompt.reshape(bp * lp, d)
    xs = x_sample.reshape(bs * ls, d)
    xm = meta_tokens.astype(F32)

    hp_l, cp_l, sp_l, hs_l, cs_l, ss_l = [], [], [], [], [], []
    for l in range(depth):
        o1, o2 = 2 * d, 2 * d + 2 * qk + 2 * vw
        wa = w_in[l][:, :o1].astype(BF16)
        wb = jnp.pad(w_in[l][:, o1:o2 + lowrank], ((0, 0), (0, LANES - lowrank))).astype(BF16)
        wm = w_in[l][:, o2 + lowrank:].astype(BF16)
        wf = jnp.pad(gla_wf2[l], ((0, LANES - lowrank), (0, 0))).astype(BF16)
        wa4 = _block_diag_tiles(rg_wa[l]).astype(BF16)
        wi4 = _block_diag_tiles(rg_wi[l]).astype(BF16)
        mixer_w = (conv_w[l], _row(conv_b[l]), wa4, _row(rg_ba[l]), wi4, _row(rg_bi[l]), _row(rg_lambda[l]))
        gla_w = (wf, _row(gla_bf[l]), _row(gla_norm_g[l]))

        ua, ub, _ = _inproj(xm, wa, wb, wm, tm=N_META)
        _, h_m, c_m = _rglru(ua[None], jnp.zeros((1, CONV_W - 1, d), F32), jnp.zeros((1, 1, d), F32),
                             *mixer_w, tb=N_META, l_real=N_META)
        _, s_m = _gla(ub[None], jnp.zeros((1, nh, dk, dv), F32), *gla_w,
                      tb=N_META, chunk=N_META, l_real=N_META)

        ua, ub, um = _inproj(xp, wa, wb, wm, tm=256)
        ya, h_p, c_p = _rglru(ua.reshape(bp, lp, -1), jnp.broadcast_to(c_m, (bp,) + c_m.shape[1:]),
                              jnp.broadcast_to(h_m, (bp,) + h_m.shape[1:]), *mixer_w, tb=256, l_real=lp)
        yb, s_p = _gla(ub.reshape(bp, lp, -1), jnp.broadcast_to(s_m, (bp,) + s_m.shape[1:]), *gla_w,
                       tb=256, chunk=GLA_CHUNK, l_real=lp)
        x1p = _mix(ya.reshape(bp * lp, d), yb.reshape(bp * lp, d), um, xp, w_out[l].astype(BF16),
                   _row(ln1_g[l]), _row(ln1_b[l]), alpha=alpha, tm=512)

        xs_pad = _pad_time(xs.reshape(bs, ls, d), ls_pad).reshape(bs * ls_pad, d)
        ua, ub, um = _inproj(xs_pad, wa, wb, wm, tm=256)
        ya, h_s, c_s = _rglru(ua.reshape(bs, ls_pad, -1), state_conv[l].astype(F32),
                              state_rglru_h[l].astype(F32)[:, None, :], *mixer_w, tb=ls_pad, l_real=ls)
        yb, s_s = _gla(ub.reshape(bs, ls_pad, -1), state_gla[l].astype(F32), *gla_w,
                       tb=ls_pad, chunk=ls_pad, l_real=ls)
        unpad = lambda a: a.reshape(bs, ls_pad, -1)[:, :ls].reshape(bs * ls, -1)
        x1s = _mix(unpad(ya), unpad(yb), unpad(um), xs, w_out[l].astype(BF16),
                   _row(ln1_g[l]), _row(ln1_b[l]), alpha=alpha, tm=512)

        x1 = jnp.concatenate([x1p, x1s], axis=0)
        x2 = _moe(x1, w_router[l], router_bias[l], w_gate[l], w_up[l], w_down[l],
                  ws_gate[l], ws_up[l], ws_down[l], ln2_g[l], ln2_b[l], alpha)
        xp, xs = x2[:bp * lp], x2[bp * lp:]

        hp_l.append(h_p[:, 0]); cp_l.append(c_p); sp_l.append(s_p)
        hs_l.append(h_s[:, 0]); cs_l.append(c_s); ss_l.append(s_s)

    y_prompt = xp.reshape(bp, lp, d)
    y_sample = xs.reshape(bs, ls, d)
    return (y_prompt, y_sample,
            jnp.stack(hp_l).astype(state_rglru_h.dtype), jnp.stack(cp_l).astype(state_conv.dtype),
            jnp.stack(sp_l).astype(state_gla.dtype),
            jnp.stack(hs_l).astype(state_rglru_h.dtype), jnp.stack(cs_l).astype(state_conv.dtype),
            jnp.stack(ss_l).astype(state_gla.dtype))
```

```python
import functools

import jax
import jax.numpy as jnp
from jax import lax
from jax.experimental import pallas as pl
from jax.experimental.pallas import tpu as pltpu

F32 = jnp.float32
BF16 = jnp.bfloat16
HIGHEST = lax.Precision.HIGHEST

N_META = 16
CONV_W = 4
RG_C = 8.0
RNN_BLOCKS = 16
GLA_HEADS = 4
GLA_TAU = 16.0
GLA_CHUNK = 64
N_GROUPS = 8
TOPK_GROUPS = 4
TOP_K = 8
ROUTED_SCALE = 2.5
LN_EPS = 1e-5
RMS_EPS = 1e-6

LANES = 128
SUBLANES = 8
MXU_DIM = 256
VMEM_LIMIT_BYTES = 56 * 1024 * 1024

MOE_ROWS = 128


def _params(*sem):
    return pltpu.CompilerParams(dimension_semantics=sem, vmem_limit_bytes=VMEM_LIMIT_BYTES)


def _sigmoid(x):
    return jax.nn.sigmoid(x)


def _softplus(z):
    return jnp.maximum(z, 0.0) + jnp.log1p(jnp.exp(-jnp.abs(z)))


def _layer_norm(z, g, b):
    mu = jnp.mean(z, axis=-1, keepdims=True)
    zc = z - mu
    var = jnp.mean(zc * zc, axis=-1, keepdims=True)
    return zc * lax.rsqrt(var + LN_EPS) * g + b


def _const_spec(shape):
    nd = len(shape)
    return pl.BlockSpec(shape, lambda *_: (0,) * nd)


def _inproj_kernel(x_ref, wa_ref, wb_ref, wm_ref, ua_ref, ub_ref, um_ref):
    xb = x_ref[...].astype(BF16)
    ua_ref[...] = jnp.dot(xb, wa_ref[...], preferred_element_type=F32)
    ub_ref[...] = jnp.dot(xb, wb_ref[...], preferred_element_type=F32)
    um_ref[...] = jnp.dot(xb, wm_ref[...], preferred_element_type=F32)


def _inproj(x2d, wa, wb, wm, tm):
    m, d = x2d.shape
    na, nb, nm = wa.shape[1], wb.shape[1], wm.shape[1]
    return pl.pallas_call(
        _inproj_kernel,
        grid=(m // tm,),
        in_specs=[pl.BlockSpec((tm, d), lambda i: (i, 0)),
                  _const_spec(wa.shape), _const_spec(wb.shape), _const_spec(wm.shape)],
        out_specs=[pl.BlockSpec((tm, na), lambda i: (i, 0)),
                   pl.BlockSpec((tm, nb), lambda i: (i, 0)),
                   pl.BlockSpec((tm, nm), lambda i: (i, 0))],
        out_shape=[jax.ShapeDtypeStruct((m, na), F32),
                   jax.ShapeDtypeStruct((m, nb), F32),
                   jax.ShapeDtypeStruct((m, nm), F32)],
        compiler_params=_params("arbitrary"),
        name="inproj",
    )(x2d, wa, wb, wm)


def _rglru_kernel(ua_ref, cbuf_ref, h0_ref, cw_ref, cb_ref, wa_ref, ba_ref, wi_ref, bi_ref, lam_ref,
                  ya_ref, hlast_ref, cnew_ref, ext_scr, h_scr, *, bb, tb, l_last, nblk, d):
    j = pl.program_id(1)
    ntail = CONV_W - 1
    rows = bb * tb

    @pl.when(j == 0)
    def _():
        h_scr[...] = h0_ref[...]
        ext_scr[:, SUBLANES - ntail:SUBLANES, :] = cbuf_ref[...]

    xr = ua_ref[:, :, :d]
    gr = ua_ref[:, :, d:].reshape(rows, d)
    ext_scr[:, SUBLANES:, :] = xr
    xc = cb_ref[...] + cw_ref[ntail:ntail + 1, :] * xr
    for s in range(1, CONV_W):
        xc = xc + cw_ref[ntail - s:ntail - s + 1, :] * ext_scr[:, SUBLANES - s:SUBLANES - s + tb, :]
    xc = xc.reshape(rows, d)

    xcb = xc.astype(BF16)
    nt = d // MXU_DIM
    ra = jnp.concatenate(
        [jnp.dot(xcb[:, q * MXU_DIM:(q + 1) * MXU_DIM], wa_ref[q], preferred_element_type=F32)
         for q in range(nt)], axis=-1)
    ia = jnp.concatenate(
        [jnp.dot(xcb[:, q * MXU_DIM:(q + 1) * MXU_DIM], wi_ref[q], preferred_element_type=F32)
         for q in range(nt)], axis=-1)
    r = _sigmoid(ra + ba_ref[...])
    ig = _sigmoid(ia + bi_ref[...])
    log_a = (-RG_C) * r * _softplus(-lam_ref[...])
    a = jnp.exp(log_a)
    mult = jnp.sqrt(-jnp.tanh(log_a) * (a * a + 1.0))
    b = mult * (ig * xc)

    t_in = jnp.bitwise_and(lax.broadcasted_iota(jnp.int32, (rows, d), 0), tb - 1)
    h_prev = jnp.broadcast_to(h_scr[...], (bb, tb, d)).reshape(rows, d)
    b = b + jnp.where(t_in == 0, a * h_prev, 0.0)
    s = 1
    while s < tb:
        a_sh = jnp.where(t_in >= s, pltpu.roll(a, s, 0), 1.0)
        b_sh = jnp.where(t_in >= s, pltpu.roll(b, s, 0), 0.0)
        b = b + a * b_sh
        a = a * a_sh
        s *= 2
    ya_ref[...] = (b * jax.nn.gelu(gr, approximate=True)).reshape(bb, tb, d)
    h = b.reshape(bb, tb, d)
    h_scr[...] = h[:, tb - 1:tb, :]

    @pl.when(j == nblk - 1)
    def _():
        hlast_ref[...] = h[:, l_last - 1:l_last, :]
        cnew_ref[...] = ext_scr[:, SUBLANES + l_last - ntail:SUBLANES + l_last, :]

    tail = ext_scr[:, tb + SUBLANES - ntail:tb + SUBLANES, :]
    ext_scr[:, SUBLANES - ntail:SUBLANES, :] = tail


def _rglru(ua, cbuf, h0, cw, cb, wa4, ba, wi4, bi, lam, *, bb, tb, l_real):
    bsz, lp, n2 = ua.shape
    d = n2 // 2
    nblk = lp // tb
    l_last = l_real - (nblk - 1) * tb
    assert tb & (tb - 1) == 0 and bsz % bb == 0
    kern = functools.partial(_rglru_kernel, bb=bb, tb=tb, l_last=l_last, nblk=nblk, d=d)
    return pl.pallas_call(
        kern,
        grid=(bsz // bb, nblk),
        in_specs=[pl.BlockSpec((bb, tb, n2), lambda b, j: (b, j, 0)),
                  pl.BlockSpec((bb, CONV_W - 1, d), lambda b, j: (b, 0, 0)),
                  pl.BlockSpec((bb, 1, d), lambda b, j: (b, 0, 0)),
                  _const_spec(cw.shape), _const_spec(cb.shape),
                  _const_spec(wa4.shape), _const_spec(ba.shape),
                  _const_spec(wi4.shape), _const_spec(bi.shape), _const_spec(lam.shape)],
        out_specs=[pl.BlockSpec((bb, tb, d), lambda b, j: (b, j, 0)),
                   pl.BlockSpec((bb, 1, d), lambda b, j: (b, 0, 0)),
                   pl.BlockSpec((bb, CONV_W - 1, d), lambda b, j: (b, 0, 0))],
        out_shape=[jax.ShapeDtypeStruct((bsz, lp, d), F32),
                   jax.ShapeDtypeStruct((bsz, 1, d), F32),
                   jax.ShapeDtypeStruct((bsz, CONV_W - 1, d), F32)],
        scratch_shapes=[pltpu.VMEM((bb, tb + SUBLANES, d), F32), pltpu.VMEM((bb, 1, d), F32)],
        compiler_params=_params("arbitrary", "arbitrary"),
        name="rglru",
    )(ua, cbuf, h0, cw, cb, wa4, ba, wi4, bi, lam)


def _gla_kernel(ub_ref, s0_ref, wf_ref, bf_ref, g_ref, yb_ref, sout_ref, s_scr,
                *, bb, tb, chunk, l_real, nblk, dk, dv):
    j = pl.program_id(1)
    nh = GLA_HEADS
    qk = nh * dk
    vw = nh * dv

    @pl.when(j == 0)
    def _():
        s_scr[...] = s0_ref[...]

    off_k, off_v, off_g, off_f = qk, 2 * qk, 2 * qk + vw, 2 * qk + 2 * vw
    ri = lax.broadcasted_iota(jnp.int32, (chunk, chunk), 0)
    ci = lax.broadcasted_iota(jnp.int32, (chunk, chunk), 1)
    causal = ri >= ci
    tril = causal.astype(F32)
    mid = chunk // 2
    scale = dk ** -0.5
    tdims = (((0,), (0,)), ((), ()))

    for bi in range(bb):
        fl = ub_ref[bi, :, off_f:off_f + LANES].astype(BF16)
        logit = jnp.dot(fl, wf_ref[...], preferred_element_type=F32) + bf_ref[...]
        log_f = (jnp.minimum(logit, 0.0) - jnp.log1p(jnp.exp(-jnp.abs(logit)))) * (1.0 / GLA_TAU)
        if l_real < tb * nblk:
            rows = lax.broadcasted_iota(jnp.int32, log_f.shape, 0) + j * tb
            log_f = jnp.where(rows < l_real, log_f, 0.0)

        for c in range(tb // chunk):
            r0 = c * chunk
            for h in range(nh):
                q = ub_ref[bi, r0:r0 + chunk, h * dk:(h + 1) * dk] * scale
                k = ub_ref[bi, r0:r0 + chunk, off_k + h * dk:off_k + (h + 1) * dk]
                v = ub_ref[bi, r0:r0 + chunk, off_v + h * dv:off_v + (h + 1) * dv]
                go = ub_ref[bi, r0:r0 + chunk, off_g + h * dv:off_g + (h + 1) * dv]
                f = log_f[r0:r0 + chunk, h * dk:(h + 1) * dk]
                bcum = jnp.dot(tril, f, precision=HIGHEST, preferred_element_type=F32)
                ref = bcum[mid:mid + 1, :]
                blast = bcum[chunk - 1:chunk, :]
                s_old = s_scr[bi, h]
                vb = v.astype(BF16)
                o_inter = jnp.dot((q * jnp.exp(bcum)).astype(BF16), s_old.astype(BF16),
                                  preferred_element_type=F32)
                att = lax.dot_general((q * jnp.exp(bcum - ref)).astype(BF16),
                                      (k * jnp.exp(ref - bcum)).astype(BF16),
                                      (((1,), (1,)), ((), ())), preferred_element_type=F32)
                att = jnp.where(causal, att, 0.0)
                o = o_inter + jnp.dot(att.astype(BF16), vb, preferred_element_type=F32)
                k_dec = (k * jnp.exp(blast - bcum)).astype(BF16)
                dec = jnp.transpose(jnp.broadcast_to(jnp.exp(blast), (dk, dk)))
                dec = jnp.concatenate([dec] * (dv // dk), axis=1)
                s_scr[bi, h] = dec * s_old + lax.dot_general(k_dec, vb, tdims, preferred_element_type=F32)
                o = o * lax.rsqrt(jnp.mean(o * o, axis=-1, keepdims=True) + RMS_EPS) * g_ref[...]
                yb_ref[bi, r0:r0 + chunk, h * dv:(h + 1) * dv] = o * (go * _sigmoid(go))

    @pl.when(j == nblk - 1)
    def _():
        sout_ref[...] = s_scr[...]


def _gla(ub, s0, wf, bf, g, *, bb, tb, chunk, l_real):
    bsz, lp, nb = ub.shape
    _, nh, dk, dv = s0.shape
    nblk = lp // tb
    assert bsz % bb == 0 and dv % dk == 0
    kern = functools.partial(_gla_kernel, bb=bb, tb=tb, chunk=chunk, l_real=l_real, nblk=nblk, dk=dk, dv=dv)
    return pl.pallas_call(
        kern,
        grid=(bsz // bb, nblk),
        in_specs=[pl.BlockSpec((bb, tb, nb), lambda b, j: (b, j, 0)),
                  pl.BlockSpec((bb, nh, dk, dv), lambda b, j: (b, 0, 0, 0)),
                  _const_spec(wf.shape), _const_spec(bf.shape), _const_spec(g.shape)],
        out_specs=[pl.BlockSpec((bb, tb, nh * dv), lambda b, j: (b, j, 0)),
                   pl.BlockSpec((bb, nh, dk, dv), lambda b, j: (b, 0, 0, 0))],
        out_shape=[jax.ShapeDtypeStruct((bsz, lp, nh * dv), F32),
                   jax.ShapeDtypeStruct((bsz, nh, dk, dv), F32)],
        scratch_shapes=[pltpu.VMEM((bb, nh, dk, dv), F32)],
        compiler_params=_params("arbitrary", "arbitrary"),
        name="gla",
    )(ub, s0, wf, bf, g)


def _mix_kernel(ya_ref, yb_ref, um_ref, x_ref, wo_ref, g_ref, b_ref, o_ref, *, alpha, d):
    merged = _sigmoid(um_ref[:, :d]) * ya_ref[...] + _sigmoid(um_ref[:, d:]) * yb_ref[...]
    mix = jnp.dot(merged.astype(BF16), wo_ref[...], preferred_element_type=F32)
    o_ref[...] = _layer_norm(alpha * x_ref[...] + mix, g_ref[...], b_ref[...])


def _mix(ya, yb, um, x, wo, g, b, *, alpha, tm):
    m, d = x.shape
    row = lambda w: pl.BlockSpec((tm, w), lambda i: (i, 0))
    return pl.pallas_call(
        functools.partial(_mix_kernel, alpha=alpha, d=d),
        grid=(m // tm,),
        in_specs=[row(d), row(d), row(2 * d), row(d),
                  _const_spec(wo.shape), _const_spec(g.shape), _const_spec(b.shape)],
        out_specs=row(d),
        out_shape=jax.ShapeDtypeStruct((m, d), F32),
        compiler_params=_params("arbitrary"),
        name="mix_ln1",
    )(ya, yb, um, x, wo, g, b)


def _router_kernel(x_ref, wr_ref, bias_ref, eidx_ref, gate_ref, rank_ref, cnt_ref, run_scr,
                   *, tm, ne):
    i = pl.program_id(0)

    @pl.when(i == 0)
    def _():
        run_scr[...] = jnp.zeros_like(run_scr)

    logits = jnp.dot(x_ref[...], wr_ref[...], precision=HIGHEST, preferred_element_type=F32)
    s = _sigmoid(logits)
    sel = s + bias_ref[...]
    lane = lax.broadcasted_iota(jnp.int32, (tm, ne), 1)
    lane_f = lane.astype(F32)
    gsz = ne // N_GROUPS
    neg = -jnp.inf

    gscores = []
    for g in range(N_GROUPS):
        ing = (lane >= g * gsz) & (lane < (g + 1) * gsz)
        v = jnp.where(ing, sel, neg)
        m1 = jnp.max(v, axis=1, keepdims=True)
        i1 = jnp.min(jnp.where(v == m1, lane_f, float(ne)), axis=1, keepdims=True)
        m2 = jnp.max(jnp.where(lane_f == i1, neg, v), axis=1, keepdims=True)
        gscores.append(m1 + m2)
    cur = jnp.full((tm, ne), neg, F32)
    for g in range(N_GROUPS):
        ahead = jnp.zeros((tm, 1), F32)
        for g2 in range(N_GROUPS):
            if g2 == g:
                continue
            beats = (gscores[g2] >= gscores[g]) if g2 < g else (gscores[g2] > gscores[g])
            ahead = ahead + beats.astype(F32)
        ing = (lane >= g * gsz) & (lane < (g + 1) * gsz)
        cur = jnp.where(ing & (ahead < float(TOPK_GROUPS)), sel, cur)

    col = lax.broadcasted_iota(jnp.int32, (tm, TOP_K), 1)
    eidx = jnp.zeros((tm, TOP_K), F32)
    wsel = jnp.zeros((tm, TOP_K), F32)
    onehot_sum = jnp.zeros((tm, ne), F32)
    picks = []
    for k in range(TOP_K):
        m = jnp.max(cur, axis=1, keepdims=True)
        ik = jnp.min(jnp.where(cur == m, lane_f, float(ne)), axis=1, keepdims=True)
        hit = lane_f == ik
        wk = jnp.sum(jnp.where(hit, s, 0.0), axis=1, keepdims=True)
        cur = jnp.where(hit, neg, cur)
        onehot_sum = onehot_sum + hit.astype(F32)
        picks.append(hit)
        eidx = jnp.where(col == k, ik, eidx)
        wsel = jnp.where(col == k, wk, wsel)
    gate_ref[...] = wsel / jnp.sum(wsel, axis=1, keepdims=True) * ROUTED_SCALE
    eidx_ref[...] = eidx.astype(jnp.int32)

    ri = lax.broadcasted_iota(jnp.int32, (tm, tm), 0)
    ci = lax.broadcasted_iota(jnp.int32, (tm, tm), 1)
    before = (ci < ri).astype(BF16)
    counts = jnp.dot(before, onehot_sum.astype(BF16), preferred_element_type=F32) + run_scr[...]
    rank = jnp.zeros((tm, TOP_K), F32)
    for k in range(TOP_K):
        rk = jnp.sum(jnp.where(picks[k], counts, 0.0), axis=1, keepdims=True)
        rank = jnp.where(col == k, rk, rank)
    rank_ref[...] = rank.astype(jnp.int32)
    run_scr[...] = run_scr[...] + jnp.sum(onehot_sum, axis=0, keepdims=True)
    cnt_ref[...] = run_scr[...].astype(jnp.int32)


def _router(x, wr, bias, *, tm):
    t, d = x.shape
    ne = wr.shape[1]
    out_row = lambda: pl.BlockSpec((tm, TOP_K), lambda i: (i, 0))
    return pl.pallas_call(
        functools.partial(_router_kernel, tm=tm, ne=ne),
        grid=(t // tm,),
        in_specs=[pl.BlockSpec((tm, d), lambda i: (i, 0)), _const_spec(wr.shape), _const_spec(bias.shape)],
        out_specs=[out_row(), out_row(), out_row(), _const_spec((1, ne))],
        out_shape=[jax.ShapeDtypeStruct((t, TOP_K), jnp.int32),
                   jax.ShapeDtypeStruct((t, TOP_K), F32),
                   jax.ShapeDtypeStruct((t, TOP_K), jnp.int32),
                   jax.ShapeDtypeStruct((1, ne), jnp.int32)],
        scratch_shapes=[pltpu.VMEM((1, ne), F32)],
        compiler_params=_params("arbitrary"),
        name="router",
    )(x, wr, bias)


def _dest_kernel(eidx_ref, rank_ref, start_ref, dest_ref, *, tm, ne):
    lane = lax.broadcasted_iota(jnp.int32, (tm, ne), 1)
    col = lax.broadcasted_iota(jnp.int32, (tm, TOP_K), 1)
    starts = start_ref[...].astype(F32)
    eidx = eidx_ref[...]
    dest = jnp.zeros((tm, TOP_K), F32)
    for k in range(TOP_K):
        hit = lane == eidx[:, k:k + 1]
        sk = jnp.sum(jnp.where(hit, starts, 0.0), axis=1, keepdims=True)
        dest = jnp.where(col == k, sk, dest)
    dest_ref[...] = dest.astype(jnp.int32) + rank_ref[...]


def _dest(eidx, rank, starts, *, tm):
    t = eidx.shape[0]
    ne = starts.shape[1]
    row = lambda: pl.BlockSpec((tm, TOP_K), lambda i: (i, 0))
    return pl.pallas_call(
        functools.partial(_dest_kernel, tm=tm, ne=ne),
        grid=(t // tm,),
        in_specs=[row(), row(), _const_spec((1, ne))],
        out_specs=row(),
        out_shape=jax.ShapeDtypeStruct((t, TOP_K), jnp.int32),
        compiler_params=_params("arbitrary"),
        name="dest",
    )(eidx, rank, starts)


def _row_copy(src_ref, src_row, dst_ref, dst_row, sem):
    return pltpu.make_async_copy(src_ref.at[pl.ds(src_row, 1), :], dst_ref.at[pl.ds(dst_row, 1), :], sem)


def _dispatch_kernel(dest_ref, x_ref, xs_ref, sem, *, tt):
    def start(t, c):
        for k in range(TOP_K):
            _row_copy(x_ref, t, xs_ref, dest_ref[t * TOP_K + k], sem).start(priority=k % 2)
        return c

    lax.fori_loop(0, tt, start, 0)
    for _ in range(TOP_K):
        pltpu.make_async_copy(x_ref, xs_ref.at[pl.ds(0, tt), :], sem).wait()


def _dispatch(dest_flat, x, n_slots, *, tt):
    t, d = x.shape
    return pl.pallas_call(
        functools.partial(_dispatch_kernel, tt=tt),
        grid=(t // tt,),
        in_specs=[pl.BlockSpec((tt * TOP_K,), lambda i: (i,), memory_space=pltpu.SMEM),
                  pl.BlockSpec((tt, d), lambda i: (i, 0))],
        out_specs=pl.BlockSpec(memory_space=pl.ANY),
        out_shape=jax.ShapeDtypeStruct((n_slots, d), F32),
        scratch_shapes=[pltpu.SemaphoreType.DMA(())],
        compiler_params=_params("arbitrary"),
        name="dispatch",
    )(dest_flat, x)


def _expert_kernel(base_ref, nblk_ref, cnt_ref, nu_ref, xs_ref, wg_ref, wu_ref, wd_ref, ys_ref,
                   xbuf, ybuf, wgb, wub, wdb, sem_in, sem_out):
    e = pl.program_id(0)
    n_used = nu_ref[0]
    nb = nblk_ref[e]
    base = base_ref[e]
    cnt = cnt_ref[e]

    def x_copy(blk, slot):
        return pltpu.make_async_copy(xs_ref.at[pl.ds(blk * MOE_ROWS, MOE_ROWS), :], xbuf.at[slot], sem_in.at[slot])

    def y_copy(blk, slot):
        return pltpu.make_async_copy(ybuf.at[slot], ys_ref.at[pl.ds(blk * MOE_ROWS, MOE_ROWS), :], sem_out.at[slot])

    @pl.when(e == 0)
    def _():
        x_copy(0, 0).start()

    @pl.when(nb > 0)
    def _():
        wgb[...] = wg_ref[0].astype(BF16)
        wub[...] = wu_ref[0].astype(BF16)
        wdb[...] = wd_ref[0].astype(BF16)

    def block(jb, c):
        blk = base + jb
        slot = jnp.bitwise_and(blk, 1)
        x_copy(blk, slot).wait()

        @pl.when(blk + 1 < n_used)
        def _():
            x_copy(blk + 1, 1 - slot).start()

        x = xbuf[slot]
        rows = lax.broadcasted_iota(jnp.int32, x.shape, 0)
        xb = jnp.where(rows < cnt - jb * MOE_ROWS, x, 0.0).astype(BF16)
        g = jnp.dot(xb, wgb[...], preferred_element_type=F32)
        u = jnp.dot(xb, wub[...], preferred_element_type=F32)
        hmid = (g * _sigmoid(g)) * u
        y = jnp.dot(hmid.astype(BF16), wdb[...], preferred_element_type=F32)

        @pl.when(blk >= 2)
        def _():
            y_copy(blk - 2, slot).wait()

        ybuf[slot] = y
        y_copy(blk, slot).start()
        return c

    lax.fori_loop(0, nb, block, 0)

    @pl.when(e == pl.num_programs(0) - 1)
    def _():
        for back in range(2):
            @pl.when(n_used > back)
            def _():
                last = n_used - 1 - back
                y_copy(last, jnp.bitwise_and(last, 1)).wait()


def _experts(base, nblk, cnt, n_used, xs, wg, wu, wd):
    p, d = xs.shape
    ne, _, de = wg.shape
    grid_spec = pltpu.PrefetchScalarGridSpec(
        num_scalar_prefetch=4,
        grid=(ne,),
        in_specs=[pl.BlockSpec(memory_space=pl.ANY),
                  pl.BlockSpec((1, d, de), lambda e, *_: (e, 0, 0)),
                  pl.BlockSpec((1, d, de), lambda e, *_: (e, 0, 0)),
                  pl.BlockSpec((1, de, d), lambda e, *_: (e, 0, 0))],
        out_specs=pl.BlockSpec(memory_space=pl.ANY),
        scratch_shapes=[pltpu.VMEM((2, MOE_ROWS, d), F32), pltpu.VMEM((2, MOE_ROWS, d), F32),
                        pltpu.VMEM((d, de), BF16), pltpu.VMEM((d, de), BF16), pltpu.VMEM((de, d), BF16),
                        pltpu.SemaphoreType.DMA((2,)), pltpu.SemaphoreType.DMA((2,))],
    )
    return pl.pallas_call(
        _expert_kernel,
        grid_spec=grid_spec,
        out_shape=jax.ShapeDtypeStruct((p, d), F32),
        compiler_params=_params("arbitrary"),
        name="experts",
    )(base, nblk, cnt, n_used, xs, wg, wu, wd)


def _combine_kernel(dest_ref, dest_next_ref, x_ref, gate_ref, ys_ref, wsg_ref, wsu_ref, wsd_ref, g_ref, b_ref,
                    o_ref, ybuf, sem, *, tt, alpha):
    i = pl.program_id(0)
    slot = jnp.bitwise_and(i, 1)

    def gather(dref, s):
        def start(t, c):
            for k in range(TOP_K):
                _row_copy(ys_ref, dref[t * TOP_K + k], ybuf.at[s], k * tt + t, sem.at[s]).start(priority=k % 2)
            return c
        lax.fori_loop(0, tt, start, 0)

    @pl.when(i == 0)
    def _():
        gather(dest_ref, 0)

    @pl.when(i + 1 < pl.num_programs(0))
    def _():
        gather(dest_next_ref, 1 - slot)

    x = x_ref[...]
    xb = x.astype(BF16)
    sg = jnp.dot(xb, wsg_ref[...], preferred_element_type=F32)
    su = jnp.dot(xb, wsu_ref[...], preferred_element_type=F32)
    acc = jnp.dot(((sg * _sigmoid(sg)) * su).astype(BF16), wsd_ref[...], preferred_element_type=F32)
    pltpu.make_async_copy(ys_ref.at[pl.ds(0, tt * TOP_K), :], ybuf.at[slot], sem.at[slot]).wait()
    gate = gate_ref[...]
    for k in range(TOP_K):
        acc = acc + gate[:, k:k + 1] * ybuf[slot, k * tt:(k + 1) * tt, :]
    o_ref[...] = _layer_norm(alpha * x + acc, g_ref[...], b_ref[...])


def _combine(dest_flat, x, gate, ys, wsg, wsu, wsd, g, b, *, tt, alpha):
    t, d = x.shape
    nsteps = t // tt
    return pl.pallas_call(
        functools.partial(_combine_kernel, tt=tt, alpha=alpha),
        grid=(nsteps,),
        in_specs=[pl.BlockSpec((tt * TOP_K,), lambda i: (i,), memory_space=pltpu.SMEM),
                  pl.BlockSpec((tt * TOP_K,), lambda i: (jnp.minimum(i + 1, nsteps - 1),),
                               memory_space=pltpu.SMEM),
                  pl.BlockSpec((tt, d), lambda i: (i, 0)),
                  pl.BlockSpec((tt, TOP_K), lambda i: (i, 0)),
                  pl.BlockSpec(memory_space=pl.ANY),
                  _const_spec(wsg.shape), _const_spec(wsu.shape), _const_spec(wsd.shape),
                  _const_spec(g.shape), _const_spec(b.shape)],
        out_specs=pl.BlockSpec((tt, d), lambda i: (i, 0)),
        out_shape=jax.ShapeDtypeStruct((t, d), F32),
        scratch_shapes=[pltpu.VMEM((2, tt * TOP_K, d), F32), pltpu.SemaphoreType.DMA((2,))],
        compiler_params=_params("arbitrary"),
        name="combine_ln2",
    )(dest_flat, dest_flat, x, gate, ys, wsg, wsu, wsd, g, b)


def _block_diag_tiles(w):
    n, s, _ = w.shape
    per = MXU_DIM // s
    w = w.reshape(n // per, per, s, s)
    eye = jnp.eye(per, dtype=w.dtype)
    return jnp.einsum('tpij,pq->tpiqj', w, eye).reshape(n // per, MXU_DIM, MXU_DIM)


def _row(v):
    return v.reshape(1, -1)


def _moe(x1, w_router, router_bias, w_gate, w_up, w_down, ws_gate, ws_up, ws_down, ln_g, ln_b, alpha):
    t, d = x1.shape
    ne = w_router.shape[1]
    eidx, gate, rank, counts = _router(x1, w_router, _row(router_bias), tm=256)
    counts = counts[0]
    nblk = (counts + MOE_ROWS - 1) // MOE_ROWS
    blk_end = jnp.cumsum(nblk)
    blk_base = (blk_end - nblk).astype(jnp.int32)
    n_used = blk_end[-1:].astype(jnp.int32)
    n_slots = ((t * TOP_K) // MOE_ROWS + ne) * MOE_ROWS

    dest = _dest(eidx, rank, _row(blk_base * MOE_ROWS), tm=256)
    dest_flat = dest.reshape(-1)
    xs = _dispatch(dest_flat, x1, n_slots, tt=256)
    ys = _experts(blk_base, nblk.astype(jnp.int32), counts, n_used, xs, w_gate, w_up, w_down)
    return _combine(dest_flat, x1, gate, ys, ws_gate.astype(BF16), ws_up.astype(BF16),
                    ws_down.astype(BF16), _row(ln_g), _row(ln_b), tt=128, alpha=alpha)


def _pad_time(a, lp):
    return jnp.pad(a, ((0, 0), (0, lp - a.shape[1]), (0, 0)))


def kernel(x_prompt, x_sample, state_rglru_h, state_conv, state_gla, meta_tokens, w_in, conv_w, conv_b,
           rg_wa, rg_ba, rg_wi, rg_bi, rg_lambda, gla_wf2, gla_bf, gla_norm_g, w_out, ln1_g, ln1_b,
           w_router, router_bias, w_gate, w_up, w_down, ws_gate, ws_up, ws_down, ln2_g, ln2_b):
    bp, lp, d = x_prompt.shape
    bs, ls, _ = x_sample.shape
    depth = w_in.shape[0]
    nh, dk, dv = state_gla.shape[2:]
    qk, vw = nh * dk, nh * dv
    lowrank = gla_wf2.shape[1]
    alpha = (2.0 * depth) ** 0.25
    ls_pad = SUBLANES
    assert depth == 1 and ls <= ls_pad and lp % 256 == 0 and N_META % SUBLANES == 0

    xp = x_prompt.reshape(bp * lp, d)
    xs = x_sample.reshape(bs * ls, d)
    xm = meta_tokens.astype(F32)

    hp_l, cp_l, sp_l, hs_l, cs_l, ss_l = [], [], [], [], [], []
    for l in range(depth):
        o1, o2 = 2 * d, 2 * d + 2 * qk + 2 * vw
        wa = w_in[l][:, :o1].astype(BF16)
        wb = jnp.pad(w_in[l][:, o1:o2 + lowrank], ((0, 0), (0, LANES - lowrank))).astype(BF16)
        wm = w_in[l][:, o2 + lowrank:].astype(BF16)
        wf = jnp.pad(gla_wf2[l], ((0, LANES - lowrank), (0, 0))).astype(BF16)
        wa4 = _block_diag_tiles(rg_wa[l]).astype(BF16)
        wi4 = _block_diag_tiles(rg_wi[l]).astype(BF16)
        mixer_w = (conv_w[l], _row(conv_b[l]), wa4, _row(rg_ba[l]), wi4, _row(rg_bi[l]), _row(rg_lambda[l]))
        gla_w = (wf, _row(gla_bf[l]), _row(gla_norm_g[l]))

        ua, ub, _ = _inproj(xm, wa, wb, wm, tm=N_META)
        _, h_m, c_m = _rglru(ua[None], jnp.zeros((1, CONV_W - 1, d), F32), jnp.zeros((1, 1, d), F32),
                             *mixer_w, bb=1, tb=N_META, l_real=N_META)
        _, s_m = _gla(ub[None], jnp.zeros((1, nh, dk, dv), F32), *gla_w,
                      bb=1, tb=N_META, chunk=N_META, l_real=N_META)

        ua, ub, um = _inproj(xp, wa, wb, wm, tm=256)
        ya, h_p, c_p = _rglru(ua.reshape(bp, lp, -1), jnp.broadcast_to(c_m, (bp,) + c_m.shape[1:]),
                              jnp.broadcast_to(h_m, (bp,) + h_m.shape[1:]), *mixer_w,
                              bb=1, tb=256, l_real=lp)
        yb, s_p = _gla(ub.reshape(bp, lp, -1), jnp.broadcast_to(s_m, (bp,) + s_m.shape[1:]), *gla_w,
                       bb=1, tb=256, chunk=GLA_CHUNK, l_real=lp)
        x1p = _mix(ya.reshape(bp * lp, d), yb.reshape(bp * lp, d), um, xp, w_out[l].astype(BF16),
                   _row(ln1_g[l]), _row(ln1_b[l]), alpha=alpha, tm=512)

        xs_pad = _pad_time(xs.reshape(bs, ls, d), ls_pad).reshape(bs * ls_pad, d)
        ua, ub, um = _inproj(xs_pad, wa, wb, wm, tm=256)
        ya, h_s, c_s = _rglru(ua.reshape(bs, ls_pad, -1), state_conv[l].astype(F32),
                              state_rglru_h[l].astype(F32)[:, None, :], *mixer_w,
                              bb=32, tb=ls_pad, l_real=ls)
        yb, s_s = _gla(ub.reshape(bs, ls_pad, -1), state_gla[l].astype(F32), *gla_w,
                       bb=8, tb=ls_pad, chunk=ls_pad, l_real=ls)
        unpad = lambda a: a.reshape(bs, ls_pad, -1)[:, :ls].reshape(bs * ls, -1)
        x1s = _mix(unpad(ya), unpad(yb), unpad(um), xs, w_out[l].astype(BF16),
                   _row(ln1_g[l]), _row(ln1_b[l]), alpha=alpha, tm=512)

        x1 = jnp.concatenate([x1p, x1s], axis=0)
        x2 = _moe(x1, w_router[l], router_bias[l], w_gate[l], w_up[l], w_down[l],
                  ws_gate[l], ws_up[l], ws_down[l], ln2_g[l], ln2_b[l], alpha)
        xp, xs = x2[:bp * lp], x2[bp * lp:]

        hp_l.append(h_p[:, 0]); cp_l.append(c_p); sp_l.append(s_p)
        hs_l.append(h_s[:, 0]); cs_l.append(c_s); ss_l.append(s_s)

    y_prompt = xp.reshape(bp, lp, d)
    y_sample = xs.reshape(bs, ls, d)
    return (y_prompt, y_sample,
            jnp.stack(hp_l).astype(state_rglru_h.dtype), jnp.stack(cp_l).astype(state_conv.dtype),
            jnp.stack(sp_l).astype(state_gla.dtype),
            jnp.stack(hs_l).astype(state_rglru_h.dtype), jnp.stack(cs_l).astype(state_conv.dtype),
            jnp.stack(ss_l).astype(state_gla.dtype))
```

```python
import functools

import jax
import jax.numpy as jnp
from jax import lax
from jax.experimental import pallas as pl
from jax.experimental.pallas import tpu as pltpu

F32 = jnp.float32
BF16 = jnp.bfloat16
HIGHEST = lax.Precision.HIGHEST

N_META = 16
CONV_W = 4
RG_C = 8.0
RNN_BLOCKS = 16
GLA_HEADS = 4
GLA_TAU = 16.0
GLA_CHUNK = 64
N_GROUPS = 8
TOPK_GROUPS = 4
TOP_K = 8
ROUTED_SCALE = 2.5
LN_EPS = 1e-5
RMS_EPS = 1e-6

LANES = 128
SUBLANES = 8
MXU_DIM = 256
VMEM_LIMIT_BYTES = 56 * 1024 * 1024

MOE_ROWS = 128


def _params(*sem):
    return pltpu.CompilerParams(dimension_semantics=sem, vmem_limit_bytes=VMEM_LIMIT_BYTES)


def _sigmoid(x):
    return jax.nn.sigmoid(x)


def _softplus(z):
    return jnp.maximum(z, 0.0) + jnp.log1p(jnp.exp(-jnp.abs(z)))


def _layer_norm(z, g, b):
    mu = jnp.mean(z, axis=-1, keepdims=True)
    zc = z - mu
    var = jnp.mean(zc * zc, axis=-1, keepdims=True)
    return zc * lax.rsqrt(var + LN_EPS) * g + b


def _const_spec(shape):
    nd = len(shape)
    return pl.BlockSpec(shape, lambda *_: (0,) * nd)


def _inproj_kernel(x_ref, wa_ref, wb_ref, wm_ref, ua_ref, ub_ref, um_ref):
    xb = x_ref[...].astype(BF16)
    ua_ref[...] = jnp.dot(xb, wa_ref[...], preferred_element_type=F32)
    ub_ref[...] = jnp.dot(xb, wb_ref[...], preferred_element_type=F32)
    um_ref[...] = jnp.dot(xb, wm_ref[...], preferred_element_type=F32)


def _inproj(x2d, wa, wb, wm, tm):
    m, d = x2d.shape
    na, nb, nm = wa.shape[1], wb.shape[1], wm.shape[1]
    return pl.pallas_call(
        _inproj_kernel,
        grid=(m // tm,),
        in_specs=[pl.BlockSpec((tm, d), lambda i: (i, 0)),
                  _const_spec(wa.shape), _const_spec(wb.shape), _const_spec(wm.shape)],
        out_specs=[pl.BlockSpec((tm, na), lambda i: (i, 0)),
                   pl.BlockSpec((tm, nb), lambda i: (i, 0)),
                   pl.BlockSpec((tm, nm), lambda i: (i, 0))],
        out_shape=[jax.ShapeDtypeStruct((m, na), F32),
                   jax.ShapeDtypeStruct((m, nb), F32),
                   jax.ShapeDtypeStruct((m, nm), F32)],
        compiler_params=_params("arbitrary"),
        name="inproj",
    )(x2d, wa, wb, wm)


def _rglru_kernel(ua_ref, cbuf_ref, h0_ref, cw_ref, cb_ref, wa_ref, ba_ref, wi_ref, bi_ref, lam_ref,
                  ya_ref, hlast_ref, cnew_ref, ext_scr, h_scr, *, bb, tb, l_last, nblk, d):
    j = pl.program_id(1)
    ntail = CONV_W - 1
    rows = bb * tb

    @pl.when(j == 0)
    def _():
        h_scr[...] = h0_ref[...]
        ext_scr[:, SUBLANES - ntail:SUBLANES, :] = cbuf_ref[...]

    xr = ua_ref[:, :, :d]
    gr = ua_ref[:, :, d:].reshape(rows, d)
    ext_scr[:, SUBLANES:, :] = xr
    xc = cb_ref[...] + cw_ref[ntail:ntail + 1, :] * xr
    for s in range(1, CONV_W):
        xc = xc + cw_ref[ntail - s:ntail - s + 1, :] * ext_scr[:, SUBLANES - s:SUBLANES - s + tb, :]
    xc = xc.reshape(rows, d)

    xcb = xc.astype(BF16)
    nt = d // MXU_DIM
    ra = jnp.concatenate(
        [jnp.dot(xcb[:, q * MXU_DIM:(q + 1) * MXU_DIM], wa_ref[q], preferred_element_type=F32)
         for q in range(nt)], axis=-1)
    ia = jnp.concatenate(
        [jnp.dot(xcb[:, q * MXU_DIM:(q + 1) * MXU_DIM], wi_ref[q], preferred_element_type=F32)
         for q in range(nt)], axis=-1)
    r = _sigmoid(ra + ba_ref[...])
    ig = _sigmoid(ia + bi_ref[...])
    log_a = (-RG_C) * r * _softplus(-lam_ref[...])
    a = jnp.exp(log_a)
    mult = jnp.sqrt(-jnp.tanh(log_a) * (a * a + 1.0))
    b = mult * (ig * xc)

    t_in = jnp.bitwise_and(lax.broadcasted_iota(jnp.int32, (rows, d), 0), tb - 1)
    h_prev = jnp.broadcast_to(h_scr[...], (bb, tb, d)).reshape(rows, d)
    b = b + jnp.where(t_in == 0, a * h_prev, 0.0)
    s = 1
    while s < tb:
        a_sh = jnp.where(t_in >= s, pltpu.roll(a, s, 0), 1.0)
        b_sh = jnp.where(t_in >= s, pltpu.roll(b, s, 0), 0.0)
        b = b + a * b_sh
        a = a * a_sh
        s *= 2
    ya_ref[...] = (b * jax.nn.gelu(gr, approximate=True)).reshape(bb, tb, d)
    h = b.reshape(bb, tb, d)
    h_scr[...] = h[:, tb - 1:tb, :]

    @pl.when(j == nblk - 1)
    def _():
        hlast_ref[...] = h[:, l_last - 1:l_last, :]
        cnew_ref[...] = ext_scr[:, SUBLANES + l_last - ntail:SUBLANES + l_last, :]

    tail = ext_scr[:, tb + SUBLANES - ntail:tb + SUBLANES, :]
    ext_scr[:, SUBLANES - ntail:SUBLANES, :] = tail


def _rglru(ua, cbuf, h0, cw, cb, wa4, ba, wi4, bi, lam, *, bb, tb, l_real):
    bsz, lp, n2 = ua.shape
    d = n2 // 2
    nblk = lp // tb
    l_last = l_real - (nblk - 1) * tb
    assert tb & (tb - 1) == 0 and bsz % bb == 0
    kern = functools.partial(_rglru_kernel, bb=bb, tb=tb, l_last=l_last, nblk=nblk, d=d)
    return pl.pallas_call(
        kern,
        grid=(bsz // bb, nblk),
        in_specs=[pl.BlockSpec((bb, tb, n2), lambda b, j: (b, j, 0)),
                  pl.BlockSpec((bb, CONV_W - 1, d), lambda b, j: (b, 0, 0)),
                  pl.BlockSpec((bb, 1, d), lambda b, j: (b, 0, 0)),
                  _const_spec(cw.shape), _const_spec(cb.shape),
                  _const_spec(wa4.shape), _const_spec(ba.shape),
                  _const_spec(wi4.shape), _const_spec(bi.shape), _const_spec(lam.shape)],
        out_specs=[pl.BlockSpec((bb, tb, d), lambda b, j: (b, j, 0)),
                   pl.BlockSpec((bb, 1, d), lambda b, j: (b, 0, 0)),
                   pl.BlockSpec((bb, CONV_W - 1, d), lambda b, j: (b, 0, 0))],
        out_shape=[jax.ShapeDtypeStruct((bsz, lp, d), F32),
                   jax.ShapeDtypeStruct((bsz, 1, d), F32),
                   jax.ShapeDtypeStruct((bsz, CONV_W - 1, d), F32)],
        scratch_shapes=[pltpu.VMEM((bb, tb + SUBLANES, d), F32), pltpu.VMEM((bb, 1, d), F32)],
        compiler_params=_params("arbitrary", "arbitrary"),
        name="rglru",
    )(ua, cbuf, h0, cw, cb, wa4, ba, wi4, bi, lam)


def _gla_kernel(ub_ref, s0_ref, wf_ref, bf_ref, g_ref, yb_ref, sout_ref, s_scr,
                *, bb, tb, chunk, l_real, nblk, dk, dv):
    j = pl.program_id(1)
    nh = GLA_HEADS
    qk = nh * dk
    vw = nh * dv

    @pl.when(j == 0)
    def _():
        s_scr[...] = s0_ref[...]

    off_k, off_v, off_g, off_f = qk, 2 * qk, 2 * qk + vw, 2 * qk + 2 * vw
    ri = lax.broadcasted_iota(jnp.int32, (chunk, chunk), 0)
    ci = lax.broadcasted_iota(jnp.int32, (chunk, chunk), 1)
    causal = ri >= ci
    tril = causal.astype(F32)
    mid = chunk // 2
    scale = dk ** -0.5
    tdims = (((0,), (0,)), ((), ()))

    for bi in range(bb):
        fl = ub_ref[bi, :, off_f:off_f + LANES].astype(BF16)
        logit = jnp.dot(fl, wf_ref[...], preferred_element_type=F32) + bf_ref[...]
        log_f = (jnp.minimum(logit, 0.0) - jnp.log1p(jnp.exp(-jnp.abs(logit)))) * (1.0 / GLA_TAU)
        if l_real < tb * nblk:
            rows = lax.broadcasted_iota(jnp.int32, log_f.shape, 0) + j * tb
            log_f = jnp.where(rows < l_real, log_f, 0.0)

        for c in range(tb // chunk):
            r0 = c * chunk
            for h in range(nh):
                q = ub_ref[bi, r0:r0 + chunk, h * dk:(h + 1) * dk] * scale
                k = ub_ref[bi, r0:r0 + chunk, off_k + h * dk:off_k + (h + 1) * dk]
                v = ub_ref[bi, r0:r0 + chunk, off_v + h * dv:off_v + (h + 1) * dv]
                go = ub_ref[bi, r0:r0 + chunk, off_g + h * dv:off_g + (h + 1) * dv]
                f = log_f[r0:r0 + chunk, h * dk:(h + 1) * dk]
                bcum = jnp.dot(tril, f, precision=HIGHEST, preferred_element_type=F32)
                ref = bcum[mid:mid + 1, :]
                blast = bcum[chunk - 1:chunk, :]
                s_old = s_scr[bi, h]
                vb = v.astype(BF16)
                o_inter = jnp.dot((q * jnp.exp(bcum)).astype(BF16), s_old.astype(BF16),
                                  preferred_element_type=F32)
                att = lax.dot_general((q * jnp.exp(bcum - ref)).astype(BF16),
                                      (k * jnp.exp(ref - bcum)).astype(BF16),
                                      (((1,), (1,)), ((), ())), preferred_element_type=F32)
                att = jnp.where(causal, att, 0.0)
                o = o_inter + jnp.dot(att.astype(BF16), vb, preferred_element_type=F32)
                k_dec = (k * jnp.exp(blast - bcum)).astype(BF16)
                dec = jnp.transpose(jnp.broadcast_to(jnp.exp(blast), (dk, dk)))
                dec = jnp.concatenate([dec] * (dv // dk), axis=1)
                s_scr[bi, h] = dec * s_old + lax.dot_general(k_dec, vb, tdims, preferred_element_type=F32)
                o = o * lax.rsqrt(jnp.mean(o * o, axis=-1, keepdims=True) + RMS_EPS) * g_ref[...]
                yb_ref[bi, r0:r0 + chunk, h * dv:(h + 1) * dv] = o * (go * _sigmoid(go))

    @pl.when(j == nblk - 1)
    def _():
        sout_ref[...] = s_scr[...]


def _gla(ub, s0, wf, bf, g, *, bb, tb, chunk, l_real):
    bsz, lp, nb = ub.shape
    _, nh, dk, dv = s0.shape
    nblk = lp // tb
    assert bsz % bb == 0 and dv % dk == 0
    kern = functools.partial(_gla_kernel, bb=bb, tb=tb, chunk=chunk, l_real=l_real, nblk=nblk, dk=dk, dv=dv)
    return pl.pallas_call(
        kern,
        grid=(bsz // bb, nblk),
        in_specs=[pl.BlockSpec((bb, tb, nb), lambda b, j: (b, j, 0)),
                  pl.BlockSpec((bb, nh, dk, dv), lambda b, j: (b, 0, 0, 0)),
                  _const_spec(wf.shape), _const_spec(bf.shape), _const_spec(g.shape)],
        out_specs=[pl.BlockSpec((bb, tb, nh * dv), lambda b, j: (b, j, 0)),
                   pl.BlockSpec((bb, nh, dk, dv), lambda b, j: (b, 0, 0, 0))],
        out_shape=[jax.ShapeDtypeStruct((bsz, lp, nh * dv), F32),
                   jax.ShapeDtypeStruct((bsz, nh, dk, dv), F32)],
        scratch_shapes=[pltpu.VMEM((bb, nh, dk, dv), F32)],
        compiler_params=_params("arbitrary", "arbitrary"),
        name="gla",
    )(ub, s0, wf, bf, g)


def _mix_kernel(ya_ref, yb_ref, um_ref, x_ref, wo_ref, g_ref, b_ref, o_ref, *, alpha, d):
    merged = _sigmoid(um_ref[:, :d]) * ya_ref[...] + _sigmoid(um_ref[:, d:]) * yb_ref[...]
    mix = jnp.dot(merged.astype(BF16), wo_ref[...], preferred_element_type=F32)
    o_ref[...] = _layer_norm(alpha * x_ref[...] + mix, g_ref[...], b_ref[...])


def _mix(ya, yb, um, x, wo, g, b, *, alpha, tm):
    m, d = x.shape
    row = lambda w: pl.BlockSpec((tm, w), lambda i: (i, 0))
    return pl.pallas_call(
        functools.partial(_mix_kernel, alpha=alpha, d=d),
        grid=(m // tm,),
        in_specs=[row(d), row(d), row(2 * d), row(d),
                  _const_spec(wo.shape), _const_spec(g.shape), _const_spec(b.shape)],
        out_specs=row(d),
        out_shape=jax.ShapeDtypeStruct((m, d), F32),
        compiler_params=_params("arbitrary"),
        name="mix_ln1",
    )(ya, yb, um, x, wo, g, b)


def _router_kernel(x_ref, wr_ref, bias_ref, eidx_ref, gate_ref, rank_ref, cnt_ref, run_scr,
                   *, tm, ne):
    i = pl.program_id(0)

    @pl.when(i == 0)
    def _():
        run_scr[...] = jnp.zeros_like(run_scr)

    logits = jnp.dot(x_ref[...], wr_ref[...], precision=HIGHEST, preferred_element_type=F32)
    s = _sigmoid(logits)
    sel = s + bias_ref[...]
    lane = lax.broadcasted_iota(jnp.int32, (tm, ne), 1)
    lane_f = lane.astype(F32)
    gsz = ne // N_GROUPS
    neg = -jnp.inf

    gscores = []
    for g in range(N_GROUPS):
        ing = (lane >= g * gsz) & (lane < (g + 1) * gsz)
        v = jnp.where(ing, sel, neg)
        m1 = jnp.max(v, axis=1, keepdims=True)
        i1 = jnp.min(jnp.where(v == m1, lane_f, float(ne)), axis=1, keepdims=True)
        m2 = jnp.max(jnp.where(lane_f == i1, neg, v), axis=1, keepdims=True)
        gscores.append(m1 + m2)
    cur = jnp.full((tm, ne), neg, F32)
    for g in range(N_GROUPS):
        ahead = jnp.zeros((tm, 1), F32)
        for g2 in range(N_GROUPS):
            if g2 == g:
                continue
            beats = (gscores[g2] >= gscores[g]) if g2 < g else (gscores[g2] > gscores[g])
            ahead = ahead + beats.astype(F32)
        ing = (lane >= g * gsz) & (lane < (g + 1) * gsz)
        cur = jnp.where(ing & (ahead < float(TOPK_GROUPS)), sel, cur)

    col = lax.broadcasted_iota(jnp.int32, (tm, TOP_K), 1)
    eidx = jnp.zeros((tm, TOP_K), F32)
    wsel = jnp.zeros((tm, TOP_K), F32)
    onehot_sum = jnp.zeros((tm, ne), F32)
    picks = []
    for k in range(TOP_K):
        m = jnp.max(cur, axis=1, keepdims=True)
        ik = jnp.min(jnp.where(cur == m, lane_f, float(ne)), axis=1, keepdims=True)
        hit = lane_f == ik
        wk = jnp.sum(jnp.where(hit, s, 0.0), axis=1, keepdims=True)
        cur = jnp.where(hit, neg, cur)
        onehot_sum = onehot_sum + hit.astype(F32)
        picks.append(hit)
        eidx = jnp.where(col == k, ik, eidx)
        wsel = jnp.where(col == k, wk, wsel)
    gate_ref[...] = wsel / jnp.sum(wsel, axis=1, keepdims=True) * ROUTED_SCALE
    eidx_ref[...] = eidx.astype(jnp.int32)

    ri = lax.broadcasted_iota(jnp.int32, (tm, tm), 0)
    ci = lax.broadcasted_iota(jnp.int32, (tm, tm), 1)
    before = (ci < ri).astype(BF16)
    counts = jnp.dot(before, onehot_sum.astype(BF16), preferred_element_type=F32) + run_scr[...]
    rank = jnp.zeros((tm, TOP_K), F32)
    for k in range(TOP_K):
        rk = jnp.sum(jnp.where(picks[k], counts, 0.0), axis=1, keepdims=True)
        rank = jnp.where(col == k, rk, rank)
    rank_ref[...] = rank.astype(jnp.int32)
    run_scr[...] = run_scr[...] + jnp.sum(onehot_sum, axis=0, keepdims=True)
    cnt_ref[...] = run_scr[...].astype(jnp.int32)


def _router(x, wr, bias, *, tm):
    t, d = x.shape
    ne = wr.shape[1]
    out_row = lambda: pl.BlockSpec((tm, TOP_K), lambda i: (i, 0))
    return pl.pallas_call(
        functools.partial(_router_kernel, tm=tm, ne=ne),
        grid=(t // tm,),
        in_specs=[pl.BlockSpec((tm, d), lambda i: (i, 0)), _const_spec(wr.shape), _const_spec(bias.shape)],
        out_specs=[out_row(), out_row(), out_row(), _const_spec((1, ne))],
        out_shape=[jax.ShapeDtypeStruct((t, TOP_K), jnp.int32),
                   jax.ShapeDtypeStruct((t, TOP_K), F32),
                   jax.ShapeDtypeStruct((t, TOP_K), jnp.int32),
                   jax.ShapeDtypeStruct((1, ne), jnp.int32)],
        scratch_shapes=[pltpu.VMEM((1, ne), F32)],
        compiler_params=_params("arbitrary"),
        name="router",
    )(x, wr, bias)


def _dest_kernel(eidx_ref, rank_ref, start_ref, dest_ref, *, tm, ne):
    lane = lax.broadcasted_iota(jnp.int32, (tm, ne), 1)
    col = lax.broadcasted_iota(jnp.int32, (tm, TOP_K), 1)
    starts = start_ref[...].astype(F32)
    eidx = eidx_ref[...]
    dest = jnp.zeros((tm, TOP_K), F32)
    for k in range(TOP_K):
        hit = lane == eidx[:, k:k + 1]
        sk = jnp.sum(jnp.where(hit, starts, 0.0), axis=1, keepdims=True)
        dest = jnp.where(col == k, sk, dest)
    dest_ref[...] = dest.astype(jnp.int32) + rank_ref[...]


def _dest(eidx, rank, starts, *, tm):
    t = eidx.shape[0]
    ne = starts.shape[1]
    row = lambda: pl.BlockSpec((tm, TOP_K), lambda i: (i, 0))
    return pl.pallas_call(
        functools.partial(_dest_kernel, tm=tm, ne=ne),
        grid=(t // tm,),
        in_specs=[row(), row(), _const_spec((1, ne))],
        out_specs=row(),
        out_shape=jax.ShapeDtypeStruct((t, TOP_K), jnp.int32),
        compiler_params=_params("arbitrary"),
        name="dest",
    )(eidx, rank, starts)


def _to_token_tiles(tiles_ref, rows):
    for c in range(SUBLANES):
        tiles_ref[:, c, :] = rows[:, c * LANES:(c + 1) * LANES]


def _from_token_tiles(tiles_ref):
    return jnp.concatenate([tiles_ref[:, c, :] for c in range(SUBLANES)], axis=1)


def _token_copy(src_ref, src_row, dst_ref, dst_row, sem):
    return pltpu.make_async_copy(src_ref.at[pl.ds(src_row, 1)], dst_ref.at[pl.ds(dst_row, 1)], sem)


def _dispatch_kernel(dest_ref, x_ref, xs_ref, xt, sem, *, tt):
    _to_token_tiles(xt, x_ref[...])

    def start(t, c):
        for k in range(TOP_K):
            _token_copy(xt, t, xs_ref, dest_ref[t * TOP_K + k], sem).start(priority=k % 2)
        return c

    lax.fori_loop(0, tt, start, 0)
    for _ in range(TOP_K):
        pltpu.make_async_copy(xt, xs_ref.at[pl.ds(0, tt)], sem).wait()


def _dispatch(dest_flat, x, n_slots, *, tt):
    t, d = x.shape
    assert d == SUBLANES * LANES
    return pl.pallas_call(
        functools.partial(_dispatch_kernel, tt=tt),
        grid=(t // tt,),
        in_specs=[pl.BlockSpec((tt * TOP_K,), lambda i: (i,), memory_space=pltpu.SMEM),
                  pl.BlockSpec((tt, d), lambda i: (i, 0))],
        out_specs=pl.BlockSpec(memory_space=pl.ANY),
        out_shape=jax.ShapeDtypeStruct((n_slots, SUBLANES, LANES), F32),
        scratch_shapes=[pltpu.VMEM((tt, SUBLANES, LANES), F32), pltpu.SemaphoreType.DMA(())],
        compiler_params=_params("arbitrary"),
        name="dispatch",
    )(dest_flat, x)


EXPERT_SLOTS = 4


def _expert_kernel(base_ref, nblk_ref, cnt_ref, nu_ref, xs_ref, wg_ref, wu_ref, wd_ref, ys_ref,
                   xbuf, ybuf, wgb, wub, wdb, sem_in, sem_out):
    e = pl.program_id(0)
    n_used = nu_ref[0]
    nb = nblk_ref[e]
    base = base_ref[e]
    cnt = cnt_ref[e]
    ahead = EXPERT_SLOTS - 1

    def x_copy(blk, slot):
        return pltpu.make_async_copy(xs_ref.at[pl.ds(blk * MOE_ROWS, MOE_ROWS)], xbuf.at[slot], sem_in.at[slot])

    def y_copy(blk, slot):
        return pltpu.make_async_copy(ybuf.at[slot], ys_ref.at[pl.ds(blk * MOE_ROWS, MOE_ROWS)], sem_out.at[slot])

    @pl.when(e == 0)
    def _():
        for first in range(ahead):
            @pl.when(first < n_used)
            def _():
                x_copy(first, first).start()

    @pl.when(nb > 0)
    def _():
        wgb[...] = wg_ref[0].astype(BF16)
        wub[...] = wu_ref[0].astype(BF16)
        wdb[...] = wd_ref[0].astype(BF16)

    def block(jb, c):
        blk = base + jb
        slot = jnp.bitwise_and(blk, EXPERT_SLOTS - 1)
        x_copy(blk, slot).wait()

        @pl.when(blk + ahead < n_used)
        def _():
            x_copy(blk + ahead, jnp.bitwise_and(blk + ahead, EXPERT_SLOTS - 1)).start()

        x = _from_token_tiles(xbuf.at[slot])
        rows = lax.broadcasted_iota(jnp.int32, x.shape, 0)
        xb = jnp.where(rows < cnt - jb * MOE_ROWS, x, 0.0).astype(BF16)
        g = jnp.dot(xb, wgb[...], preferred_element_type=F32)
        u = jnp.dot(xb, wub[...], preferred_element_type=F32)
        hmid = (g * _sigmoid(g)) * u
        y = jnp.dot(hmid.astype(BF16), wdb[...], preferred_element_type=F32)

        @pl.when(blk >= EXPERT_SLOTS)
        def _():
            y_copy(blk - EXPERT_SLOTS, slot).wait()

        _to_token_tiles(ybuf.at[slot], y)
        y_copy(blk, slot).start()
        return c

    lax.fori_loop(0, nb, block, 0)

    @pl.when(e == pl.num_programs(0) - 1)
    def _():
        for back in range(EXPERT_SLOTS):
            @pl.when(n_used > back)
            def _():
                last = n_used - 1 - back
                y_copy(last, jnp.bitwise_and(last, EXPERT_SLOTS - 1)).wait()


def _experts(base, nblk, cnt, n_used, xs, wg, wu, wd):
    p = xs.shape[0]
    ne, d, de = wg.shape
    assert EXPERT_SLOTS & (EXPERT_SLOTS - 1) == 0 and d == SUBLANES * LANES
    ring = pltpu.VMEM((EXPERT_SLOTS, MOE_ROWS, SUBLANES, LANES), F32)
    grid_spec = pltpu.PrefetchScalarGridSpec(
        num_scalar_prefetch=4,
        grid=(ne,),
        in_specs=[pl.BlockSpec(memory_space=pl.ANY),
                  pl.BlockSpec((1, d, de), lambda e, *_: (e, 0, 0)),
                  pl.BlockSpec((1, d, de), lambda e, *_: (e, 0, 0)),
                  pl.BlockSpec((1, de, d), lambda e, *_: (e, 0, 0))],
        out_specs=pl.BlockSpec(memory_space=pl.ANY),
        scratch_shapes=[ring, ring,
                        pltpu.VMEM((d, de), BF16), pltpu.VMEM((d, de), BF16), pltpu.VMEM((de, d), BF16),
                        pltpu.SemaphoreType.DMA((EXPERT_SLOTS,)), pltpu.SemaphoreType.DMA((EXPERT_SLOTS,))],
    )
    return pl.pallas_call(
        _expert_kernel,
        grid_spec=grid_spec,
        out_shape=jax.ShapeDtypeStruct((p, SUBLANES, LANES), F32),
        compiler_params=_params("arbitrary"),
        name="experts",
    )(base, nblk, cnt, n_used, xs, wg, wu, wd)


def _combine_kernel(dest_ref, dest_next_ref, x_ref, gate_ref, ys_ref, wsg_ref, wsu_ref, wsd_ref, g_ref, b_ref,
                    o_ref, ybuf, sem, *, tt, alpha):
    i = pl.program_id(0)
    slot = jnp.bitwise_and(i, 1)

    def gather(dref, s):
        def start(t, c):
            for k in range(TOP_K):
                _token_copy(ys_ref, dref[t * TOP_K + k], ybuf.at[s], k * tt + t, sem.at[s]).start(priority=k % 2)
            return c
        lax.fori_loop(0, tt, start, 0)

    @pl.when(i == 0)
    def _():
        gather(dest_ref, 0)

    @pl.when(i + 1 < pl.num_programs(0))
    def _():
        gather(dest_next_ref, 1 - slot)

    x = x_ref[...]
    xb = x.astype(BF16)
    sg = jnp.dot(xb, wsg_ref[...], preferred_element_type=F32)
    su = jnp.dot(xb, wsu_ref[...], preferred_element_type=F32)
    acc = jnp.dot(((sg * _sigmoid(sg)) * su).astype(BF16), wsd_ref[...], preferred_element_type=F32)
    pltpu.make_async_copy(ys_ref.at[pl.ds(0, tt * TOP_K)], ybuf.at[slot], sem.at[slot]).wait()
    gate = gate_ref[...]
    routed = []
    for c in range(SUBLANES):
        part = gate[:, 0:1] * ybuf[slot, 0:tt, c, :]
        for k in range(1, TOP_K):
            part = part + gate[:, k:k + 1] * ybuf[slot, k * tt:(k + 1) * tt, c, :]
        routed.append(part)
    acc = acc + jnp.concatenate(routed, axis=1)
    o_ref[...] = _layer_norm(alpha * x + acc, g_ref[...], b_ref[...])


def _combine(dest_flat, x, gate, ys, wsg, wsu, wsd, g, b, *, tt, alpha):
    t, d = x.shape
    nsteps = t // tt
    return pl.pallas_call(
        functools.partial(_combine_kernel, tt=tt, alpha=alpha),
        grid=(nsteps,),
        in_specs=[pl.BlockSpec((tt * TOP_K,), lambda i: (i,), memory_space=pltpu.SMEM),
                  pl.BlockSpec((tt * TOP_K,), lambda i: (jnp.minimum(i + 1, nsteps - 1),),
                               memory_space=pltpu.SMEM),
                  pl.BlockSpec((tt, d), lambda i: (i, 0)),
                  pl.BlockSpec((tt, TOP_K), lambda i: (i, 0)),
                  pl.BlockSpec(memory_space=pl.ANY),
                  _const_spec(wsg.shape), _const_spec(wsu.shape), _const_spec(wsd.shape),
                  _const_spec(g.shape), _const_spec(b.shape)],
        out_specs=pl.BlockSpec((tt, d), lambda i: (i, 0)),
        out_shape=jax.ShapeDtypeStruct((t, d), F32),
        scratch_shapes=[pltpu.VMEM((2, tt * TOP_K, SUBLANES, LANES), F32), pltpu.SemaphoreType.DMA((2,))],
        compiler_params=_params("arbitrary"),
        name="combine_ln2",
    )(dest_flat, dest_flat, x, gate, ys, wsg, wsu, wsd, g, b)


def _block_diag_tiles(w):
    n, s, _ = w.shape
    per = MXU_DIM // s
    w = w.reshape(n // per, per, s, s)
    eye = jnp.eye(per, dtype=w.dtype)
    return jnp.einsum('tpij,pq->tpiqj', w, eye).reshape(n // per, MXU_DIM, MXU_DIM)


def _row(v):
    return v.reshape(1, -1)


def _moe(x1, w_router, router_bias, w_gate, w_up, w_down, ws_gate, ws_up, ws_down, ln_g, ln_b, alpha):
    t, d = x1.shape
    ne = w_router.shape[1]
    eidx, gate, rank, counts = _router(x1, w_router, _row(router_bias), tm=256)
    counts = counts[0]
    nblk = (counts + MOE_ROWS - 1) // MOE_ROWS
    blk_end = jnp.cumsum(nblk)
    blk_base = (blk_end - nblk).astype(jnp.int32)
    n_used = blk_end[-1:].astype(jnp.int32)
    n_slots = ((t * TOP_K) // MOE_ROWS + ne) * MOE_ROWS

    dest = _dest(eidx, rank, _row(blk_base * MOE_ROWS), tm=256)
    dest_flat = dest.reshape(-1)
    xs = _dispatch(dest_flat, x1, n_slots, tt=256)
    ys = _experts(blk_base, nblk.astype(jnp.int32), counts, n_used, xs, w_gate, w_up, w_down)
    return _combine(dest_flat, x1, gate, ys, ws_gate.astype(BF16), ws_up.astype(BF16),
                    ws_down.astype(BF16), _row(ln_g), _row(ln_b), tt=128, alpha=alpha)


def _pad_time(a, lp):
    return jnp.pad(a, ((0, 0), (0, lp - a.shape[1]), (0, 0)))


def kernel(x_prompt, x_sample, state_rglru_h, state_conv, state_gla, meta_tokens, w_in, conv_w, conv_b,
           rg_wa, rg_ba, rg_wi, rg_bi, rg_lambda, gla_wf2, gla_bf, gla_norm_g, w_out, ln1_g, ln1_b,
           w_router, router_bias, w_gate, w_up, w_down, ws_gate, ws_up, ws_down, ln2_g, ln2_b):
    bp, lp, d = x_prompt.shape
    bs, ls, _ = x_sample.shape
    depth = w_in.shape[0]
    nh, dk, dv = state_gla.shape[2:]
    qk, vw = nh * dk, nh * dv
    lowrank = gla_wf2.shape[1]
    alpha = (2.0 * depth) ** 0.25
    ls_pad = SUBLANES
    assert depth == 1 and ls <= ls_pad and lp % 256 == 0 and N_META % SUBLANES == 0

    xp = x_prompt.reshape(bp * lp, d)
    xs = x_sample.reshape(bs * ls, d)
    xm = meta_tokens.astype(F32)

    hp_l, cp_l, sp_l, hs_l, cs_l, ss_l = [], [], [], [], [], []
    for l in range(depth):
        o1, o2 = 2 * d, 2 * d + 2 * qk + 2 * vw
        wa = w_in[l][:, :o1].astype(BF16)
        wb = jnp.pad(w_in[l][:, o1:o2 + lowrank], ((0, 0), (0, LANES - lowrank))).astype(BF16)
        wm = w_in[l][:, o2 + lowrank:].astype(BF16)
        wf = jnp.pad(gla_wf2[l], ((0, LANES - lowrank), (0, 0))).astype(BF16)
        wa4 = _block_diag_tiles(rg_wa[l]).astype(BF16)
        wi4 = _block_diag_tiles(rg_wi[l]).astype(BF16)
        mixer_w = (conv_w[l], _row(conv_b[l]), wa4, _row(rg_ba[l]), wi4, _row(rg_bi[l]), _row(rg_lambda[l]))
        gla_w = (wf, _row(gla_bf[l]), _row(gla_norm_g[l]))

        ua, ub, _ = _inproj(xm, wa, wb, wm, tm=N_META)
        _, h_m, c_m = _rglru(ua[None], jnp.zeros((1, CONV_W - 1, d), F32), jnp.zeros((1, 1, d), F32),
                             *mixer_w, bb=1, tb=N_META, l_real=N_META)
        _, s_m = _gla(ub[None], jnp.zeros((1, nh, dk, dv), F32), *gla_w,
                      bb=1, tb=N_META, chunk=N_META, l_real=N_META)

        ua, ub, um = _inproj(xp, wa, wb, wm, tm=256)
        ya, h_p, c_p = _rglru(ua.reshape(bp, lp, -1), jnp.broadcast_to(c_m, (bp,) + c_m.shape[1:]),
                              jnp.broadcast_to(h_m, (bp,) + h_m.shape[1:]), *mixer_w,
                              bb=1, tb=256, l_real=lp)
        yb, s_p = _gla(ub.reshape(bp, lp, -1), jnp.broadcast_to(s_m, (bp,) + s_m.shape[1:]), *gla_w,
                       bb=1, tb=256, chunk=GLA_CHUNK, l_real=lp)
        x1p = _mix(ya.reshape(bp * lp, d), yb.reshape(bp * lp, d), um, xp, w_out[l].astype(BF16),
                   _row(ln1_g[l]), _row(ln1_b[l]), alpha=alpha, tm=512)

        xs_pad = _pad_time(xs.reshape(bs, ls, d), ls_pad).reshape(bs * ls_pad, d)
        ua, ub, um = _inproj(xs_pad, wa, wb, wm, tm=256)
        ya, h_s, c_s = _rglru(ua.reshape(bs, ls_pad, -1), state_conv[l].astype(F32),
                              state_rglru_h[l].astype(F32)[:, None, :], *mixer_w,
                              bb=32, tb=ls_pad, l_real=ls)
        yb, s_s = _gla(ub.reshape(bs, ls_pad, -1), state_gla[l].astype(F32), *gla_w,
                       bb=8, tb=ls_pad, chunk=ls_pad, l_real=ls)
        unpad = lambda a: a.reshape(bs, ls_pad, -1)[:, :ls].reshape(bs * ls, -1)
        x1s = _mix(unpad(ya), unpad(yb), unpad(um), xs, w_out[l].astype(BF16),
                   _row(ln1_g[l]), _row(ln1_b[l]), alpha=alpha, tm=512)

        x1 = jnp.concatenate([x1p, x1s], axis=0)
        x2 = _moe(x1, w_router[l], router_bias[l], w_gate[l], w_up[l], w_down[l],
                  ws_gate[l], ws_up[l], ws_down[l], ln2_g[l], ln2_b[l], alpha)
        xp, xs = x2[:bp * lp], x2[bp * lp:]

        hp_l.append(h_p[:, 0]); cp_l.append(c_p); sp_l.append(s_p)
        hs_l.append(h_s[:, 0]); cs_l.append(c_s); ss_l.append(s_s)

    y_prompt = xp.reshape(bp, lp, d)
    y_sample = xs.reshape(bs, ls, d)
    return (y_prompt, y_sample,
            jnp.stack(hp_l).astype(state_rglru_h.dtype), jnp.stack(cp_l).astype(state_conv.dtype),
            jnp.stack(sp_l).astype(state_gla.dtype),
            jnp.stack(hs_l).astype(state_rglru_h.dtype), jnp.stack(cs_l).astype(state_conv.dtype),
            jnp.stack(ss_l).astype(state_gla.dtype))
```

```python
import functools

import jax
import jax.numpy as jnp
from jax import lax
from jax.experimental import pallas as pl
from jax.experimental.pallas import tpu as pltpu

F32 = jnp.float32
BF16 = jnp.bfloat16
HIGHEST = lax.Precision.HIGHEST

N_META = 16
CONV_W = 4
RG_C = 8.0
RNN_BLOCKS = 16
GLA_HEADS = 4
GLA_TAU = 16.0
GLA_CHUNK = 64
N_GROUPS = 8
TOPK_GROUPS = 4
TOP_K = 8
ROUTED_SCALE = 2.5
LN_EPS = 1e-5
RMS_EPS = 1e-6

LANES = 128
SUBLANES = 8
MXU_DIM = 256
VMEM_LIMIT_BYTES = 56 * 1024 * 1024

MOE_ROWS = 128


def _params(*sem):
    return pltpu.CompilerParams(dimension_semantics=sem, vmem_limit_bytes=VMEM_LIMIT_BYTES)


def _sigmoid(x):
    return jax.nn.sigmoid(x)


def _softplus(z):
    return jnp.maximum(z, 0.0) + jnp.log1p(jnp.exp(-jnp.abs(z)))


def _layer_norm(z, g, b):
    mu = jnp.mean(z, axis=-1, keepdims=True)
    zc = z - mu
    var = jnp.mean(zc * zc, axis=-1, keepdims=True)
    return zc * lax.rsqrt(var + LN_EPS) * g + b


def _const_spec(shape):
    nd = len(shape)
    return pl.BlockSpec(shape, lambda *_: (0,) * nd)


def _inproj_kernel(x_ref, wa_ref, wb_ref, wm_ref, ua_ref, ub_ref, um_ref):
    xb = x_ref[...].astype(BF16)
    ua_ref[...] = jnp.dot(xb, wa_ref[...], preferred_element_type=F32)
    ub_ref[...] = jnp.dot(xb, wb_ref[...], preferred_element_type=F32)
    um_ref[...] = jnp.dot(xb, wm_ref[...], preferred_element_type=F32)


def _inproj(x2d, wa, wb, wm, tm):
    m, d = x2d.shape
    na, nb, nm = wa.shape[1], wb.shape[1], wm.shape[1]
    return pl.pallas_call(
        _inproj_kernel,
        grid=(m // tm,),
        in_specs=[pl.BlockSpec((tm, d), lambda i: (i, 0)),
                  _const_spec(wa.shape), _const_spec(wb.shape), _const_spec(wm.shape)],
        out_specs=[pl.BlockSpec((tm, na), lambda i: (i, 0)),
                   pl.BlockSpec((tm, nb), lambda i: (i, 0)),
                   pl.BlockSpec((tm, nm), lambda i: (i, 0))],
        out_shape=[jax.ShapeDtypeStruct((m, na), F32),
                   jax.ShapeDtypeStruct((m, nb), F32),
                   jax.ShapeDtypeStruct((m, nm), F32)],
        compiler_params=_params("arbitrary"),
        name="inproj",
    )(x2d, wa, wb, wm)


def _rglru_kernel(ua_ref, cbuf_ref, h0_ref, cw_ref, cb_ref, wa_ref, ba_ref, wi_ref, bi_ref, lam_ref,
                  ya_ref, hlast_ref, cnew_ref, ext_scr, h_scr, *, bb, tb, l_last, nblk, d):
    j = pl.program_id(1)
    ntail = CONV_W - 1
    rows = bb * tb

    @pl.when(j == 0)
    def _():
        h_scr[...] = h0_ref[...]
        ext_scr[:, SUBLANES - ntail:SUBLANES, :] = cbuf_ref[...]

    xr = ua_ref[:, :, :d]
    gr = ua_ref[:, :, d:].reshape(rows, d)
    ext_scr[:, SUBLANES:, :] = xr
    xc = cb_ref[...] + cw_ref[ntail:ntail + 1, :] * xr
    for s in range(1, CONV_W):
        xc = xc + cw_ref[ntail - s:ntail - s + 1, :] * ext_scr[:, SUBLANES - s:SUBLANES - s + tb, :]
    xc = xc.reshape(rows, d)

    xcb = xc.astype(BF16)
    nt = d // MXU_DIM
    ra = jnp.concatenate(
        [jnp.dot(xcb[:, q * MXU_DIM:(q + 1) * MXU_DIM], wa_ref[q], preferred_element_type=F32)
         for q in range(nt)], axis=-1)
    ia = jnp.concatenate(
        [jnp.dot(xcb[:, q * MXU_DIM:(q + 1) * MXU_DIM], wi_ref[q], preferred_element_type=F32)
         for q in range(nt)], axis=-1)
    r = _sigmoid(ra + ba_ref[...])
    ig = _sigmoid(ia + bi_ref[...])
    log_a = (-RG_C) * r * _softplus(-lam_ref[...])
    a = jnp.exp(log_a)
    mult = jnp.sqrt(-jnp.tanh(log_a) * (a * a + 1.0))
    b = mult * (ig * xc)

    t_in = jnp.bitwise_and(lax.broadcasted_iota(jnp.int32, (rows, d), 0), tb - 1)
    h_prev = jnp.broadcast_to(h_scr[...], (bb, tb, d)).reshape(rows, d)
    b = b + jnp.where(t_in == 0, a * h_prev, 0.0)
    s = 1
    while s < tb:
        a_sh = jnp.where(t_in >= s, pltpu.roll(a, s, 0), 1.0)
        b_sh = jnp.where(t_in >= s, pltpu.roll(b, s, 0), 0.0)
        b = b + a * b_sh
        a = a * a_sh
        s *= 2
    ya_ref[...] = (b * jax.nn.gelu(gr, approximate=True)).reshape(bb, tb, d)
    h = b.reshape(bb, tb, d)
    h_scr[...] = h[:, tb - 1:tb, :]

    @pl.when(j == nblk - 1)
    def _():
        hlast_ref[...] = h[:, l_last - 1:l_last, :]
        cnew_ref[...] = ext_scr[:, SUBLANES + l_last - ntail:SUBLANES + l_last, :]

    tail = ext_scr[:, tb + SUBLANES - ntail:tb + SUBLANES, :]
    ext_scr[:, SUBLANES - ntail:SUBLANES, :] = tail


def _rglru(ua, cbuf, h0, cw, cb, wa4, ba, wi4, bi, lam, *, bb, tb, l_real):
    bsz, lp, n2 = ua.shape
    d = n2 // 2
    nblk = lp // tb
    l_last = l_real - (nblk - 1) * tb
    assert tb & (tb - 1) == 0 and bsz % bb == 0
    kern = functools.partial(_rglru_kernel, bb=bb, tb=tb, l_last=l_last, nblk=nblk, d=d)
    return pl.pallas_call(
        kern,
        grid=(bsz // bb, nblk),
        in_specs=[pl.BlockSpec((bb, tb, n2), lambda b, j: (b, j, 0)),
                  pl.BlockSpec((bb, CONV_W - 1, d), lambda b, j: (b, 0, 0)),
                  pl.BlockSpec((bb, 1, d), lambda b, j: (b, 0, 0)),
                  _const_spec(cw.shape), _const_spec(cb.shape),
                  _const_spec(wa4.shape), _const_spec(ba.shape),
                  _const_spec(wi4.shape), _const_spec(bi.shape), _const_spec(lam.shape)],
        out_specs=[pl.BlockSpec((bb, tb, d), lambda b, j: (b, j, 0)),
                   pl.BlockSpec((bb, 1, d), lambda b, j: (b, 0, 0)),
                   pl.BlockSpec((bb, CONV_W - 1, d), lambda b, j: (b, 0, 0))],
        out_shape=[jax.ShapeDtypeStruct((bsz, lp, d), F32),
                   jax.ShapeDtypeStruct((bsz, 1, d), F32),
                   jax.ShapeDtypeStruct((bsz, CONV_W - 1, d), F32)],
        scratch_shapes=[pltpu.VMEM((bb, tb + SUBLANES, d), F32), pltpu.VMEM((bb, 1, d), F32)],
        compiler_params=_params("arbitrary", "arbitrary"),
        name="rglru",
    )(ua, cbuf, h0, cw, cb, wa4, ba, wi4, bi, lam)


def _gla_kernel(ub_ref, s0_ref, wf_ref, bf_ref, g_ref, yb_ref, sout_ref, s_scr,
                *, bb, tb, chunk, l_real, nblk, dk, dv):
    j = pl.program_id(1)
    nh = GLA_HEADS
    qk = nh * dk
    vw = nh * dv

    @pl.when(j == 0)
    def _():
        s_scr[...] = s0_ref[...]

    off_k, off_v, off_g, off_f = qk, 2 * qk, 2 * qk + vw, 2 * qk + 2 * vw
    ri = lax.broadcasted_iota(jnp.int32, (chunk, chunk), 0)
    ci = lax.broadcasted_iota(jnp.int32, (chunk, chunk), 1)
    causal = ri >= ci
    tril = causal.astype(F32)
    mid = chunk // 2
    scale = dk ** -0.5
    tdims = (((0,), (0,)), ((), ()))

    for bi in range(bb):
        fl = ub_ref[bi, :, off_f:off_f + LANES].astype(BF16)
        logit = jnp.dot(fl, wf_ref[...], preferred_element_type=F32) + bf_ref[...]
        log_f = (jnp.minimum(logit, 0.0) - jnp.log1p(jnp.exp(-jnp.abs(logit)))) * (1.0 / GLA_TAU)
        if l_real < tb * nblk:
            rows = lax.broadcasted_iota(jnp.int32, log_f.shape, 0) + j * tb
            log_f = jnp.where(rows < l_real, log_f, 0.0)

        for c in range(tb // chunk):
            r0 = c * chunk
            for h in range(nh):
                q = ub_ref[bi, r0:r0 + chunk, h * dk:(h + 1) * dk] * scale
                k = ub_ref[bi, r0:r0 + chunk, off_k + h * dk:off_k + (h + 1) * dk]
                v = ub_ref[bi, r0:r0 + chunk, off_v + h * dv:off_v + (h + 1) * dv]
                go = ub_ref[bi, r0:r0 + chunk, off_g + h * dv:off_g + (h + 1) * dv]
                f = log_f[r0:r0 + chunk, h * dk:(h + 1) * dk]
                bcum = jnp.dot(tril, f, precision=HIGHEST, preferred_element_type=F32)
                ref = bcum[mid:mid + 1, :]
                blast = bcum[chunk - 1:chunk, :]
                s_old = s_scr[bi, h]
                vb = v.astype(BF16)
                o_inter = jnp.dot((q * jnp.exp(bcum)).astype(BF16), s_old.astype(BF16),
                                  preferred_element_type=F32)
                att = lax.dot_general((q * jnp.exp(bcum - ref)).astype(BF16),
                                      (k * jnp.exp(ref - bcum)).astype(BF16),
                                      (((1,), (1,)), ((), ())), preferred_element_type=F32)
                att = jnp.where(causal, att, 0.0)
                o = o_inter + jnp.dot(att.astype(BF16), vb, preferred_element_type=F32)
                k_dec = (k * jnp.exp(blast - bcum)).astype(BF16)
                dec = jnp.transpose(jnp.broadcast_to(jnp.exp(blast), (dk, dk)))
                dec = jnp.concatenate([dec] * (dv // dk), axis=1)
                s_scr[bi, h] = dec * s_old + lax.dot_general(k_dec, vb, tdims, preferred_element_type=F32)
                o = o * lax.rsqrt(jnp.mean(o * o, axis=-1, keepdims=True) + RMS_EPS) * g_ref[...]
                yb_ref[bi, r0:r0 + chunk, h * dv:(h + 1) * dv] = o * (go * _sigmoid(go))

    @pl.when(j == nblk - 1)
    def _():
        sout_ref[...] = s_scr[...]


def _gla(ub, s0, wf, bf, g, *, bb, tb, chunk, l_real):
    bsz, lp, nb = ub.shape
    _, nh, dk, dv = s0.shape
    nblk = lp // tb
    assert bsz % bb == 0 and dv % dk == 0
    kern = functools.partial(_gla_kernel, bb=bb, tb=tb, chunk=chunk, l_real=l_real, nblk=nblk, dk=dk, dv=dv)
    return pl.pallas_call(
        kern,
        grid=(bsz // bb, nblk),
        in_specs=[pl.BlockSpec((bb, tb, nb), lambda b, j: (b, j, 0)),
                  pl.BlockSpec((bb, nh, dk, dv), lambda b, j: (b, 0, 0, 0)),
                  _const_spec(wf.shape), _const_spec(bf.shape), _const_spec(g.shape)],
        out_specs=[pl.BlockSpec((bb, tb, nh * dv), lambda b, j: (b, j, 0)),
                   pl.BlockSpec((bb, nh, dk, dv), lambda b, j: (b, 0, 0, 0))],
        out_shape=[jax.ShapeDtypeStruct((bsz, lp, nh * dv), F32),
                   jax.ShapeDtypeStruct((bsz, nh, dk, dv), F32)],
        scratch_shapes=[pltpu.VMEM((bb, nh, dk, dv), F32)],
        compiler_params=_params("arbitrary", "arbitrary"),
        name="gla",
    )(ub, s0, wf, bf, g)


def _mix_kernel(ya_ref, yb_ref, um_ref, x_ref, wo_ref, g_ref, b_ref, o_ref, *, alpha, d):
    merged = _sigmoid(um_ref[:, :d]) * ya_ref[...] + _sigmoid(um_ref[:, d:]) * yb_ref[...]
    mix = jnp.dot(merged.astype(BF16), wo_ref[...], preferred_element_type=F32)
    o_ref[...] = _layer_norm(alpha * x_ref[...] + mix, g_ref[...], b_ref[...])


def _mix(ya, yb, um, x, wo, g, b, *, alpha, tm):
    m, d = x.shape
    row = lambda w: pl.BlockSpec((tm, w), lambda i: (i, 0))
    return pl.pallas_call(
        functools.partial(_mix_kernel, alpha=alpha, d=d),
        grid=(m // tm,),
        in_specs=[row(d), row(d), row(2 * d), row(d),
                  _const_spec(wo.shape), _const_spec(g.shape), _const_spec(b.shape)],
        out_specs=row(d),
        out_shape=jax.ShapeDtypeStruct((m, d), F32),
        compiler_params=_params("arbitrary"),
        name="mix_ln1",
    )(ya, yb, um, x, wo, g, b)


def _router_kernel(x_ref, wr_ref, bias_ref, eidx_ref, gate_ref, rank_ref, cnt_ref, run_scr,
                   *, tm, ne):
    i = pl.program_id(0)

    @pl.when(i == 0)
    def _():
        run_scr[...] = jnp.zeros_like(run_scr)

    logits = jnp.dot(x_ref[...], wr_ref[...], precision=HIGHEST, preferred_element_type=F32)
    s = _sigmoid(logits)
    sel = s + bias_ref[...]
    lane = lax.broadcasted_iota(jnp.int32, (tm, ne), 1)
    lane_f = lane.astype(F32)
    gsz = ne // N_GROUPS
    neg = -jnp.inf

    gscores = []
    for g in range(N_GROUPS):
        ing = (lane >= g * gsz) & (lane < (g + 1) * gsz)
        v = jnp.where(ing, sel, neg)
        m1 = jnp.max(v, axis=1, keepdims=True)
        i1 = jnp.min(jnp.where(v == m1, lane_f, float(ne)), axis=1, keepdims=True)
        m2 = jnp.max(jnp.where(lane_f == i1, neg, v), axis=1, keepdims=True)
        gscores.append(m1 + m2)
    cur = jnp.full((tm, ne), neg, F32)
    for g in range(N_GROUPS):
        ahead = jnp.zeros((tm, 1), F32)
        for g2 in range(N_GROUPS):
            if g2 == g:
                continue
            beats = (gscores[g2] >= gscores[g]) if g2 < g else (gscores[g2] > gscores[g])
            ahead = ahead + beats.astype(F32)
        ing = (lane >= g * gsz) & (lane < (g + 1) * gsz)
        cur = jnp.where(ing & (ahead < float(TOPK_GROUPS)), sel, cur)

    col = lax.broadcasted_iota(jnp.int32, (tm, TOP_K), 1)
    eidx = jnp.zeros((tm, TOP_K), F32)
    wsel = jnp.zeros((tm, TOP_K), F32)
    onehot_sum = jnp.zeros((tm, ne), F32)
    picks = []
    for k in range(TOP_K):
        m = jnp.max(cur, axis=1, keepdims=True)
        ik = jnp.min(jnp.where(cur == m, lane_f, float(ne)), axis=1, keepdims=True)
        hit = lane_f == ik
        wk = jnp.sum(jnp.where(hit, s, 0.0), axis=1, keepdims=True)
        cur = jnp.where(hit, neg, cur)
        onehot_sum = onehot_sum + hit.astype(F32)
        picks.append(hit)
        eidx = jnp.where(col == k, ik, eidx)
        wsel = jnp.where(col == k, wk, wsel)
    gate_ref[...] = wsel / jnp.sum(wsel, axis=1, keepdims=True) * ROUTED_SCALE
    eidx_ref[...] = eidx.astype(jnp.int32)

    ri = lax.broadcasted_iota(jnp.int32, (tm, tm), 0)
    ci = lax.broadcasted_iota(jnp.int32, (tm, tm), 1)
    before = (ci < ri).astype(BF16)
    counts = jnp.dot(before, onehot_sum.astype(BF16), preferred_element_type=F32) + run_scr[...]
    rank = jnp.zeros((tm, TOP_K), F32)
    for k in range(TOP_K):
        rk = jnp.sum(jnp.where(picks[k], counts, 0.0), axis=1, keepdims=True)
        rank = jnp.where(col == k, rk, rank)
    rank_ref[...] = rank.astype(jnp.int32)
    run_scr[...] = run_scr[...] + jnp.sum(onehot_sum, axis=0, keepdims=True)
    cnt_ref[...] = run_scr[...].astype(jnp.int32)


def _router(x, wr, bias, *, tm):
    t, d = x.shape
    ne = wr.shape[1]
    out_row = lambda: pl.BlockSpec((tm, TOP_K), lambda i: (i, 0))
    return pl.pallas_call(
        functools.partial(_router_kernel, tm=tm, ne=ne),
        grid=(t // tm,),
        in_specs=[pl.BlockSpec((tm, d), lambda i: (i, 0)), _const_spec(wr.shape), _const_spec(bias.shape)],
        out_specs=[out_row(), out_row(), out_row(), _const_spec((1, ne))],
        out_shape=[jax.ShapeDtypeStruct((t, TOP_K), jnp.int32),
                   jax.ShapeDtypeStruct((t, TOP_K), F32),
                   jax.ShapeDtypeStruct((t, TOP_K), jnp.int32),
                   jax.ShapeDtypeStruct((1, ne), jnp.int32)],
        scratch_shapes=[pltpu.VMEM((1, ne), F32)],
        compiler_params=_params("arbitrary"),
        name="router",
    )(x, wr, bias)


def _dest_kernel(eidx_ref, rank_ref, start_ref, dest_ref, *, tm, ne):
    lane = lax.broadcasted_iota(jnp.int32, (tm, ne), 1)
    col = lax.broadcasted_iota(jnp.int32, (tm, TOP_K), 1)
    starts = start_ref[...].astype(F32)
    eidx = eidx_ref[...]
    dest = jnp.zeros((tm, TOP_K), F32)
    for k in range(TOP_K):
        hit = lane == eidx[:, k:k + 1]
        sk = jnp.sum(jnp.where(hit, starts, 0.0), axis=1, keepdims=True)
        dest = jnp.where(col == k, sk, dest)
    dest_ref[...] = dest.astype(jnp.int32) + rank_ref[...]


def _dest(eidx, rank, starts, *, tm):
    t = eidx.shape[0]
    ne = starts.shape[1]
    row = lambda: pl.BlockSpec((tm, TOP_K), lambda i: (i, 0))
    return pl.pallas_call(
        functools.partial(_dest_kernel, tm=tm, ne=ne),
        grid=(t // tm,),
        in_specs=[row(), row(), _const_spec((1, ne))],
        out_specs=row(),
        out_shape=jax.ShapeDtypeStruct((t, TOP_K), jnp.int32),
        compiler_params=_params("arbitrary"),
        name="dest",
    )(eidx, rank, starts)


def _to_token_tiles(rows):
    chunks = jnp.stack([rows[:, c * LANES:(c + 1) * LANES] for c in range(SUBLANES)], axis=0)
    return jnp.swapaxes(chunks, 0, 1)


def _chunks_of_token_tiles(tiles):
    chunks = jnp.swapaxes(tiles, 0, 1)
    return [chunks[c] for c in range(SUBLANES)]


def _token_copy(src_ref, src_row, dst_ref, dst_row, sem):
    return pltpu.make_async_copy(src_ref.at[pl.ds(src_row, 1)], dst_ref.at[pl.ds(dst_row, 1)], sem)


def _dispatch_kernel(dest_ref, x_ref, xs_ref, xt, sem, *, tt):
    xt[...] = _to_token_tiles(x_ref[...])

    def start(t, c):
        for k in range(TOP_K):
            _token_copy(xt, t, xs_ref, dest_ref[t * TOP_K + k], sem).start(priority=k % 2)
        return c

    lax.fori_loop(0, tt, start, 0)
    for _ in range(TOP_K):
        pltpu.make_async_copy(xt, xs_ref.at[pl.ds(0, tt)], sem).wait()


def _dispatch(dest_flat, x, n_slots, *, tt):
    t, d = x.shape
    assert d == SUBLANES * LANES
    return pl.pallas_call(
        functools.partial(_dispatch_kernel, tt=tt),
        grid=(t // tt,),
        in_specs=[pl.BlockSpec((tt * TOP_K,), lambda i: (i,), memory_space=pltpu.SMEM),
                  pl.BlockSpec((tt, d), lambda i: (i, 0))],
        out_specs=pl.BlockSpec(memory_space=pl.ANY),
        out_shape=jax.ShapeDtypeStruct((n_slots, SUBLANES, LANES), F32),
        scratch_shapes=[pltpu.VMEM((tt, SUBLANES, LANES), F32), pltpu.SemaphoreType.DMA(())],
        compiler_params=_params("arbitrary"),
        name="dispatch",
    )(dest_flat, x)


EXPERT_SLOTS = 4


def _expert_kernel(base_ref, nblk_ref, cnt_ref, nu_ref, xs_ref, wg_ref, wu_ref, wd_ref, ys_ref,
                   xbuf, ybuf, wgb, wub, wdb, sem_in, sem_out):
    e = pl.program_id(0)
    n_used = nu_ref[0]
    nb = nblk_ref[e]
    base = base_ref[e]
    cnt = cnt_ref[e]
    ahead = EXPERT_SLOTS - 1

    def x_copy(blk, slot):
        return pltpu.make_async_copy(xs_ref.at[pl.ds(blk * MOE_ROWS, MOE_ROWS)], xbuf.at[slot], sem_in.at[slot])

    def y_copy(blk, slot):
        return pltpu.make_async_copy(ybuf.at[slot], ys_ref.at[pl.ds(blk * MOE_ROWS, MOE_ROWS)], sem_out.at[slot])

    @pl.when(e == 0)
    def _():
        for first in range(ahead):
            @pl.when(first < n_used)
            def _():
                x_copy(first, first).start()

    @pl.when(nb > 0)
    def _():
        wgb[...] = wg_ref[0].astype(BF16)
        wub[...] = wu_ref[0].astype(BF16)
        wdb[...] = wd_ref[0].astype(BF16)

    def block(jb, c):
        blk = base + jb
        slot = jnp.bitwise_and(blk, EXPERT_SLOTS - 1)
        x_copy(blk, slot).wait()

        @pl.when(blk + ahead < n_used)
        def _():
            x_copy(blk + ahead, jnp.bitwise_and(blk + ahead, EXPERT_SLOTS - 1)).start()

        x = jnp.concatenate(_chunks_of_token_tiles(xbuf[slot]), axis=1)
        rows = lax.broadcasted_iota(jnp.int32, x.shape, 0)
        xb = jnp.where(rows < cnt - jb * MOE_ROWS, x, 0.0).astype(BF16)
        g = jnp.dot(xb, wgb[...], preferred_element_type=F32)
        u = jnp.dot(xb, wub[...], preferred_element_type=F32)
        hmid = (g * _sigmoid(g)) * u
        y = jnp.dot(hmid.astype(BF16), wdb[...], preferred_element_type=F32)

        @pl.when(blk >= EXPERT_SLOTS)
        def _():
            y_copy(blk - EXPERT_SLOTS, slot).wait()

        ybuf[slot] = _to_token_tiles(y)
        y_copy(blk, slot).start()
        return c

    lax.fori_loop(0, nb, block, 0)

    @pl.when(e == pl.num_programs(0) - 1)
    def _():
        for back in range(EXPERT_SLOTS):
            @pl.when(n_used > back)
            def _():
                last = n_used - 1 - back
                y_copy(last, jnp.bitwise_and(last, EXPERT_SLOTS - 1)).wait()


def _experts(base, nblk, cnt, n_used, xs, wg, wu, wd):
    p = xs.shape[0]
    ne, d, de = wg.shape
    assert EXPERT_SLOTS & (EXPERT_SLOTS - 1) == 0 and d == SUBLANES * LANES
    ring = pltpu.VMEM((EXPERT_SLOTS, MOE_ROWS, SUBLANES, LANES), F32)
    grid_spec = pltpu.PrefetchScalarGridSpec(
        num_scalar_prefetch=4,
        grid=(ne,),
        in_specs=[pl.BlockSpec(memory_space=pl.ANY),
                  pl.BlockSpec((1, d, de), lambda e, *_: (e, 0, 0)),
                  pl.BlockSpec((1, d, de), lambda e, *_: (e, 0, 0)),
                  pl.BlockSpec((1, de, d), lambda e, *_: (e, 0, 0))],
        out_specs=pl.BlockSpec(memory_space=pl.ANY),
        scratch_shapes=[ring, ring,
                        pltpu.VMEM((d, de), BF16), pltpu.VMEM((d, de), BF16), pltpu.VMEM((de, d), BF16),
                        pltpu.SemaphoreType.DMA((EXPERT_SLOTS,)), pltpu.SemaphoreType.DMA((EXPERT_SLOTS,))],
    )
    return pl.pallas_call(
        _expert_kernel,
        grid_spec=grid_spec,
        out_shape=jax.ShapeDtypeStruct((p, SUBLANES, LANES), F32),
        compiler_params=_params("arbitrary"),
        name="experts",
    )(base, nblk, cnt, n_used, xs, wg, wu, wd)


def _combine_kernel(dest_ref, dest_next_ref, x_ref, gate_ref, ys_ref, wsg_ref, wsu_ref, wsd_ref, g_ref, b_ref,
                    o_ref, ybuf, sem, *, tt, alpha):
    i = pl.program_id(0)
    slot = jnp.bitwise_and(i, 1)

    def gather(dref, s):
        def start(t, c):
            for k in range(TOP_K):
                _token_copy(ys_ref, dref[t * TOP_K + k], ybuf.at[s], k * tt + t, sem.at[s]).start(priority=k % 2)
            return c
        lax.fori_loop(0, tt, start, 0)

    @pl.when(i == 0)
    def _():
        gather(dest_ref, 0)

    @pl.when(i + 1 < pl.num_programs(0))
    def _():
        gather(dest_next_ref, 1 - slot)

    x = x_ref[...]
    xb = x.astype(BF16)
    sg = jnp.dot(xb, wsg_ref[...], preferred_element_type=F32)
    su = jnp.dot(xb, wsu_ref[...], preferred_element_type=F32)
    acc = jnp.dot(((sg * _sigmoid(sg)) * su).astype(BF16), wsd_ref[...], preferred_element_type=F32)
    pltpu.make_async_copy(ys_ref.at[pl.ds(0, tt * TOP_K)], ybuf.at[slot], sem.at[slot]).wait()
    gate = gate_ref[...]
    routed = None
    for k in range(TOP_K):
        gk = jnp.broadcast_to(gate[:, k:k + 1], (tt, LANES))
        part = [gk * ch for ch in _chunks_of_token_tiles(ybuf[slot, k * tt:(k + 1) * tt])]
        routed = part if routed is None else [r + p for r, p in zip(routed, part)]
    acc = acc + jnp.concatenate(routed, axis=1)
    o_ref[...] = _layer_norm(alpha * x + acc, g_ref[...], b_ref[...])


def _combine(dest_flat, x, gate, ys, wsg, wsu, wsd, g, b, *, tt, alpha):
    t, d = x.shape
    nsteps = t // tt
    return pl.pallas_call(
        functools.partial(_combine_kernel, tt=tt, alpha=alpha),
        grid=(nsteps,),
        in_specs=[pl.BlockSpec((tt * TOP_K,), lambda i: (i,), memory_space=pltpu.SMEM),
                  pl.BlockSpec((tt * TOP_K,), lambda i: (jnp.minimum(i + 1, nsteps - 1),),
                               memory_space=pltpu.SMEM),
                  pl.BlockSpec((tt, d), lambda i: (i, 0)),
                  pl.BlockSpec((tt, TOP_K), lambda i: (i, 0)),
                  pl.BlockSpec(memory_space=pl.ANY),
                  _const_spec(wsg.shape), _const_spec(wsu.shape), _const_spec(wsd.shape),
                  _const_spec(g.shape), _const_spec(b.shape)],
        out_specs=pl.BlockSpec((tt, d), lambda i: (i, 0)),
        out_shape=jax.ShapeDtypeStruct((t, d), F32),
        scratch_shapes=[pltpu.VMEM((2, tt * TOP_K, SUBLANES, LANES), F32), pltpu.SemaphoreType.DMA((2,))],
        compiler_params=_params("arbitrary"),
        name="combine_ln2",
    )(dest_flat, dest_flat, x, gate, ys, wsg, wsu, wsd, g, b)


def _block_diag_tiles(w):
    n, s, _ = w.shape
    per = MXU_DIM // s
    w = w.reshape(n // per, per, s, s)
    eye = jnp.eye(per, dtype=w.dtype)
    return jnp.einsum('tpij,pq->tpiqj', w, eye).reshape(n // per, MXU_DIM, MXU_DIM)


def _row(v):
    return v.reshape(1, -1)


def _moe(x1, w_router, router_bias, w_gate, w_up, w_down, ws_gate, ws_up, ws_down, ln_g, ln_b, alpha):
    t, d = x1.shape
    ne = w_router.shape[1]
    eidx, gate, rank, counts = _router(x1, w_router, _row(router_bias), tm=256)
    counts = counts[0]
    nblk = (counts + MOE_ROWS - 1) // MOE_ROWS
    blk_end = jnp.cumsum(nblk)
    blk_base = (blk_end - nblk).astype(jnp.int32)
    n_used = blk_end[-1:].astype(jnp.int32)
    n_slots = ((t * TOP_K) // MOE_ROWS + ne) * MOE_ROWS

    dest = _dest(eidx, rank, _row(blk_base * MOE_ROWS), tm=256)
    dest_flat = dest.reshape(-1)
    xs = _dispatch(dest_flat, x1, n_slots, tt=256)
    ys = _experts(blk_base, nblk.astype(jnp.int32), counts, n_used, xs, w_gate, w_up, w_down)
    return _combine(dest_flat, x1, gate, ys, ws_gate.astype(BF16), ws_up.astype(BF16),
                    ws_down.astype(BF16), _row(ln_g), _row(ln_b), tt=128, alpha=alpha)


def _pad_time(a, lp):
    return jnp.pad(a, ((0, 0), (0, lp - a.shape[1]), (0, 0)))


def kernel(x_prompt, x_sample, state_rglru_h, state_conv, state_gla, meta_tokens, w_in, conv_w, conv_b,
           rg_wa, rg_ba, rg_wi, rg_bi, rg_lambda, gla_wf2, gla_bf, gla_norm_g, w_out, ln1_g, ln1_b,
           w_router, router_bias, w_gate, w_up, w_down, ws_gate, ws_up, ws_down, ln2_g, ln2_b):
    bp, lp, d = x_prompt.shape
    bs, ls, _ = x_sample.shape
    depth = w_in.shape[0]
    nh, dk, dv = state_gla.shape[2:]
    qk, vw = nh * dk, nh * dv
    lowrank = gla_wf2.shape[1]
    alpha = (2.0 * depth) ** 0.25
    ls_pad = SUBLANES
    assert depth == 1 and ls <= ls_pad and lp % 256 == 0 and N_META % SUBLANES == 0

    xp = x_prompt.reshape(bp * lp, d)
    xs = x_sample.reshape(bs * ls, d)
    xm = meta_tokens.astype(F32)

    hp_l, cp_l, sp_l, hs_l, cs_l, ss_l = [], [], [], [], [], []
    for l in range(depth):
        o1, o2 = 2 * d, 2 * d + 2 * qk + 2 * vw
        wa = w_in[l][:, :o1].astype(BF16)
        wb = jnp.pad(w_in[l][:, o1:o2 + lowrank], ((0, 0), (0, LANES - lowrank))).astype(BF16)
        wm = w_in[l][:, o2 + lowrank:].astype(BF16)
        wf = jnp.pad(gla_wf2[l], ((0, LANES - lowrank), (0, 0))).astype(BF16)
        wa4 = _block_diag_tiles(rg_wa[l]).astype(BF16)
        wi4 = _block_diag_tiles(rg_wi[l]).astype(BF16)
        mixer_w = (conv_w[l], _row(conv_b[l]), wa4, _row(rg_ba[l]), wi4, _row(rg_bi[l]), _row(rg_lambda[l]))
        gla_w = (wf, _row(gla_bf[l]), _row(gla_norm_g[l]))

        ua, ub, _ = _inproj(xm, wa, wb, wm, tm=N_META)
        _, h_m, c_m = _rglru(ua[None], jnp.zeros((1, CONV_W - 1, d), F32), jnp.zeros((1, 1, d), F32),
                             *mixer_w, bb=1, tb=N_META, l_real=N_META)
        _, s_m = _gla(ub[None], jnp.zeros((1, nh, dk, dv), F32), *gla_w,
                      bb=1, tb=N_META, chunk=N_META, l_real=N_META)

        ua, ub, um = _inproj(xp, wa, wb, wm, tm=256)
        ya, h_p, c_p = _rglru(ua.reshape(bp, lp, -1), jnp.broadcast_to(c_m, (bp,) + c_m.shape[1:]),
                              jnp.broadcast_to(h_m, (bp,) + h_m.shape[1:]), *mixer_w,
                              bb=1, tb=256, l_real=lp)
        yb, s_p = _gla(ub.reshape(bp, lp, -1), jnp.broadcast_to(s_m, (bp,) + s_m.shape[1:]), *gla_w,
                       bb=1, tb=256, chunk=GLA_CHUNK, l_real=lp)
        x1p = _mix(ya.reshape(bp * lp, d), yb.reshape(bp * lp, d), um, xp, w_out[l].astype(BF16),
                   _row(ln1_g[l]), _row(ln1_b[l]), alpha=alpha, tm=512)

        xs_pad = _pad_time(xs.reshape(bs, ls, d), ls_pad).reshape(bs * ls_pad, d)
        ua, ub, um = _inproj(xs_pad, wa, wb, wm, tm=256)
        ya, h_s, c_s = _rglru(ua.reshape(bs, ls_pad, -1), state_conv[l].astype(F32),
                              state_rglru_h[l].astype(F32)[:, None, :], *mixer_w,
                              bb=32, tb=ls_pad, l_real=ls)
        yb, s_s = _gla(ub.reshape(bs, ls_pad, -1), state_gla[l].astype(F32), *gla_w,
                       bb=8, tb=ls_pad, chunk=ls_pad, l_real=ls)
        unpad = lambda a: a.reshape(bs, ls_pad, -1)[:, :ls].reshape(bs * ls, -1)
        x1s = _mix(unpad(ya), unpad(yb), unpad(um), xs, w_out[l].astype(BF16),
                   _row(ln1_g[l]), _row(ln1_b[l]), alpha=alpha, tm=512)

        x1 = jnp.concatenate([x1p, x1s], axis=0)
        x2 = _moe(x1, w_router[l], router_bias[l], w_gate[l], w_up[l], w_down[l],
                  ws_gate[l], ws_up[l], ws_down[l], ln2_g[l], ln2_b[l], alpha)
        xp, xs = x2[:bp * lp], x2[bp * lp:]

        hp_l.append(h_p[:, 0]); cp_l.append(c_p); sp_l.append(s_p)
        hs_l.append(h_s[:, 0]); cs_l.append(c_s); ss_l.append(s_s)

    y_prompt = xp.reshape(bp, lp, d)
    y_sample = xs.reshape(bs, ls, d)
    return (y_prompt, y_sample,
            jnp.stack(hp_l).astype(state_rglru_h.dtype), jnp.stack(cp_l).astype(state_conv.dtype),
            jnp.stack(sp_l).astype(state_gla.dtype),
            jnp.stack(hs_l).astype(state_rglru_h.dtype), jnp.stack(cs_l).astype(state_conv.dtype),
            jnp.stack(ss_l).astype(state_gla.dtype))
```

```python
import functools

import jax
import jax.numpy as jnp
from jax import lax
from jax.experimental import pallas as pl
from jax.experimental.pallas import tpu as pltpu

F32 = jnp.float32
BF16 = jnp.bfloat16
HIGHEST = lax.Precision.HIGHEST

N_META = 16
CONV_W = 4
RG_C = 8.0
RNN_BLOCKS = 16
GLA_HEADS = 4
GLA_TAU = 16.0
GLA_CHUNK = 64
N_GROUPS = 8
TOPK_GROUPS = 4
TOP_K = 8
ROUTED_SCALE = 2.5
LN_EPS = 1e-5
RMS_EPS = 1e-6

LANES = 128
SUBLANES = 8
MXU_DIM = 256
VMEM_LIMIT_BYTES = 56 * 1024 * 1024

MOE_ROWS = 128


def _params(*sem):
    return pltpu.CompilerParams(dimension_semantics=sem, vmem_limit_bytes=VMEM_LIMIT_BYTES)


def _sigmoid(x):
    return jax.nn.sigmoid(x)


def _softplus(z):
    return jnp.maximum(z, 0.0) + jnp.log1p(jnp.exp(-jnp.abs(z)))


def _layer_norm(z, g, b):
    mu = jnp.mean(z, axis=-1, keepdims=True)
    zc = z - mu
    var = jnp.mean(zc * zc, axis=-1, keepdims=True)
    return zc * lax.rsqrt(var + LN_EPS) * g + b


def _const_spec(shape):
    nd = len(shape)
    return pl.BlockSpec(shape, lambda *_: (0,) * nd)


def _inproj_kernel(x_ref, wa_ref, wb_ref, wm_ref, ua_ref, ub_ref, um_ref):
    xb = x_ref[...].astype(BF16)
    ua_ref[...] = jnp.dot(xb, wa_ref[...], preferred_element_type=F32)
    ub_ref[...] = jnp.dot(xb, wb_ref[...], preferred_element_type=F32)
    um_ref[...] = jnp.dot(xb, wm_ref[...], preferred_element_type=F32)


def _inproj(x2d, wa, wb, wm, tm):
    m, d = x2d.shape
    na, nb, nm = wa.shape[1], wb.shape[1], wm.shape[1]
    return pl.pallas_call(
        _inproj_kernel,
        grid=(m // tm,),
        in_specs=[pl.BlockSpec((tm, d), lambda i: (i, 0)),
                  _const_spec(wa.shape), _const_spec(wb.shape), _const_spec(wm.shape)],
        out_specs=[pl.BlockSpec((tm, na), lambda i: (i, 0)),
                   pl.BlockSpec((tm, nb), lambda i: (i, 0)),
                   pl.BlockSpec((tm, nm), lambda i: (i, 0))],
        out_shape=[jax.ShapeDtypeStruct((m, na), F32),
                   jax.ShapeDtypeStruct((m, nb), F32),
                   jax.ShapeDtypeStruct((m, nm), F32)],
        compiler_params=_params("arbitrary"),
        name="inproj",
    )(x2d, wa, wb, wm)


def _rglru_kernel(ua_ref, cbuf_ref, h0_ref, cw_ref, cb_ref, wa_ref, ba_ref, wi_ref, bi_ref, lam_ref,
                  ya_ref, hlast_ref, cnew_ref, ext_scr, h_scr, *, bb, tb, l_last, nblk, d):
    j = pl.program_id(1)
    ntail = CONV_W - 1
    rows = bb * tb

    @pl.when(j == 0)
    def _():
        h_scr[...] = h0_ref[...]
        ext_scr[:, SUBLANES - ntail:SUBLANES, :] = cbuf_ref[...]

    xr = ua_ref[:, :, :d]
    gr = ua_ref[:, :, d:].reshape(rows, d)
    ext_scr[:, SUBLANES:, :] = xr
    xc = cb_ref[...] + cw_ref[ntail:ntail + 1, :] * xr
    for s in range(1, CONV_W):
        xc = xc + cw_ref[ntail - s:ntail - s + 1, :] * ext_scr[:, SUBLANES - s:SUBLANES - s + tb, :]
    xc = xc.reshape(rows, d)

    xcb = xc.astype(BF16)
    nt = d // MXU_DIM
    ra = jnp.concatenate(
        [jnp.dot(xcb[:, q * MXU_DIM:(q + 1) * MXU_DIM], wa_ref[q], preferred_element_type=F32)
         for q in range(nt)], axis=-1)
    ia = jnp.concatenate(
        [jnp.dot(xcb[:, q * MXU_DIM:(q + 1) * MXU_DIM], wi_ref[q], preferred_element_type=F32)
         for q in range(nt)], axis=-1)
    r = _sigmoid(ra + ba_ref[...])
    ig = _sigmoid(ia + bi_ref[...])
    log_a = (-RG_C) * r * _softplus(-lam_ref[...])
    a = jnp.exp(log_a)
    mult = jnp.sqrt(-jnp.tanh(log_a) * (a * a + 1.0))
    b = mult * (ig * xc)

    t_in = jnp.bitwise_and(lax.broadcasted_iota(jnp.int32, (rows, d), 0), tb - 1)
    h_prev = jnp.broadcast_to(h_scr[...], (bb, tb, d)).reshape(rows, d)
    b = b + jnp.where(t_in == 0, a * h_prev, 0.0)
    s = 1
    while s < tb:
        a_sh = jnp.where(t_in >= s, pltpu.roll(a, s, 0), 1.0)
        b_sh = jnp.where(t_in >= s, pltpu.roll(b, s, 0), 0.0)
        b = b + a * b_sh
        a = a * a_sh
        s *= 2
    ya_ref[...] = (b * jax.nn.gelu(gr, approximate=True)).reshape(bb, tb, d)
    h = b.reshape(bb, tb, d)
    h_scr[...] = h[:, tb - 1:tb, :]

    @pl.when(j == nblk - 1)
    def _():
        hlast_ref[...] = h[:, l_last - 1:l_last, :]
        cnew_ref[...] = ext_scr[:, SUBLANES + l_last - ntail:SUBLANES + l_last, :]

    tail = ext_scr[:, tb + SUBLANES - ntail:tb + SUBLANES, :]
    ext_scr[:, SUBLANES - ntail:SUBLANES, :] = tail


def _rglru(ua, cbuf, h0, cw, cb, wa4, ba, wi4, bi, lam, *, bb, tb, l_real):
    bsz, lp, n2 = ua.shape
    d = n2 // 2
    nblk = lp // tb
    l_last = l_real - (nblk - 1) * tb
    assert tb & (tb - 1) == 0 and bsz % bb == 0
    kern = functools.partial(_rglru_kernel, bb=bb, tb=tb, l_last=l_last, nblk=nblk, d=d)
    return pl.pallas_call(
        kern,
        grid=(bsz // bb, nblk),
        in_specs=[pl.BlockSpec((bb, tb, n2), lambda b, j: (b, j, 0)),
                  pl.BlockSpec((bb, CONV_W - 1, d), lambda b, j: (b, 0, 0)),
                  pl.BlockSpec((bb, 1, d), lambda b, j: (b, 0, 0)),
                  _const_spec(cw.shape), _const_spec(cb.shape),
                  _const_spec(wa4.shape), _const_spec(ba.shape),
                  _const_spec(wi4.shape), _const_spec(bi.shape), _const_spec(lam.shape)],
        out_specs=[pl.BlockSpec((bb, tb, d), lambda b, j: (b, j, 0)),
                   pl.BlockSpec((bb, 1, d), lambda b, j: (b, 0, 0)),
                   pl.BlockSpec((bb, CONV_W - 1, d), lambda b, j: (b, 0, 0))],
        out_shape=[jax.ShapeDtypeStruct((bsz, lp, d), F32),
                   jax.ShapeDtypeStruct((bsz, 1, d), F32),
                   jax.ShapeDtypeStruct((bsz, CONV_W - 1, d), F32)],
        scratch_shapes=[pltpu.VMEM((bb, tb + SUBLANES, d), F32), pltpu.VMEM((bb, 1, d), F32)],
        compiler_params=_params("arbitrary", "arbitrary"),
        name="rglru",
    )(ua, cbuf, h0, cw, cb, wa4, ba, wi4, bi, lam)


def _gla_kernel(ub_ref, s0_ref, wf_ref, bf_ref, g_ref, yb_ref, sout_ref, s_scr,
                *, bb, tb, chunk, l_real, nblk, dk, dv):
    j = pl.program_id(1)
    nh = GLA_HEADS
    qk = nh * dk
    vw = nh * dv

    @pl.when(j == 0)
    def _():
        s_scr[...] = s0_ref[...]

    off_k, off_v, off_g, off_f = qk, 2 * qk, 2 * qk + vw, 2 * qk + 2 * vw
    nc = tb // chunk
    shift = chunk.bit_length() - 1
    ri = lax.broadcasted_iota(jnp.int32, (tb, tb), 0)
    ci = lax.broadcasted_iota(jnp.int32, (tb, tb), 1)
    keep = (lax.shift_right_logical(ri, shift) == lax.shift_right_logical(ci, shift)) & (ri >= ci)
    tril = keep.astype(F32)
    row_chunk = lax.shift_right_logical(lax.broadcasted_iota(jnp.int32, (tb, dk), 0), shift)
    mid = chunk // 2
    scale = dk ** -0.5

    def by_chunk(a):
        if nc == 1:
            return a
        return jnp.concatenate([jnp.where(row_chunk == c, a, 0.0) for c in range(nc)], axis=1)

    def per_chunk_row(a, r):
        return jnp.concatenate([jnp.broadcast_to(a[c * chunk + r:c * chunk + r + 1, :], (chunk, a.shape[1]))
                                for c in range(nc)], axis=0)

    for bi in range(bb):
        fl = ub_ref[bi, :, off_f:off_f + LANES].astype(BF16)
        logit = jnp.dot(fl, wf_ref[...], preferred_element_type=F32) + bf_ref[...]
        log_f = (jnp.minimum(logit, 0.0) - jnp.log1p(jnp.exp(-jnp.abs(logit)))) * (1.0 / GLA_TAU)
        if l_real < tb * nblk:
            rows = lax.broadcasted_iota(jnp.int32, log_f.shape, 0) + j * tb
            log_f = jnp.where(rows < l_real, log_f, 0.0)
        bcum = jnp.dot(tril, log_f, precision=HIGHEST, preferred_element_type=F32)
        ref = per_chunk_row(bcum, mid)
        last = per_chunk_row(bcum, chunk - 1)
        e_inter = jnp.exp(bcum)
        e_q = jnp.exp(bcum - ref)
        e_k = jnp.exp(ref - bcum)
        e_dec = jnp.exp(last - bcum)

        for h in range(nh):
            hk = slice(h * dk, (h + 1) * dk)
            q = ub_ref[bi, :, h * dk:(h + 1) * dk] * scale
            k = ub_ref[bi, :, off_k + h * dk:off_k + (h + 1) * dk]
            vb = ub_ref[bi, :, off_v + h * dv:off_v + (h + 1) * dv].astype(BF16)
            go = ub_ref[bi, :, off_g + h * dv:off_g + (h + 1) * dv]
            att = lax.dot_general((q * e_q[:, hk]).astype(BF16), (k * e_k[:, hk]).astype(BF16),
                                  (((1,), (1,)), ((), ())), preferred_element_type=F32)
            att = jnp.where(keep, att, 0.0)
            o = jnp.dot(att.astype(BF16), vb, preferred_element_type=F32)
            kd = by_chunk(k * e_dec[:, hk])
            kv = lax.dot_general(kd.astype(BF16), vb, (((0,), (0,)), ((), ())),
                                 preferred_element_type=F32)
            states = [s_scr[bi, h]]
            for c in range(nc):
                drow = jnp.exp(bcum[c * chunk + chunk - 1:c * chunk + chunk, hk])
                dec = jnp.transpose(jnp.broadcast_to(drow, (dk, dk)))
                dec = jnp.concatenate([dec] * (dv // dk), axis=1)
                states.append(dec * states[c] + kv[c * dk:(c + 1) * dk])
            s_scr[bi, h] = states[nc]
            qi = by_chunk(q * e_inter[:, hk])
            s_in = jnp.concatenate(states[:nc], axis=0).astype(BF16)
            o = o + jnp.dot(qi.astype(BF16), s_in, preferred_element_type=F32)
            o = o * lax.rsqrt(jnp.mean(o * o, axis=-1, keepdims=True) + RMS_EPS) * g_ref[...]
            yb_ref[bi, :, h * dv:(h + 1) * dv] = o * (go * _sigmoid(go))

    @pl.when(j == nblk - 1)
    def _():
        sout_ref[...] = s_scr[...]


def _gla(ub, s0, wf, bf, g, *, bb, tb, chunk, l_real):
    bsz, lp, nb = ub.shape
    _, nh, dk, dv = s0.shape
    nblk = lp // tb
    assert bsz % bb == 0 and dv % dk == 0 and chunk & (chunk - 1) == 0 and tb % chunk == 0
    kern = functools.partial(_gla_kernel, bb=bb, tb=tb, chunk=chunk, l_real=l_real, nblk=nblk, dk=dk, dv=dv)
    return pl.pallas_call(
        kern,
        grid=(bsz // bb, nblk),
        in_specs=[pl.BlockSpec((bb, tb, nb), lambda b, j: (b, j, 0)),
                  pl.BlockSpec((bb, nh, dk, dv), lambda b, j: (b, 0, 0, 0)),
                  _const_spec(wf.shape), _const_spec(bf.shape), _const_spec(g.shape)],
        out_specs=[pl.BlockSpec((bb, tb, nh * dv), lambda b, j: (b, j, 0)),
                   pl.BlockSpec((bb, nh, dk, dv), lambda b, j: (b, 0, 0, 0))],
        out_shape=[jax.ShapeDtypeStruct((bsz, lp, nh * dv), F32),
                   jax.ShapeDtypeStruct((bsz, nh, dk, dv), F32)],
        scratch_shapes=[pltpu.VMEM((bb, nh, dk, dv), F32)],
        compiler_params=_params("arbitrary", "arbitrary"),
        name="gla",
    )(ub, s0, wf, bf, g)


def _mix_kernel(ya_ref, yb_ref, um_ref, x_ref, wo_ref, g_ref, b_ref, o_ref, *, alpha, d):
    merged = _sigmoid(um_ref[:, :d]) * ya_ref[...] + _sigmoid(um_ref[:, d:]) * yb_ref[...]
    mix = jnp.dot(merged.astype(BF16), wo_ref[...], preferred_element_type=F32)
    o_ref[...] = _layer_norm(alpha * x_ref[...] + mix, g_ref[...], b_ref[...])


def _mix(ya, yb, um, x, wo, g, b, *, alpha, tm):
    m, d = x.shape
    row = lambda w: pl.BlockSpec((tm, w), lambda i: (i, 0))
    return pl.pallas_call(
        functools.partial(_mix_kernel, alpha=alpha, d=d),
        grid=(m // tm,),
        in_specs=[row(d), row(d), row(2 * d), row(d),
                  _const_spec(wo.shape), _const_spec(g.shape), _const_spec(b.shape)],
        out_specs=row(d),
        out_shape=jax.ShapeDtypeStruct((m, d), F32),
        compiler_params=_params("arbitrary"),
        name="mix_ln1",
    )(ya, yb, um, x, wo, g, b)


def _router_kernel(x_ref, wr_ref, bias_ref, eidx_ref, gate_ref, rank_ref, cnt_ref, run_scr,
                   *, tm, ne):
    i = pl.program_id(0)

    @pl.when(i == 0)
    def _():
        run_scr[...] = jnp.zeros_like(run_scr)

    logits = jnp.dot(x_ref[...], wr_ref[...], precision=HIGHEST, preferred_element_type=F32)
    s = _sigmoid(logits)
    sel = s + bias_ref[...]
    lane = lax.broadcasted_iota(jnp.int32, (tm, ne), 1)
    lane_f = lane.astype(F32)
    gsz = ne // N_GROUPS
    neg = -jnp.inf

    gscores = []
    for g in range(N_GROUPS):
        ing = (lane >= g * gsz) & (lane < (g + 1) * gsz)
        v = jnp.where(ing, sel, neg)
        m1 = jnp.max(v, axis=1, keepdims=True)
        i1 = jnp.min(jnp.where(v == m1, lane_f, float(ne)), axis=1, keepdims=True)
        m2 = jnp.max(jnp.where(lane_f == i1, neg, v), axis=1, keepdims=True)
        gscores.append(m1 + m2)
    cur = jnp.full((tm, ne), neg, F32)
    for g in range(N_GROUPS):
        ahead = jnp.zeros((tm, 1), F32)
        for g2 in range(N_GROUPS):
            if g2 == g:
                continue
            beats = (gscores[g2] >= gscores[g]) if g2 < g else (gscores[g2] > gscores[g])
            ahead = ahead + beats.astype(F32)
        ing = (lane >= g * gsz) & (lane < (g + 1) * gsz)
        cur = jnp.where(ing & (ahead < float(TOPK_GROUPS)), sel, cur)

    col = lax.broadcasted_iota(jnp.int32, (tm, TOP_K), 1)
    eidx = jnp.zeros((tm, TOP_K), F32)
    wsel = jnp.zeros((tm, TOP_K), F32)
    onehot_sum = jnp.zeros((tm, ne), F32)
    picks = []
    for k in range(TOP_K):
        m = jnp.max(cur, axis=1, keepdims=True)
        ik = jnp.min(jnp.where(cur == m, lane_f, float(ne)), axis=1, keepdims=True)
        hit = lane_f == ik
        wk = jnp.sum(jnp.where(hit, s, 0.0), axis=1, keepdims=True)
        cur = jnp.where(hit, neg, cur)
        onehot_sum = onehot_sum + hit.astype(F32)
        picks.append(hit)
        eidx = jnp.where(col == k, ik, eidx)
        wsel = jnp.where(col == k, wk, wsel)
    gate_ref[...] = wsel / jnp.sum(wsel, axis=1, keepdims=True) * ROUTED_SCALE
    eidx_ref[...] = eidx.astype(jnp.int32)

    ri = lax.broadcasted_iota(jnp.int32, (tm, tm), 0)
    ci = lax.broadcasted_iota(jnp.int32, (tm, tm), 1)
    before = (ci < ri).astype(BF16)
    counts = jnp.dot(before, onehot_sum.astype(BF16), preferred_element_type=F32) + run_scr[...]
    rank = jnp.zeros((tm, TOP_K), F32)
    for k in range(TOP_K):
        rk = jnp.sum(jnp.where(picks[k], counts, 0.0), axis=1, keepdims=True)
        rank = jnp.where(col == k, rk, rank)
    rank_ref[...] = rank.astype(jnp.int32)
    run_scr[...] = run_scr[...] + jnp.sum(onehot_sum, axis=0, keepdims=True)
    cnt_ref[...] = run_scr[...].astype(jnp.int32)


def _router(x, wr, bias, *, tm):
    t, d = x.shape
    ne = wr.shape[1]
    out_row = lambda: pl.BlockSpec((tm, TOP_K), lambda i: (i, 0))
    return pl.pallas_call(
        functools.partial(_router_kernel, tm=tm, ne=ne),
        grid=(t // tm,),
        in_specs=[pl.BlockSpec((tm, d), lambda i: (i, 0)), _const_spec(wr.shape), _const_spec(bias.shape)],
        out_specs=[out_row(), out_row(), out_row(), _const_spec((1, ne))],
        out_shape=[jax.ShapeDtypeStruct((t, TOP_K), jnp.int32),
                   jax.ShapeDtypeStruct((t, TOP_K), F32),
                   jax.ShapeDtypeStruct((t, TOP_K), jnp.int32),
                   jax.ShapeDtypeStruct((1, ne), jnp.int32)],
        scratch_shapes=[pltpu.VMEM((1, ne), F32)],
        compiler_params=_params("arbitrary"),
        name="router",
    )(x, wr, bias)


def _dest_kernel(eidx_ref, rank_ref, start_ref, dest_ref, *, tm, ne):
    lane = lax.broadcasted_iota(jnp.int32, (tm, ne), 1)
    col = lax.broadcasted_iota(jnp.int32, (tm, TOP_K), 1)
    starts = start_ref[...].astype(F32)
    eidx = eidx_ref[...]
    dest = jnp.zeros((tm, TOP_K), F32)
    for k in range(TOP_K):
        hit = lane == eidx[:, k:k + 1]
        sk = jnp.sum(jnp.where(hit, starts, 0.0), axis=1, keepdims=True)
        dest = jnp.where(col == k, sk, dest)
    dest_ref[...] = dest.astype(jnp.int32) + rank_ref[...]


def _dest(eidx, rank, starts, *, tm):
    t = eidx.shape[0]
    ne = starts.shape[1]
    row = lambda: pl.BlockSpec((tm, TOP_K), lambda i: (i, 0))
    return pl.pallas_call(
        functools.partial(_dest_kernel, tm=tm, ne=ne),
        grid=(t // tm,),
        in_specs=[row(), row(), _const_spec((1, ne))],
        out_specs=row(),
        out_shape=jax.ShapeDtypeStruct((t, TOP_K), jnp.int32),
        compiler_params=_params("arbitrary"),
        name="dest",
    )(eidx, rank, starts)


def _to_token_tiles(rows):
    chunks = jnp.stack([rows[:, c * LANES:(c + 1) * LANES] for c in range(SUBLANES)], axis=0)
    return jnp.swapaxes(chunks, 0, 1)


def _chunks_of_token_tiles(tiles):
    chunks = jnp.swapaxes(tiles, 0, 1)
    return [chunks[c] for c in range(SUBLANES)]


def _token_copy(src_ref, src_row, dst_ref, dst_row, sem):
    return pltpu.make_async_copy(src_ref.at[pl.ds(src_row, 1)], dst_ref.at[pl.ds(dst_row, 1)], sem)


def _dispatch_kernel(dest_ref, x_ref, xs_ref, xt, sem, *, tt):
    xt[...] = _to_token_tiles(x_ref[...])

    def start(t, c):
        for k in range(TOP_K):
            _token_copy(xt, t, xs_ref, dest_ref[t * TOP_K + k], sem).start(priority=k % 2)
        return c

    lax.fori_loop(0, tt, start, 0)
    for _ in range(TOP_K):
        pltpu.make_async_copy(xt, xs_ref.at[pl.ds(0, tt)], sem).wait()


def _dispatch(dest_flat, x, n_slots, *, tt):
    t, d = x.shape
    assert d == SUBLANES * LANES
    return pl.pallas_call(
        functools.partial(_dispatch_kernel, tt=tt),
        grid=(t // tt,),
        in_specs=[pl.BlockSpec((tt * TOP_K,), lambda i: (i,), memory_space=pltpu.SMEM),
                  pl.BlockSpec((tt, d), lambda i: (i, 0))],
        out_specs=pl.BlockSpec(memory_space=pl.ANY),
        out_shape=jax.ShapeDtypeStruct((n_slots, SUBLANES, LANES), F32),
        scratch_shapes=[pltpu.VMEM((tt, SUBLANES, LANES), F32), pltpu.SemaphoreType.DMA(())],
        compiler_params=_params("arbitrary"),
        name="dispatch",
    )(dest_flat, x)


EXPERT_SLOTS = 8
EXPERT_AHEAD = 6


def _expert_kernel(base_ref, nblk_ref, cnt_ref, nu_ref, xs_ref, wg_ref, wu_ref, wd_ref, ys_ref,
                   xbuf, ybuf, wgb, wub, wdb, sem_in, sem_out):
    e = pl.program_id(0)
    n_used = nu_ref[0]
    nb = nblk_ref[e]
    base = base_ref[e]
    cnt = cnt_ref[e]
    ahead = EXPERT_AHEAD

    def x_copy(blk, slot):
        return pltpu.make_async_copy(xs_ref.at[pl.ds(blk * MOE_ROWS, MOE_ROWS)], xbuf.at[slot], sem_in.at[slot])

    def y_copy(blk, slot):
        return pltpu.make_async_copy(ybuf.at[slot], ys_ref.at[pl.ds(blk * MOE_ROWS, MOE_ROWS)], sem_out.at[slot])

    @pl.when(e == 0)
    def _():
        for first in range(ahead):
            @pl.when(first < n_used)
            def _():
                x_copy(first, first).start()

    @pl.when(nb > 0)
    def _():
        wgb[...] = wg_ref[0].astype(BF16)
        wub[...] = wu_ref[0].astype(BF16)
        wdb[...] = wd_ref[0].astype(BF16)

    def step(jb, nblocks):
        blks = [base + jb + i for i in range(nblocks)]
        slots = [jnp.bitwise_and(b, EXPERT_SLOTS - 1) for b in blks]
        for b, s in zip(blks, slots):
            x_copy(b, s).wait()
        for b in blks:
            @pl.when(b + ahead < n_used)
            def _():
                x_copy(b + ahead, jnp.bitwise_and(b + ahead, EXPERT_SLOTS - 1)).start()

        x = jnp.concatenate([jnp.concatenate(_chunks_of_token_tiles(xbuf[s]), axis=1) for s in slots], axis=0)
        rows = lax.broadcasted_iota(jnp.int32, x.shape, 0)
        xb = jnp.where(rows < cnt - jb * MOE_ROWS, x, 0.0).astype(BF16)
        g = jnp.dot(xb, wgb[...], preferred_element_type=F32)
        u = jnp.dot(xb, wub[...], preferred_element_type=F32)
        hmid = (g * _sigmoid(g)) * u
        y = jnp.dot(hmid.astype(BF16), wdb[...], preferred_element_type=F32)

        for i, (b, s) in enumerate(zip(blks, slots)):
            @pl.when(b >= EXPERT_SLOTS)
            def _():
                y_copy(b - EXPERT_SLOTS, s).wait()

            ybuf[s] = _to_token_tiles(y[i * MOE_ROWS:(i + 1) * MOE_ROWS])
            y_copy(b, s).start()

    def pair(jp, c):
        step(2 * jp, 2)
        return c

    lax.fori_loop(0, lax.shift_right_logical(nb, 1), pair, 0)

    @pl.when(jnp.bitwise_and(nb, 1) == 1)
    def _():
        step(nb - 1, 1)

    @pl.when(e == pl.num_programs(0) - 1)
    def _():
        for back in range(EXPERT_SLOTS):
            @pl.when(n_used > back)
            def _():
                last = n_used - 1 - back
                y_copy(last, jnp.bitwise_and(last, EXPERT_SLOTS - 1)).wait()


def _experts(base, nblk, cnt, n_used, xs, wg, wu, wd):
    p = xs.shape[0]
    ne, d, de = wg.shape
    assert EXPERT_SLOTS & (EXPERT_SLOTS - 1) == 0 and EXPERT_AHEAD <= EXPERT_SLOTS - 2 and d == SUBLANES * LANES
    ring = pltpu.VMEM((EXPERT_SLOTS, MOE_ROWS, SUBLANES, LANES), F32)
    grid_spec = pltpu.PrefetchScalarGridSpec(
        num_scalar_prefetch=4,
        grid=(ne,),
        in_specs=[pl.BlockSpec(memory_space=pl.ANY),
                  pl.BlockSpec((1, d, de), lambda e, *_: (e, 0, 0)),
                  pl.BlockSpec((1, d, de), lambda e, *_: (e, 0, 0)),
                  pl.BlockSpec((1, de, d), lambda e, *_: (e, 0, 0))],
        out_specs=pl.BlockSpec(memory_space=pl.ANY),
        scratch_shapes=[ring, ring,
                        pltpu.VMEM((d, de), BF16), pltpu.VMEM((d, de), BF16), pltpu.VMEM((de, d), BF16),
                        pltpu.SemaphoreType.DMA((EXPERT_SLOTS,)), pltpu.SemaphoreType.DMA((EXPERT_SLOTS,))],
    )
    return pl.pallas_call(
        _expert_kernel,
        grid_spec=grid_spec,
        out_shape=jax.ShapeDtypeStruct((p, SUBLANES, LANES), F32),
        compiler_params=_params("arbitrary"),
        name="experts",
    )(base, nblk, cnt, n_used, xs, wg, wu, wd)


def _combine_kernel(dest_ref, dest_next_ref, x_ref, gate_ref, ys_ref, wsg_ref, wsu_ref, wsd_ref, g_ref, b_ref,
                    o_ref, ybuf, sem, *, tt, alpha):
    i = pl.program_id(0)
    slot = jnp.bitwise_and(i, 1)

    def gather(dref, s):
        def start(t, c):
            for k in range(TOP_K):
                _token_copy(ys_ref, dref[t * TOP_K + k], ybuf.at[s], k * tt + t, sem.at[s]).start(priority=k % 2)
            return c
        lax.fori_loop(0, tt, start, 0)

    @pl.when(i == 0)
    def _():
        gather(dest_ref, 0)

    @pl.when(i + 1 < pl.num_programs(0))
    def _():
        gather(dest_next_ref, 1 - slot)

    x = x_ref[...]
    xb = x.astype(BF16)
    sg = jnp.dot(xb, wsg_ref[...], preferred_element_type=F32)
    su = jnp.dot(xb, wsu_ref[...], preferred_element_type=F32)
    acc = jnp.dot(((sg * _sigmoid(sg)) * su).astype(BF16), wsd_ref[...], preferred_element_type=F32)
    pltpu.make_async_copy(ys_ref.at[pl.ds(0, tt * TOP_K)], ybuf.at[slot], sem.at[slot]).wait()
    gate = gate_ref[...]
    routed = None
    for k in range(TOP_K):
        gk = jnp.broadcast_to(gate[:, k:k + 1], (tt, LANES))
        part = [gk * ch for ch in _chunks_of_token_tiles(ybuf[slot, k * tt:(k + 1) * tt])]
        routed = part if routed is None else [r + p for r, p in zip(routed, part)]
    acc = acc + jnp.concatenate(routed, axis=1)
    o_ref[...] = _layer_norm(alpha * x + acc, g_ref[...], b_ref[...])


def _combine(dest_flat, x, gate, ys, wsg, wsu, wsd, g, b, *, tt, alpha):
    t, d = x.shape
    nsteps = t // tt
    return pl.pallas_call(
        functools.partial(_combine_kernel, tt=tt, alpha=alpha),
        grid=(nsteps,),
        in_specs=[pl.BlockSpec((tt * TOP_K,), lambda i: (i,), memory_space=pltpu.SMEM),
                  pl.BlockSpec((tt * TOP_K,), lambda i: (jnp.minimum(i + 1, nsteps - 1),),
                               memory_space=pltpu.SMEM),
                  pl.BlockSpec((tt, d), lambda i: (i, 0)),
                  pl.BlockSpec((tt, TOP_K), lambda i: (i, 0)),
                  pl.BlockSpec(memory_space=pl.ANY),
                  _const_spec(wsg.shape), _const_spec(wsu.shape), _const_spec(wsd.shape),
                  _const_spec(g.shape), _const_spec(b.shape)],
        out_specs=pl.BlockSpec((tt, d), lambda i: (i, 0)),
        out_shape=jax.ShapeDtypeStruct((t, d), F32),
        scratch_shapes=[pltpu.VMEM((2, tt * TOP_K, SUBLANES, LANES), F32), pltpu.SemaphoreType.DMA((2,))],
        compiler_params=_params("arbitrary"),
        name="combine_ln2",
    )(dest_flat, dest_flat, x, gate, ys, wsg, wsu, wsd, g, b)


def _block_diag_tiles(w):
    n, s, _ = w.shape
    per = MXU_DIM // s
    w = w.reshape(n // per, per, s, s)
    eye = jnp.eye(per, dtype=w.dtype)
    return jnp.einsum('tpij,pq->tpiqj', w, eye).reshape(n // per, MXU_DIM, MXU_DIM)


def _row(v):
    return v.reshape(1, -1)


def _moe(x1, w_router, router_bias, w_gate, w_up, w_down, ws_gate, ws_up, ws_down, ln_g, ln_b, alpha):
    t, d = x1.shape
    ne = w_router.shape[1]
    eidx, gate, rank, counts = _router(x1, w_router, _row(router_bias), tm=256)
    counts = counts[0]
    nblk = (counts + MOE_ROWS - 1) // MOE_ROWS
    blk_end = jnp.cumsum(nblk)
    blk_base = (blk_end - nblk).astype(jnp.int32)
    n_used = blk_end[-1:].astype(jnp.int32)
    n_slots = ((t * TOP_K) // MOE_ROWS + ne) * MOE_ROWS

    dest = _dest(eidx, rank, _row(blk_base * MOE_ROWS), tm=256)
    dest_flat = dest.reshape(-1)
    xs = _dispatch(dest_flat, x1, n_slots, tt=256)
    ys = _experts(blk_base, nblk.astype(jnp.int32), counts, n_used, xs, w_gate, w_up, w_down)
    return _combine(dest_flat, x1, gate, ys, ws_gate.astype(BF16), ws_up.astype(BF16),
                    ws_down.astype(BF16), _row(ln_g), _row(ln_b), tt=128, alpha=alpha)


def _pad_time(a, lp):
    return jnp.pad(a, ((0, 0), (0, lp - a.shape[1]), (0, 0)))


def kernel(x_prompt, x_sample, state_rglru_h, state_conv, state_gla, meta_tokens, w_in, conv_w, conv_b,
           rg_wa, rg_ba, rg_wi, rg_bi, rg_lambda, gla_wf2, gla_bf, gla_norm_g, w_out, ln1_g, ln1_b,
           w_router, router_bias, w_gate, w_up, w_down, ws_gate, ws_up, ws_down, ln2_g, ln2_b):
    bp, lp, d = x_prompt.shape
    bs, ls, _ = x_sample.shape
    depth = w_in.shape[0]
    nh, dk, dv = state_gla.shape[2:]
    qk, vw = nh * dk, nh * dv
    lowrank = gla_wf2.shape[1]
    alpha = (2.0 * depth) ** 0.25
    ls_pad = SUBLANES
    assert depth == 1 and ls <= ls_pad and lp % 256 == 0 and N_META % SUBLANES == 0

    xp = x_prompt.reshape(bp * lp, d)
    xs = x_sample.reshape(bs * ls, d)
    xm = meta_tokens.astype(F32)

    hp_l, cp_l, sp_l, hs_l, cs_l, ss_l = [], [], [], [], [], []
    for l in range(depth):
        o1, o2 = 2 * d, 2 * d + 2 * qk + 2 * vw
        wa = w_in[l][:, :o1].astype(BF16)
        wb = jnp.pad(w_in[l][:, o1:o2 + lowrank], ((0, 0), (0, LANES - lowrank))).astype(BF16)
        wm = w_in[l][:, o2 + lowrank:].astype(BF16)
        wf = jnp.pad(gla_wf2[l], ((0, LANES - lowrank), (0, 0))).astype(BF16)
        wa4 = _block_diag_tiles(rg_wa[l]).astype(BF16)
        wi4 = _block_diag_tiles(rg_wi[l]).astype(BF16)
        mixer_w = (conv_w[l], _row(conv_b[l]), wa4, _row(rg_ba[l]), wi4, _row(rg_bi[l]), _row(rg_lambda[l]))
        gla_w = (wf, _row(gla_bf[l]), _row(gla_norm_g[l]))

        ua, ub, _ = _inproj(xm, wa, wb, wm, tm=N_META)
        _, h_m, c_m = _rglru(ua[None], jnp.zeros((1, CONV_W - 1, d), F32), jnp.zeros((1, 1, d), F32),
                             *mixer_w, bb=1, tb=N_META, l_real=N_META)
        _, s_m = _gla(ub[None], jnp.zeros((1, nh, dk, dv), F32), *gla_w,
                      bb=1, tb=N_META, chunk=N_META, l_real=N_META)

        ua, ub, um = _inproj(xp, wa, wb, wm, tm=256)
        ya, h_p, c_p = _rglru(ua.reshape(bp, lp, -1), jnp.broadcast_to(c_m, (bp,) + c_m.shape[1:]),
                              jnp.broadcast_to(h_m, (bp,) + h_m.shape[1:]), *mixer_w,
                              bb=1, tb=256, l_real=lp)
        yb, s_p = _gla(ub.reshape(bp, lp, -1), jnp.broadcast_to(s_m, (bp,) + s_m.shape[1:]), *gla_w,
                       bb=1, tb=256, chunk=GLA_CHUNK, l_real=lp)
        x1p = _mix(ya.reshape(bp * lp, d), yb.reshape(bp * lp, d), um, xp, w_out[l].astype(BF16),
                   _row(ln1_g[l]), _row(ln1_b[l]), alpha=alpha, tm=512)

        xs_pad = _pad_time(xs.reshape(bs, ls, d), ls_pad).reshape(bs * ls_pad, d)
        ua, ub, um = _inproj(xs_pad, wa, wb, wm, tm=256)
        ya, h_s, c_s = _rglru(ua.reshape(bs, ls_pad, -1), state_conv[l].astype(F32),
                              state_rglru_h[l].astype(F32)[:, None, :], *mixer_w,
                              bb=32, tb=ls_pad, l_real=ls)
        yb, s_s = _gla(ub.reshape(bs, ls_pad, -1), state_gla[l].astype(F32), *gla_w,
                       bb=8, tb=ls_pad, chunk=ls_pad, l_real=ls)
        unpad = lambda a: a.reshape(bs, ls_pad, -1)[:, :ls].reshape(bs * ls, -1)
        x1s = _mix(unpad(ya), unpad(yb), unpad(um), xs, w_out[l].astype(BF16),
                   _row(ln1_g[l]), _row(ln1_b[l]), alpha=alpha, tm=512)

        x1 = jnp.concatenate([x1p, x1s], axis=0)
        x2 = _moe(x1, w_router[l], router_bias[l], w_gate[l], w_up[l], w_down[l],
                  ws_gate[l], ws_up[l], ws_down[l], ln2_g[l], ln2_b[l], alpha)
        xp, xs = x2[:bp * lp], x2[bp * lp:]

        hp_l.append(h_p[:, 0]); cp_l.append(c_p); sp_l.append(s_p)
        hs_l.append(h_s[:, 0]); cs_l.append(c_s); ss_l.append(s_s)

    y_prompt = xp.reshape(bp, lp, d)
    y_sample = xs.reshape(bs, ls, d)
    return (y_prompt, y_sample,
            jnp.stack(hp_l).astype(state_rglru_h.dtype), jnp.stack(cp_l).astype(state_conv.dtype),
            jnp.stack(sp_l).astype(state_gla.dtype),
            jnp.stack(hs_l).astype(state_rglru_h.dtype), jnp.stack(cs_l).astype(state_conv.dtype),
            jnp.stack(ss_l).astype(state_gla.dtype))
```

```python
import functools

import jax
import jax.numpy as jnp
from jax import lax
from jax.experimental import pallas as pl
from jax.experimental.pallas import tpu as pltpu

F32 = jnp.float32
BF16 = jnp.bfloat16
HIGHEST = lax.Precision.HIGHEST

N_META = 16
CONV_W = 4
RG_C = 8.0
RNN_BLOCKS = 16
GLA_HEADS = 4
GLA_TAU = 16.0
GLA_CHUNK = 64
N_GROUPS = 8
TOPK_GROUPS = 4
TOP_K = 8
ROUTED_SCALE = 2.5
LN_EPS = 1e-5
RMS_EPS = 1e-6

LANES = 128
SUBLANES = 8
MXU_DIM = 256
VMEM_LIMIT_BYTES = 56 * 1024 * 1024

MOE_ROWS = 128


def _params(*sem):
    return pltpu.CompilerParams(dimension_semantics=sem, vmem_limit_bytes=VMEM_LIMIT_BYTES)


def _sigmoid(x):
    return jax.nn.sigmoid(x)


def _softplus(z):
    return jnp.maximum(z, 0.0) + jnp.log1p(jnp.exp(-jnp.abs(z)))


def _layer_norm(z, g, b):
    mu = jnp.mean(z, axis=-1, keepdims=True)
    zc = z - mu
    var = jnp.mean(zc * zc, axis=-1, keepdims=True)
    return zc * lax.rsqrt(var + LN_EPS) * g + b


def _const_spec(shape):
    nd = len(shape)
    return pl.BlockSpec(shape, lambda *_: (0,) * nd)


def _inproj_kernel(x_ref, wa_ref, wb_ref, wm_ref, ua_ref, ub_ref, um_ref):
    xb = x_ref[...].astype(BF16)
    ua_ref[...] = jnp.dot(xb, wa_ref[...], preferred_element_type=F32)
    ub_ref[...] = jnp.dot(xb, wb_ref[...], preferred_element_type=F32)
    um_ref[...] = jnp.dot(xb, wm_ref[...], preferred_element_type=F32)


def _inproj(x2d, wa, wb, wm, tm):
    m, d = x2d.shape
    na, nb, nm = wa.shape[1], wb.shape[1], wm.shape[1]
    return pl.pallas_call(
        _inproj_kernel,
        grid=(m // tm,),
        in_specs=[pl.BlockSpec((tm, d), lambda i: (i, 0)),
                  _const_spec(wa.shape), _const_spec(wb.shape), _const_spec(wm.shape)],
        out_specs=[pl.BlockSpec((tm, na), lambda i: (i, 0)),
                   pl.BlockSpec((tm, nb), lambda i: (i, 0)),
                   pl.BlockSpec((tm, nm), lambda i: (i, 0))],
        out_shape=[jax.ShapeDtypeStruct((m, na), F32),
                   jax.ShapeDtypeStruct((m, nb), F32),
                   jax.ShapeDtypeStruct((m, nm), F32)],
        compiler_params=_params("arbitrary"),
        name="inproj",
    )(x2d, wa, wb, wm)


def _rglru_kernel(ua_ref, cbuf_ref, h0_ref, cw_ref, cb_ref, wa_ref, ba_ref, wi_ref, bi_ref, lam_ref,
                  ya_ref, hlast_ref, cnew_ref, ext_scr, h_scr, *, bb, tb, l_last, nblk, d):
    j = pl.program_id(1)
    ntail = CONV_W - 1
    rows = bb * tb

    @pl.when(j == 0)
    def _():
        h_scr[...] = h0_ref[...]
        ext_scr[:, SUBLANES - ntail:SUBLANES, :] = cbuf_ref[...]

    xr = ua_ref[:, :, :d]
    gr = ua_ref[:, :, d:].reshape(rows, d)
    ext_scr[:, SUBLANES:, :] = xr
    xc = cb_ref[...] + cw_ref[ntail:ntail + 1, :] * xr
    for s in range(1, CONV_W):
        xc = xc + cw_ref[ntail - s:ntail - s + 1, :] * ext_scr[:, SUBLANES - s:SUBLANES - s + tb, :]
    xc = xc.reshape(rows, d)

    xcb = xc.astype(BF16)
    nt = d // MXU_DIM
    ra = jnp.concatenate(
        [jnp.dot(xcb[:, q * MXU_DIM:(q + 1) * MXU_DIM], wa_ref[q], preferred_element_type=F32)
         for q in range(nt)], axis=-1)
    ia = jnp.concatenate(
        [jnp.dot(xcb[:, q * MXU_DIM:(q + 1) * MXU_DIM], wi_ref[q], preferred_element_type=F32)
         for q in range(nt)], axis=-1)
    r = _sigmoid(ra + ba_ref[...])
    ig = _sigmoid(ia + bi_ref[...])
    log_a = (-RG_C) * r * _softplus(-lam_ref[...])
    a = jnp.exp(log_a)
    mult = jnp.sqrt(-jnp.tanh(log_a) * (a * a + 1.0))
    b = mult * (ig * xc)

    t_in = jnp.bitwise_and(lax.broadcasted_iota(jnp.int32, (rows, d), 0), tb - 1)
    h_prev = jnp.broadcast_to(h_scr[...], (bb, tb, d)).reshape(rows, d)
    b = b + jnp.where(t_in == 0, a * h_prev, 0.0)
    s = 1
    while s < tb:
        a_sh = jnp.where(t_in >= s, pltpu.roll(a, s, 0), 1.0)
        b_sh = jnp.where(t_in >= s, pltpu.roll(b, s, 0), 0.0)
        b = b + a * b_sh
        a = a * a_sh
        s *= 2
    ya_ref[...] = (b * jax.nn.gelu(gr, approximate=True)).reshape(bb, tb, d)
    h = b.reshape(bb, tb, d)
    h_scr[...] = h[:, tb - 1:tb, :]

    @pl.when(j == nblk - 1)
    def _():
        hlast_ref[...] = h[:, l_last - 1:l_last, :]
        cnew_ref[...] = ext_scr[:, SUBLANES + l_last - ntail:SUBLANES + l_last, :]

    tail = ext_scr[:, tb + SUBLANES - ntail:tb + SUBLANES, :]
    ext_scr[:, SUBLANES - ntail:SUBLANES, :] = tail


def _rglru(ua, cbuf, h0, cw, cb, wa4, ba, wi4, bi, lam, *, bb, tb, l_real):
    bsz, lp, n2 = ua.shape
    d = n2 // 2
    nblk = lp // tb
    l_last = l_real - (nblk - 1) * tb
    assert tb & (tb - 1) == 0 and bsz % bb == 0
    kern = functools.partial(_rglru_kernel, bb=bb, tb=tb, l_last=l_last, nblk=nblk, d=d)
    return pl.pallas_call(
        kern,
        grid=(bsz // bb, nblk),
        in_specs=[pl.BlockSpec((bb, tb, n2), lambda b, j: (b, j, 0)),
                  pl.BlockSpec((bb, CONV_W - 1, d), lambda b, j: (b, 0, 0)),
                  pl.BlockSpec((bb, 1, d), lambda b, j: (b, 0, 0)),
                  _const_spec(cw.shape), _const_spec(cb.shape),
                  _const_spec(wa4.shape), _const_spec(ba.shape),
                  _const_spec(wi4.shape), _const_spec(bi.shape), _const_spec(lam.shape)],
        out_specs=[pl.BlockSpec((bb, tb, d), lambda b, j: (b, j, 0)),
                   pl.BlockSpec((bb, 1, d), lambda b, j: (b, 0, 0)),
                   pl.BlockSpec((bb, CONV_W - 1, d), lambda b, j: (b, 0, 0))],
        out_shape=[jax.ShapeDtypeStruct((bsz, lp, d), F32),
                   jax.ShapeDtypeStruct((bsz, 1, d), F32),
                   jax.ShapeDtypeStruct((bsz, CONV_W - 1, d), F32)],
        scratch_shapes=[pltpu.VMEM((bb, tb + SUBLANES, d), F32), pltpu.VMEM((bb, 1, d), F32)],
        compiler_params=_params("arbitrary", "arbitrary"),
        name="rglru",
    )(ua, cbuf, h0, cw, cb, wa4, ba, wi4, bi, lam)


def _gla_kernel(ub_ref, s0_ref, wf_ref, bf_ref, g_ref, yb_ref, sout_ref, s_scr,
                *, bb, tb, chunk, l_real, nblk, dk, dv):
    j = pl.program_id(1)
    nh = GLA_HEADS
    qk = nh * dk
    vw = nh * dv

    @pl.when(j == 0)
    def _():
        s_scr[...] = s0_ref[...]

    off_k, off_v, off_g, off_f = qk, 2 * qk, 2 * qk + vw, 2 * qk + 2 * vw
    nc = tb // chunk
    shift = chunk.bit_length() - 1
    ri = lax.broadcasted_iota(jnp.int32, (tb, tb), 0)
    ci = lax.broadcasted_iota(jnp.int32, (tb, tb), 1)
    keep = (lax.shift_right_logical(ri, shift) == lax.shift_right_logical(ci, shift)) & (ri >= ci)
    tril = keep.astype(F32)
    row_chunk = lax.shift_right_logical(lax.broadcasted_iota(jnp.int32, (tb, dk), 0), shift)
    mid = chunk // 2
    scale = dk ** -0.5

    def by_chunk(a):
        if nc == 1:
            return a
        return jnp.concatenate([jnp.where(row_chunk == c, a, 0.0) for c in range(nc)], axis=1)

    def per_chunk_row(a, r):
        return jnp.concatenate([jnp.broadcast_to(a[c * chunk + r:c * chunk + r + 1, :], (chunk, a.shape[1]))
                                for c in range(nc)], axis=0)

    for bi in range(bb):
        fl = ub_ref[bi, :, off_f:off_f + LANES].astype(BF16)
        logit = jnp.dot(fl, wf_ref[...], preferred_element_type=F32) + bf_ref[...]
        log_f = (jnp.minimum(logit, 0.0) - jnp.log1p(jnp.exp(-jnp.abs(logit)))) * (1.0 / GLA_TAU)
        if l_real < tb * nblk:
            rows = lax.broadcasted_iota(jnp.int32, log_f.shape, 0) + j * tb
            log_f = jnp.where(rows < l_real, log_f, 0.0)
        bcum = jnp.dot(tril, log_f, precision=HIGHEST, preferred_element_type=F32)
        ref = per_chunk_row(bcum, mid)
        last = per_chunk_row(bcum, chunk - 1)
        e_inter = jnp.exp(bcum)
        e_q = jnp.exp(bcum - ref)
        e_k = jnp.exp(ref - bcum)
        e_dec = jnp.exp(last - bcum)

        for h in range(nh):
            hk = slice(h * dk, (h + 1) * dk)
            q = ub_ref[bi, :, h * dk:(h + 1) * dk] * scale
            k = ub_ref[bi, :, off_k + h * dk:off_k + (h + 1) * dk]
            vb = ub_ref[bi, :, off_v + h * dv:off_v + (h + 1) * dv].astype(BF16)
            go = ub_ref[bi, :, off_g + h * dv:off_g + (h + 1) * dv]
            att = lax.dot_general((q * e_q[:, hk]).astype(BF16), (k * e_k[:, hk]).astype(BF16),
                                  (((1,), (1,)), ((), ())), preferred_element_type=F32)
            att = jnp.where(keep, att, 0.0)
            o = jnp.dot(att.astype(BF16), vb, preferred_element_type=F32)
            kd = by_chunk(k * e_dec[:, hk])
            kv = lax.dot_general(kd.astype(BF16), vb, (((0,), (0,)), ((), ())),
                                 preferred_element_type=F32)
            states = [s_scr[bi, h]]
            for c in range(nc):
                drow = jnp.exp(bcum[c * chunk + chunk - 1:c * chunk + chunk, hk])
                dec = jnp.transpose(jnp.broadcast_to(drow, (dk, dk)))
                dec = jnp.concatenate([dec] * (dv // dk), axis=1)
                states.append(dec * states[c] + kv[c * dk:(c + 1) * dk])
            s_scr[bi, h] = states[nc]
            qi = by_chunk(q * e_inter[:, hk])
            s_in = jnp.concatenate(states[:nc], axis=0).astype(BF16)
            o = o + jnp.dot(qi.astype(BF16), s_in, preferred_element_type=F32)
            o = o * lax.rsqrt(jnp.mean(o * o, axis=-1, keepdims=True) + RMS_EPS) * g_ref[...]
            yb_ref[bi, :, h * dv:(h + 1) * dv] = o * (go * _sigmoid(go))

    @pl.when(j == nblk - 1)
    def _():
        sout_ref[...] = s_scr[...]


def _gla(ub, s0, wf, bf, g, *, bb, tb, chunk, l_real):
    bsz, lp, nb = ub.shape
    _, nh, dk, dv = s0.shape
    nblk = lp // tb
    assert bsz % bb == 0 and dv % dk == 0 and chunk & (chunk - 1) == 0 and tb % chunk == 0
    kern = functools.partial(_gla_kernel, bb=bb, tb=tb, chunk=chunk, l_real=l_real, nblk=nblk, dk=dk, dv=dv)
    return pl.pallas_call(
        kern,
        grid=(bsz // bb, nblk),
        in_specs=[pl.BlockSpec((bb, tb, nb), lambda b, j: (b, j, 0)),
                  pl.BlockSpec((bb, nh, dk, dv), lambda b, j: (b, 0, 0, 0)),
                  _const_spec(wf.shape), _const_spec(bf.shape), _const_spec(g.shape)],
        out_specs=[pl.BlockSpec((bb, tb, nh * dv), lambda b, j: (b, j, 0)),
                   pl.BlockSpec((bb, nh, dk, dv), lambda b, j: (b, 0, 0, 0))],
        out_shape=[jax.ShapeDtypeStruct((bsz, lp, nh * dv), F32),
                   jax.ShapeDtypeStruct((bsz, nh, dk, dv), F32)],
        scratch_shapes=[pltpu.VMEM((bb, nh, dk, dv), F32)],
        compiler_params=_params("arbitrary", "arbitrary"),
        name="gla",
    )(ub, s0, wf, bf, g)


def _mix_kernel(ya_ref, yb_ref, um_ref, x_ref, wo_ref, g_ref, b_ref, *rest, alpha, d):
    o_ref = rest[-1]
    merged = _sigmoid(um_ref[:, :d]) * ya_ref[...] + _sigmoid(um_ref[:, d:]) * yb_ref[...]
    mix = jnp.dot(merged.astype(BF16), wo_ref[...], preferred_element_type=F32)
    o_ref[...] = _layer_norm(alpha * x_ref[...] + mix, g_ref[...], b_ref[...])


def _mix(ya, yb, um, x, wo, g, b, *, alpha, tm, total_rows, row_offset=0, into=None):
    m, d = x.shape
    assert m % tm == 0 and row_offset % tm == 0
    row = lambda w: pl.BlockSpec((tm, w), lambda i: (i, 0))
    in_specs = [row(d), row(d), row(2 * d), row(d),
                _const_spec(wo.shape), _const_spec(g.shape), _const_spec(b.shape)]
    args = (ya, yb, um, x, wo, g, b)
    aliases = {}
    if into is not None:
        in_specs.append(pl.BlockSpec(memory_space=pl.ANY))
        aliases = {len(args): 0}
        args = args + (into,)
    return pl.pallas_call(
        functools.partial(_mix_kernel, alpha=alpha, d=d),
        grid=(m // tm,),
        in_specs=in_specs,
        out_specs=pl.BlockSpec((tm, d), lambda i: (i + row_offset // tm, 0)),
        out_shape=jax.ShapeDtypeStruct((total_rows, d), F32),
        input_output_aliases=aliases,
        compiler_params=_params("arbitrary"),
        name="mix_ln1",
    )(*args)


def _router_kernel(x_ref, wr_ref, bias_ref, eidx_ref, gate_ref, rank_ref, cnt_ref, run_scr,
                   *, tm, ne):
    i = pl.program_id(0)

    @pl.when(i == 0)
    def _():
        run_scr[...] = jnp.zeros_like(run_scr)

    logits = jnp.dot(x_ref[...], wr_ref[...], precision=HIGHEST, preferred_element_type=F32)
    s = _sigmoid(logits)
    sel = s + bias_ref[...]
    lane = lax.broadcasted_iota(jnp.int32, (tm, ne), 1)
    lane_f = lane.astype(F32)
    gsz = ne // N_GROUPS
    neg = -jnp.inf

    gscores = []
    for g in range(N_GROUPS):
        ing = (lane >= g * gsz) & (lane < (g + 1) * gsz)
        v = jnp.where(ing, sel, neg)
        m1 = jnp.max(v, axis=1, keepdims=True)
        i1 = jnp.min(jnp.where(v == m1, lane_f, float(ne)), axis=1, keepdims=True)
        m2 = jnp.max(jnp.where(lane_f == i1, neg, v), axis=1, keepdims=True)
        gscores.append(m1 + m2)
    cur = jnp.full((tm, ne), neg, F32)
    for g in range(N_GROUPS):
        ahead = jnp.zeros((tm, 1), F32)
        for g2 in range(N_GROUPS):
            if g2 == g:
                continue
            beats = (gscores[g2] >= gscores[g]) if g2 < g else (gscores[g2] > gscores[g])
            ahead = ahead + beats.astype(F32)
        ing = (lane >= g * gsz) & (lane < (g + 1) * gsz)
        cur = jnp.where(ing & (ahead < float(TOPK_GROUPS)), sel, cur)

    col = lax.broadcasted_iota(jnp.int32, (tm, TOP_K), 1)
    eidx = jnp.zeros((tm, TOP_K), F32)
    wsel = jnp.zeros((tm, TOP_K), F32)
    onehot_sum = jnp.zeros((tm, ne), F32)
    picks = []
    for k in range(TOP_K):
        m = jnp.max(cur, axis=1, keepdims=True)
        ik = jnp.min(jnp.where(cur == m, lane_f, float(ne)), axis=1, keepdims=True)
        hit = lane_f == ik
        wk = jnp.sum(jnp.where(hit, s, 0.0), axis=1, keepdims=True)
        cur = jnp.where(hit, neg, cur)
        onehot_sum = onehot_sum + hit.astype(F32)
        picks.append(hit)
        eidx = jnp.where(col == k, ik, eidx)
        wsel = jnp.where(col == k, wk, wsel)
    gate_ref[...] = wsel / jnp.sum(wsel, axis=1, keepdims=True) * ROUTED_SCALE
    eidx_ref[...] = eidx.astype(jnp.int32)

    ri = lax.broadcasted_iota(jnp.int32, (tm, tm), 0)
    ci = lax.broadcasted_iota(jnp.int32, (tm, tm), 1)
    before = (ci < ri).astype(BF16)
    counts = jnp.dot(before, onehot_sum.astype(BF16), preferred_element_type=F32) + run_scr[...]
    rank = jnp.zeros((tm, TOP_K), F32)
    for k in range(TOP_K):
        rk = jnp.sum(jnp.where(picks[k], counts, 0.0), axis=1, keepdims=True)
        rank = jnp.where(col == k, rk, rank)
    rank_ref[...] = rank.astype(jnp.int32)
    run_scr[...] = run_scr[...] + jnp.sum(onehot_sum, axis=0, keepdims=True)
    cnt_ref[...] = run_scr[...].astype(jnp.int32)


def _router(x, wr, bias, *, tm):
    t, d = x.shape
    ne = wr.shape[1]
    out_row = lambda: pl.BlockSpec((tm, TOP_K), lambda i: (i, 0))
    return pl.pallas_call(
        functools.partial(_router_kernel, tm=tm, ne=ne),
        grid=(t // tm,),
        in_specs=[pl.BlockSpec((tm, d), lambda i: (i, 0)), _const_spec(wr.shape), _const_spec(bias.shape)],
        out_specs=[out_row(), out_row(), out_row(), _const_spec((1, ne))],
        out_shape=[jax.ShapeDtypeStruct((t, TOP_K), jnp.int32),
                   jax.ShapeDtypeStruct((t, TOP_K), F32),
                   jax.ShapeDtypeStruct((t, TOP_K), jnp.int32),
                   jax.ShapeDtypeStruct((1, ne), jnp.int32)],
        scratch_shapes=[pltpu.VMEM((1, ne), F32)],
        compiler_params=_params("arbitrary"),
        name="router",
    )(x, wr, bias)


def _dest_kernel(eidx_ref, rank_ref, start_ref, dest_ref, *, tm, ne):
    lane = lax.broadcasted_iota(jnp.int32, (tm, ne), 1)
    col = lax.broadcasted_iota(jnp.int32, (tm, TOP_K), 1)
    starts = start_ref[...].astype(F32)
    eidx = eidx_ref[...]
    dest = jnp.zeros((tm, TOP_K), F32)
    for k in range(TOP_K):
        hit = lane == eidx[:, k:k + 1]
        sk = jnp.sum(jnp.where(hit, starts, 0.0), axis=1, keepdims=True)
        dest = jnp.where(col == k, sk, dest)
    dest_ref[...] = dest.astype(jnp.int32) + rank_ref[...]


def _dest(eidx, rank, starts, *, tm):
    t = eidx.shape[0]
    ne = starts.shape[1]
    row = lambda: pl.BlockSpec((tm, TOP_K), lambda i: (i, 0))
    return pl.pallas_call(
        functools.partial(_dest_kernel, tm=tm, ne=ne),
        grid=(t // tm,),
        in_specs=[row(), row(), _const_spec((1, ne))],
        out_specs=row(),
        out_shape=jax.ShapeDtypeStruct((t, TOP_K), jnp.int32),
        compiler_params=_params("arbitrary"),
        name="dest",
    )(eidx, rank, starts)


def _router_t_kernel(x_ref, whi_ref, wlo_ref, bias_ref, eidx_ref, gate_ref, rank_ref, cnt_ref, run_scr,
                     *, tm, ne):
    i = pl.program_id(0)

    @pl.when(i == 0)
    def _():
        run_scr[...] = jnp.zeros_like(run_scr)

    x = x_ref[...]
    x_hi = x.astype(BF16)
    x_lo = (x - x_hi.astype(F32)).astype(BF16)
    nt = (((1,), (1,)), ((), ()))
    logits = (lax.dot_general(whi_ref[...], x_hi, nt, preferred_element_type=F32)
              + lax.dot_general(whi_ref[...], x_lo, nt, preferred_element_type=F32)
              + lax.dot_general(wlo_ref[...], x_hi, nt, preferred_element_type=F32))
    s = _sigmoid(logits)
    sel = s + bias_ref[...]
    row = lax.broadcasted_iota(jnp.int32, (ne, tm), 0).astype(F32)
    gsz = ne // N_GROUPS
    neg = -jnp.inf

    def colmax(a):
        return jnp.max(a, axis=0, keepdims=True)

    def colmin(a):
        return jnp.min(a, axis=0, keepdims=True)

    gscores = []
    row_in_group = lax.broadcasted_iota(jnp.int32, (gsz, tm), 0).astype(F32)
    for g in range(N_GROUPS):
        v = sel[g * gsz:(g + 1) * gsz]
        rg = row_in_group + float(g * gsz)
        m1 = colmax(v)
        i1 = colmin(jnp.where(v == m1, rg, float(ne)))
        gscores.append(m1 + colmax(jnp.where(rg == i1, neg, v)))
    kept = []
    for g in range(N_GROUPS):
        ahead = jnp.zeros((1, tm), F32)
        for g2 in range(N_GROUPS):
            if g2 == g:
                continue
            beats = (gscores[g2] >= gscores[g]) if g2 < g else (gscores[g2] > gscores[g])
            ahead = ahead + beats.astype(F32)
        keep = jnp.broadcast_to(ahead < float(TOPK_GROUPS), (gsz, tm))
        kept.append(jnp.where(keep, sel[g * gsz:(g + 1) * gsz], neg))
    cur = jnp.concatenate(kept, axis=0)

    picks, eidx, wsel = [], [], []
    onehot_sum = jnp.zeros((ne, tm), F32)
    for k in range(TOP_K):
        ik = colmin(jnp.where(cur == colmax(cur), row, float(ne)))
        hit = row == ik
        wsel.append(jnp.sum(jnp.where(hit, s, 0.0), axis=0, keepdims=True))
        cur = jnp.where(hit, neg, cur)
        onehot_sum = onehot_sum + hit.astype(F32)
        picks.append(hit)
        eidx.append(ik)
    wsel = jnp.concatenate(wsel, axis=0)
    gate_ref[...] = wsel / jnp.sum(wsel, axis=0, keepdims=True) * ROUTED_SCALE
    eidx_ref[...] = jnp.concatenate(eidx, axis=0).astype(jnp.int32)

    ri = lax.broadcasted_iota(jnp.int32, (tm, tm), 0)
    ci = lax.broadcasted_iota(jnp.int32, (tm, tm), 1)
    earlier = (ri < ci).astype(BF16)
    ohb = onehot_sum.astype(BF16)
    counts = jnp.dot(ohb, earlier, preferred_element_type=F32) + jnp.concatenate([run_scr[...]] * (tm // LANES), axis=1)
    rank = [jnp.sum(jnp.where(picks[k], counts, 0.0), axis=0, keepdims=True) for k in range(TOP_K)]
    rank_ref[...] = jnp.concatenate(rank, axis=0).astype(jnp.int32)
    run_scr[...] = run_scr[...] + jnp.dot(ohb, jnp.ones((tm, LANES), BF16), preferred_element_type=F32)
    cnt_ref[...] = run_scr[...].astype(jnp.int32)


def _router_t(x, w_hi, w_lo, bias_b, *, tm):
    t, d = x.shape
    ne = w_hi.shape[0]
    out_col = lambda: pl.BlockSpec((TOP_K, tm), lambda i: (0, i))
    return pl.pallas_call(
        functools.partial(_router_t_kernel, tm=tm, ne=ne),
        grid=(t // tm,),
        in_specs=[pl.BlockSpec((tm, d), lambda i: (i, 0)), _const_spec(w_hi.shape), _const_spec(w_lo.shape),
                  _const_spec(bias_b.shape)],
        out_specs=[out_col(), out_col(), out_col(), _const_spec((ne, LANES))],
        out_shape=[jax.ShapeDtypeStruct((TOP_K, t), jnp.int32),
                   jax.ShapeDtypeStruct((TOP_K, t), F32),
                   jax.ShapeDtypeStruct((TOP_K, t), jnp.int32),
                   jax.ShapeDtypeStruct((ne, LANES), jnp.int32)],
        scratch_shapes=[pltpu.VMEM((ne, LANES), F32)],
        compiler_params=_params("arbitrary"),
        name="router",
    )(x, w_hi, w_lo, bias_b)


def _dest_t_kernel(eidx_ref, rank_ref, start_ref, dest_ref, *, tm, ne):
    row = lax.broadcasted_iota(jnp.int32, (ne, tm), 0)
    starts = start_ref[...].astype(F32)
    eidx = eidx_ref[...]
    dest = [jnp.sum(jnp.where(row == eidx[k:k + 1, :], starts, 0.0), axis=0, keepdims=True) for k in range(TOP_K)]
    dest_ref[...] = jnp.concatenate(dest, axis=0).astype(jnp.int32) + rank_ref[...]


def _dest_t(eidx, rank, starts_b, *, tm):
    t = eidx.shape[1]
    ne = starts_b.shape[0]
    col = lambda: pl.BlockSpec((TOP_K, tm), lambda i: (0, i))
    return pl.pallas_call(
        functools.partial(_dest_t_kernel, tm=tm, ne=ne),
        grid=(t // tm,),
        in_specs=[col(), col(), _const_spec(starts_b.shape)],
        out_specs=col(),
        out_shape=jax.ShapeDtypeStruct((TOP_K, t), jnp.int32),
        compiler_params=_params("arbitrary"),
        name="dest",
    )(eidx, rank, starts_b)


def _to_token_tiles(rows):
    chunks = jnp.stack([rows[:, c * LANES:(c + 1) * LANES] for c in range(SUBLANES)], axis=0)
    return jnp.swapaxes(chunks, 0, 1)


def _chunks_of_token_tiles(tiles):
    chunks = jnp.swapaxes(tiles, 0, 1)
    return [chunks[c] for c in range(SUBLANES)]


def _token_copy(src_ref, src_row, dst_ref, dst_row, sem):
    return pltpu.make_async_copy(src_ref.at[pl.ds(src_row, 1)], dst_ref.at[pl.ds(dst_row, 1)], sem)


def _dispatch_kernel(dest_ref, x_ref, xs_ref, xt, sem, *, tt):
    xt[...] = _to_token_tiles(x_ref[...])

    def start(t, c):
        for k in range(TOP_K):
            _token_copy(xt, t, xs_ref, dest_ref[t * TOP_K + k], sem).start(priority=k % 2)
        return c

    lax.fori_loop(0, tt, start, 0)
    for _ in range(TOP_K):
        pltpu.make_async_copy(xt, xs_ref.at[pl.ds(0, tt)], sem).wait()


def _dispatch(dest_flat, x, n_slots, *, tt):
    t, d = x.shape
    assert d == SUBLANES * LANES
    return pl.pallas_call(
        functools.partial(_dispatch_kernel, tt=tt),
        grid=(t // tt,),
        in_specs=[pl.BlockSpec((tt * TOP_K,), lambda i: (i,), memory_space=pltpu.SMEM),
                  pl.BlockSpec((tt, d), lambda i: (i, 0))],
        out_specs=pl.BlockSpec(memory_space=pl.ANY),
        out_shape=jax.ShapeDtypeStruct((n_slots, SUBLANES, LANES), F32),
        scratch_shapes=[pltpu.VMEM((tt, SUBLANES, LANES), F32), pltpu.SemaphoreType.DMA(())],
        compiler_params=_params("arbitrary"),
        name="dispatch",
    )(dest_flat, x)


EXPERT_SLOTS = 8
EXPERT_AHEAD = 6


def _expert_kernel(base_ref, nblk_ref, cnt_ref, nu_ref, xs_ref, wg_ref, wu_ref, wd_ref, ys_ref,
                   xbuf, ybuf, wgb, wub, wdb, sem_in, sem_out):
    e = pl.program_id(0)
    n_used = nu_ref[0]
    nb = nblk_ref[e]
    base = base_ref[e]
    cnt = cnt_ref[e]
    ahead = EXPERT_AHEAD

    def x_copy(blk, slot):
        return pltpu.make_async_copy(xs_ref.at[pl.ds(blk * MOE_ROWS, MOE_ROWS)], xbuf.at[slot], sem_in.at[slot])

    def y_copy(blk, slot):
        return pltpu.make_async_copy(ybuf.at[slot], ys_ref.at[pl.ds(blk * MOE_ROWS, MOE_ROWS)], sem_out.at[slot])

    @pl.when(e == 0)
    def _():
        for first in range(ahead):
            @pl.when(first < n_used)
            def _():
                x_copy(first, first).start()

    @pl.when(nb > 0)
    def _():
        wgb[...] = wg_ref[0].astype(BF16)
        wub[...] = wu_ref[0].astype(BF16)
        wdb[...] = wd_ref[0].astype(BF16)

    def step(jb, nblocks):
        blks = [base + jb + i for i in range(nblocks)]
        slots = [jnp.bitwise_and(b, EXPERT_SLOTS - 1) for b in blks]
        for b, s in zip(blks, slots):
            x_copy(b, s).wait()
        for b in blks:
            @pl.when(b + ahead < n_used)
            def _():
                x_copy(b + ahead, jnp.bitwise_and(b + ahead, EXPERT_SLOTS - 1)).start()

        x = jnp.concatenate([jnp.concatenate(_chunks_of_token_tiles(xbuf[s]), axis=1) for s in slots], axis=0)
        rows = lax.broadcasted_iota(jnp.int32, x.shape, 0)
        xb = jnp.where(rows < cnt - jb * MOE_ROWS, x, 0.0).astype(BF16)
        g = jnp.dot(xb, wgb[...], preferred_element_type=F32)
        u = jnp.dot(xb, wub[...], preferred_element_type=F32)
        hmid = (g * _sigmoid(g)) * u
        y = jnp.dot(hmid.astype(BF16), wdb[...], preferred_element_type=F32)

        for i, (b, s) in enumerate(zip(blks, slots)):
            @pl.when(b >= EXPERT_SLOTS)
            def _():
                y_copy(b - EXPERT_SLOTS, s).wait()

            ybuf[s] = _to_token_tiles(y[i * MOE_ROWS:(i + 1) * MOE_ROWS])
            y_copy(b, s).start()

    def pair(jp, c):
        step(2 * jp, 2)
        return c

    lax.fori_loop(0, lax.shift_right_logical(nb, 1), pair, 0)

    @pl.when(jnp.bitwise_and(nb, 1) == 1)
    def _():
        step(nb - 1, 1)

    @pl.when(e == pl.num_programs(0) - 1)
    def _():
        for back in range(EXPERT_SLOTS):
            @pl.when(n_used > back)
            def _():
                last = n_used - 1 - back
                y_copy(last, jnp.bitwise_and(last, EXPERT_SLOTS - 1)).wait()


def _experts(base, nblk, cnt, n_used, xs, wg, wu, wd):
    p = xs.shape[0]
    ne, d, de = wg.shape
    assert EXPERT_SLOTS & (EXPERT_SLOTS - 1) == 0 and EXPERT_AHEAD <= EXPERT_SLOTS - 2 and d == SUBLANES * LANES
    ring = pltpu.VMEM((EXPERT_SLOTS, MOE_ROWS, SUBLANES, LANES), F32)
    grid_spec = pltpu.PrefetchScalarGridSpec(
        num_scalar_prefetch=4,
        grid=(ne,),
        in_specs=[pl.BlockSpec(memory_space=pl.ANY),
                  pl.BlockSpec((1, d, de), lambda e, *_: (e, 0, 0)),
                  pl.BlockSpec((1, d, de), lambda e, *_: (e, 0, 0)),
                  pl.BlockSpec((1, de, d), lambda e, *_: (e, 0, 0))],
        out_specs=pl.BlockSpec(memory_space=pl.ANY),
        scratch_shapes=[ring, ring,
                        pltpu.VMEM((d, de), BF16), pltpu.VMEM((d, de), BF16), pltpu.VMEM((de, d), BF16),
                        pltpu.SemaphoreType.DMA((EXPERT_SLOTS,)), pltpu.SemaphoreType.DMA((EXPERT_SLOTS,))],
    )
    return pl.pallas_call(
        _expert_kernel,
        grid_spec=grid_spec,
        out_shape=jax.ShapeDtypeStruct((p, SUBLANES, LANES), F32),
        compiler_params=_params("arbitrary"),
        name="experts",
    )(base, nblk, cnt, n_used, xs, wg, wu, wd)


def _combine_kernel(dest_ref, dest_next_ref, x_ref, gate_ref, ys_ref, wsg_ref, wsu_ref, wsd_ref, g_ref, b_ref,
                    o_head_ref, o_tail_ref, ybuf, sem, *, tt, alpha, head_steps):
    i = pl.program_id(0)
    slot = jnp.bitwise_and(i, 1)

    def gather(dref, s):
        def start(t, c):
            for k in range(TOP_K):
                _token_copy(ys_ref, dref[t * TOP_K + k], ybuf.at[s], k * tt + t, sem.at[s]).start(priority=k % 2)
            return c
        lax.fori_loop(0, tt, start, 0)

    @pl.when(i == 0)
    def _():
        gather(dest_ref, 0)

    @pl.when(i + 1 < pl.num_programs(0))
    def _():
        gather(dest_next_ref, 1 - slot)

    x = x_ref[...]
    xb = x.astype(BF16)
    sg = jnp.dot(xb, wsg_ref[...], preferred_element_type=F32)
    su = jnp.dot(xb, wsu_ref[...], preferred_element_type=F32)
    acc = jnp.dot(((sg * _sigmoid(sg)) * su).astype(BF16), wsd_ref[...], preferred_element_type=F32)
    pltpu.make_async_copy(ys_ref.at[pl.ds(0, tt * TOP_K)], ybuf.at[slot], sem.at[slot]).wait()
    gate = gate_ref[...]
    routed = None
    for k in range(TOP_K):
        gk = jnp.broadcast_to(gate[:, k:k + 1], (tt, LANES))
        part = [gk * ch for ch in _chunks_of_token_tiles(ybuf[slot, k * tt:(k + 1) * tt])]
        routed = part if routed is None else [r + p for r, p in zip(routed, part)]
    acc = acc + jnp.concatenate(routed, axis=1)
    out = _layer_norm(alpha * x + acc, g_ref[...], b_ref[...])

    @pl.when(i < head_steps)
    def _():
        o_head_ref[...] = out

    @pl.when(i >= head_steps)
    def _():
        o_tail_ref[...] = out


def _combine(dest_flat, x, gate, ys, wsg, wsu, wsd, g, b, *, tt, alpha, head_rows):
    t, d = x.shape
    nsteps = t // tt
    head_steps = head_rows // tt
    assert head_rows % tt == 0 and 0 < head_steps < nsteps
    return pl.pallas_call(
        functools.partial(_combine_kernel, tt=tt, alpha=alpha, head_steps=head_steps),
        grid=(nsteps,),
        in_specs=[pl.BlockSpec((tt * TOP_K,), lambda i: (i,), memory_space=pltpu.SMEM),
                  pl.BlockSpec((tt * TOP_K,), lambda i: (jnp.minimum(i + 1, nsteps - 1),),
                               memory_space=pltpu.SMEM),
                  pl.BlockSpec((tt, d), lambda i: (i, 0)),
                  pl.BlockSpec((tt, TOP_K), lambda i: (i, 0)),
                  pl.BlockSpec(memory_space=pl.ANY),
                  _const_spec(wsg.shape), _const_spec(wsu.shape), _const_spec(wsd.shape),
                  _const_spec(g.shape), _const_spec(b.shape)],
        out_specs=[pl.BlockSpec((tt, d), lambda i: (jnp.minimum(i, head_steps - 1), 0)),
                   pl.BlockSpec((tt, d), lambda i: (jnp.maximum(i - head_steps, 0), 0))],
        out_shape=[jax.ShapeDtypeStruct((head_rows, d), F32), jax.ShapeDtypeStruct((t - head_rows, d), F32)],
        scratch_shapes=[pltpu.VMEM((2, tt * TOP_K, SUBLANES, LANES), F32), pltpu.SemaphoreType.DMA((2,))],
        compiler_params=_params("arbitrary"),
        name="combine_ln2",
    )(dest_flat, dest_flat, x, gate, ys, wsg, wsu, wsd, g, b)


def _block_diag_tiles(w):
    n, s, _ = w.shape
    per = MXU_DIM // s
    w = w.reshape(n // per, per, s, s)
    eye = jnp.eye(per, dtype=w.dtype)
    return jnp.einsum('tpij,pq->tpiqj', w, eye).reshape(n // per, MXU_DIM, MXU_DIM)


def _row(v):
    return v.reshape(1, -1)


def _moe(x1, w_router, router_bias, w_gate, w_up, w_down, ws_gate, ws_up, ws_down, ln_g, ln_b, alpha, head_rows):
    t, d = x1.shape
    ne = w_router.shape[1]
    tm = 256
    w_t = w_router.T
    w_hi = w_t.astype(BF16)
    w_lo = (w_t - w_hi.astype(F32)).astype(BF16)
    bias_b = jnp.broadcast_to(router_bias.reshape(ne, 1), (ne, tm))
    eidx_t, gate_t, rank_t, counts = _router_t(x1, w_hi, w_lo, bias_b, tm=tm)
    counts = counts[:, 0]
    nblk = (counts + MOE_ROWS - 1) // MOE_ROWS
    blk_end = jnp.cumsum(nblk)
    blk_base = (blk_end - nblk).astype(jnp.int32)
    n_used = blk_end[-1:].astype(jnp.int32)
    n_slots = ((t * TOP_K) // MOE_ROWS + ne) * MOE_ROWS

    starts_b = jnp.broadcast_to((blk_base * MOE_ROWS).reshape(ne, 1), (ne, tm))
    dest_flat = _dest_t(eidx_t, rank_t, starts_b, tm=tm).T.reshape(-1)
    xs = _dispatch(dest_flat, x1, n_slots, tt=256)
    ys = _experts(blk_base, nblk.astype(jnp.int32), counts, n_used, xs, w_gate, w_up, w_down)
    return _combine(dest_flat, x1, gate_t.T, ys, ws_gate.astype(BF16), ws_up.astype(BF16),
                    ws_down.astype(BF16), _row(ln_g), _row(ln_b), tt=128, alpha=alpha, head_rows=head_rows)


def _pad_time(a, lp):
    return jnp.pad(a, ((0, 0), (0, lp - a.shape[1]), (0, 0)))


def kernel(x_prompt, x_sample, state_rglru_h, state_conv, state_gla, meta_tokens, w_in, conv_w, conv_b,
           rg_wa, rg_ba, rg_wi, rg_bi, rg_lambda, gla_wf2, gla_bf, gla_norm_g, w_out, ln1_g, ln1_b,
           w_router, router_bias, w_gate, w_up, w_down, ws_gate, ws_up, ws_down, ln2_g, ln2_b):
    bp, lp, d = x_prompt.shape
    bs, ls, _ = x_sample.shape
    depth = w_in.shape[0]
    nh, dk, dv = state_gla.shape[2:]
    qk, vw = nh * dk, nh * dv
    lowrank = gla_wf2.shape[1]
    alpha = (2.0 * depth) ** 0.25
    ls_pad = SUBLANES
    assert depth == 1 and ls <= ls_pad and lp % 256 == 0 and N_META % SUBLANES == 0

    xp = x_prompt.reshape(bp * lp, d)
    xs = x_sample.reshape(bs * ls, d)
    xm = meta_tokens.astype(F32)

    hp_l, cp_l, sp_l, hs_l, cs_l, ss_l = [], [], [], [], [], []
    for l in range(depth):
        o1, o2 = 2 * d, 2 * d + 2 * qk + 2 * vw
        wa = w_in[l][:, :o1].astype(BF16)
        wb = jnp.pad(w_in[l][:, o1:o2 + lowrank], ((0, 0), (0, LANES - lowrank))).astype(BF16)
        wm = w_in[l][:, o2 + lowrank:].astype(BF16)
        wf = jnp.pad(gla_wf2[l], ((0, LANES - lowrank), (0, 0))).astype(BF16)
        wa4 = _block_diag_tiles(rg_wa[l]).astype(BF16)
        wi4 = _block_diag_tiles(rg_wi[l]).astype(BF16)
        mixer_w = (conv_w[l], _row(conv_b[l]), wa4, _row(rg_ba[l]), wi4, _row(rg_bi[l]), _row(rg_lambda[l]))
        gla_w = (wf, _row(gla_bf[l]), _row(gla_norm_g[l]))

        ua, ub, _ = _inproj(xm, wa, wb, wm, tm=N_META)
        _, h_m, c_m = _rglru(ua[None], jnp.zeros((1, CONV_W - 1, d), F32), jnp.zeros((1, 1, d), F32),
                             *mixer_w, bb=1, tb=N_META, l_real=N_META)
        _, s_m = _gla(ub[None], jnp.zeros((1, nh, dk, dv), F32), *gla_w,
                      bb=1, tb=N_META, chunk=N_META, l_real=N_META)

        ua, ub, um = _inproj(xp, wa, wb, wm, tm=256)
        ya, h_p, c_p = _rglru(ua.reshape(bp, lp, -1), jnp.broadcast_to(c_m, (bp,) + c_m.shape[1:]),
                              jnp.broadcast_to(h_m, (bp,) + h_m.shape[1:]), *mixer_w,
                              bb=1, tb=256, l_real=lp)
        yb, s_p = _gla(ub.reshape(bp, lp, -1), jnp.broadcast_to(s_m, (bp,) + s_m.shape[1:]), *gla_w,
                       bb=1, tb=256, chunk=GLA_CHUNK, l_real=lp)
        n_tok = bp * lp + bs * ls
        x1 = _mix(ya.reshape(bp * lp, d), yb.reshape(bp * lp, d), um, xp, w_out[l].astype(BF16),
                  _row(ln1_g[l]), _row(ln1_b[l]), alpha=alpha, tm=512, total_rows=n_tok)

        xs_pad = _pad_time(xs.reshape(bs, ls, d), ls_pad).reshape(bs * ls_pad, d)
        ua, ub, um = _inproj(xs_pad, wa, wb, wm, tm=256)
        ya, h_s, c_s = _rglru(ua.reshape(bs, ls_pad, -1), state_conv[l].astype(F32),
                              state_rglru_h[l].astype(F32)[:, None, :], *mixer_w,
                              bb=32, tb=ls_pad, l_real=ls)
        yb, s_s = _gla(ub.reshape(bs, ls_pad, -1), state_gla[l].astype(F32), *gla_w,
                       bb=8, tb=ls_pad, chunk=ls_pad, l_real=ls)
        unpad = lambda a: a.reshape(bs, ls_pad, -1)[:, :ls].reshape(bs * ls, -1)
        x1 = _mix(unpad(ya), unpad(yb), unpad(um), xs, w_out[l].astype(BF16),
                  _row(ln1_g[l]), _row(ln1_b[l]), alpha=alpha, tm=512, total_rows=n_tok,
                  row_offset=bp * lp, into=x1)

        xp, xs = _moe(x1, w_router[l], router_bias[l], w_gate[l], w_up[l], w_down[l],
                      ws_gate[l], ws_up[l], ws_down[l], ln2_g[l], ln2_b[l], alpha, bp * lp)

        hp_l.append(h_p[:, 0]); cp_l.append(c_p); sp_l.append(s_p)
        hs_l.append(h_s[:, 0]); cs_l.append(c_s); ss_l.append(s_s)

    y_prompt = xp.reshape(bp, lp, d)
    y_sample = xs.reshape(bs, ls, d)
    return (y_prompt, y_sample,
            jnp.stack(hp_l).astype(state_rglru_h.dtype), jnp.stack(cp_l).astype(state_conv.dtype),
            jnp.stack(sp_l).astype(state_gla.dtype),
            jnp.stack(hs_l).astype(state_rglru_h.dtype), jnp.stack(cs_l).astype(state_conv.dtype),
            jnp.stack(ss_l).astype(state_gla.dtype))
```

```python
import functools

import jax
import jax.numpy as jnp
from jax import lax
from jax.experimental import pallas as pl
from jax.experimental.pallas import tpu as pltpu

F32 = jnp.float32
BF16 = jnp.bfloat16
HIGHEST = lax.Precision.HIGHEST

N_META = 16
CONV_W = 4
RG_C = 8.0
RNN_BLOCKS = 16
GLA_HEADS = 4
GLA_TAU = 16.0
GLA_CHUNK = 64
N_GROUPS = 8
TOPK_GROUPS = 4
TOP_K = 8
ROUTED_SCALE = 2.5
LN_EPS = 1e-5
RMS_EPS = 1e-6

LANES = 128
SUBLANES = 8
MXU_DIM = 256
VMEM_LIMIT_BYTES = 56 * 1024 * 1024

MOE_ROWS = 128
ISSUE_UNROLL = 4


def _params(*sem):
    return pltpu.CompilerParams(dimension_semantics=sem, vmem_limit_bytes=VMEM_LIMIT_BYTES)


def _sigmoid(x):
    return jax.nn.sigmoid(x)


def _softplus(z):
    return jnp.maximum(z, 0.0) + jnp.log1p(jnp.exp(-jnp.abs(z)))


def _layer_norm(z, g, b):
    mu = jnp.mean(z, axis=-1, keepdims=True)
    zc = z - mu
    var = jnp.mean(zc * zc, axis=-1, keepdims=True)
    return zc * lax.rsqrt(var + LN_EPS) * g + b


def _const_spec(shape):
    nd = len(shape)
    return pl.BlockSpec(shape, lambda *_: (0,) * nd)


def _inproj_kernel(x_ref, wa_ref, wb_ref, wm_ref, ua_ref, ub_ref, um_ref):
    xb = x_ref[...].astype(BF16)
    ua_ref[...] = jnp.dot(xb, wa_ref[...], preferred_element_type=F32)
    ub_ref[...] = jnp.dot(xb, wb_ref[...], preferred_element_type=F32)
    um_ref[...] = jnp.dot(xb, wm_ref[...], preferred_element_type=F32)


def _inproj(x2d, wa, wb, wm, tm):
    m, d = x2d.shape
    na, nb, nm = wa.shape[1], wb.shape[1], wm.shape[1]
    return pl.pallas_call(
        _inproj_kernel,
        grid=(m // tm,),
        in_specs=[pl.BlockSpec((tm, d), lambda i: (i, 0)),
                  _const_spec(wa.shape), _const_spec(wb.shape), _const_spec(wm.shape)],
        out_specs=[pl.BlockSpec((tm, na), lambda i: (i, 0)),
                   pl.BlockSpec((tm, nb), lambda i: (i, 0)),
                   pl.BlockSpec((tm, nm), lambda i: (i, 0))],
        out_shape=[jax.ShapeDtypeStruct((m, na), F32),
                   jax.ShapeDtypeStruct((m, nb), F32),
                   jax.ShapeDtypeStruct((m, nm), F32)],
        compiler_params=_params("arbitrary"),
        name="inproj",
    )(x2d, wa, wb, wm)


def _rglru_kernel(ua_ref, cbuf_ref, h0_ref, cw_ref, cb_ref, wa_ref, ba_ref, wi_ref, bi_ref, lam_ref,
                  ya_ref, hlast_ref, cnew_ref, ext_scr, h_scr, *, bb, tb, l_last, nblk, d):
    j = pl.program_id(1)
    ntail = CONV_W - 1
    rows = bb * tb

    @pl.when(j == 0)
    def _():
        h_scr[...] = h0_ref[...]
        ext_scr[:, SUBLANES - ntail:SUBLANES, :] = cbuf_ref[...]

    xr = ua_ref[:, :, :d]
    gr = ua_ref[:, :, d:].reshape(rows, d)
    ext_scr[:, SUBLANES:, :] = xr
    xc = cb_ref[...] + cw_ref[ntail:ntail + 1, :] * xr
    for s in range(1, CONV_W):
        xc = xc + cw_ref[ntail - s:ntail - s + 1, :] * ext_scr[:, SUBLANES - s:SUBLANES - s + tb, :]
    xc = xc.reshape(rows, d)

    xcb = xc.astype(BF16)
    nt = d // MXU_DIM
    ra = jnp.concatenate(
        [jnp.dot(xcb[:, q * MXU_DIM:(q + 1) * MXU_DIM], wa_ref[q], preferred_element_type=F32)
         for q in range(nt)], axis=-1)
    ia = jnp.concatenate(
        [jnp.dot(xcb[:, q * MXU_DIM:(q + 1) * MXU_DIM], wi_ref[q], preferred_element_type=F32)
         for q in range(nt)], axis=-1)
    r = _sigmoid(ra + ba_ref[...])
    ig = _sigmoid(ia + bi_ref[...])
    log_a = (-RG_C) * r * _softplus(-lam_ref[...])
    a = jnp.exp(log_a)
    z = -jnp.tanh(log_a) * (a * a + 1.0)
    mult = jnp.where(z > 0.0, z * lax.rsqrt(z), 0.0)
    b = mult * (ig * xc)

    t_in = jnp.bitwise_and(lax.broadcasted_iota(jnp.int32, (rows, d), 0), SUBLANES - 1)
    s = 1
    while s < SUBLANES:
        a_sh = jnp.where(t_in >= s, pltpu.roll(a, s, 0), 1.0)
        b_sh = jnp.where(t_in >= s, pltpu.roll(b, s, 0), 0.0)
        b = b + a * b_sh
        a = a * a_sh
        s *= 2
    a = a.reshape(bb, tb, d)
    b = b.reshape(bb, tb, d)
    carry = h_scr[...]
    tiles = []
    for i in range(tb // SUBLANES):
        rs = slice(i * SUBLANES, (i + 1) * SUBLANES)
        hi = b[:, rs, :] + a[:, rs, :] * carry
        carry = hi[:, SUBLANES - 1:SUBLANES, :]
        tiles.append(hi)
    h = jnp.concatenate(tiles, axis=1)
    h_scr[...] = carry
    ya_ref[...] = h * jax.nn.gelu(gr, approximate=True).reshape(bb, tb, d)

    @pl.when(j == nblk - 1)
    def _():
        hlast_ref[...] = h[:, l_last - 1:l_last, :]
        cnew_ref[...] = ext_scr[:, SUBLANES + l_last - ntail:SUBLANES + l_last, :]

    tail = ext_scr[:, tb + SUBLANES - ntail:tb + SUBLANES, :]
    ext_scr[:, SUBLANES - ntail:SUBLANES, :] = tail


def _rglru(ua, cbuf, h0, cw, cb, wa4, ba, wi4, bi, lam, *, bb, tb, l_real):
    bsz, lp, n2 = ua.shape
    d = n2 // 2
    nblk = lp // tb
    l_last = l_real - (nblk - 1) * tb
    assert tb & (tb - 1) == 0 and bsz % bb == 0
    kern = functools.partial(_rglru_kernel, bb=bb, tb=tb, l_last=l_last, nblk=nblk, d=d)
    return pl.pallas_call(
        kern,
        grid=(bsz // bb, nblk),
        in_specs=[pl.BlockSpec((bb, tb, n2), lambda b, j: (b, j, 0)),
                  pl.BlockSpec((bb, CONV_W - 1, d), lambda b, j: (b, 0, 0)),
                  pl.BlockSpec((bb, 1, d), lambda b, j: (b, 0, 0)),
                  _const_spec(cw.shape), _const_spec(cb.shape),
                  _const_spec(wa4.shape), _const_spec(ba.shape),
                  _const_spec(wi4.shape), _const_spec(bi.shape), _const_spec(lam.shape)],
        out_specs=[pl.BlockSpec((bb, tb, d), lambda b, j: (b, j, 0)),
                   pl.BlockSpec((bb, 1, d), lambda b, j: (b, 0, 0)),
                   pl.BlockSpec((bb, CONV_W - 1, d), lambda b, j: (b, 0, 0))],
        out_shape=[jax.ShapeDtypeStruct((bsz, lp, d), F32),
                   jax.ShapeDtypeStruct((bsz, 1, d), F32),
                   jax.ShapeDtypeStruct((bsz, CONV_W - 1, d), F32)],
        scratch_shapes=[pltpu.VMEM((bb, tb + SUBLANES, d), F32), pltpu.VMEM((bb, 1, d), F32)],
        compiler_params=_params("arbitrary", "arbitrary"),
        name="rglru",
    )(ua, cbuf, h0, cw, cb, wa4, ba, wi4, bi, lam)


def _gla_kernel(ub_ref, s0_ref, wf_ref, bf_ref, g_ref, yb_ref, sout_ref, s_scr,
                *, bb, tb, chunk, l_real, nblk, dk, dv):
    j = pl.program_id(1)
    nh = GLA_HEADS
    qk = nh * dk
    vw = nh * dv

    @pl.when(j == 0)
    def _():
        s_scr[...] = s0_ref[...]

    off_k, off_v, off_g, off_f = qk, 2 * qk, 2 * qk + vw, 2 * qk + 2 * vw
    nc = tb // chunk
    shift = chunk.bit_length() - 1
    ri = lax.broadcasted_iota(jnp.int32, (tb, tb), 0)
    ci = lax.broadcasted_iota(jnp.int32, (tb, tb), 1)
    keep = (lax.shift_right_logical(ri, shift) == lax.shift_right_logical(ci, shift)) & (ri >= ci)
    tril = keep.astype(F32)
    row_chunk = lax.shift_right_logical(lax.broadcasted_iota(jnp.int32, (tb, dk), 0), shift)
    mid = chunk // 2
    scale = dk ** -0.5

    def by_chunk(a):
        if nc == 1:
            return a
        return jnp.concatenate([jnp.where(row_chunk == c, a, 0.0) for c in range(nc)], axis=1)

    def per_chunk_row(a, r):
        return jnp.concatenate([jnp.broadcast_to(a[c * chunk + r:c * chunk + r + 1, :], (chunk, a.shape[1]))
                                for c in range(nc)], axis=0)

    for bi in range(bb):
        fl = ub_ref[bi, :, off_f:off_f + LANES].astype(BF16)
        logit = jnp.dot(fl, wf_ref[...], preferred_element_type=F32) + bf_ref[...]
        log_f = (jnp.minimum(logit, 0.0) - jnp.log1p(jnp.exp(-jnp.abs(logit)))) * (1.0 / GLA_TAU)
        if l_real < tb * nblk:
            rows = lax.broadcasted_iota(jnp.int32, log_f.shape, 0) + j * tb
            log_f = jnp.where(rows < l_real, log_f, 0.0)
        bcum = jnp.dot(tril, log_f, precision=HIGHEST, preferred_element_type=F32)
        ref = per_chunk_row(bcum, mid)
        last = per_chunk_row(bcum, chunk - 1)
        e_inter = jnp.exp(bcum)
        e_q = jnp.exp(bcum - ref)
        e_k = jnp.exp(ref - bcum)
        e_dec = jnp.exp(last - bcum)

        for h in range(nh):
            hk = slice(h * dk, (h + 1) * dk)
            q = ub_ref[bi, :, h * dk:(h + 1) * dk] * scale
            k = ub_ref[bi, :, off_k + h * dk:off_k + (h + 1) * dk]
            vb = ub_ref[bi, :, off_v + h * dv:off_v + (h + 1) * dv].astype(BF16)
            go = ub_ref[bi, :, off_g + h * dv:off_g + (h + 1) * dv]
            att = lax.dot_general((q * e_q[:, hk]).astype(BF16), (k * e_k[:, hk]).astype(BF16),
                                  (((1,), (1,)), ((), ())), preferred_element_type=F32)
            att = jnp.where(keep, att, 0.0)
            o = jnp.dot(att.astype(BF16), vb, preferred_element_type=F32)
            kd = by_chunk(k * e_dec[:, hk])
            kv = lax.dot_general(kd.astype(BF16), vb, (((0,), (0,)), ((), ())),
                                 preferred_element_type=F32)
            states = [s_scr[bi, h]]
            for c in range(nc):
                drow = jnp.exp(bcum[c * chunk + chunk - 1:c * chunk + chunk, hk])
                dec = jnp.transpose(jnp.broadcast_to(drow, (dk, dk)))
                dec = jnp.concatenate([dec] * (dv // dk), axis=1)
                states.append(dec * states[c] + kv[c * dk:(c + 1) * dk])
            s_scr[bi, h] = states[nc]
            qi = by_chunk(q * e_inter[:, hk])
            s_in = jnp.concatenate(states[:nc], axis=0).astype(BF16)
            o = o + jnp.dot(qi.astype(BF16), s_in, preferred_element_type=F32)
            o = o * lax.rsqrt(jnp.mean(o * o, axis=-1, keepdims=True) + RMS_EPS) * g_ref[...]
            yb_ref[bi, :, h * dv:(h + 1) * dv] = o * (go * _sigmoid(go))

    @pl.when(j == nblk - 1)
    def _():
        sout_ref[...] = s_scr[...]


def _gla(ub, s0, wf, bf, g, *, bb, tb, chunk, l_real):
    bsz, lp, nb = ub.shape
    _, nh, dk, dv = s0.shape
    nblk = lp // tb
    assert bsz % bb == 0 and dv % dk == 0 and chunk & (chunk - 1) == 0 and tb % chunk == 0
    kern = functools.partial(_gla_kernel, bb=bb, tb=tb, chunk=chunk, l_real=l_real, nblk=nblk, dk=dk, dv=dv)
    return pl.pallas_call(
        kern,
        grid=(bsz // bb, nblk),
        in_specs=[pl.BlockSpec((bb, tb, nb), lambda b, j: (b, j, 0)),
                  pl.BlockSpec((bb, nh, dk, dv), lambda b, j: (b, 0, 0, 0)),
                  _const_spec(wf.shape), _const_spec(bf.shape), _const_spec(g.shape)],
        out_specs=[pl.BlockSpec((bb, tb, nh * dv), lambda b, j: (b, j, 0)),
                   pl.BlockSpec((bb, nh, dk, dv), lambda b, j: (b, 0, 0, 0))],
        out_shape=[jax.ShapeDtypeStruct((bsz, lp, nh * dv), F32),
                   jax.ShapeDtypeStruct((bsz, nh, dk, dv), F32)],
        scratch_shapes=[pltpu.VMEM((bb, nh, dk, dv), F32)],
        compiler_params=_params("arbitrary", "arbitrary"),
        name="gla",
    )(ub, s0, wf, bf, g)


def _mix_kernel(ya_ref, yb_ref, um_ref, x_ref, wo_ref, g_ref, b_ref, *rest, alpha, d):
    o_ref = rest[-1]
    merged = _sigmoid(um_ref[:, :d]) * ya_ref[...] + _sigmoid(um_ref[:, d:]) * yb_ref[...]
    mix = jnp.dot(merged.astype(BF16), wo_ref[...], preferred_element_type=F32)
    o_ref[...] = _layer_norm(alpha * x_ref[...] + mix, g_ref[...], b_ref[...])


def _mix(ya, yb, um, x, wo, g, b, *, alpha, tm, total_rows, row_offset=0, into=None):
    m, d = x.shape
    assert m % tm == 0 and row_offset % tm == 0
    row = lambda w: pl.BlockSpec((tm, w), lambda i: (i, 0))
    in_specs = [row(d), row(d), row(2 * d), row(d),
                _const_spec(wo.shape), _const_spec(g.shape), _const_spec(b.shape)]
    args = (ya, yb, um, x, wo, g, b)
    aliases = {}
    if into is not None:
        in_specs.append(pl.BlockSpec(memory_space=pl.ANY))
        aliases = {len(args): 0}
        args = args + (into,)
    return pl.pallas_call(
        functools.partial(_mix_kernel, alpha=alpha, d=d),
        grid=(m // tm,),
        in_specs=in_specs,
        out_specs=pl.BlockSpec((tm, d), lambda i: (i + row_offset // tm, 0)),
        out_shape=jax.ShapeDtypeStruct((total_rows, d), F32),
        input_output_aliases=aliases,
        compiler_params=_params("arbitrary"),
        name="mix_ln1",
    )(*args)


def _router_kernel(x_ref, wr_ref, bias_ref, eidx_ref, gate_ref, rank_ref, cnt_ref, run_scr,
                   *, tm, ne):
    i = pl.program_id(0)

    @pl.when(i == 0)
    def _():
        run_scr[...] = jnp.zeros_like(run_scr)

    logits = jnp.dot(x_ref[...], wr_ref[...], precision=HIGHEST, preferred_element_type=F32)
    s = _sigmoid(logits)
    sel = s + bias_ref[...]
    lane = lax.broadcasted_iota(jnp.int32, (tm, ne), 1)
    lane_f = lane.astype(F32)
    gsz = ne // N_GROUPS
    neg = -jnp.inf

    gscores = []
    for g in range(N_GROUPS):
        ing = (lane >= g * gsz) & (lane < (g + 1) * gsz)
        v = jnp.where(ing, sel, neg)
        m1 = jnp.max(v, axis=1, keepdims=True)
        i1 = jnp.min(jnp.where(v == m1, lane_f, float(ne)), axis=1, keepdims=True)
        m2 = jnp.max(jnp.where(lane_f == i1, neg, v), axis=1, keepdims=True)
        gscores.append(m1 + m2)
    cur = jnp.full((tm, ne), neg, F32)
    for g in range(N_GROUPS):
        ahead = jnp.zeros((tm, 1), F32)
        for g2 in range(N_GROUPS):
            if g2 == g:
                continue
            beats = (gscores[g2] >= gscores[g]) if g2 < g else (gscores[g2] > gscores[g])
            ahead = ahead + beats.astype(F32)
        ing = (lane >= g * gsz) & (lane < (g + 1) * gsz)
        cur = jnp.where(ing & (ahead < float(TOPK_GROUPS)), sel, cur)

    col = lax.broadcasted_iota(jnp.int32, (tm, TOP_K), 1)
    eidx = jnp.zeros((tm, TOP_K), F32)
    wsel = jnp.zeros((tm, TOP_K), F32)
    onehot_sum = jnp.zeros((tm, ne), F32)
    picks = []
    for k in range(TOP_K):
        m = jnp.max(cur, axis=1, keepdims=True)
        ik = jnp.min(jnp.where(cur == m, lane_f, float(ne)), axis=1, keepdims=True)
        hit = lane_f == ik
        wk = jnp.sum(jnp.where(hit, s, 0.0), axis=1, keepdims=True)
        cur = jnp.where(hit, neg, cur)
        onehot_sum = onehot_sum + hit.astype(F32)
        picks.append(hit)
        eidx = jnp.where(col == k, ik, eidx)
        wsel = jnp.where(col == k, wk, wsel)
    gate_ref[...] = wsel / jnp.sum(wsel, axis=1, keepdims=True) * ROUTED_SCALE
    eidx_ref[...] = eidx.astype(jnp.int32)

    ri = lax.broadcasted_iota(jnp.int32, (tm, tm), 0)
    ci = lax.broadcasted_iota(jnp.int32, (tm, tm), 1)
    before = (ci < ri).astype(BF16)
    counts = jnp.dot(before, onehot_sum.astype(BF16), preferred_element_type=F32) + run_scr[...]
    rank = jnp.zeros((tm, TOP_K), F32)
    for k in range(TOP_K):
        rk = jnp.sum(jnp.where(picks[k], counts, 0.0), axis=1, keepdims=True)
        rank = jnp.where(col == k, rk, rank)
    rank_ref[...] = rank.astype(jnp.int32)
    run_scr[...] = run_scr[...] + jnp.sum(onehot_sum, axis=0, keepdims=True)
    cnt_ref[...] = run_scr[...].astype(jnp.int32)


def _router(x, wr, bias, *, tm):
    t, d = x.shape
    ne = wr.shape[1]
    out_row = lambda: pl.BlockSpec((tm, TOP_K), lambda i: (i, 0))
    return pl.pallas_call(
        functools.partial(_router_kernel, tm=tm, ne=ne),
        grid=(t // tm,),
        in_specs=[pl.BlockSpec((tm, d), lambda i: (i, 0)), _const_spec(wr.shape), _const_spec(bias.shape)],
        out_specs=[out_row(), out_row(), out_row(), _const_spec((1, ne))],
        out_shape=[jax.ShapeDtypeStruct((t, TOP_K), jnp.int32),
                   jax.ShapeDtypeStruct((t, TOP_K), F32),
                   jax.ShapeDtypeStruct((t, TOP_K), jnp.int32),
                   jax.ShapeDtypeStruct((1, ne), jnp.int32)],
        scratch_shapes=[pltpu.VMEM((1, ne), F32)],
        compiler_params=_params("arbitrary"),
        name="router",
    )(x, wr, bias)


def _dest_kernel(eidx_ref, rank_ref, start_ref, dest_ref, *, tm, ne):
    lane = lax.broadcasted_iota(jnp.int32, (tm, ne), 1)
    col = lax.broadcasted_iota(jnp.int32, (tm, TOP_K), 1)
    starts = start_ref[...].astype(F32)
    eidx = eidx_ref[...]
    dest = jnp.zeros((tm, TOP_K), F32)
    for k in range(TOP_K):
        hit = lane == eidx[:, k:k + 1]
        sk = jnp.sum(jnp.where(hit, starts, 0.0), axis=1, keepdims=True)
        dest = jnp.where(col == k, sk, dest)
    dest_ref[...] = dest.astype(jnp.int32) + rank_ref[...]


def _dest(eidx, rank, starts, *, tm):
    t = eidx.shape[0]
    ne = starts.shape[1]
    row = lambda: pl.BlockSpec((tm, TOP_K), lambda i: (i, 0))
    return pl.pallas_call(
        functools.partial(_dest_kernel, tm=tm, ne=ne),
        grid=(t // tm,),
        in_specs=[row(), row(), _const_spec((1, ne))],
        out_specs=row(),
        out_shape=jax.ShapeDtypeStruct((t, TOP_K), jnp.int32),
        compiler_params=_params("arbitrary"),
        name="dest",
    )(eidx, rank, starts)


def _router_t_kernel(x_ref, whi_ref, wlo_ref, bias_ref, eidx_ref, gate_ref, rank_ref, cnt_ref, run_scr,
                     *, tm, ne):
    i = pl.program_id(0)

    @pl.when(i == 0)
    def _():
        run_scr[...] = jnp.zeros_like(run_scr)

    x = x_ref[...]
    x_hi = x.astype(BF16)
    x_lo = (x - x_hi.astype(F32)).astype(BF16)
    nt = (((1,), (1,)), ((), ()))
    logits = (lax.dot_general(whi_ref[...], x_hi, nt, preferred_element_type=F32)
              + lax.dot_general(whi_ref[...], x_lo, nt, preferred_element_type=F32)
              + lax.dot_general(wlo_ref[...], x_hi, nt, preferred_element_type=F32))
    s = _sigmoid(logits)
    sel = s + bias_ref[...]
    row = lax.broadcasted_iota(jnp.int32, (ne, tm), 0).astype(F32)
    gsz = ne // N_GROUPS
    neg = -jnp.inf

    def colmax(a):
        return jnp.max(a, axis=0, keepdims=True)

    def colmin(a):
        return jnp.min(a, axis=0, keepdims=True)

    gscores = []
    row_in_group = lax.broadcasted_iota(jnp.int32, (gsz, tm), 0).astype(F32)
    for g in range(N_GROUPS):
        v = sel[g * gsz:(g + 1) * gsz]
        rg = row_in_group + float(g * gsz)
        m1 = colmax(v)
        i1 = colmin(jnp.where(v == m1, rg, float(ne)))
        gscores.append(m1 + colmax(jnp.where(rg == i1, neg, v)))
    kept = []
    for g in range(N_GROUPS):
        ahead = jnp.zeros((1, tm), F32)
        for g2 in range(N_GROUPS):
            if g2 == g:
                continue
            beats = (gscores[g2] >= gscores[g]) if g2 < g else (gscores[g2] > gscores[g])
            ahead = ahead + beats.astype(F32)
        keep = jnp.broadcast_to(ahead < float(TOPK_GROUPS), (gsz, tm))
        kept.append(jnp.where(keep, sel[g * gsz:(g + 1) * gsz], neg))
    cur = jnp.concatenate(kept, axis=0)

    picks, eidx, wsel = [], [], []
    onehot_sum = jnp.zeros((ne, tm), F32)
    for k in range(TOP_K):
        ik = colmin(jnp.where(cur == colmax(cur), row, float(ne)))
        hit = row == ik
        wsel.append(jnp.sum(jnp.where(hit, s, 0.0), axis=0, keepdims=True))
        cur = jnp.where(hit, neg, cur)
        onehot_sum = onehot_sum + hit.astype(F32)
        picks.append(hit)
        eidx.append(ik)
    wsel = jnp.concatenate(wsel, axis=0)
    gate_ref[...] = wsel / jnp.sum(wsel, axis=0, keepdims=True) * ROUTED_SCALE
    eidx_ref[...] = jnp.concatenate(eidx, axis=0).astype(jnp.int32)

    ri = lax.broadcasted_iota(jnp.int32, (tm, tm), 0)
    ci = lax.broadcasted_iota(jnp.int32, (tm, tm), 1)
    earlier = (ri < ci).astype(BF16)
    ohb = onehot_sum.astype(BF16)
    counts = jnp.dot(ohb, earlier, preferred_element_type=F32) + jnp.concatenate([run_scr[...]] * (tm // LANES), axis=1)
    rank = [jnp.sum(jnp.where(picks[k], counts, 0.0), axis=0, keepdims=True) for k in range(TOP_K)]
    rank_ref[...] = jnp.concatenate(rank, axis=0).astype(jnp.int32)
    run_scr[...] = run_scr[...] + jnp.dot(ohb, jnp.ones((tm, LANES), BF16), preferred_element_type=F32)
    cnt_ref[...] = run_scr[...].astype(jnp.int32)


def _router_t(x, w_hi, w_lo, bias_b, *, tm):
    t, d = x.shape
    ne = w_hi.shape[0]
    out_col = lambda: pl.BlockSpec((TOP_K, tm), lambda i: (0, i))
    return pl.pallas_call(
        functools.partial(_router_t_kernel, tm=tm, ne=ne),
        grid=(t // tm,),
        in_specs=[pl.BlockSpec((tm, d), lambda i: (i, 0)), _const_spec(w_hi.shape), _const_spec(w_lo.shape),
                  _const_spec(bias_b.shape)],
        out_specs=[out_col(), out_col(), out_col(), _const_spec((ne, LANES))],
        out_shape=[jax.ShapeDtypeStruct((TOP_K, t), jnp.int32),
                   jax.ShapeDtypeStruct((TOP_K, t), F32),
                   jax.ShapeDtypeStruct((TOP_K, t), jnp.int32),
                   jax.ShapeDtypeStruct((ne, LANES), jnp.int32)],
        scratch_shapes=[pltpu.VMEM((ne, LANES), F32)],
        compiler_params=_params("arbitrary"),
        name="router",
    )(x, w_hi, w_lo, bias_b)


def _dest_t_kernel(eidx_ref, rank_ref, start_ref, dest_ref, *, tm, ne):
    row = lax.broadcasted_iota(jnp.int32, (ne, tm), 0)
    starts = start_ref[...].astype(F32)
    eidx = eidx_ref[...]
    dest = [jnp.sum(jnp.where(row == eidx[k:k + 1, :], starts, 0.0), axis=0, keepdims=True) for k in range(TOP_K)]
    dest_ref[...] = jnp.concatenate(dest, axis=0).astype(jnp.int32) + rank_ref[...]


def _dest_t(eidx, rank, starts_b, *, tm):
    t = eidx.shape[1]
    ne = starts_b.shape[0]
    col = lambda: pl.BlockSpec((TOP_K, tm), lambda i: (0, i))
    return pl.pallas_call(
        functools.partial(_dest_t_kernel, tm=tm, ne=ne),
        grid=(t // tm,),
        in_specs=[col(), col(), _const_spec(starts_b.shape)],
        out_specs=col(),
        out_shape=jax.ShapeDtypeStruct((TOP_K, t), jnp.int32),
        compiler_params=_params("arbitrary"),
        name="dest",
    )(eidx, rank, starts_b)


def _to_token_tiles(rows):
    chunks = jnp.stack([rows[:, c * LANES:(c + 1) * LANES] for c in range(SUBLANES)], axis=0)
    return jnp.swapaxes(chunks, 0, 1)


def _chunks_of_token_tiles(tiles):
    chunks = jnp.swapaxes(tiles, 0, 1)
    return [chunks[c] for c in range(SUBLANES)]


def _token_copy(src_ref, src_row, dst_ref, dst_row, sem):
    return pltpu.make_async_copy(src_ref.at[pl.ds(src_row, 1)], dst_ref.at[pl.ds(dst_row, 1)], sem)


def _dispatch_kernel(dest_ref, x_ref, xs_ref, xt, sem, *, tt):
    xt[...] = _to_token_tiles(x_ref[...])

    def start(tp, c):
        for u in range(ISSUE_UNROLL):
            t = tp * ISSUE_UNROLL + u
            for k in range(TOP_K):
                _token_copy(xt, t, xs_ref, dest_ref[t * TOP_K + k], sem).start(priority=k % 2)
        return c

    lax.fori_loop(0, tt // ISSUE_UNROLL, start, 0)
    for _ in range(TOP_K):
        pltpu.make_async_copy(xt, xs_ref.at[pl.ds(0, tt)], sem).wait()


def _dispatch(dest_flat, x, n_slots, *, tt):
    t, d = x.shape
    assert d == SUBLANES * LANES
    return pl.pallas_call(
        functools.partial(_dispatch_kernel, tt=tt),
        grid=(t // tt,),
        in_specs=[pl.BlockSpec((tt * TOP_K,), lambda i: (i,), memory_space=pltpu.SMEM),
                  pl.BlockSpec((tt, d), lambda i: (i, 0))],
        out_specs=pl.BlockSpec(memory_space=pl.ANY),
        out_shape=jax.ShapeDtypeStruct((n_slots, SUBLANES, LANES), F32),
        scratch_shapes=[pltpu.VMEM((tt, SUBLANES, LANES), F32), pltpu.SemaphoreType.DMA(())],
        compiler_params=_params("arbitrary"),
        name="dispatch",
    )(dest_flat, x)


EXPERT_SLOTS = 8
EXPERT_AHEAD = 6


def _expert_kernel(base_ref, nblk_ref, cnt_ref, nu_ref, xs_ref, wg_ref, wu_ref, wd_ref, ys_ref,
                   xbuf, ybuf, wgb, wub, wdb, sem_in, sem_out):
    e = pl.program_id(0)
    n_used = nu_ref[0]
    nb = nblk_ref[e]
    base = base_ref[e]
    cnt = cnt_ref[e]
    ahead = EXPERT_AHEAD

    def x_copy(blk, slot):
        return pltpu.make_async_copy(xs_ref.at[pl.ds(blk * MOE_ROWS, MOE_ROWS)], xbuf.at[slot], sem_in.at[slot])

    def y_copy(blk, slot):
        return pltpu.make_async_copy(ybuf.at[slot], ys_ref.at[pl.ds(blk * MOE_ROWS, MOE_ROWS)], sem_out.at[slot])

    @pl.when(e == 0)
    def _():
        for first in range(ahead):
            @pl.when(first < n_used)
            def _():
                x_copy(first, first).start()

    @pl.when(nb > 0)
    def _():
        wgb[...] = wg_ref[0].astype(BF16)
        wub[...] = wu_ref[0].astype(BF16)
        wdb[...] = wd_ref[0].astype(BF16)

    def step(jb, nblocks):
        blks = [base + jb + i for i in range(nblocks)]
        slots = [jnp.bitwise_and(b, EXPERT_SLOTS - 1) for b in blks]
        for b, s in zip(blks, slots):
            x_copy(b, s).wait()
        for b in blks:
            @pl.when(b + ahead < n_used)
            def _():
                x_copy(b + ahead, jnp.bitwise_and(b + ahead, EXPERT_SLOTS - 1)).start()

        x = jnp.concatenate([jnp.concatenate(_chunks_of_token_tiles(xbuf[s]), axis=1) for s in slots], axis=0)
        rows = lax.broadcasted_iota(jnp.int32, x.shape, 0)
        xb = jnp.where(rows < cnt - jb * MOE_ROWS, x, 0.0).astype(BF16)
        g = jnp.dot(xb, wgb[...], preferred_element_type=F32)
        u = jnp.dot(xb, wub[...], preferred_element_type=F32)
        hmid = (g * _sigmoid(g)) * u
        y = jnp.dot(hmid.astype(BF16), wdb[...], preferred_element_type=F32)

        for i, (b, s) in enumerate(zip(blks, slots)):
            @pl.when(b >= EXPERT_SLOTS)
            def _():
                y_copy(b - EXPERT_SLOTS, s).wait()

            ybuf[s] = _to_token_tiles(y[i * MOE_ROWS:(i + 1) * MOE_ROWS])
            y_copy(b, s).start()

    def pair(jp, c):
        step(2 * jp, 2)
        return c

    lax.fori_loop(0, lax.shift_right_logical(nb, 1), pair, 0)

    @pl.when(jnp.bitwise_and(nb, 1) == 1)
    def _():
        step(nb - 1, 1)

    @pl.when(e == pl.num_programs(0) - 1)
    def _():
        for back in range(EXPERT_SLOTS):
            @pl.when(n_used > back)
            def _():
                last = n_used - 1 - back
                y_copy(last, jnp.bitwise_and(last, EXPERT_SLOTS - 1)).wait()


def _experts(base, nblk, cnt, n_used, xs, wg, wu, wd):
    p = xs.shape[0]
    ne, d, de = wg.shape
    assert EXPERT_SLOTS & (EXPERT_SLOTS - 1) == 0 and EXPERT_AHEAD <= EXPERT_SLOTS - 2 and d == SUBLANES * LANES
    ring = pltpu.VMEM((EXPERT_SLOTS, MOE_ROWS, SUBLANES, LANES), F32)
    grid_spec = pltpu.PrefetchScalarGridSpec(
        num_scalar_prefetch=4,
        grid=(ne,),
        in_specs=[pl.BlockSpec(memory_space=pl.ANY),
                  pl.BlockSpec((1, d, de), lambda e, *_: (e, 0, 0)),
                  pl.BlockSpec((1, d, de), lambda e, *_: (e, 0, 0)),
                  pl.BlockSpec((1, de, d), lambda e, *_: (e, 0, 0))],
        out_specs=pl.BlockSpec(memory_space=pl.ANY),
        scratch_shapes=[ring, ring,
                        pltpu.VMEM((d, de), BF16), pltpu.VMEM((d, de), BF16), pltpu.VMEM((de, d), BF16),
                        pltpu.SemaphoreType.DMA((EXPERT_SLOTS,)), pltpu.SemaphoreType.DMA((EXPERT_SLOTS,))],
    )
    return pl.pallas_call(
        _expert_kernel,
        grid_spec=grid_spec,
        out_shape=jax.ShapeDtypeStruct((p, SUBLANES, LANES), F32),
        compiler_params=_params("arbitrary"),
        name="experts",
    )(base, nblk, cnt, n_used, xs, wg, wu, wd)


def _combine_kernel(dest_ref, dest_next_ref, x_ref, gate_ref, ys_ref, wsg_ref, wsu_ref, wsd_ref, g_ref, b_ref,
                    o_head_ref, o_tail_ref, ybuf, sem, *, tt, alpha, head_steps):
    i = pl.program_id(0)
    slot = jnp.bitwise_and(i, 1)

    def gather(dref, s):
        def start(tp, c):
            for u in range(ISSUE_UNROLL):
                t = tp * ISSUE_UNROLL + u
                for k in range(TOP_K):
                    _token_copy(ys_ref, dref[t * TOP_K + k], ybuf.at[s], k * tt + t, sem.at[s]).start(priority=k % 2)
            return c
        lax.fori_loop(0, tt // ISSUE_UNROLL, start, 0)

    @pl.when(i == 0)
    def _():
        gather(dest_ref, 0)

    def prefetch_group(k):
        nxt = 1 - slot
        for t in range(tt):
            _token_copy(ys_ref, dest_next_ref[t * TOP_K + k], ybuf.at[nxt], k * tt + t, sem.at[nxt]).start(priority=t % 2)

    x = x_ref[...]
    xb = x.astype(BF16)
    sg = jnp.dot(xb, wsg_ref[...], preferred_element_type=F32)
    su = jnp.dot(xb, wsu_ref[...], preferred_element_type=F32)
    acc = jnp.dot(((sg * _sigmoid(sg)) * su).astype(BF16), wsd_ref[...], preferred_element_type=F32)
    pltpu.make_async_copy(ys_ref.at[pl.ds(0, tt * TOP_K)], ybuf.at[slot], sem.at[slot]).wait()
    gate = gate_ref[...]
    routed = None
    for k in range(TOP_K):
        prefetch_group(k)
        gk = jnp.broadcast_to(gate[:, k:k + 1], (tt, LANES))
        part = [gk * ch for ch in _chunks_of_token_tiles(ybuf[slot, k * tt:(k + 1) * tt])]
        routed = part if routed is None else [r + p for r, p in zip(routed, part)]
    acc = acc + jnp.concatenate(routed, axis=1)
    out = _layer_norm(alpha * x + acc, g_ref[...], b_ref[...])

    @pl.when(i < head_steps)
    def _():
        o_head_ref[...] = out

    @pl.when(i >= head_steps)
    def _():
        o_tail_ref[...] = out

    @pl.when(i == pl.num_programs(0) - 1)
    def _():
        pltpu.make_async_copy(ys_ref.at[pl.ds(0, tt * TOP_K)], ybuf.at[1 - slot], sem.at[1 - slot]).wait()


def _combine(dest_flat, x, gate, ys, wsg, wsu, wsd, g, b, *, tt, alpha, head_rows):
    t, d = x.shape
    nsteps = t // tt
    head_steps = head_rows // tt
    assert head_rows % tt == 0 and 0 < head_steps < nsteps
    return pl.pallas_call(
        functools.partial(_combine_kernel, tt=tt, alpha=alpha, head_steps=head_steps),
        grid=(nsteps,),
        in_specs=[pl.BlockSpec((tt * TOP_K,), lambda i: (i,), memory_space=pltpu.SMEM),
                  pl.BlockSpec((tt * TOP_K,), lambda i: (jnp.minimum(i + 1, nsteps - 1),),
                               memory_space=pltpu.SMEM),
                  pl.BlockSpec((tt, d), lambda i: (i, 0)),
                  pl.BlockSpec((tt, TOP_K), lambda i: (i, 0)),
                  pl.BlockSpec(memory_space=pl.ANY),
                  _const_spec(wsg.shape), _const_spec(wsu.shape), _const_spec(wsd.shape),
                  _const_spec(g.shape), _const_spec(b.shape)],
        out_specs=[pl.BlockSpec((tt, d), lambda i: (jnp.minimum(i, head_steps - 1), 0)),
                   pl.BlockSpec((tt, d), lambda i: (jnp.maximum(i - head_steps, 0), 0))],
        out_shape=[jax.ShapeDtypeStruct((head_rows, d), F32), jax.ShapeDtypeStruct((t - head_rows, d), F32)],
        scratch_shapes=[pltpu.VMEM((2, tt * TOP_K, SUBLANES, LANES), F32), pltpu.SemaphoreType.DMA((2,))],
        compiler_params=_params("arbitrary"),
        name="combine_ln2",
    )(dest_flat, dest_flat, x, gate, ys, wsg, wsu, wsd, g, b)


def _block_diag_tiles(w):
    n, s, _ = w.shape
    per = MXU_DIM // s
    w = w.reshape(n // per, per, s, s)
    eye = jnp.eye(per, dtype=w.dtype)
    return jnp.einsum('tpij,pq->tpiqj', w, eye).reshape(n // per, MXU_DIM, MXU_DIM)


def _row(v):
    return v.reshape(1, -1)


def _moe(x1, w_router, router_bias, w_gate, w_up, w_down, ws_gate, ws_up, ws_down, ln_g, ln_b, alpha, head_rows):
    t, d = x1.shape
    ne = w_router.shape[1]
    tm = 256
    w_t = w_router.T
    w_hi = w_t.astype(BF16)
    w_lo = (w_t - w_hi.astype(F32)).astype(BF16)
    bias_b = jnp.broadcast_to(router_bias.reshape(ne, 1), (ne, tm))
    eidx_t, gate_t, rank_t, counts = _router_t(x1, w_hi, w_lo, bias_b, tm=tm)
    counts = counts[:, 0]
    nblk = (counts + MOE_ROWS - 1) // MOE_ROWS
    blk_end = jnp.cumsum(nblk)
    blk_base = (blk_end - nblk).astype(jnp.int32)
    n_used = blk_end[-1:].astype(jnp.int32)
    n_slots = ((t * TOP_K) // MOE_ROWS + ne) * MOE_ROWS

    starts_b = jnp.broadcast_to((blk_base * MOE_ROWS).reshape(ne, 1), (ne, tm))
    dest_flat = _dest_t(eidx_t, rank_t, starts_b, tm=tm).T.reshape(-1)
    xs = _dispatch(dest_flat, x1, n_slots, tt=256)
    ys = _experts(blk_base, nblk.astype(jnp.int32), counts, n_used, xs, w_gate, w_up, w_down)
    return _combine(dest_flat, x1, gate_t.T, ys, ws_gate.astype(BF16), ws_up.astype(BF16),
                    ws_down.astype(BF16), _row(ln_g), _row(ln_b), tt=128, alpha=alpha, head_rows=head_rows)


def _pad_time(a, lp):
    return jnp.pad(a, ((0, 0), (0, lp - a.shape[1]), (0, 0)))


def kernel(x_prompt, x_sample, state_rglru_h, state_conv, state_gla, meta_tokens, w_in, conv_w, conv_b,
           rg_wa, rg_ba, rg_wi, rg_bi, rg_lambda, gla_wf2, gla_bf, gla_norm_g, w_out, ln1_g, ln1_b,
           w_router, router_bias, w_gate, w_up, w_down, ws_gate, ws_up, ws_down, ln2_g, ln2_b):
    bp, lp, d = x_prompt.shape
    bs, ls, _ = x_sample.shape
    depth = w_in.shape[0]
    nh, dk, dv = state_gla.shape[2:]
    qk, vw = nh * dk, nh * dv
    lowrank = gla_wf2.shape[1]
    alpha = (2.0 * depth) ** 0.25
    ls_pad = SUBLANES
    assert depth == 1 and ls <= ls_pad and lp % 256 == 0 and N_META % SUBLANES == 0

    xp = x_prompt.reshape(bp * lp, d)
    xs = x_sample.reshape(bs * ls, d)
    xm = meta_tokens.astype(F32)

    hp_l, cp_l, sp_l, hs_l, cs_l, ss_l = [], [], [], [], [], []
    for l in range(depth):
        o1, o2 = 2 * d, 2 * d + 2 * qk + 2 * vw
        wa = w_in[l][:, :o1].astype(BF16)
        wb = jnp.pad(w_in[l][:, o1:o2 + lowrank], ((0, 0), (0, LANES - lowrank))).astype(BF16)
        wm = w_in[l][:, o2 + lowrank:].astype(BF16)
        wf = jnp.pad(gla_wf2[l], ((0, LANES - lowrank), (0, 0))).astype(BF16)
        wa4 = _block_diag_tiles(rg_wa[l]).astype(BF16)
        wi4 = _block_diag_tiles(rg_wi[l]).astype(BF16)
        mixer_w = (conv_w[l], _row(conv_b[l]), wa4, _row(rg_ba[l]), wi4, _row(rg_bi[l]), _row(rg_lambda[l]))
        gla_w = (wf, _row(gla_bf[l]), _row(gla_norm_g[l]))

        ua, ub, _ = _inproj(xm, wa, wb, wm, tm=N_META)
        _, h_m, c_m = _rglru(ua[None], jnp.zeros((1, CONV_W - 1, d), F32), jnp.zeros((1, 1, d), F32),
                             *mixer_w, bb=1, tb=N_META, l_real=N_META)
        _, s_m = _gla(ub[None], jnp.zeros((1, nh, dk, dv), F32), *gla_w,
                      bb=1, tb=N_META, chunk=N_META, l_real=N_META)

        ua, ub, um = _inproj(xp, wa, wb, wm, tm=256)
        ya, h_p, c_p = _rglru(ua.reshape(bp, lp, -1), jnp.broadcast_to(c_m, (bp,) + c_m.shape[1:]),
                              jnp.broadcast_to(h_m, (bp,) + h_m.shape[1:]), *mixer_w,
                              bb=1, tb=256, l_real=lp)
        yb, s_p = _gla(ub.reshape(bp, lp, -1), jnp.broadcast_to(s_m, (bp,) + s_m.shape[1:]), *gla_w,
                       bb=1, tb=256, chunk=GLA_CHUNK, l_real=lp)
        n_tok = bp * lp + bs * ls
        x1 = _mix(ya.reshape(bp * lp, d), yb.reshape(bp * lp, d), um, xp, w_out[l].astype(BF16),
                  _row(ln1_g[l]), _row(ln1_b[l]), alpha=alpha, tm=512, total_rows=n_tok)

        xs_pad = _pad_time(xs.reshape(bs, ls, d), ls_pad).reshape(bs * ls_pad, d)
        ua, ub, um = _inproj(xs_pad, wa, wb, wm, tm=256)
        ya, h_s, c_s = _rglru(ua.reshape(bs, ls_pad, -1), state_conv[l].astype(F32),
                              state_rglru_h[l].astype(F32)[:, None, :], *mixer_w,
                              bb=32, tb=ls_pad, l_real=ls)
        yb, s_s = _gla(ub.reshape(bs, ls_pad, -1), state_gla[l].astype(F32), *gla_w,
                       bb=8, tb=ls_pad, chunk=ls_pad, l_real=ls)
        unpad = lambda a: a.reshape(bs, ls_pad, -1)[:, :ls].reshape(bs * ls, -1)
        x1 = _mix(unpad(ya), unpad(yb), unpad(um), xs, w_out[l].astype(BF16),
                  _row(ln1_g[l]), _row(ln1_b[l]), alpha=alpha, tm=512, total_rows=n_tok,
                  row_offset=bp * lp, into=x1)

        xp, xs = _moe(x1, w_router[l], router_bias[l], w_gate[l], w_up[l], w_down[l],
                      ws_gate[l], ws_up[l], ws_down[l], ln2_g[l], ln2_b[l], alpha, bp * lp)

        hp_l.append(h_p[:, 0]); cp_l.append(c_p); sp_l.append(s_p)
        hs_l.append(h_s[:, 0]); cs_l.append(c_s); ss_l.append(s_s)

    y_prompt = xp.reshape(bp, lp, d)
    y_sample = xs.reshape(bs, ls, d)
    return (y_prompt, y_sample,
            jnp.stack(hp_l).astype(state_rglru_h.dtype), jnp.stack(cp_l).astype(state_conv.dtype),
            jnp.stack(sp_l).astype(state_gla.dtype),
            jnp.stack(hs_l).astype(state_rglru_h.dtype), jnp.stack(cs_l).astype(state_conv.dtype),
            jnp.stack(ss_l).astype(state_gla.dtype))
```

```python
import functools

import jax
import jax.numpy as jnp
from jax import lax
from jax.experimental import pallas as pl
from jax.experimental.pallas import tpu as pltpu

F32 = jnp.float32
BF16 = jnp.bfloat16
HIGHEST = lax.Precision.HIGHEST

N_META = 16
CONV_W = 4
RG_C = 8.0
RNN_BLOCKS = 16
GLA_HEADS = 4
GLA_TAU = 16.0
GLA_CHUNK = 64
N_GROUPS = 8
TOPK_GROUPS = 4
TOP_K = 8
ROUTED_SCALE = 2.5
LN_EPS = 1e-5
RMS_EPS = 1e-6

LANES = 128
SUBLANES = 8
MXU_DIM = 256
VMEM_LIMIT_BYTES = 56 * 1024 * 1024

MOE_ROWS = 128
ISSUE_UNROLL = 4


def _params(*sem):
    return pltpu.CompilerParams(dimension_semantics=sem, vmem_limit_bytes=VMEM_LIMIT_BYTES)


def _sigmoid(x):
    return jax.nn.sigmoid(x)


def _softplus(z):
    return jnp.maximum(z, 0.0) + jnp.log1p(jnp.exp(-jnp.abs(z)))


def _layer_norm(z, g, b):
    mu = jnp.mean(z, axis=-1, keepdims=True)
    zc = z - mu
    var = jnp.mean(zc * zc, axis=-1, keepdims=True)
    return zc * lax.rsqrt(var + LN_EPS) * g + b


def _const_spec(shape):
    nd = len(shape)
    return pl.BlockSpec(shape, lambda *_: (0,) * nd)


def _inproj_kernel(x_ref, wa_ref, wb_ref, wm_ref, ua_ref, ub_ref, um_ref):
    xb = x_ref[...].astype(BF16)
    ua_ref[...] = jnp.dot(xb, wa_ref[...], preferred_element_type=F32)
    ub_ref[...] = jnp.dot(xb, wb_ref[...], preferred_element_type=F32)
    um_ref[...] = jnp.dot(xb, wm_ref[...], preferred_element_type=F32)


def _inproj(x2d, wa, wb, wm, tm):
    m, d = x2d.shape
    na, nb, nm = wa.shape[1], wb.shape[1], wm.shape[1]
    return pl.pallas_call(
        _inproj_kernel,
        grid=(m // tm,),
        in_specs=[pl.BlockSpec((tm, d), lambda i: (i, 0)),
                  _const_spec(wa.shape), _const_spec(wb.shape), _const_spec(wm.shape)],
        out_specs=[pl.BlockSpec((tm, na), lambda i: (i, 0)),
                   pl.BlockSpec((tm, nb), lambda i: (i, 0)),
                   pl.BlockSpec((tm, nm), lambda i: (i, 0))],
        out_shape=[jax.ShapeDtypeStruct((m, na), F32),
                   jax.ShapeDtypeStruct((m, nb), F32),
                   jax.ShapeDtypeStruct((m, nm), F32)],
        compiler_params=_params("arbitrary"),
        name="inproj",
    )(x2d, wa, wb, wm)


def _rglru_kernel(ua_ref, cbuf_ref, h0_ref, cw_ref, cb_ref, wa_ref, ba_ref, wi_ref, bi_ref, lam_ref,
                  ya_ref, hlast_ref, cnew_ref, ext_scr, h_scr, *, bb, tb, l_last, nblk, d):
    j = pl.program_id(1)
    ntail = CONV_W - 1
    rows = bb * tb

    @pl.when(j == 0)
    def _():
        h_scr[...] = h0_ref[...]
        ext_scr[:, SUBLANES - ntail:SUBLANES, :] = cbuf_ref[...]

    xr = ua_ref[:, :, :d]
    gr = ua_ref[:, :, d:].reshape(rows, d)
    ext_scr[:, SUBLANES:, :] = xr
    xc = cb_ref[...] + cw_ref[ntail:ntail + 1, :] * xr
    for s in range(1, CONV_W):
        xc = xc + cw_ref[ntail - s:ntail - s + 1, :] * ext_scr[:, SUBLANES - s:SUBLANES - s + tb, :]
    xc = xc.reshape(rows, d)

    xcb = xc.astype(BF16)
    nt = d // MXU_DIM
    ra = jnp.concatenate(
        [jnp.dot(xcb[:, q * MXU_DIM:(q + 1) * MXU_DIM], wa_ref[q], preferred_element_type=F32)
         for q in range(nt)], axis=-1)
    ia = jnp.concatenate(
        [jnp.dot(xcb[:, q * MXU_DIM:(q + 1) * MXU_DIM], wi_ref[q], preferred_element_type=F32)
         for q in range(nt)], axis=-1)
    r = _sigmoid(ra + ba_ref[...])
    ig = _sigmoid(ia + bi_ref[...])
    log_a = (-RG_C) * r * _softplus(-lam_ref[...])
    a = jnp.exp(log_a)
    z = -jnp.tanh(log_a) * (a * a + 1.0)
    mult = jnp.where(z > 0.0, z * lax.rsqrt(z), 0.0)
    b = mult * (ig * xc)

    t_in = jnp.bitwise_and(lax.broadcasted_iota(jnp.int32, (rows, d), 0), SUBLANES - 1)
    s = 1
    while s < SUBLANES:
        a_sh = jnp.where(t_in >= s, pltpu.roll(a, s, 0), 1.0)
        b_sh = jnp.where(t_in >= s, pltpu.roll(b, s, 0), 0.0)
        b = b + a * b_sh
        a = a * a_sh
        s *= 2
    a = a.reshape(bb, tb, d)
    b = b.reshape(bb, tb, d)
    carry = h_scr[...]
    tiles = []
    for i in range(tb // SUBLANES):
        rs = slice(i * SUBLANES, (i + 1) * SUBLANES)
        hi = b[:, rs, :] + a[:, rs, :] * carry
        carry = hi[:, SUBLANES - 1:SUBLANES, :]
        tiles.append(hi)
    h = jnp.concatenate(tiles, axis=1)
    h_scr[...] = carry
    ya_ref[...] = h * jax.nn.gelu(gr, approximate=True).reshape(bb, tb, d)

    @pl.when(j == nblk - 1)
    def _():
        hlast_ref[...] = h[:, l_last - 1:l_last, :]
        cnew_ref[...] = ext_scr[:, SUBLANES + l_last - ntail:SUBLANES + l_last, :]

    tail = ext_scr[:, tb + SUBLANES - ntail:tb + SUBLANES, :]
    ext_scr[:, SUBLANES - ntail:SUBLANES, :] = tail


def _rglru(ua, cbuf, h0, cw, cb, wa4, ba, wi4, bi, lam, *, bb, tb, l_real):
    bsz, lp, n2 = ua.shape
    d = n2 // 2
    nblk = lp // tb
    l_last = l_real - (nblk - 1) * tb
    assert tb & (tb - 1) == 0 and bsz % bb == 0
    kern = functools.partial(_rglru_kernel, bb=bb, tb=tb, l_last=l_last, nblk=nblk, d=d)
    return pl.pallas_call(
        kern,
        grid=(bsz // bb, nblk),
        in_specs=[pl.BlockSpec((bb, tb, n2), lambda b, j: (b, j, 0)),
                  pl.BlockSpec((bb, CONV_W - 1, d), lambda b, j: (b, 0, 0)),
                  pl.BlockSpec((bb, 1, d), lambda b, j: (b, 0, 0)),
                  _const_spec(cw.shape), _const_spec(cb.shape),
                  _const_spec(wa4.shape), _const_spec(ba.shape),
                  _const_spec(wi4.shape), _const_spec(bi.shape), _const_spec(lam.shape)],
        out_specs=[pl.BlockSpec((bb, tb, d), lambda b, j: (b, j, 0)),
                   pl.BlockSpec((bb, 1, d), lambda b, j: (b, 0, 0)),
                   pl.BlockSpec((bb, CONV_W - 1, d), lambda b, j: (b, 0, 0))],
        out_shape=[jax.ShapeDtypeStruct((bsz, lp, d), F32),
                   jax.ShapeDtypeStruct((bsz, 1, d), F32),
                   jax.ShapeDtypeStruct((bsz, CONV_W - 1, d), F32)],
        scratch_shapes=[pltpu.VMEM((bb, tb + SUBLANES, d), F32), pltpu.VMEM((bb, 1, d), F32)],
        compiler_params=_params("arbitrary", "arbitrary"),
        name="rglru",
    )(ua, cbuf, h0, cw, cb, wa4, ba, wi4, bi, lam)


def _gla_kernel(ub_ref, s0_ref, wf_ref, bf_ref, g_ref, yb_ref, sout_ref, s_scr,
                *, bb, tb, chunk, l_real, nblk, dk, dv):
    j = pl.program_id(1)
    nh = GLA_HEADS
    qk = nh * dk
    vw = nh * dv

    @pl.when(j == 0)
    def _():
        s_scr[...] = s0_ref[...]

    off_k, off_v, off_g, off_f = qk, 2 * qk, 2 * qk + vw, 2 * qk + 2 * vw
    nc = tb // chunk
    shift = chunk.bit_length() - 1
    ri = lax.broadcasted_iota(jnp.int32, (tb, tb), 0)
    ci = lax.broadcasted_iota(jnp.int32, (tb, tb), 1)
    keep = (lax.shift_right_logical(ri, shift) == lax.shift_right_logical(ci, shift)) & (ri >= ci)
    tril = keep.astype(F32)
    row_chunk = lax.shift_right_logical(lax.broadcasted_iota(jnp.int32, (tb, dk), 0), shift)
    mid = chunk // 2
    scale = dk ** -0.5

    def by_chunk(a):
        if nc == 1:
            return a
        return jnp.concatenate([jnp.where(row_chunk == c, a, 0.0) for c in range(nc)], axis=1)

    def per_chunk_row(a, r):
        return jnp.concatenate([jnp.broadcast_to(a[c * chunk + r:c * chunk + r + 1, :], (chunk, a.shape[1]))
                                for c in range(nc)], axis=0)

    for bi in range(bb):
        fl = ub_ref[bi, :, off_f:off_f + LANES].astype(BF16)
        logit = jnp.dot(fl, wf_ref[...], preferred_element_type=F32) + bf_ref[...]
        log_f = (jnp.minimum(logit, 0.0) - jnp.log1p(jnp.exp(-jnp.abs(logit)))) * (1.0 / GLA_TAU)
        if l_real < tb * nblk:
            rows = lax.broadcasted_iota(jnp.int32, log_f.shape, 0) + j * tb
            log_f = jnp.where(rows < l_real, log_f, 0.0)
        bcum = jnp.dot(tril, log_f, precision=HIGHEST, preferred_element_type=F32)
        ref = per_chunk_row(bcum, mid)
        last = per_chunk_row(bcum, chunk - 1)
        e_inter = jnp.exp(bcum)
        e_q = jnp.exp(bcum - ref)
        e_k = jnp.exp(ref - bcum)
        e_dec = jnp.exp(last - bcum)

        for h in range(nh):
            hk = slice(h * dk, (h + 1) * dk)
            q = ub_ref[bi, :, h * dk:(h + 1) * dk] * scale
            k = ub_ref[bi, :, off_k + h * dk:off_k + (h + 1) * dk]
            vb = ub_ref[bi, :, off_v + h * dv:off_v + (h + 1) * dv].astype(BF16)
            go = ub_ref[bi, :, off_g + h * dv:off_g + (h + 1) * dv]
            att = lax.dot_general((q * e_q[:, hk]).astype(BF16), (k * e_k[:, hk]).astype(BF16),
                                  (((1,), (1,)), ((), ())), preferred_element_type=F32)
            att = jnp.where(keep, att, 0.0)
            o = jnp.dot(att.astype(BF16), vb, preferred_element_type=F32)
            kd = by_chunk(k * e_dec[:, hk])
            kv = lax.dot_general(kd.astype(BF16), vb, (((0,), (0,)), ((), ())),
                                 preferred_element_type=F32)
            states = [s_scr[bi, h]]
            for c in range(nc):
                drow = jnp.exp(bcum[c * chunk + chunk - 1:c * chunk + chunk, hk])
                dec = jnp.transpose(jnp.broadcast_to(drow, (dk, dk)))
                dec = jnp.concatenate([dec] * (dv // dk), axis=1)
                states.append(dec * states[c] + kv[c * dk:(c + 1) * dk])
            s_scr[bi, h] = states[nc]
            qi = by_chunk(q * e_inter[:, hk])
            s_in = jnp.concatenate(states[:nc], axis=0).astype(BF16)
            o = o + jnp.dot(qi.astype(BF16), s_in, preferred_element_type=F32)
            o = o * lax.rsqrt(jnp.mean(o * o, axis=-1, keepdims=True) + RMS_EPS) * g_ref[...]
            yb_ref[bi, :, h * dv:(h + 1) * dv] = o * (go * _sigmoid(go))

    @pl.when(j == nblk - 1)
    def _():
        sout_ref[...] = s_scr[...]


def _gla(ub, s0, wf, bf, g, *, bb, tb, chunk, l_real):
    bsz, lp, nb = ub.shape
    _, nh, dk, dv = s0.shape
    nblk = lp // tb
    assert bsz % bb == 0 and dv % dk == 0 and chunk & (chunk - 1) == 0 and tb % chunk == 0
    kern = functools.partial(_gla_kernel, bb=bb, tb=tb, chunk=chunk, l_real=l_real, nblk=nblk, dk=dk, dv=dv)
    return pl.pallas_call(
        kern,
        grid=(bsz // bb, nblk),
        in_specs=[pl.BlockSpec((bb, tb, nb), lambda b, j: (b, j, 0)),
                  pl.BlockSpec((bb, nh, dk, dv), lambda b, j: (b, 0, 0, 0)),
                  _const_spec(wf.shape), _const_spec(bf.shape), _const_spec(g.shape)],
        out_specs=[pl.BlockSpec((bb, tb, nh * dv), lambda b, j: (b, j, 0)),
                   pl.BlockSpec((bb, nh, dk, dv), lambda b, j: (b, 0, 0, 0))],
        out_shape=[jax.ShapeDtypeStruct((bsz, lp, nh * dv), F32),
                   jax.ShapeDtypeStruct((bsz, nh, dk, dv), F32)],
        scratch_shapes=[pltpu.VMEM((bb, nh, dk, dv), F32)],
        compiler_params=_params("arbitrary", "arbitrary"),
        name="gla",
    )(ub, s0, wf, bf, g)


def _mix_kernel(ya_ref, yb_ref, um_ref, x_ref, wo_ref, g_ref, b_ref, *rest, alpha, d):
    o_ref = rest[-1]
    merged = _sigmoid(um_ref[:, :d]) * ya_ref[...] + _sigmoid(um_ref[:, d:]) * yb_ref[...]
    mix = jnp.dot(merged.astype(BF16), wo_ref[...], preferred_element_type=F32)
    o_ref[...] = _layer_norm(alpha * x_ref[...] + mix, g_ref[...], b_ref[...])


def _mix(ya, yb, um, x, wo, g, b, *, alpha, tm, total_rows, row_offset=0, into=None):
    m, d = x.shape
    assert m % tm == 0 and row_offset % tm == 0
    row = lambda w: pl.BlockSpec((tm, w), lambda i: (i, 0))
    in_specs = [row(d), row(d), row(2 * d), row(d),
                _const_spec(wo.shape), _const_spec(g.shape), _const_spec(b.shape)]
    args = (ya, yb, um, x, wo, g, b)
    aliases = {}
    if into is not None:
        in_specs.append(pl.BlockSpec(memory_space=pl.ANY))
        aliases = {len(args): 0}
        args = args + (into,)
    return pl.pallas_call(
        functools.partial(_mix_kernel, alpha=alpha, d=d),
        grid=(m // tm,),
        in_specs=in_specs,
        out_specs=pl.BlockSpec((tm, d), lambda i: (i + row_offset // tm, 0)),
        out_shape=jax.ShapeDtypeStruct((total_rows, d), F32),
        input_output_aliases=aliases,
        compiler_params=_params("arbitrary"),
        name="mix_ln1",
    )(*args)


def _router_kernel(x_ref, wr_ref, bias_ref, eidx_ref, gate_ref, rank_ref, cnt_ref, run_scr,
                   *, tm, ne):
    i = pl.program_id(0)

    @pl.when(i == 0)
    def _():
        run_scr[...] = jnp.zeros_like(run_scr)

    logits = jnp.dot(x_ref[...], wr_ref[...], precision=HIGHEST, preferred_element_type=F32)
    s = _sigmoid(logits)
    sel = s + bias_ref[...]
    lane = lax.broadcasted_iota(jnp.int32, (tm, ne), 1)
    lane_f = lane.astype(F32)
    gsz = ne // N_GROUPS
    neg = -jnp.inf

    gscores = []
    for g in range(N_GROUPS):
        ing = (lane >= g * gsz) & (lane < (g + 1) * gsz)
        v = jnp.where(ing, sel, neg)
        m1 = jnp.max(v, axis=1, keepdims=True)
        i1 = jnp.min(jnp.where(v == m1, lane_f, float(ne)), axis=1, keepdims=True)
        m2 = jnp.max(jnp.where(lane_f == i1, neg, v), axis=1, keepdims=True)
        gscores.append(m1 + m2)
    cur = jnp.full((tm, ne), neg, F32)
    for g in range(N_GROUPS):
        ahead = jnp.zeros((tm, 1), F32)
        for g2 in range(N_GROUPS):
            if g2 == g:
                continue
            beats = (gscores[g2] >= gscores[g]) if g2 < g else (gscores[g2] > gscores[g])
            ahead = ahead + beats.astype(F32)
        ing = (lane >= g * gsz) & (lane < (g + 1) * gsz)
        cur = jnp.where(ing & (ahead < float(TOPK_GROUPS)), sel, cur)

    col = lax.broadcasted_iota(jnp.int32, (tm, TOP_K), 1)
    eidx = jnp.zeros((tm, TOP_K), F32)
    wsel = jnp.zeros((tm, TOP_K), F32)
    onehot_sum = jnp.zeros((tm, ne), F32)
    picks = []
    for k in range(TOP_K):
        m = jnp.max(cur, axis=1, keepdims=True)
        ik = jnp.min(jnp.where(cur == m, lane_f, float(ne)), axis=1, keepdims=True)
        hit = lane_f == ik
        wk = jnp.sum(jnp.where(hit, s, 0.0), axis=1, keepdims=True)
        cur = jnp.where(hit, neg, cur)
        onehot_sum = onehot_sum + hit.astype(F32)
        picks.append(hit)
        eidx = jnp.where(col == k, ik, eidx)
        wsel = jnp.where(col == k, wk, wsel)
    gate_ref[...] = wsel / jnp.sum(wsel, axis=1, keepdims=True) * ROUTED_SCALE
    eidx_ref[...] = eidx.astype(jnp.int32)

    ri = lax.broadcasted_iota(jnp.int32, (tm, tm), 0)
    ci = lax.broadcasted_iota(jnp.int32, (tm, tm), 1)
    before = (ci < ri).astype(BF16)
    counts = jnp.dot(before, onehot_sum.astype(BF16), preferred_element_type=F32) + run_scr[...]
    rank = jnp.zeros((tm, TOP_K), F32)
    for k in range(TOP_K):
        rk = jnp.sum(jnp.where(picks[k], counts, 0.0), axis=1, keepdims=True)
        rank = jnp.where(col == k, rk, rank)
    rank_ref[...] = rank.astype(jnp.int32)
    run_scr[...] = run_scr[...] + jnp.sum(onehot_sum, axis=0, keepdims=True)
    cnt_ref[...] = run_scr[...].astype(jnp.int32)


def _router(x, wr, bias, *, tm):
    t, d = x.shape
    ne = wr.shape[1]
    out_row = lambda: pl.BlockSpec((tm, TOP_K), lambda i: (i, 0))
    return pl.pallas_call(
        functools.partial(_router_kernel, tm=tm, ne=ne),
        grid=(t // tm,),
        in_specs=[pl.BlockSpec((tm, d), lambda i: (i, 0)), _const_spec(wr.shape), _const_spec(bias.shape)],
        out_specs=[out_row(), out_row(), out_row(), _const_spec((1, ne))],
        out_shape=[jax.ShapeDtypeStruct((t, TOP_K), jnp.int32),
                   jax.ShapeDtypeStruct((t, TOP_K), F32),
                   jax.ShapeDtypeStruct((t, TOP_K), jnp.int32),
                   jax.ShapeDtypeStruct((1, ne), jnp.int32)],
        scratch_shapes=[pltpu.VMEM((1, ne), F32)],
        compiler_params=_params("arbitrary"),
        name="router",
    )(x, wr, bias)


def _dest_kernel(eidx_ref, rank_ref, start_ref, dest_ref, *, tm, ne):
    lane = lax.broadcasted_iota(jnp.int32, (tm, ne), 1)
    col = lax.broadcasted_iota(jnp.int32, (tm, TOP_K), 1)
    starts = start_ref[...].astype(F32)
    eidx = eidx_ref[...]
    dest = jnp.zeros((tm, TOP_K), F32)
    for k in range(TOP_K):
        hit = lane == eidx[:, k:k + 1]
        sk = jnp.sum(jnp.where(hit, starts, 0.0), axis=1, keepdims=True)
        dest = jnp.where(col == k, sk, dest)
    dest_ref[...] = dest.astype(jnp.int32) + rank_ref[...]


def _dest(eidx, rank, starts, *, tm):
    t = eidx.shape[0]
    ne = starts.shape[1]
    row = lambda: pl.BlockSpec((tm, TOP_K), lambda i: (i, 0))
    return pl.pallas_call(
        functools.partial(_dest_kernel, tm=tm, ne=ne),
        grid=(t // tm,),
        in_specs=[row(), row(), _const_spec((1, ne))],
        out_specs=row(),
        out_shape=jax.ShapeDtypeStruct((t, TOP_K), jnp.int32),
        compiler_params=_params("arbitrary"),
        name="dest",
    )(eidx, rank, starts)


def _router_t_kernel(x_ref, whi_ref, wlo_ref, bias_ref, eidx_ref, gate_ref, rank_ref, cnt_ref, run_scr,
                     *, tm, ne):
    i = pl.program_id(0)

    @pl.when(i == 0)
    def _():
        run_scr[...] = jnp.zeros_like(run_scr)

    x = x_ref[...]
    x_hi = x.astype(BF16)
    x_lo = (x - x_hi.astype(F32)).astype(BF16)
    nt = (((1,), (1,)), ((), ()))
    logits = (lax.dot_general(whi_ref[...], x_hi, nt, preferred_element_type=F32)
              + lax.dot_general(whi_ref[...], x_lo, nt, preferred_element_type=F32)
              + lax.dot_general(wlo_ref[...], x_hi, nt, preferred_element_type=F32))
    s = _sigmoid(logits)
    sel = s + bias_ref[...]
    row = lax.broadcasted_iota(jnp.int32, (ne, tm), 0).astype(F32)
    gsz = ne // N_GROUPS
    neg = -jnp.inf

    def colmax(a):
        return jnp.max(a, axis=0, keepdims=True)

    def colmin(a):
        return jnp.min(a, axis=0, keepdims=True)

    gscores = []
    row_in_group = lax.broadcasted_iota(jnp.int32, (gsz, tm), 0).astype(F32)
    for g in range(N_GROUPS):
        v = sel[g * gsz:(g + 1) * gsz]
        rg = row_in_group + float(g * gsz)
        m1 = colmax(v)
        i1 = colmin(jnp.where(v == m1, rg, float(ne)))
        gscores.append(m1 + colmax(jnp.where(rg == i1, neg, v)))
    kept = []
    for g in range(N_GROUPS):
        ahead = jnp.zeros((1, tm), F32)
        for g2 in range(N_GROUPS):
            if g2 == g:
                continue
            beats = (gscores[g2] >= gscores[g]) if g2 < g else (gscores[g2] > gscores[g])
            ahead = ahead + beats.astype(F32)
        keep = jnp.broadcast_to(ahead < float(TOPK_GROUPS), (gsz, tm))
        kept.append(jnp.where(keep, sel[g * gsz:(g + 1) * gsz], neg))
    cur = jnp.concatenate(kept, axis=0)

    picks, eidx, wsel = [], [], []
    onehot_sum = jnp.zeros((ne, tm), F32)
    for k in range(TOP_K):
        ik = colmin(jnp.where(cur == colmax(cur), row, float(ne)))
        hit = row == ik
        wsel.append(jnp.sum(jnp.where(hit, s, 0.0), axis=0, keepdims=True))
        cur = jnp.where(hit, neg, cur)
        onehot_sum = onehot_sum + hit.astype(F32)
        picks.append(hit)
        eidx.append(ik)
    wsel = jnp.concatenate(wsel, axis=0)
    gate_ref[...] = wsel / jnp.sum(wsel, axis=0, keepdims=True) * ROUTED_SCALE
    eidx_ref[...] = jnp.concatenate(eidx, axis=0).astype(jnp.int32)

    ri = lax.broadcasted_iota(jnp.int32, (tm, tm), 0)
    ci = lax.broadcasted_iota(jnp.int32, (tm, tm), 1)
    earlier = (ri < ci).astype(BF16)
    ohb = onehot_sum.astype(BF16)
    counts = jnp.dot(ohb, earlier, preferred_element_type=F32) + jnp.concatenate([run_scr[...]] * (tm // LANES), axis=1)
    rank = [jnp.sum(jnp.where(picks[k], counts, 0.0), axis=0, keepdims=True) for k in range(TOP_K)]
    rank_ref[...] = jnp.concatenate(rank, axis=0).astype(jnp.int32)
    run_scr[...] = run_scr[...] + jnp.dot(ohb, jnp.ones((tm, LANES), BF16), preferred_element_type=F32)
    cnt_ref[...] = run_scr[...].astype(jnp.int32)


def _router_t(x, w_hi, w_lo, bias_b, *, tm):
    t, d = x.shape
    ne = w_hi.shape[0]
    out_col = lambda: pl.BlockSpec((TOP_K, tm), lambda i: (0, i))
    return pl.pallas_call(
        functools.partial(_router_t_kernel, tm=tm, ne=ne),
        grid=(t // tm,),
        in_specs=[pl.BlockSpec((tm, d), lambda i: (i, 0)), _const_spec(w_hi.shape), _const_spec(w_lo.shape),
                  _const_spec(bias_b.shape)],
        out_specs=[out_col(), out_col(), out_col(), _const_spec((ne, LANES))],
        out_shape=[jax.ShapeDtypeStruct((TOP_K, t), jnp.int32),
                   jax.ShapeDtypeStruct((TOP_K, t), F32),
                   jax.ShapeDtypeStruct((TOP_K, t), jnp.int32),
                   jax.ShapeDtypeStruct((ne, LANES), jnp.int32)],
        scratch_shapes=[pltpu.VMEM((ne, LANES), F32)],
        compiler_params=_params("arbitrary"),
        name="router",
    )(x, w_hi, w_lo, bias_b)


def _dest_t_kernel(eidx_ref, rank_ref, start_ref, dest_ref, *, tm, ne):
    row = lax.broadcasted_iota(jnp.int32, (ne, tm), 0)
    starts = start_ref[...].astype(F32)
    eidx = eidx_ref[...]
    dest = [jnp.sum(jnp.where(row == eidx[k:k + 1, :], starts, 0.0), axis=0, keepdims=True) for k in range(TOP_K)]
    dest_ref[...] = jnp.concatenate(dest, axis=0).astype(jnp.int32) + rank_ref[...]


def _dest_t(eidx, rank, starts_b, *, tm):
    t = eidx.shape[1]
    ne = starts_b.shape[0]
    col = lambda: pl.BlockSpec((TOP_K, tm), lambda i: (0, i))
    return pl.pallas_call(
        functools.partial(_dest_t_kernel, tm=tm, ne=ne),
        grid=(t // tm,),
        in_specs=[col(), col(), _const_spec(starts_b.shape)],
        out_specs=col(),
        out_shape=jax.ShapeDtypeStruct((TOP_K, t), jnp.int32),
        compiler_params=_params("arbitrary"),
        name="dest",
    )(eidx, rank, starts_b)


TOKEN_DTYPE = BF16


def _to_token_tiles(rows):
    chunks = jnp.stack([rows[:, c * LANES:(c + 1) * LANES] for c in range(SUBLANES)], axis=0)
    return jnp.swapaxes(chunks, 0, 1).astype(TOKEN_DTYPE)


def _chunks_of_token_tiles(tiles):
    chunks = jnp.swapaxes(tiles.astype(F32), 0, 1)
    return [chunks[c] for c in range(SUBLANES)]


def _token_copy(src_ref, src_row, dst_ref, dst_row, sem):
    return pltpu.make_async_copy(src_ref.at[pl.ds(src_row, 1)], dst_ref.at[pl.ds(dst_row, 1)], sem)


def _dispatch_kernel(dest_ref, x_ref, xs_ref, xt, sem, *, tt):
    xt[...] = _to_token_tiles(x_ref[...])

    def start(tp, c):
        for u in range(ISSUE_UNROLL):
            t = tp * ISSUE_UNROLL + u
            for k in range(TOP_K):
                _token_copy(xt, t, xs_ref, dest_ref[t * TOP_K + k], sem).start(priority=k % 2)
        return c

    lax.fori_loop(0, tt // ISSUE_UNROLL, start, 0)
    for _ in range(TOP_K):
        pltpu.make_async_copy(xt, xs_ref.at[pl.ds(0, tt)], sem).wait()


def _dispatch(dest_flat, x, n_slots, *, tt):
    t, d = x.shape
    assert d == SUBLANES * LANES
    return pl.pallas_call(
        functools.partial(_dispatch_kernel, tt=tt),
        grid=(t // tt,),
        in_specs=[pl.BlockSpec((tt * TOP_K,), lambda i: (i,), memory_space=pltpu.SMEM),
                  pl.BlockSpec((tt, d), lambda i: (i, 0))],
        out_specs=pl.BlockSpec(memory_space=pl.ANY),
        out_shape=jax.ShapeDtypeStruct((n_slots, SUBLANES, LANES), TOKEN_DTYPE),
        scratch_shapes=[pltpu.VMEM((tt, SUBLANES, LANES), TOKEN_DTYPE), pltpu.SemaphoreType.DMA(())],
        compiler_params=_params("arbitrary"),
        name="dispatch",
    )(dest_flat, x)


EXPERT_SLOTS = 8
EXPERT_AHEAD = 6


def _expert_kernel(base_ref, nblk_ref, cnt_ref, nu_ref, xs_ref, wg_ref, wu_ref, wd_ref, ys_ref,
                   xbuf, ybuf, wgb, wub, wdb, sem_in, sem_out):
    e = pl.program_id(0)
    n_used = nu_ref[0]
    nb = nblk_ref[e]
    base = base_ref[e]
    cnt = cnt_ref[e]
    ahead = EXPERT_AHEAD

    def x_copy(blk, slot):
        return pltpu.make_async_copy(xs_ref.at[pl.ds(blk * MOE_ROWS, MOE_ROWS)], xbuf.at[slot], sem_in.at[slot])

    def y_copy(blk, slot):
        return pltpu.make_async_copy(ybuf.at[slot], ys_ref.at[pl.ds(blk * MOE_ROWS, MOE_ROWS)], sem_out.at[slot])

    @pl.when(e == 0)
    def _():
        for first in range(ahead):
            @pl.when(first < n_used)
            def _():
                x_copy(first, first).start()

    @pl.when(nb > 0)
    def _():
        wgb[...] = wg_ref[0].astype(BF16)
        wub[...] = wu_ref[0].astype(BF16)
        wdb[...] = wd_ref[0].astype(BF16)

    def step(jb, nblocks):
        blks = [base + jb + i for i in range(nblocks)]
        slots = [jnp.bitwise_and(b, EXPERT_SLOTS - 1) for b in blks]
        for b, s in zip(blks, slots):
            x_copy(b, s).wait()
        for b in blks:
            @pl.when(b + ahead < n_used)
            def _():
                x_copy(b + ahead, jnp.bitwise_and(b + ahead, EXPERT_SLOTS - 1)).start()

        x = jnp.concatenate([jnp.concatenate(_chunks_of_token_tiles(xbuf[s]), axis=1) for s in slots], axis=0)
        rows = lax.broadcasted_iota(jnp.int32, x.shape, 0)
        xb = jnp.where(rows < cnt - jb * MOE_ROWS, x, 0.0).astype(BF16)
        g = jnp.dot(xb, wgb[...], preferred_element_type=F32)
        u = jnp.dot(xb, wub[...], preferred_element_type=F32)
        hmid = (g * _sigmoid(g)) * u
        y = jnp.dot(hmid.astype(BF16), wdb[...], preferred_element_type=F32)

        for i, (b, s) in enumerate(zip(blks, slots)):
            @pl.when(b >= EXPERT_SLOTS)
            def _():
                y_copy(b - EXPERT_SLOTS, s).wait()

            ybuf[s] = _to_token_tiles(y[i * MOE_ROWS:(i + 1) * MOE_ROWS])
            y_copy(b, s).start()

    def pair(jp, c):
        step(2 * jp, 2)
        return c

    lax.fori_loop(0, lax.shift_right_logical(nb, 1), pair, 0)

    @pl.when(jnp.bitwise_and(nb, 1) == 1)
    def _():
        step(nb - 1, 1)

    @pl.when(e == pl.num_programs(0) - 1)
    def _():
        for back in range(EXPERT_SLOTS):
            @pl.when(n_used > back)
            def _():
                last = n_used - 1 - back
                y_copy(last, jnp.bitwise_and(last, EXPERT_SLOTS - 1)).wait()


def _experts(base, nblk, cnt, n_used, xs, wg, wu, wd):
    p = xs.shape[0]
    ne, d, de = wg.shape
    assert EXPERT_SLOTS & (EXPERT_SLOTS - 1) == 0 and EXPERT_AHEAD <= EXPERT_SLOTS - 2 and d == SUBLANES * LANES
    ring = pltpu.VMEM((EXPERT_SLOTS, MOE_ROWS, SUBLANES, LANES), TOKEN_DTYPE)
    grid_spec = pltpu.PrefetchScalarGridSpec(
        num_scalar_prefetch=4,
        grid=(ne,),
        in_specs=[pl.BlockSpec(memory_space=pl.ANY),
                  pl.BlockSpec((1, d, de), lambda e, *_: (e, 0, 0)),
                  pl.BlockSpec((1, d, de), lambda e, *_: (e, 0, 0)),
                  pl.BlockSpec((1, de, d), lambda e, *_: (e, 0, 0))],
        out_specs=pl.BlockSpec(memory_space=pl.ANY),
        scratch_shapes=[ring, ring,
                        pltpu.VMEM((d, de), BF16), pltpu.VMEM((d, de), BF16), pltpu.VMEM((de, d), BF16),
                        pltpu.SemaphoreType.DMA((EXPERT_SLOTS,)), pltpu.SemaphoreType.DMA((EXPERT_SLOTS,))],
    )
    return pl.pallas_call(
        _expert_kernel,
        grid_spec=grid_spec,
        out_shape=jax.ShapeDtypeStruct((p, SUBLANES, LANES), TOKEN_DTYPE),
        compiler_params=_params("arbitrary"),
        name="experts",
    )(base, nblk, cnt, n_used, xs, wg, wu, wd)


def _combine_kernel(dest_ref, dest_next_ref, x_ref, gate_ref, ys_ref, wsg_ref, wsu_ref, wsd_ref, g_ref, b_ref,
                    o_head_ref, o_tail_ref, ybuf, sem, *, tt, alpha, head_steps):
    i = pl.program_id(0)
    slot = jnp.bitwise_and(i, 1)

    def gather(dref, s):
        def start(tp, c):
            for u in range(ISSUE_UNROLL):
                t = tp * ISSUE_UNROLL + u
                for k in range(TOP_K):
                    _token_copy(ys_ref, dref[t * TOP_K + k], ybuf.at[s], k * tt + t, sem.at[s]).start(priority=k % 2)
            return c
        lax.fori_loop(0, tt // ISSUE_UNROLL, start, 0)

    @pl.when(i == 0)
    def _():
        gather(dest_ref, 0)

    def prefetch_group(k):
        nxt = 1 - slot
        for t in range(tt):
            _token_copy(ys_ref, dest_next_ref[t * TOP_K + k], ybuf.at[nxt], k * tt + t, sem.at[nxt]).start(priority=t % 2)

    x = x_ref[...]
    xb = x.astype(BF16)
    sg = jnp.dot(xb, wsg_ref[...], preferred_element_type=F32)
    su = jnp.dot(xb, wsu_ref[...], preferred_element_type=F32)
    acc = jnp.dot(((sg * _sigmoid(sg)) * su).astype(BF16), wsd_ref[...], preferred_element_type=F32)
    pltpu.make_async_copy(ys_ref.at[pl.ds(0, tt * TOP_K)], ybuf.at[slot], sem.at[slot]).wait()
    gate = gate_ref[...]
    routed = None
    for k in range(TOP_K):
        prefetch_group(k)
        gk = jnp.broadcast_to(gate[:, k:k + 1], (tt, LANES))
        part = [gk * ch for ch in _chunks_of_token_tiles(ybuf[slot, k * tt:(k + 1) * tt])]
        routed = part if routed is None else [r + p for r, p in zip(routed, part)]
    acc = acc + jnp.concatenate(routed, axis=1)
    out = _layer_norm(alpha * x + acc, g_ref[...], b_ref[...])

    @pl.when(i < head_steps)
    def _():
        o_head_ref[...] = out

    @pl.when(i >= head_steps)
    def _():
        o_tail_ref[...] = out

    @pl.when(i == pl.num_programs(0) - 1)
    def _():
        pltpu.make_async_copy(ys_ref.at[pl.ds(0, tt * TOP_K)], ybuf.at[1 - slot], sem.at[1 - slot]).wait()


def _combine(dest_flat, x, gate, ys, wsg, wsu, wsd, g, b, *, tt, alpha, head_rows):
    t, d = x.shape
    nsteps = t // tt
    head_steps = head_rows // tt
    assert head_rows % tt == 0 and 0 < head_steps < nsteps
    return pl.pallas_call(
        functools.partial(_combine_kernel, tt=tt, alpha=alpha, head_steps=head_steps),
        grid=(nsteps,),
        in_specs=[pl.BlockSpec((tt * TOP_K,), lambda i: (i,), memory_space=pltpu.SMEM),
                  pl.BlockSpec((tt * TOP_K,), lambda i: (jnp.minimum(i + 1, nsteps - 1),),
                               memory_space=pltpu.SMEM),
                  pl.BlockSpec((tt, d), lambda i: (i, 0)),
                  pl.BlockSpec((tt, TOP_K), lambda i: (i, 0)),
                  pl.BlockSpec(memory_space=pl.ANY),
                  _const_spec(wsg.shape), _const_spec(wsu.shape), _const_spec(wsd.shape),
                  _const_spec(g.shape), _const_spec(b.shape)],
        out_specs=[pl.BlockSpec((tt, d), lambda i: (jnp.minimum(i, head_steps - 1), 0)),
                   pl.BlockSpec((tt, d), lambda i: (jnp.maximum(i - head_steps, 0), 0))],
        out_shape=[jax.ShapeDtypeStruct((head_rows, d), F32), jax.ShapeDtypeStruct((t - head_rows, d), F32)],
        scratch_shapes=[pltpu.VMEM((2, tt * TOP_K, SUBLANES, LANES), TOKEN_DTYPE), pltpu.SemaphoreType.DMA((2,))],
        compiler_params=_params("arbitrary"),
        name="combine_ln2",
    )(dest_flat, dest_flat, x, gate, ys, wsg, wsu, wsd, g, b)


def _block_diag_tiles(w):
    n, s, _ = w.shape
    per = MXU_DIM // s
    w = w.reshape(n // per, per, s, s)
    eye = jnp.eye(per, dtype=w.dtype)
    return jnp.einsum('tpij,pq->tpiqj', w, eye).reshape(n // per, MXU_DIM, MXU_DIM)


def _row(v):
    return v.reshape(1, -1)


def _moe(x1, w_router, router_bias, w_gate, w_up, w_down, ws_gate, ws_up, ws_down, ln_g, ln_b, alpha, head_rows):
    t, d = x1.shape
    ne = w_router.shape[1]
    tm = 256
    w_t = w_router.T
    w_hi = w_t.astype(BF16)
    w_lo = (w_t - w_hi.astype(F32)).astype(BF16)
    bias_b = jnp.broadcast_to(router_bias.reshape(ne, 1), (ne, tm))
    eidx_t, gate_t, rank_t, counts = _router_t(x1, w_hi, w_lo, bias_b, tm=tm)
    counts = counts[:, 0]
    nblk = (counts + MOE_ROWS - 1) // MOE_ROWS
    blk_end = jnp.cumsum(nblk)
    blk_base = (blk_end - nblk).astype(jnp.int32)
    n_used = blk_end[-1:].astype(jnp.int32)
    n_slots = ((t * TOP_K) // MOE_ROWS + ne) * MOE_ROWS

    starts_b = jnp.broadcast_to((blk_base * MOE_ROWS).reshape(ne, 1), (ne, tm))
    dest_flat = _dest_t(eidx_t, rank_t, starts_b, tm=tm).T.reshape(-1)
    xs = _dispatch(dest_flat, x1, n_slots, tt=256)
    ys = _experts(blk_base, nblk.astype(jnp.int32), counts, n_used, xs, w_gate, w_up, w_down)
    return _combine(dest_flat, x1, gate_t.T, ys, ws_gate.astype(BF16), ws_up.astype(BF16),
                    ws_down.astype(BF16), _row(ln_g), _row(ln_b), tt=128, alpha=alpha, head_rows=head_rows)


def _pad_time(a, lp):
    return jnp.pad(a, ((0, 0), (0, lp - a.shape[1]), (0, 0)))


def kernel(x_prompt, x_sample, state_rglru_h, state_conv, state_gla, meta_tokens, w_in, conv_w, conv_b,
           rg_wa, rg_ba, rg_wi, rg_bi, rg_lambda, gla_wf2, gla_bf, gla_norm_g, w_out, ln1_g, ln1_b,
           w_router, router_bias, w_gate, w_up, w_down, ws_gate, ws_up, ws_down, ln2_g, ln2_b):
    bp, lp, d = x_prompt.shape
    bs, ls, _ = x_sample.shape
    depth = w_in.shape[0]
    nh, dk, dv = state_gla.shape[2:]
    qk, vw = nh * dk, nh * dv
    lowrank = gla_wf2.shape[1]
    alpha = (2.0 * depth) ** 0.25
    ls_pad = SUBLANES
    assert depth == 1 and ls <= ls_pad and lp % 256 == 0 and N_META % SUBLANES == 0

    xp = x_prompt.reshape(bp * lp, d)
    xs = x_sample.reshape(bs * ls, d)
    xm = meta_tokens.astype(F32)

    hp_l, cp_l, sp_l, hs_l, cs_l, ss_l = [], [], [], [], [], []
    for l in range(depth):
        o1, o2 = 2 * d, 2 * d + 2 * qk + 2 * vw
        wa = w_in[l][:, :o1].astype(BF16)
        wb = jnp.pad(w_in[l][:, o1:o2 + lowrank], ((0, 0), (0, LANES - lowrank))).astype(BF16)
        wm = w_in[l][:, o2 + lowrank:].astype(BF16)
        wf = jnp.pad(gla_wf2[l], ((0, LANES - lowrank), (0, 0))).astype(BF16)
        wa4 = _block_diag_tiles(rg_wa[l]).astype(BF16)
        wi4 = _block_diag_tiles(rg_wi[l]).astype(BF16)
        mixer_w = (conv_w[l], _row(conv_b[l]), wa4, _row(rg_ba[l]), wi4, _row(rg_bi[l]), _row(rg_lambda[l]))
        gla_w = (wf, _row(gla_bf[l]), _row(gla_norm_g[l]))

        ua, ub, _ = _inproj(xm, wa, wb, wm, tm=N_META)
        _, h_m, c_m = _rglru(ua[None], jnp.zeros((1, CONV_W - 1, d), F32), jnp.zeros((1, 1, d), F32),
                             *mixer_w, bb=1, tb=N_META, l_real=N_META)
        _, s_m = _gla(ub[None], jnp.zeros((1, nh, dk, dv), F32), *gla_w,
                      bb=1, tb=N_META, chunk=N_META, l_real=N_META)

        ua, ub, um = _inproj(xp, wa, wb, wm, tm=256)
        ya, h_p, c_p = _rglru(ua.reshape(bp, lp, -1), jnp.broadcast_to(c_m, (bp,) + c_m.shape[1:]),
                              jnp.broadcast_to(h_m, (bp,) + h_m.shape[1:]), *mixer_w,
                              bb=1, tb=256, l_real=lp)
        yb, s_p = _gla(ub.reshape(bp, lp, -1), jnp.broadcast_to(s_m, (bp,) + s_m.shape[1:]), *gla_w,
                       bb=1, tb=256, chunk=GLA_CHUNK, l_real=lp)
        n_tok = bp * lp + bs * ls
        x1 = _mix(ya.reshape(bp * lp, d), yb.reshape(bp * lp, d), um, xp, w_out[l].astype(BF16),
                  _row(ln1_g[l]), _row(ln1_b[l]), alpha=alpha, tm=512, total_rows=n_tok)

        xs_pad = _pad_time(xs.reshape(bs, ls, d), ls_pad).reshape(bs * ls_pad, d)
        ua, ub, um = _inproj(xs_pad, wa, wb, wm, tm=256)
        ya, h_s, c_s = _rglru(ua.reshape(bs, ls_pad, -1), state_conv[l].astype(F32),
                              state_rglru_h[l].astype(F32)[:, None, :], *mixer_w,
                              bb=32, tb=ls_pad, l_real=ls)
        yb, s_s = _gla(ub.reshape(bs, ls_pad, -1), state_gla[l].astype(F32), *gla_w,
                       bb=8, tb=ls_pad, chunk=ls_pad, l_real=ls)
        unpad = lambda a: a.reshape(bs, ls_pad, -1)[:, :ls].reshape(bs * ls, -1)
        x1 = _mix(unpad(ya), unpad(yb), unpad(um), xs, w_out[l].astype(BF16),
                  _row(ln1_g[l]), _row(ln1_b[l]), alpha=alpha, tm=512, total_rows=n_tok,
                  row_offset=bp * lp, into=x1)

        xp, xs = _moe(x1, w_router[l], router_bias[l], w_gate[l], w_up[l], w_down[l],
                      ws_gate[l], ws_up[l], ws_down[l], ln2_g[l], ln2_b[l], alpha, bp * lp)

        hp_l.append(h_p[:, 0]); cp_l.append(c_p); sp_l.append(s_p)
        hs_l.append(h_s[:, 0]); cs_l.append(c_s); ss_l.append(s_s)

    y_prompt = xp.reshape(bp, lp, d)
    y_sample = xs.reshape(bs, ls, d)
    return (y_prompt, y_sample,
            jnp.stack(hp_l).astype(state_rglru_h.dtype), jnp.stack(cp_l).astype(state_conv.dtype),
            jnp.stack(sp_l).astype(state_gla.dtype),
            jnp.stack(hs_l).astype(state_rglru_h.dtype), jnp.stack(cs_l).astype(state_conv.dtype),
            jnp.stack(ss_l).astype(state_gla.dtype))
```

```python
import functools

import jax
import jax.numpy as jnp
from jax import lax
from jax.experimental import pallas as pl
from jax.experimental.pallas import tpu as pltpu

F32 = jnp.float32
BF16 = jnp.bfloat16
HIGHEST = lax.Precision.HIGHEST

N_META = 16
CONV_W = 4
RG_C = 8.0
RNN_BLOCKS = 16
GLA_HEADS = 4
GLA_TAU = 16.0
GLA_CHUNK = 64
N_GROUPS = 8
TOPK_GROUPS = 4
TOP_K = 8
ROUTED_SCALE = 2.5
LN_EPS = 1e-5
RMS_EPS = 1e-6

LANES = 128
SUBLANES = 8
MXU_DIM = 256
VMEM_LIMIT_BYTES = 56 * 1024 * 1024

MOE_ROWS = 128
ISSUE_UNROLL = 4


def _params(*sem):
    return pltpu.CompilerParams(dimension_semantics=sem, vmem_limit_bytes=VMEM_LIMIT_BYTES)


def _sigmoid(x):
    return jax.nn.sigmoid(x)


def _softplus(z):
    return jnp.maximum(z, 0.0) + jnp.log1p(jnp.exp(-jnp.abs(z)))


def _layer_norm(z, g, b):
    mu = jnp.mean(z, axis=-1, keepdims=True)
    zc = z - mu
    var = jnp.mean(zc * zc, axis=-1, keepdims=True)
    return zc * lax.rsqrt(var + LN_EPS) * g + b


def _const_spec(shape):
    nd = len(shape)
    return pl.BlockSpec(shape, lambda *_: (0,) * nd)


def _inproj_kernel(x_ref, wa_ref, wb_ref, wm_ref, ua_ref, ub_ref, um_ref):
    xb = x_ref[...].astype(BF16)
    ua_ref[...] = jnp.dot(xb, wa_ref[...], preferred_element_type=F32)
    ub_ref[...] = jnp.dot(xb, wb_ref[...], preferred_element_type=F32)
    um_ref[...] = jnp.dot(xb, wm_ref[...], preferred_element_type=F32).astype(um_ref.dtype)


def _inproj(x2d, wa, wb, wm, tm, um_dtype=F32):
    m, d = x2d.shape
    na, nb, nm = wa.shape[1], wb.shape[1], wm.shape[1]
    return pl.pallas_call(
        _inproj_kernel,
        grid=(m // tm,),
        in_specs=[pl.BlockSpec((tm, d), lambda i: (i, 0)),
                  _const_spec(wa.shape), _const_spec(wb.shape), _const_spec(wm.shape)],
        out_specs=[pl.BlockSpec((tm, na), lambda i: (i, 0)),
                   pl.BlockSpec((tm, nb), lambda i: (i, 0)),
                   pl.BlockSpec((tm, nm), lambda i: (i, 0))],
        out_shape=[jax.ShapeDtypeStruct((m, na), F32),
                   jax.ShapeDtypeStruct((m, nb), F32),
                   jax.ShapeDtypeStruct((m, nm), um_dtype)],
        compiler_params=_params("arbitrary"),
        name="inproj",
    )(x2d, wa, wb, wm)


def _rglru_kernel(ua_ref, cbuf_ref, h0_ref, cw_ref, cb_ref, wa_ref, ba_ref, wi_ref, bi_ref, lam_ref,
                  ya_ref, hlast_ref, cnew_ref, ext_scr, h_scr, *, bb, tb, l_last, nblk, d):
    j = pl.program_id(1)
    ntail = CONV_W - 1
    rows = bb * tb

    @pl.when(j == 0)
    def _():
        h_scr[...] = h0_ref[...]
        ext_scr[:, SUBLANES - ntail:SUBLANES, :] = cbuf_ref[...]

    xr = ua_ref[:, :, :d]
    gr = ua_ref[:, :, d:].reshape(rows, d)
    ext_scr[:, SUBLANES:, :] = xr
    xc = cb_ref[...] + cw_ref[ntail:ntail + 1, :] * xr
    for s in range(1, CONV_W):
        xc = xc + cw_ref[ntail - s:ntail - s + 1, :] * ext_scr[:, SUBLANES - s:SUBLANES - s + tb, :]
    xc = xc.reshape(rows, d)

    xcb = xc.astype(BF16)
    nt = d // MXU_DIM
    ra = jnp.concatenate(
        [jnp.dot(xcb[:, q * MXU_DIM:(q + 1) * MXU_DIM], wa_ref[q], preferred_element_type=F32)
         for q in range(nt)], axis=-1)
    ia = jnp.concatenate(
        [jnp.dot(xcb[:, q * MXU_DIM:(q + 1) * MXU_DIM], wi_ref[q], preferred_element_type=F32)
         for q in range(nt)], axis=-1)
    r = _sigmoid(ra + ba_ref[...])
    ig = _sigmoid(ia + bi_ref[...])
    log_a = (-RG_C) * r * _softplus(-lam_ref[...])
    a = jnp.exp(log_a)
    z = -jnp.tanh(log_a) * (a * a + 1.0)
    mult = jnp.where(z > 0.0, z * lax.rsqrt(z), 0.0)
    b = mult * (ig * xc)

    t_in = jnp.bitwise_and(lax.broadcasted_iota(jnp.int32, (rows, d), 0), SUBLANES - 1)
    s = 1
    while s < SUBLANES:
        a_sh = jnp.where(t_in >= s, pltpu.roll(a, s, 0), 1.0)
        b_sh = jnp.where(t_in >= s, pltpu.roll(b, s, 0), 0.0)
        b = b + a * b_sh
        a = a * a_sh
        s *= 2
    a = a.reshape(bb, tb, d)
    b = b.reshape(bb, tb, d)
    carry = h_scr[...]
    tiles = []
    for i in range(tb // SUBLANES):
        rs = slice(i * SUBLANES, (i + 1) * SUBLANES)
        hi = b[:, rs, :] + a[:, rs, :] * carry
        carry = hi[:, SUBLANES - 1:SUBLANES, :]
        tiles.append(hi)
    h = jnp.concatenate(tiles, axis=1)
    h_scr[...] = carry
    ya_ref[...] = (h * jax.nn.gelu(gr, approximate=True).reshape(bb, tb, d)).astype(ya_ref.dtype)

    @pl.when(j == nblk - 1)
    def _():
        hlast_ref[...] = h[:, l_last - 1:l_last, :]
        cnew_ref[...] = ext_scr[:, SUBLANES + l_last - ntail:SUBLANES + l_last, :]

    tail = ext_scr[:, tb + SUBLANES - ntail:tb + SUBLANES, :]
    ext_scr[:, SUBLANES - ntail:SUBLANES, :] = tail


def _rglru(ua, cbuf, h0, cw, cb, wa4, ba, wi4, bi, lam, *, bb, tb, l_real, y_dtype=F32):
    bsz, lp, n2 = ua.shape
    d = n2 // 2
    nblk = lp // tb
    l_last = l_real - (nblk - 1) * tb
    assert tb & (tb - 1) == 0 and bsz % bb == 0
    kern = functools.partial(_rglru_kernel, bb=bb, tb=tb, l_last=l_last, nblk=nblk, d=d)
    return pl.pallas_call(
        kern,
        grid=(bsz // bb, nblk),
        in_specs=[pl.BlockSpec((bb, tb, n2), lambda b, j: (b, j, 0)),
                  pl.BlockSpec((bb, CONV_W - 1, d), lambda b, j: (b, 0, 0)),
                  pl.BlockSpec((bb, 1, d), lambda b, j: (b, 0, 0)),
                  _const_spec(cw.shape), _const_spec(cb.shape),
                  _const_spec(wa4.shape), _const_spec(ba.shape),
                  _const_spec(wi4.shape), _const_spec(bi.shape), _const_spec(lam.shape)],
        out_specs=[pl.BlockSpec((bb, tb, d), lambda b, j: (b, j, 0)),
                   pl.BlockSpec((bb, 1, d), lambda b, j: (b, 0, 0)),
                   pl.BlockSpec((bb, CONV_W - 1, d), lambda b, j: (b, 0, 0))],
        out_shape=[jax.ShapeDtypeStruct((bsz, lp, d), y_dtype),
                   jax.ShapeDtypeStruct((bsz, 1, d), F32),
                   jax.ShapeDtypeStruct((bsz, CONV_W - 1, d), F32)],
        scratch_shapes=[pltpu.VMEM((bb, tb + SUBLANES, d), F32), pltpu.VMEM((bb, 1, d), F32)],
        compiler_params=_params("arbitrary", "arbitrary"),
        name="rglru",
    )(ua, cbuf, h0, cw, cb, wa4, ba, wi4, bi, lam)


def _gla_kernel(ub_ref, s0_ref, wf_ref, bf_ref, g_ref, yb_ref, sout_ref, s_scr,
                *, bb, tb, chunk, l_real, nblk, dk, dv):
    j = pl.program_id(1)
    nh = GLA_HEADS
    qk = nh * dk
    vw = nh * dv

    @pl.when(j == 0)
    def _():
        s_scr[...] = s0_ref[...]

    off_k, off_v, off_g, off_f = qk, 2 * qk, 2 * qk + vw, 2 * qk + 2 * vw
    merge = bb > 1 and nblk == 1 and tb == chunk
    nrow = bb * tb if merge else tb
    nc = nrow // chunk
    shift = chunk.bit_length() - 1
    ri = lax.broadcasted_iota(jnp.int32, (nrow, nrow), 0)
    ci = lax.broadcasted_iota(jnp.int32, (nrow, nrow), 1)
    keep = (lax.shift_right_logical(ri, shift) == lax.shift_right_logical(ci, shift)) & (ri >= ci)
    tril = keep.astype(F32)
    row_chunk = lax.shift_right_logical(lax.broadcasted_iota(jnp.int32, (nrow, dk), 0), shift)
    mid = chunk // 2
    scale = dk ** -0.5

    def by_chunk(a):
        if nc == 1:
            return a
        return jnp.concatenate([jnp.where(row_chunk == c, a, 0.0) for c in range(nc)], axis=1)

    def per_chunk_row(a, r):
        return jnp.concatenate([jnp.broadcast_to(a[c * chunk + r:c * chunk + r + 1, :], (chunk, a.shape[1]))
                                for c in range(nc)], axis=0)

    for bi in range(1 if merge else bb):
        def cols(c0, c1, bi=bi):
            if merge:
                return ub_ref[:, :, c0:c1].reshape(nrow, c1 - c0)
            return ub_ref[bi, :, c0:c1]

        fl = cols(off_f, off_f + LANES).astype(BF16)
        logit = jnp.dot(fl, wf_ref[...], preferred_element_type=F32) + bf_ref[...]
        log_f = (jnp.minimum(logit, 0.0) - jnp.log1p(jnp.exp(-jnp.abs(logit)))) * (1.0 / GLA_TAU)
        if l_real < tb * nblk:
            rows = lax.broadcasted_iota(jnp.int32, log_f.shape, 0)
            rows = (jnp.bitwise_and(rows, tb - 1) if merge else rows) + j * tb
            log_f = jnp.where(rows < l_real, log_f, 0.0)
        bcum = jnp.dot(tril, log_f, precision=HIGHEST, preferred_element_type=F32)
        ref = per_chunk_row(bcum, mid)
        last = per_chunk_row(bcum, chunk - 1)
        e_inter = jnp.exp(bcum)
        e_q = jnp.exp(bcum - ref)
        e_k = jnp.exp(ref - bcum)
        e_dec = jnp.exp(last - bcum)

        for h in range(nh):
            hk = slice(h * dk, (h + 1) * dk)
            q = cols(h * dk, (h + 1) * dk) * scale
            k = cols(off_k + h * dk, off_k + (h + 1) * dk)
            vb = cols(off_v + h * dv, off_v + (h + 1) * dv).astype(BF16)
            go = cols(off_g + h * dv, off_g + (h + 1) * dv)
            att = lax.dot_general((q * e_q[:, hk]).astype(BF16), (k * e_k[:, hk]).astype(BF16),
                                  (((1,), (1,)), ((), ())), preferred_element_type=F32)
            att = jnp.where(keep, att, 0.0)
            o = jnp.dot(att.astype(BF16), vb, preferred_element_type=F32)
            kd = by_chunk(k * e_dec[:, hk])
            kv = lax.dot_general(kd.astype(BF16), vb, (((0,), (0,)), ((), ())),
                                 preferred_element_type=F32)
            states = [s_scr[c, h] for c in range(nc)] if merge else [s_scr[bi, h]]
            for c in range(nc):
                drow = jnp.exp(bcum[c * chunk + chunk - 1:c * chunk + chunk, hk])
                dec = jnp.transpose(jnp.broadcast_to(drow, (dk, dk)))
                dec = jnp.concatenate([dec] * (dv // dk), axis=1)
                s_next = dec * states[c] + kv[c * dk:(c + 1) * dk]
                if merge:
                    s_scr[c, h] = s_next
                else:
                    states.append(s_next)
            if not merge:
                s_scr[bi, h] = states[nc]
            qi = by_chunk(q * e_inter[:, hk])
            s_in = jnp.concatenate(states[:nc], axis=0).astype(BF16)
            o = o + jnp.dot(qi.astype(BF16), s_in, preferred_element_type=F32)
            o = o * lax.rsqrt(jnp.mean(o * o, axis=-1, keepdims=True) + RMS_EPS) * g_ref[...]
            y = (o * (go * _sigmoid(go))).astype(yb_ref.dtype)
            if merge:
                yb_ref[:, :, h * dv:(h + 1) * dv] = y.reshape(bb, tb, dv)
            else:
                yb_ref[bi, :, h * dv:(h + 1) * dv] = y

    @pl.when(j == nblk - 1)
    def _():
        sout_ref[...] = s_scr[...]


def _gla(ub, s0, wf, bf, g, *, bb, tb, chunk, l_real, y_dtype=F32):
    bsz, lp, nb = ub.shape
    _, nh, dk, dv = s0.shape
    nblk = lp // tb
    assert bsz % bb == 0 and dv % dk == 0 and chunk & (chunk - 1) == 0 and tb % chunk == 0
    kern = functools.partial(_gla_kernel, bb=bb, tb=tb, chunk=chunk, l_real=l_real, nblk=nblk, dk=dk, dv=dv)
    return pl.pallas_call(
        kern,
        grid=(bsz // bb, nblk),
        in_specs=[pl.BlockSpec((bb, tb, nb), lambda b, j: (b, j, 0)),
                  pl.BlockSpec((bb, nh, dk, dv), lambda b, j: (b, 0, 0, 0)),
                  _const_spec(wf.shape), _const_spec(bf.shape), _const_spec(g.shape)],
        out_specs=[pl.BlockSpec((bb, tb, nh * dv), lambda b, j: (b, j, 0)),
                   pl.BlockSpec((bb, nh, dk, dv), lambda b, j: (b, 0, 0, 0))],
        out_shape=[jax.ShapeDtypeStruct((bsz, lp, nh * dv), y_dtype),
                   jax.ShapeDtypeStruct((bsz, nh, dk, dv), F32)],
        scratch_shapes=[pltpu.VMEM((bb, nh, dk, dv), F32)],
        compiler_params=_params("arbitrary", "arbitrary"),
        name="gla",
    )(ub, s0, wf, bf, g)


def _mix_kernel(ya_ref, yb_ref, um_ref, x_ref, wo_ref, g_ref, b_ref, *rest, alpha, d):
    o_ref = rest[-1]
    um = um_ref[...].astype(F32)
    merged = (_sigmoid(um[:, :d]) * ya_ref[...].astype(F32)
              + _sigmoid(um[:, d:]) * yb_ref[...].astype(F32))
    mix = jnp.dot(merged.astype(BF16), wo_ref[...], preferred_element_type=F32)
    o_ref[...] = _layer_norm(alpha * x_ref[...] + mix, g_ref[...], b_ref[...])


def _mix(ya, yb, um, x, wo, g, b, *, alpha, tm, total_rows, row_offset=0, into=None):
    m, d = x.shape
    assert m % tm == 0 and row_offset % tm == 0
    row = lambda w: pl.BlockSpec((tm, w), lambda i: (i, 0))
    in_specs = [row(d), row(d), row(2 * d), row(d),
                _const_spec(wo.shape), _const_spec(g.shape), _const_spec(b.shape)]
    args = (ya, yb, um, x, wo, g, b)
    aliases = {}
    if into is not None:
        in_specs.append(pl.BlockSpec(memory_space=pl.ANY))
        aliases = {len(args): 0}
        args = args + (into,)
    return pl.pallas_call(
        functools.partial(_mix_kernel, alpha=alpha, d=d),
        grid=(m // tm,),
        in_specs=in_specs,
        out_specs=pl.BlockSpec((tm, d), lambda i: (i + row_offset // tm, 0)),
        out_shape=jax.ShapeDtypeStruct((total_rows, d), F32),
        input_output_aliases=aliases,
        compiler_params=_params("arbitrary"),
        name="mix_ln1",
    )(*args)


def _router_kernel(x_ref, wr_ref, bias_ref, eidx_ref, gate_ref, rank_ref, cnt_ref, run_scr,
                   *, tm, ne):
    i = pl.program_id(0)

    @pl.when(i == 0)
    def _():
        run_scr[...] = jnp.zeros_like(run_scr)

    logits = jnp.dot(x_ref[...], wr_ref[...], precision=HIGHEST, preferred_element_type=F32)
    s = _sigmoid(logits)
    sel = s + bias_ref[...]
    lane = lax.broadcasted_iota(jnp.int32, (tm, ne), 1)
    lane_f = lane.astype(F32)
    gsz = ne // N_GROUPS
    neg = -jnp.inf

    gscores = []
    for g in range(N_GROUPS):
        ing = (lane >= g * gsz) & (lane < (g + 1) * gsz)
        v = jnp.where(ing, sel, neg)
        m1 = jnp.max(v, axis=1, keepdims=True)
        i1 = jnp.min(jnp.where(v == m1, lane_f, float(ne)), axis=1, keepdims=True)
        m2 = jnp.max(jnp.where(lane_f == i1, neg, v), axis=1, keepdims=True)
        gscores.append(m1 + m2)
    cur = jnp.full((tm, ne), neg, F32)
    for g in range(N_GROUPS):
        ahead = jnp.zeros((tm, 1), F32)
        for g2 in range(N_GROUPS):
            if g2 == g:
                continue
            beats = (gscores[g2] >= gscores[g]) if g2 < g else (gscores[g2] > gscores[g])
            ahead = ahead + beats.astype(F32)
        ing = (lane >= g * gsz) & (lane < (g + 1) * gsz)
        cur = jnp.where(ing & (ahead < float(TOPK_GROUPS)), sel, cur)

    col = lax.broadcasted_iota(jnp.int32, (tm, TOP_K), 1)
    eidx = jnp.zeros((tm, TOP_K), F32)
    wsel = jnp.zeros((tm, TOP_K), F32)
    onehot_sum = jnp.zeros((tm, ne), F32)
    picks = []
    for k in range(TOP_K):
        m = jnp.max(cur, axis=1, keepdims=True)
        ik = jnp.min(jnp.where(cur == m, lane_f, float(ne)), axis=1, keepdims=True)
        hit = lane_f == ik
        wk = jnp.sum(jnp.where(hit, s, 0.0), axis=1, keepdims=True)
        cur = jnp.where(hit, neg, cur)
        onehot_sum = onehot_sum + hit.astype(F32)
        picks.append(hit)
        eidx = jnp.where(col == k, ik, eidx)
        wsel = jnp.where(col == k, wk, wsel)
    gate_ref[...] = wsel / jnp.sum(wsel, axis=1, keepdims=True) * ROUTED_SCALE
    eidx_ref[...] = eidx.astype(jnp.int32)

    ri = lax.broadcasted_iota(jnp.int32, (tm, tm), 0)
    ci = lax.broadcasted_iota(jnp.int32, (tm, tm), 1)
    before = (ci < ri).astype(BF16)
    counts = jnp.dot(before, onehot_sum.astype(BF16), preferred_element_type=F32) + run_scr[...]
    rank = jnp.zeros((tm, TOP_K), F32)
    for k in range(TOP_K):
        rk = jnp.sum(jnp.where(picks[k], counts, 0.0), axis=1, keepdims=True)
        rank = jnp.where(col == k, rk, rank)
    rank_ref[...] = rank.astype(jnp.int32)
    run_scr[...] = run_scr[...] + jnp.sum(onehot_sum, axis=0, keepdims=True)
    cnt_ref[...] = run_scr[...].astype(jnp.int32)


def _router(x, wr, bias, *, tm):
    t, d = x.shape
    ne = wr.shape[1]
    out_row = lambda: pl.BlockSpec((tm, TOP_K), lambda i: (i, 0))
    return pl.pallas_call(
        functools.partial(_router_kernel, tm=tm, ne=ne),
        grid=(t // tm,),
        in_specs=[pl.BlockSpec((tm, d), lambda i: (i, 0)), _const_spec(wr.shape), _const_spec(bias.shape)],
        out_specs=[out_row(), out_row(), out_row(), _const_spec((1, ne))],
        out_shape=[jax.ShapeDtypeStruct((t, TOP_K), jnp.int32),
                   jax.ShapeDtypeStruct((t, TOP_K), F32),
                   jax.ShapeDtypeStruct((t, TOP_K), jnp.int32),
                   jax.ShapeDtypeStruct((1, ne), jnp.int32)],
        scratch_shapes=[pltpu.VMEM((1, ne), F32)],
        compiler_params=_params("arbitrary"),
        name="router",
    )(x, wr, bias)


def _dest_kernel(eidx_ref, rank_ref, start_ref, dest_ref, *, tm, ne):
    lane = lax.broadcasted_iota(jnp.int32, (tm, ne), 1)
    col = lax.broadcasted_iota(jnp.int32, (tm, TOP_K), 1)
    starts = start_ref[...].astype(F32)
    eidx = eidx_ref[...]
    dest = jnp.zeros((tm, TOP_K), F32)
    for k in range(TOP_K):
        hit = lane == eidx[:, k:k + 1]
        sk = jnp.sum(jnp.where(hit, starts, 0.0), axis=1, keepdims=True)
        dest = jnp.where(col == k, sk, dest)
    dest_ref[...] = dest.astype(jnp.int32) + rank_ref[...]


def _dest(eidx, rank, starts, *, tm):
    t = eidx.shape[0]
    ne = starts.shape[1]
    row = lambda: pl.BlockSpec((tm, TOP_K), lambda i: (i, 0))
    return pl.pallas_call(
        functools.partial(_dest_kernel, tm=tm, ne=ne),
        grid=(t // tm,),
        in_specs=[row(), row(), _const_spec((1, ne))],
        out_specs=row(),
        out_shape=jax.ShapeDtypeStruct((t, TOP_K), jnp.int32),
        compiler_params=_params("arbitrary"),
        name="dest",
    )(eidx, rank, starts)


def _router_t_kernel(x_ref, whi_ref, wlo_ref, bias_ref, eidx_ref, gate_ref, rank_ref, cnt_ref, run_scr,
                     *, tm, ne):
    i = pl.program_id(0)

    @pl.when(i == 0)
    def _():
        run_scr[...] = jnp.zeros_like(run_scr)

    x = x_ref[...]
    x_hi = x.astype(BF16)
    x_lo = (x - x_hi.astype(F32)).astype(BF16)
    nt = (((1,), (1,)), ((), ()))
    logits = (lax.dot_general(whi_ref[...], x_hi, nt, preferred_element_type=F32)
              + lax.dot_general(whi_ref[...], x_lo, nt, preferred_element_type=F32)
              + lax.dot_general(wlo_ref[...], x_hi, nt, preferred_element_type=F32))
    s = _sigmoid(logits)
    sel = s + bias_ref[...]
    row = lax.broadcasted_iota(jnp.int32, (ne, tm), 0).astype(F32)
    gsz = ne // N_GROUPS
    neg = -jnp.inf

    def colmax(a):
        return jnp.max(a, axis=0, keepdims=True)

    def colmin(a):
        return jnp.min(a, axis=0, keepdims=True)

    gscores = []
    row_in_group = lax.broadcasted_iota(jnp.int32, (gsz, tm), 0).astype(F32)
    for g in range(N_GROUPS):
        v = sel[g * gsz:(g + 1) * gsz]
        rg = row_in_group + float(g * gsz)
        m1 = colmax(v)
        i1 = colmin(jnp.where(v == m1, rg, float(ne)))
        gscores.append(m1 + colmax(jnp.where(rg == i1, neg, v)))
    kept = []
    for g in range(N_GROUPS):
        ahead = jnp.zeros((1, tm), F32)
        for g2 in range(N_GROUPS):
            if g2 == g:
                continue
            beats = (gscores[g2] >= gscores[g]) if g2 < g else (gscores[g2] > gscores[g])
            ahead = ahead + beats.astype(F32)
        keep = jnp.broadcast_to(ahead < float(TOPK_GROUPS), (gsz, tm))
        kept.append(jnp.where(keep, sel[g * gsz:(g + 1) * gsz], neg))
    cur = jnp.concatenate(kept, axis=0)

    picks, eidx, wsel = [], [], []
    onehot_sum = jnp.zeros((ne, tm), F32)
    for k in range(TOP_K):
        ik = colmin(jnp.where(cur == colmax(cur), row, float(ne)))
        hit = row == ik
        wsel.append(jnp.sum(jnp.where(hit, s, 0.0), axis=0, keepdims=True))
        cur = jnp.where(hit, neg, cur)
        onehot_sum = onehot_sum + hit.astype(F32)
        picks.append(hit)
        eidx.append(ik)
    wsel = jnp.concatenate(wsel, axis=0)
    gate_ref[...] = wsel / jnp.sum(wsel, axis=0, keepdims=True) * ROUTED_SCALE
    eidx_ref[...] = jnp.concatenate(eidx, axis=0).astype(jnp.int32)

    ri = lax.broadcasted_iota(jnp.int32, (tm, tm), 0)
    ci = lax.broadcasted_iota(jnp.int32, (tm, tm), 1)
    earlier = (ri < ci).astype(BF16)
    ohb = onehot_sum.astype(BF16)
    counts = jnp.dot(ohb, earlier, preferred_element_type=F32) + jnp.concatenate([run_scr[...]] * (tm // LANES), axis=1)
    rank = [jnp.sum(jnp.where(picks[k], counts, 0.0), axis=0, keepdims=True) for k in range(TOP_K)]
    rank_ref[...] = jnp.concatenate(rank, axis=0).astype(jnp.int32)
    run_scr[...] = run_scr[...] + jnp.dot(ohb, jnp.ones((tm, LANES), BF16), preferred_element_type=F32)
    cnt_ref[...] = run_scr[...].astype(jnp.int32)


def _router_t(x, w_hi, w_lo, bias_b, *, tm):
    t, d = x.shape
    ne = w_hi.shape[0]
    out_col = lambda: pl.BlockSpec((TOP_K, tm), lambda i: (0, i))
    return pl.pallas_call(
        functools.partial(_router_t_kernel, tm=tm, ne=ne),
        grid=(t // tm,),
        in_specs=[pl.BlockSpec((tm, d), lambda i: (i, 0)), _const_spec(w_hi.shape), _const_spec(w_lo.shape),
                  _const_spec(bias_b.shape)],
        out_specs=[out_col(), out_col(), out_col(), _const_spec((ne, LANES))],
        out_shape=[jax.ShapeDtypeStruct((TOP_K, t), jnp.int32),
                   jax.ShapeDtypeStruct((TOP_K, t), F32),
                   jax.ShapeDtypeStruct((TOP_K, t), jnp.int32),
                   jax.ShapeDtypeStruct((ne, LANES), jnp.int32)],
        scratch_shapes=[pltpu.VMEM((ne, LANES), F32)],
        compiler_params=_params("arbitrary"),
        name="router",
    )(x, w_hi, w_lo, bias_b)


def _dest_t_kernel(eidx_ref, rank_ref, start_ref, dest_ref, *, tm, ne):
    row = lax.broadcasted_iota(jnp.int32, (ne, tm), 0)
    starts = start_ref[...].astype(F32)
    eidx = eidx_ref[...]
    dest = [jnp.sum(jnp.where(row == eidx[k:k + 1, :], starts, 0.0), axis=0, keepdims=True) for k in range(TOP_K)]
    dest_ref[...] = jnp.concatenate(dest, axis=0).astype(jnp.int32) + rank_ref[...]


def _dest_t(eidx, rank, starts_b, *, tm):
    t = eidx.shape[1]
    ne = starts_b.shape[0]
    col = lambda: pl.BlockSpec((TOP_K, tm), lambda i: (0, i))
    return pl.pallas_call(
        functools.partial(_dest_t_kernel, tm=tm, ne=ne),
        grid=(t // tm,),
        in_specs=[col(), col(), _const_spec(starts_b.shape)],
        out_specs=col(),
        out_shape=jax.ShapeDtypeStruct((TOP_K, t), jnp.int32),
        compiler_params=_params("arbitrary"),
        name="dest",
    )(eidx, rank, starts_b)


TOKEN_DTYPE = F32


def _to_token_tiles(rows):
    chunks = jnp.stack([rows[:, c * LANES:(c + 1) * LANES] for c in range(SUBLANES)], axis=0)
    return jnp.swapaxes(chunks, 0, 1).astype(TOKEN_DTYPE)


def _chunks_of_token_tiles(tiles):
    chunks = jnp.swapaxes(tiles.astype(F32), 0, 1)
    return [chunks[c] for c in range(SUBLANES)]


def _token_copy(src_ref, src_row, dst_ref, dst_row, sem):
    return pltpu.make_async_copy(src_ref.at[pl.ds(src_row, 1)], dst_ref.at[pl.ds(dst_row, 1)], sem)


def _dispatch_kernel(dest_ref, x_ref, xs_ref, xt, sem, *, tt):
    xt[...] = _to_token_tiles(x_ref[...])

    def start(tp, c):
        for u in range(ISSUE_UNROLL):
            t = tp * ISSUE_UNROLL + u
            for k in range(TOP_K):
                _token_copy(xt, t, xs_ref, dest_ref[t * TOP_K + k], sem).start(priority=k % 2)
        return c

    lax.fori_loop(0, tt // ISSUE_UNROLL, start, 0)
    for _ in range(TOP_K):
        pltpu.make_async_copy(xt, xs_ref.at[pl.ds(0, tt)], sem).wait()


def _dispatch(dest_flat, x, n_slots, *, tt):
    t, d = x.shape
    assert d == SUBLANES * LANES
    return pl.pallas_call(
        functools.partial(_dispatch_kernel, tt=tt),
        grid=(t // tt,),
        in_specs=[pl.BlockSpec((tt * TOP_K,), lambda i: (i,), memory_space=pltpu.SMEM),
                  pl.BlockSpec((tt, d), lambda i: (i, 0))],
        out_specs=pl.BlockSpec(memory_space=pl.ANY),
        out_shape=jax.ShapeDtypeStruct((n_slots, SUBLANES, LANES), TOKEN_DTYPE),
        scratch_shapes=[pltpu.VMEM((tt, SUBLANES, LANES), TOKEN_DTYPE), pltpu.SemaphoreType.DMA(())],
        compiler_params=_params("arbitrary"),
        name="dispatch",
    )(dest_flat, x)


EXPERT_SLOTS = 8
EXPERT_AHEAD = 6


def _expert_kernel(base_ref, nblk_ref, cnt_ref, nu_ref, xs_ref, wg_ref, wu_ref, wd_ref, ys_ref,
                   xbuf, ybuf, wgb, wub, wdb, sem_in, sem_out):
    e = pl.program_id(0)
    n_used = nu_ref[0]
    nb = nblk_ref[e]
    base = base_ref[e]
    cnt = cnt_ref[e]
    ahead = EXPERT_AHEAD

    def x_copy(blk, slot):
        return pltpu.make_async_copy(xs_ref.at[pl.ds(blk * MOE_ROWS, MOE_ROWS)], xbuf.at[slot], sem_in.at[slot])

    def y_copy(blk, slot):
        return pltpu.make_async_copy(ybuf.at[slot], ys_ref.at[pl.ds(blk * MOE_ROWS, MOE_ROWS)], sem_out.at[slot])

    @pl.when(e == 0)
    def _():
        for first in range(ahead):
            @pl.when(first < n_used)
            def _():
                x_copy(first, first).start()

    @pl.when(nb > 0)
    def _():
        wgb[...] = wg_ref[0].astype(BF16)
        wub[...] = wu_ref[0].astype(BF16)
        wdb[...] = wd_ref[0].astype(BF16)

    def step(jb, nblocks):
        blks = [base + jb + i for i in range(nblocks)]
        slots = [jnp.bitwise_and(b, EXPERT_SLOTS - 1) for b in blks]
        for b, s in zip(blks, slots):
            x_copy(b, s).wait()
        for b in blks:
            @pl.when(b + ahead < n_used)
            def _():
                x_copy(b + ahead, jnp.bitwise_and(b + ahead, EXPERT_SLOTS - 1)).start()

        x = jnp.concatenate([jnp.concatenate(_chunks_of_token_tiles(xbuf[s]), axis=1) for s in slots], axis=0)
        rows = lax.broadcasted_iota(jnp.int32, x.shape, 0)
        xb = jnp.where(rows < cnt - jb * MOE_ROWS, x, 0.0).astype(BF16)
        g = jnp.dot(xb, wgb[...], preferred_element_type=F32)
        u = jnp.dot(xb, wub[...], preferred_element_type=F32)
        hmid = (g * _sigmoid(g)) * u
        y = jnp.dot(hmid.astype(BF16), wdb[...], preferred_element_type=F32)

        for i, (b, s) in enumerate(zip(blks, slots)):
            @pl.when(b >= EXPERT_SLOTS)
            def _():
                y_copy(b - EXPERT_SLOTS, s).wait()

            ybuf[s] = _to_token_tiles(y[i * MOE_ROWS:(i + 1) * MOE_ROWS])
            y_copy(b, s).start()

    def pair(jp, c):
        step(2 * jp, 2)
        return c

    lax.fori_loop(0, lax.shift_right_logical(nb, 1), pair, 0)

    @pl.when(jnp.bitwise_and(nb, 1) == 1)
    def _():
        step(nb - 1, 1)

    @pl.when(e == pl.num_programs(0) - 1)
    def _():
        for back in range(EXPERT_SLOTS):
            @pl.when(n_used > back)
            def _():
                last = n_used - 1 - back
                y_copy(last, jnp.bitwise_and(last, EXPERT_SLOTS - 1)).wait()


def _experts(base, nblk, cnt, n_used, xs, wg, wu, wd):
    p = xs.shape[0]
    ne, d, de = wg.shape
    assert EXPERT_SLOTS & (EXPERT_SLOTS - 1) == 0 and EXPERT_AHEAD <= EXPERT_SLOTS - 2 and d == SUBLANES * LANES
    ring = pltpu.VMEM((EXPERT_SLOTS, MOE_ROWS, SUBLANES, LANES), TOKEN_DTYPE)
    grid_spec = pltpu.PrefetchScalarGridSpec(
        num_scalar_prefetch=4,
        grid=(ne,),
        in_specs=[pl.BlockSpec(memory_space=pl.ANY),
                  pl.BlockSpec((1, d, de), lambda e, *_: (e, 0, 0)),
                  pl.BlockSpec((1, d, de), lambda e, *_: (e, 0, 0)),
                  pl.BlockSpec((1, de, d), lambda e, *_: (e, 0, 0))],
        out_specs=pl.BlockSpec(memory_space=pl.ANY),
        scratch_shapes=[ring, ring,
                        pltpu.VMEM((d, de), BF16), pltpu.VMEM((d, de), BF16), pltpu.VMEM((de, d), BF16),
                        pltpu.SemaphoreType.DMA((EXPERT_SLOTS,)), pltpu.SemaphoreType.DMA((EXPERT_SLOTS,))],
    )
    return pl.pallas_call(
        _expert_kernel,
        grid_spec=grid_spec,
        out_shape=jax.ShapeDtypeStruct((p, SUBLANES, LANES), TOKEN_DTYPE),
        compiler_params=_params("arbitrary"),
        name="experts",
    )(base, nblk, cnt, n_used, xs, wg, wu, wd)


def _combine_kernel(dest_ref, dest_next_ref, x_ref, gate_ref, ys_ref, wsg_ref, wsu_ref, wsd_ref, g_ref, b_ref,
                    o_head_ref, o_tail_ref, ybuf, sem, *, tt, alpha, head_steps):
    i = pl.program_id(0)
    slot = jnp.bitwise_and(i, 1)

    def gather(dref, s):
        def start(tp, c):
            for u in range(ISSUE_UNROLL):
                t = tp * ISSUE_UNROLL + u
                for k in range(TOP_K):
                    _token_copy(ys_ref, dref[t * TOP_K + k], ybuf.at[s], k * tt + t, sem.at[s]).start(priority=k % 2)
            return c
        lax.fori_loop(0, tt // ISSUE_UNROLL, start, 0)

    @pl.when(i == 0)
    def _():
        gather(dest_ref, 0)

    def prefetch_group(k):
        nxt = 1 - slot
        for t in range(tt):
            _token_copy(ys_ref, dest_next_ref[t * TOP_K + k], ybuf.at[nxt], k * tt + t, sem.at[nxt]).start(priority=t % 2)

    x = x_ref[...]
    xb = x.astype(BF16)
    sg = jnp.dot(xb, wsg_ref[...], preferred_element_type=F32)
    su = jnp.dot(xb, wsu_ref[...], preferred_element_type=F32)
    acc = jnp.dot(((sg * _sigmoid(sg)) * su).astype(BF16), wsd_ref[...], preferred_element_type=F32)
    pltpu.make_async_copy(ys_ref.at[pl.ds(0, tt * TOP_K)], ybuf.at[slot], sem.at[slot]).wait()
    gate = gate_ref[...]
    routed = None
    for k in range(TOP_K):
        prefetch_group(k)
        gk = jnp.broadcast_to(gate[:, k:k + 1], (tt, LANES))
        part = [gk * ch for ch in _chunks_of_token_tiles(ybuf[slot, k * tt:(k + 1) * tt])]
        routed = part if routed is None else [r + p for r, p in zip(routed, part)]
    acc = acc + jnp.concatenate(routed, axis=1)
    out = _layer_norm(alpha * x + acc, g_ref[...], b_ref[...])

    @pl.when(i < head_steps)
    def _():
        o_head_ref[...] = out

    @pl.when(i >= head_steps)
    def _():
        o_tail_ref[...] = out

    @pl.when(i == pl.num_programs(0) - 1)
    def _():
        pltpu.make_async_copy(ys_ref.at[pl.ds(0, tt * TOP_K)], ybuf.at[1 - slot], sem.at[1 - slot]).wait()


def _combine(dest_flat, x, gate, ys, wsg, wsu, wsd, g, b, *, tt, alpha, head_rows):
    t, d = x.shape
    nsteps = t // tt
    head_steps = head_rows // tt
    assert head_rows % tt == 0 and 0 < head_steps < nsteps
    return pl.pallas_call(
        functools.partial(_combine_kernel, tt=tt, alpha=alpha, head_steps=head_steps),
        grid=(nsteps,),
        in_specs=[pl.BlockSpec((tt * TOP_K,), lambda i: (i,), memory_space=pltpu.SMEM),
                  pl.BlockSpec((tt * TOP_K,), lambda i: (jnp.minimum(i + 1, nsteps - 1),),
                               memory_space=pltpu.SMEM),
                  pl.BlockSpec((tt, d), lambda i: (i, 0)),
                  pl.BlockSpec((tt, TOP_K), lambda i: (i, 0)),
                  pl.BlockSpec(memory_space=pl.ANY),
                  _const_spec(wsg.shape), _const_spec(wsu.shape), _const_spec(wsd.shape),
                  _const_spec(g.shape), _const_spec(b.shape)],
        out_specs=[pl.BlockSpec((tt, d), lambda i: (jnp.minimum(i, head_steps - 1), 0)),
                   pl.BlockSpec((tt, d), lambda i: (jnp.maximum(i - head_steps, 0), 0))],
        out_shape=[jax.ShapeDtypeStruct((head_rows, d), F32), jax.ShapeDtypeStruct((t - head_rows, d), F32)],
        scratch_shapes=[pltpu.VMEM((2, tt * TOP_K, SUBLANES, LANES), TOKEN_DTYPE), pltpu.SemaphoreType.DMA((2,))],
        compiler_params=_params("arbitrary"),
        name="combine_ln2",
    )(dest_flat, dest_flat, x, gate, ys, wsg, wsu, wsd, g, b)


def _block_diag_tiles(w):
    n, s, _ = w.shape
    per = MXU_DIM // s
    w = w.reshape(n // per, per, s, s)
    eye = jnp.eye(per, dtype=w.dtype)
    return jnp.einsum('tpij,pq->tpiqj', w, eye).reshape(n // per, MXU_DIM, MXU_DIM)


def _row(v):
    return v.reshape(1, -1)


def _moe(x1, w_router, router_bias, w_gate, w_up, w_down, ws_gate, ws_up, ws_down, ln_g, ln_b, alpha, head_rows):
    t, d = x1.shape
    ne = w_router.shape[1]
    tm = 256
    w_t = w_router.T
    w_hi = w_t.astype(BF16)
    w_lo = (w_t - w_hi.astype(F32)).astype(BF16)
    bias_b = jnp.broadcast_to(router_bias.reshape(ne, 1), (ne, tm))
    eidx_t, gate_t, rank_t, counts = _router_t(x1, w_hi, w_lo, bias_b, tm=tm)
    counts = counts[:, 0]
    nblk = (counts + MOE_ROWS - 1) // MOE_ROWS
    blk_end = jnp.cumsum(nblk)
    blk_base = (blk_end - nblk).astype(jnp.int32)
    n_used = blk_end[-1:].astype(jnp.int32)
    n_slots = ((t * TOP_K) // MOE_ROWS + ne) * MOE_ROWS

    starts_b = jnp.broadcast_to((blk_base * MOE_ROWS).reshape(ne, 1), (ne, tm))
    dest_flat = _dest_t(eidx_t, rank_t, starts_b, tm=tm).T.reshape(-1)
    xs = _dispatch(dest_flat, x1, n_slots, tt=256)
    ys = _experts(blk_base, nblk.astype(jnp.int32), counts, n_used, xs, w_gate, w_up, w_down)
    return _combine(dest_flat, x1, gate_t.T, ys, ws_gate.astype(BF16), ws_up.astype(BF16),
                    ws_down.astype(BF16), _row(ln_g), _row(ln_b), tt=128, alpha=alpha, head_rows=head_rows)


def _pad_time(a, lp):
    return jnp.pad(a, ((0, 0), (0, lp - a.shape[1]), (0, 0)))


def kernel(x_prompt, x_sample, state_rglru_h, state_conv, state_gla, meta_tokens, w_in, conv_w, conv_b,
           rg_wa, rg_ba, rg_wi, rg_bi, rg_lambda, gla_wf2, gla_bf, gla_norm_g, w_out, ln1_g, ln1_b,
           w_router, router_bias, w_gate, w_up, w_down, ws_gate, ws_up, ws_down, ln2_g, ln2_b):
    bp, lp, d = x_prompt.shape
    bs, ls, _ = x_sample.shape
    depth = w_in.shape[0]
    nh, dk, dv = state_gla.shape[2:]
    qk, vw = nh * dk, nh * dv
    lowrank = gla_wf2.shape[1]
    alpha = (2.0 * depth) ** 0.25
    ls_pad = SUBLANES
    assert depth == 1 and ls <= ls_pad and lp % 256 == 0 and N_META % SUBLANES == 0

    xp = x_prompt.reshape(bp * lp, d)
    xs = x_sample.reshape(bs * ls, d)
    xm = meta_tokens.astype(F32)

    hp_l, cp_l, sp_l, hs_l, cs_l, ss_l = [], [], [], [], [], []
    for l in range(depth):
        o1, o2 = 2 * d, 2 * d + 2 * qk + 2 * vw
        wa = w_in[l][:, :o1].astype(BF16)
        wb = jnp.pad(w_in[l][:, o1:o2 + lowrank], ((0, 0), (0, LANES - lowrank))).astype(BF16)
        wm = w_in[l][:, o2 + lowrank:].astype(BF16)
        wf = jnp.pad(gla_wf2[l], ((0, LANES - lowrank), (0, 0))).astype(BF16)
        wa4 = _block_diag_tiles(rg_wa[l]).astype(BF16)
        wi4 = _block_diag_tiles(rg_wi[l]).astype(BF16)
        mixer_w = (conv_w[l], _row(conv_b[l]), wa4, _row(rg_ba[l]), wi4, _row(rg_bi[l]), _row(rg_lambda[l]))
        gla_w = (wf, _row(gla_bf[l]), _row(gla_norm_g[l]))

        ua, ub, _ = _inproj(xm, wa, wb, wm, tm=N_META)
        _, h_m, c_m = _rglru(ua[None], jnp.zeros((1, CONV_W - 1, d), F32), jnp.zeros((1, 1, d), F32),
                             *mixer_w, bb=1, tb=N_META, l_real=N_META)
        _, s_m = _gla(ub[None], jnp.zeros((1, nh, dk, dv), F32), *gla_w,
                      bb=1, tb=N_META, chunk=N_META, l_real=N_META)

        ua, ub, um = _inproj(xp, wa, wb, wm, tm=256, um_dtype=BF16)
        ya, h_p, c_p = _rglru(ua.reshape(bp, lp, -1), jnp.broadcast_to(c_m, (bp,) + c_m.shape[1:]),
                              jnp.broadcast_to(h_m, (bp,) + h_m.shape[1:]), *mixer_w,
                              bb=1, tb=256, l_real=lp, y_dtype=BF16)
        yb, s_p = _gla(ub.reshape(bp, lp, -1), jnp.broadcast_to(s_m, (bp,) + s_m.shape[1:]), *gla_w,
                       bb=1, tb=256, chunk=GLA_CHUNK, l_real=lp, y_dtype=BF16)
        n_tok = bp * lp + bs * ls
        x1 = _mix(ya.reshape(bp * lp, d), yb.reshape(bp * lp, d), um, xp, w_out[l].astype(BF16),
                  _row(ln1_g[l]), _row(ln1_b[l]), alpha=alpha, tm=512, total_rows=n_tok)

        xs_pad = _pad_time(xs.reshape(bs, ls, d), ls_pad).reshape(bs * ls_pad, d)
        ua, ub, um = _inproj(xs_pad, wa, wb, wm, tm=256)
        ya, h_s, c_s = _rglru(ua.reshape(bs, ls_pad, -1), state_conv[l].astype(F32),
                              state_rglru_h[l].astype(F32)[:, None, :], *mixer_w,
                              bb=32, tb=ls_pad, l_real=ls)
        yb, s_s = _gla(ub.reshape(bs, ls_pad, -1), state_gla[l].astype(F32), *gla_w,
                       bb=8, tb=ls_pad, chunk=ls_pad, l_real=ls)
        unpad = lambda a: a.reshape(bs, ls_pad, -1)[:, :ls].reshape(bs * ls, -1)
        x1 = _mix(unpad(ya), unpad(yb), unpad(um), xs, w_out[l].astype(BF16),
                  _row(ln1_g[l]), _row(ln1_b[l]), alpha=alpha, tm=512, total_rows=n_tok,
                  row_offset=bp * lp, into=x1)

        xp, xs = _moe(x1, w_router[l], router_bias[l], w_gate[l], w_up[l], w_down[l],
                      ws_gate[l], ws_up[l], ws_down[l], ln2_g[l], ln2_b[l], alpha, bp * lp)

        hp_l.append(h_p[:, 0]); cp_l.append(c_p); sp_l.append(s_p)
        hs_l.append(h_s[:, 0]); cs_l.append(c_s); ss_l.append(s_s)

    y_prompt = xp.reshape(bp, lp, d)
    y_sample = xs.reshape(bs, ls, d)
    return (y_prompt, y_sample,
            jnp.stack(hp_l).astype(state_rglru_h.dtype), jnp.stack(cp_l).astype(state_conv.dtype),
            jnp.stack(sp_l).astype(state_gla.dtype),
            jnp.stack(hs_l).astype(state_rglru_h.dtype), jnp.stack(cs_l).astype(state_conv.dtype),
            jnp.stack(ss_l).astype(state_gla.dtype))
```

```python
import functools

import jax
import jax.numpy as jnp
from jax import lax
from jax.experimental import pallas as pl
from jax.experimental.pallas import tpu as pltpu

F32 = jnp.float32
BF16 = jnp.bfloat16
HIGHEST = lax.Precision.HIGHEST

N_META = 16
CONV_W = 4
RG_C = 8.0
RNN_BLOCKS = 16
GLA_HEADS = 4
GLA_TAU = 16.0
GLA_CHUNK = 64
N_GROUPS = 8
TOPK_GROUPS = 4
TOP_K = 8
ROUTED_SCALE = 2.5
LN_EPS = 1e-5
RMS_EPS = 1e-6

LANES = 128
SUBLANES = 8
MXU_DIM = 256
VMEM_LIMIT_BYTES = 56 * 1024 * 1024

MOE_ROWS = 128
ISSUE_UNROLL = 4


def _params(*sem):
    return pltpu.CompilerParams(dimension_semantics=sem, vmem_limit_bytes=VMEM_LIMIT_BYTES)


def _sigmoid(x):
    return jax.nn.sigmoid(x)


def _softplus(z):
    return jnp.maximum(z, 0.0) + jnp.log1p(jnp.exp(-jnp.abs(z)))


def _layer_norm(z, g, b):
    mu = jnp.mean(z, axis=-1, keepdims=True)
    zc = z - mu
    var = jnp.mean(zc * zc, axis=-1, keepdims=True)
    return zc * lax.rsqrt(var + LN_EPS) * g + b


def _const_spec(shape):
    nd = len(shape)
    return pl.BlockSpec(shape, lambda *_: (0,) * nd)


def _inproj_kernel(x_ref, wa_ref, wb_ref, wm_ref, ua_ref, ub_ref, um_ref):
    xb = x_ref[...].astype(BF16)
    ua_ref[...] = jnp.dot(xb, wa_ref[...], preferred_element_type=F32)
    ub_ref[...] = jnp.dot(xb, wb_ref[...], preferred_element_type=F32)
    um_ref[...] = jnp.dot(xb, wm_ref[...], preferred_element_type=F32).astype(um_ref.dtype)


def _inproj(x2d, wa, wb, wm, tm, um_dtype=F32):
    m, d = x2d.shape
    na, nb, nm = wa.shape[1], wb.shape[1], wm.shape[1]
    return pl.pallas_call(
        _inproj_kernel,
        grid=(m // tm,),
        in_specs=[pl.BlockSpec((tm, d), lambda i: (i, 0)),
                  _const_spec(wa.shape), _const_spec(wb.shape), _const_spec(wm.shape)],
        out_specs=[pl.BlockSpec((tm, na), lambda i: (i, 0)),
                   pl.BlockSpec((tm, nb), lambda i: (i, 0)),
                   pl.BlockSpec((tm, nm), lambda i: (i, 0))],
        out_shape=[jax.ShapeDtypeStruct((m, na), F32),
                   jax.ShapeDtypeStruct((m, nb), F32),
                   jax.ShapeDtypeStruct((m, nm), um_dtype)],
        compiler_params=_params("arbitrary"),
        name="inproj",
    )(x2d, wa, wb, wm)


def _rglru_kernel(ua_ref, cbuf_ref, h0_ref, cw_ref, cb_ref, wa_ref, ba_ref, wi_ref, bi_ref, lam_ref,
                  ya_ref, hlast_ref, cnew_ref, ext_scr, h_scr, *, bb, tb, l_last, nblk, d):
    j = pl.program_id(1)
    ntail = CONV_W - 1
    rows = bb * tb

    @pl.when(j == 0)
    def _():
        h_scr[...] = h0_ref[...]
        ext_scr[:, SUBLANES - ntail:SUBLANES, :] = cbuf_ref[...]

    xr = ua_ref[:, :, :d]
    gr = ua_ref[:, :, d:].reshape(rows, d)
    ext_scr[:, SUBLANES:, :] = xr
    xc = cb_ref[...] + cw_ref[ntail:ntail + 1, :] * xr
    for s in range(1, CONV_W):
        xc = xc + cw_ref[ntail - s:ntail - s + 1, :] * ext_scr[:, SUBLANES - s:SUBLANES - s + tb, :]
    xc = xc.reshape(rows, d)

    xcb = xc.astype(BF16)
    nt = d // MXU_DIM
    ra = jnp.concatenate(
        [jnp.dot(xcb[:, q * MXU_DIM:(q + 1) * MXU_DIM], wa_ref[q], preferred_element_type=F32)
         for q in range(nt)], axis=-1)
    ia = jnp.concatenate(
        [jnp.dot(xcb[:, q * MXU_DIM:(q + 1) * MXU_DIM], wi_ref[q], preferred_element_type=F32)
         for q in range(nt)], axis=-1)
    r = _sigmoid(ra + ba_ref[...])
    ig = _sigmoid(ia + bi_ref[...])
    log_a = (-RG_C) * r * _softplus(-lam_ref[...])
    a = jnp.exp(log_a)
    z = -jnp.tanh(log_a) * (a * a + 1.0)
    mult = jnp.where(z > 0.0, z * lax.rsqrt(z), 0.0)
    b = mult * (ig * xc)

    t_in = jnp.bitwise_and(lax.broadcasted_iota(jnp.int32, (rows, d), 0), SUBLANES - 1)
    s = 1
    while s < SUBLANES:
        a_sh = jnp.where(t_in >= s, pltpu.roll(a, s, 0), 1.0)
        b_sh = jnp.where(t_in >= s, pltpu.roll(b, s, 0), 0.0)
        b = b + a * b_sh
        a = a * a_sh
        s *= 2
    a = a.reshape(bb, tb, d)
    b = b.reshape(bb, tb, d)
    carry = h_scr[...]
    tiles = []
    for i in range(tb // SUBLANES):
        rs = slice(i * SUBLANES, (i + 1) * SUBLANES)
        hi = b[:, rs, :] + a[:, rs, :] * carry
        carry = hi[:, SUBLANES - 1:SUBLANES, :]
        tiles.append(hi)
    h = jnp.concatenate(tiles, axis=1)
    h_scr[...] = carry
    ya_ref[...] = (h * jax.nn.gelu(gr, approximate=True).reshape(bb, tb, d)).astype(ya_ref.dtype)

    @pl.when(j == nblk - 1)
    def _():
        hlast_ref[...] = h[:, l_last - 1:l_last, :]
        cnew_ref[...] = ext_scr[:, SUBLANES + l_last - ntail:SUBLANES + l_last, :]

    tail = ext_scr[:, tb + SUBLANES - ntail:tb + SUBLANES, :]
    ext_scr[:, SUBLANES - ntail:SUBLANES, :] = tail


def _rglru(ua, cbuf, h0, cw, cb, wa4, ba, wi4, bi, lam, *, bb, tb, l_real, y_dtype=F32):
    bsz, lp, n2 = ua.shape
    d = n2 // 2
    nblk = lp // tb
    l_last = l_real - (nblk - 1) * tb
    assert tb & (tb - 1) == 0 and bsz % bb == 0
    kern = functools.partial(_rglru_kernel, bb=bb, tb=tb, l_last=l_last, nblk=nblk, d=d)
    return pl.pallas_call(
        kern,
        grid=(bsz // bb, nblk),
        in_specs=[pl.BlockSpec((bb, tb, n2), lambda b, j: (b, j, 0)),
                  pl.BlockSpec((bb, CONV_W - 1, d), lambda b, j: (b, 0, 0)),
                  pl.BlockSpec((bb, 1, d), lambda b, j: (b, 0, 0)),
                  _const_spec(cw.shape), _const_spec(cb.shape),
                  _const_spec(wa4.shape), _const_spec(ba.shape),
                  _const_spec(wi4.shape), _const_spec(bi.shape), _const_spec(lam.shape)],
        out_specs=[pl.BlockSpec((bb, tb, d), lambda b, j: (b, j, 0)),
                   pl.BlockSpec((bb, 1, d), lambda b, j: (b, 0, 0)),
                   pl.BlockSpec((bb, CONV_W - 1, d), lambda b, j: (b, 0, 0))],
        out_shape=[jax.ShapeDtypeStruct((bsz, lp, d), y_dtype),
                   jax.ShapeDtypeStruct((bsz, 1, d), F32),
                   jax.ShapeDtypeStruct((bsz, CONV_W - 1, d), F32)],
        scratch_shapes=[pltpu.VMEM((bb, tb + SUBLANES, d), F32), pltpu.VMEM((bb, 1, d), F32)],
        compiler_params=_params("arbitrary", "arbitrary"),
        name="rglru",
    )(ua, cbuf, h0, cw, cb, wa4, ba, wi4, bi, lam)


def _gla_kernel(ub_ref, s0_ref, wf_ref, bf_ref, g_ref, yb_ref, sout_ref, s_scr,
                *, bb, tb, chunk, l_real, nblk, dk, dv):
    j = pl.program_id(1)
    nh = GLA_HEADS
    qk = nh * dk
    vw = nh * dv

    @pl.when(j == 0)
    def _():
        s_scr[...] = s0_ref[...]

    off_k, off_v, off_g, off_f = qk, 2 * qk, 2 * qk + vw, 2 * qk + 2 * vw
    merge = bb > 1 and nblk == 1 and tb == chunk
    nrow = bb * tb if merge else tb
    nc = nrow // chunk
    shift = chunk.bit_length() - 1
    ri = lax.broadcasted_iota(jnp.int32, (nrow, nrow), 0)
    ci = lax.broadcasted_iota(jnp.int32, (nrow, nrow), 1)
    keep = (lax.shift_right_logical(ri, shift) == lax.shift_right_logical(ci, shift)) & (ri >= ci)
    tril = keep.astype(F32)
    row_chunk = lax.shift_right_logical(lax.broadcasted_iota(jnp.int32, (nrow, dk), 0), shift)
    mid = chunk // 2
    scale = dk ** -0.5

    def by_chunk(a):
        if nc == 1:
            return a
        return jnp.concatenate([jnp.where(row_chunk == c, a, 0.0) for c in range(nc)], axis=1)

    def per_chunk_row(a, r):
        return jnp.concatenate([jnp.broadcast_to(a[c * chunk + r:c * chunk + r + 1, :], (chunk, a.shape[1]))
                                for c in range(nc)], axis=0)

    for bi in range(1 if merge else bb):
        def cols(c0, c1, bi=bi):
            if merge:
                return ub_ref[:, :, c0:c1].reshape(nrow, c1 - c0)
            return ub_ref[bi, :, c0:c1]

        fl = cols(off_f, off_f + LANES).astype(BF16)
        logit = jnp.dot(fl, wf_ref[...], preferred_element_type=F32) + bf_ref[...]
        log_f = (jnp.minimum(logit, 0.0) - jnp.log1p(jnp.exp(-jnp.abs(logit)))) * (1.0 / GLA_TAU)
        if l_real < tb * nblk:
            rows = lax.broadcasted_iota(jnp.int32, log_f.shape, 0)
            rows = (jnp.bitwise_and(rows, tb - 1) if merge else rows) + j * tb
            log_f = jnp.where(rows < l_real, log_f, 0.0)
        bcum = jnp.dot(tril, log_f, precision=HIGHEST, preferred_element_type=F32)
        ref = per_chunk_row(bcum, mid)
        last = per_chunk_row(bcum, chunk - 1)
        e_inter = jnp.exp(bcum)
        e_q = jnp.exp(bcum - ref)
        e_k = jnp.exp(ref - bcum)
        e_dec = jnp.exp(last - bcum)

        for h in range(nh):
            hk = slice(h * dk, (h + 1) * dk)
            q = cols(h * dk, (h + 1) * dk) * scale
            k = cols(off_k + h * dk, off_k + (h + 1) * dk)
            vb = cols(off_v + h * dv, off_v + (h + 1) * dv).astype(BF16)
            go = cols(off_g + h * dv, off_g + (h + 1) * dv)
            att = lax.dot_general((q * e_q[:, hk]).astype(BF16), (k * e_k[:, hk]).astype(BF16),
                                  (((1,), (1,)), ((), ())), preferred_element_type=F32)
            att = jnp.where(keep, att, 0.0)
            o = jnp.dot(att.astype(BF16), vb, preferred_element_type=F32)
            kd = by_chunk(k * e_dec[:, hk])
            kv = lax.dot_general(kd.astype(BF16), vb, (((0,), (0,)), ((), ())),
                                 preferred_element_type=F32)
            states = [s_scr[c, h] for c in range(nc)] if merge else [s_scr[bi, h]]
            for c in range(nc):
                drow = jnp.exp(bcum[c * chunk + chunk - 1:c * chunk + chunk, hk])
                dec = jnp.transpose(jnp.broadcast_to(drow, (dk, dk)))
                dec = jnp.concatenate([dec] * (dv // dk), axis=1)
                s_next = dec * states[c] + kv[c * dk:(c + 1) * dk]
                if merge:
                    s_scr[c, h] = s_next
                else:
                    states.append(s_next)
            if not merge:
                s_scr[bi, h] = states[nc]
            qi = by_chunk(q * e_inter[:, hk])
            s_in = jnp.concatenate(states[:nc], axis=0).astype(BF16)
            o = o + jnp.dot(qi.astype(BF16), s_in, preferred_element_type=F32)
            o = o * lax.rsqrt(jnp.mean(o * o, axis=-1, keepdims=True) + RMS_EPS) * g_ref[...]
            y = (o * (go * _sigmoid(go))).astype(yb_ref.dtype)
            if merge:
                yb_ref[:, :, h * dv:(h + 1) * dv] = y.reshape(bb, tb, dv)
            else:
                yb_ref[bi, :, h * dv:(h + 1) * dv] = y

    @pl.when(j == nblk - 1)
    def _():
        sout_ref[...] = s_scr[...]


def _gla(ub, s0, wf, bf, g, *, bb, tb, chunk, l_real, y_dtype=F32):
    bsz, lp, nb = ub.shape
    _, nh, dk, dv = s0.shape
    nblk = lp // tb
    assert bsz % bb == 0 and dv % dk == 0 and chunk & (chunk - 1) == 0 and tb % chunk == 0
    kern = functools.partial(_gla_kernel, bb=bb, tb=tb, chunk=chunk, l_real=l_real, nblk=nblk, dk=dk, dv=dv)
    return pl.pallas_call(
        kern,
        grid=(bsz // bb, nblk),
        in_specs=[pl.BlockSpec((bb, tb, nb), lambda b, j: (b, j, 0)),
                  pl.BlockSpec((bb, nh, dk, dv), lambda b, j: (b, 0, 0, 0)),
                  _const_spec(wf.shape), _const_spec(bf.shape), _const_spec(g.shape)],
        out_specs=[pl.BlockSpec((bb, tb, nh * dv), lambda b, j: (b, j, 0)),
                   pl.BlockSpec((bb, nh, dk, dv), lambda b, j: (b, 0, 0, 0))],
        out_shape=[jax.ShapeDtypeStruct((bsz, lp, nh * dv), y_dtype),
                   jax.ShapeDtypeStruct((bsz, nh, dk, dv), F32)],
        scratch_shapes=[pltpu.VMEM((bb, nh, dk, dv), F32)],
        compiler_params=_params("arbitrary", "arbitrary"),
        name="gla",
    )(ub, s0, wf, bf, g)


def _mix_kernel(ya_ref, yb_ref, um_ref, x_ref, wo_ref, g_ref, b_ref, *rest, alpha, d):
    o_ref = rest[-1]
    um = um_ref[...].astype(F32)
    merged = (_sigmoid(um[:, :d]) * ya_ref[...].astype(F32)
              + _sigmoid(um[:, d:]) * yb_ref[...].astype(F32))
    mix = jnp.dot(merged.astype(BF16), wo_ref[...], preferred_element_type=F32)
    o_ref[...] = _layer_norm(alpha * x_ref[...] + mix, g_ref[...], b_ref[...])


def _mix(ya, yb, um, x, wo, g, b, *, alpha, tm, total_rows, row_offset=0, into=None):
    m, d = x.shape
    assert m % tm == 0 and row_offset % tm == 0
    row = lambda w: pl.BlockSpec((tm, w), lambda i: (i, 0))
    in_specs = [row(d), row(d), row(2 * d), row(d),
                _const_spec(wo.shape), _const_spec(g.shape), _const_spec(b.shape)]
    args = (ya, yb, um, x, wo, g, b)
    aliases = {}
    if into is not None:
        in_specs.append(pl.BlockSpec(memory_space=pl.ANY))
        aliases = {len(args): 0}
        args = args + (into,)
    return pl.pallas_call(
        functools.partial(_mix_kernel, alpha=alpha, d=d),
        grid=(m // tm,),
        in_specs=in_specs,
        out_specs=pl.BlockSpec((tm, d), lambda i: (i + row_offset // tm, 0)),
        out_shape=jax.ShapeDtypeStruct((total_rows, d), F32),
        input_output_aliases=aliases,
        compiler_params=_params("arbitrary"),
        name="mix_ln1",
    )(*args)


def _router_kernel(x_ref, wr_ref, bias_ref, eidx_ref, gate_ref, rank_ref, cnt_ref, run_scr,
                   *, tm, ne):
    i = pl.program_id(0)

    @pl.when(i == 0)
    def _():
        run_scr[...] = jnp.zeros_like(run_scr)

    logits = jnp.dot(x_ref[...], wr_ref[...], precision=HIGHEST, preferred_element_type=F32)
    s = _sigmoid(logits)
    sel = s + bias_ref[...]
    lane = lax.broadcasted_iota(jnp.int32, (tm, ne), 1)
    lane_f = lane.astype(F32)
    gsz = ne // N_GROUPS
    neg = -jnp.inf

    gscores = []
    for g in range(N_GROUPS):
        ing = (lane >= g * gsz) & (lane < (g + 1) * gsz)
        v = jnp.where(ing, sel, neg)
        m1 = jnp.max(v, axis=1, keepdims=True)
        i1 = jnp.min(jnp.where(v == m1, lane_f, float(ne)), axis=1, keepdims=True)
        m2 = jnp.max(jnp.where(lane_f == i1, neg, v), axis=1, keepdims=True)
        gscores.append(m1 + m2)
    cur = jnp.full((tm, ne), neg, F32)
    for g in range(N_GROUPS):
        ahead = jnp.zeros((tm, 1), F32)
        for g2 in range(N_GROUPS):
            if g2 == g:
                continue
            beats = (gscores[g2] >= gscores[g]) if g2 < g else (gscores[g2] > gscores[g])
            ahead = ahead + beats.astype(F32)
        ing = (lane >= g * gsz) & (lane < (g + 1) * gsz)
        cur = jnp.where(ing & (ahead < float(TOPK_GROUPS)), sel, cur)

    col = lax.broadcasted_iota(jnp.int32, (tm, TOP_K), 1)
    eidx = jnp.zeros((tm, TOP_K), F32)
    wsel = jnp.zeros((tm, TOP_K), F32)
    onehot_sum = jnp.zeros((tm, ne), F32)
    picks = []
    for k in range(TOP_K):
        m = jnp.max(cur, axis=1, keepdims=True)
        ik = jnp.min(jnp.where(cur == m, lane_f, float(ne)), axis=1, keepdims=True)
        hit = lane_f == ik
        wk = jnp.sum(jnp.where(hit, s, 0.0), axis=1, keepdims=True)
        cur = jnp.where(hit, neg, cur)
        onehot_sum = onehot_sum + hit.astype(F32)
        picks.append(hit)
        eidx = jnp.where(col == k, ik, eidx)
        wsel = jnp.where(col == k, wk, wsel)
    gate_ref[...] = wsel / jnp.sum(wsel, axis=1, keepdims=True) * ROUTED_SCALE
    eidx_ref[...] = eidx.astype(jnp.int32)

    ri = lax.broadcasted_iota(jnp.int32, (tm, tm), 0)
    ci = lax.broadcasted_iota(jnp.int32, (tm, tm), 1)
    before = (ci < ri).astype(BF16)
    counts = jnp.dot(before, onehot_sum.astype(BF16), preferred_element_type=F32) + run_scr[...]
    rank = jnp.zeros((tm, TOP_K), F32)
    for k in range(TOP_K):
        rk = jnp.sum(jnp.where(picks[k], counts, 0.0), axis=1, keepdims=True)
        rank = jnp.where(col == k, rk, rank)
    rank_ref[...] = rank.astype(jnp.int32)
    run_scr[...] = run_scr[...] + jnp.sum(onehot_sum, axis=0, keepdims=True)
    cnt_ref[...] = run_scr[...].astype(jnp.int32)


def _router(x, wr, bias, *, tm):
    t, d = x.shape
    ne = wr.shape[1]
    out_row = lambda: pl.BlockSpec((tm, TOP_K), lambda i: (i, 0))
    return pl.pallas_call(
        functools.partial(_router_kernel, tm=tm, ne=ne),
        grid=(t // tm,),
        in_specs=[pl.BlockSpec((tm, d), lambda i: (i, 0)), _const_spec(wr.shape), _const_spec(bias.shape)],
        out_specs=[out_row(), out_row(), out_row(), _const_spec((1, ne))],
        out_shape=[jax.ShapeDtypeStruct((t, TOP_K), jnp.int32),
                   jax.ShapeDtypeStruct((t, TOP_K), F32),
                   jax.ShapeDtypeStruct((t, TOP_K), jnp.int32),
                   jax.ShapeDtypeStruct((1, ne), jnp.int32)],
        scratch_shapes=[pltpu.VMEM((1, ne), F32)],
        compiler_params=_params("arbitrary"),
        name="router",
    )(x, wr, bias)


def _dest_kernel(eidx_ref, rank_ref, start_ref, dest_ref, *, tm, ne):
    lane = lax.broadcasted_iota(jnp.int32, (tm, ne), 1)
    col = lax.broadcasted_iota(jnp.int32, (tm, TOP_K), 1)
    starts = start_ref[...].astype(F32)
    eidx = eidx_ref[...]
    dest = jnp.zeros((tm, TOP_K), F32)
    for k in range(TOP_K):
        hit = lane == eidx[:, k:k + 1]
        sk = jnp.sum(jnp.where(hit, starts, 0.0), axis=1, keepdims=True)
        dest = jnp.where(col == k, sk, dest)
    dest_ref[...] = dest.astype(jnp.int32) + rank_ref[...]


def _dest(eidx, rank, starts, *, tm):
    t = eidx.shape[0]
    ne = starts.shape[1]
    row = lambda: pl.BlockSpec((tm, TOP_K), lambda i: (i, 0))
    return pl.pallas_call(
        functools.partial(_dest_kernel, tm=tm, ne=ne),
        grid=(t // tm,),
        in_specs=[row(), row(), _const_spec((1, ne))],
        out_specs=row(),
        out_shape=jax.ShapeDtypeStruct((t, TOP_K), jnp.int32),
        compiler_params=_params("arbitrary"),
        name="dest",
    )(eidx, rank, starts)


def _router_t_kernel(x_ref, whi_ref, wlo_ref, bias_ref, eidx_ref, gate_ref, rank_ref, cnt_ref, run_scr,
                     *, tm, ne):
    i = pl.program_id(0)

    @pl.when(i == 0)
    def _():
        run_scr[...] = jnp.zeros_like(run_scr)

    x = x_ref[...]
    x_hi = x.astype(BF16)
    x_lo = (x - x_hi.astype(F32)).astype(BF16)
    nt = (((1,), (1,)), ((), ()))
    logits = (lax.dot_general(whi_ref[...], x_hi, nt, preferred_element_type=F32)
              + lax.dot_general(whi_ref[...], x_lo, nt, preferred_element_type=F32)
              + lax.dot_general(wlo_ref[...], x_hi, nt, preferred_element_type=F32))
    s = _sigmoid(logits)
    sel = s + bias_ref[...]
    row = lax.broadcasted_iota(jnp.int32, (ne, tm), 0).astype(F32)
    gsz = ne // N_GROUPS
    neg = -jnp.inf

    def colmax(a):
        return jnp.max(a, axis=0, keepdims=True)

    def colmin(a):
        return jnp.min(a, axis=0, keepdims=True)

    gscores = []
    row_in_group = lax.broadcasted_iota(jnp.int32, (gsz, tm), 0).astype(F32)
    for g in range(N_GROUPS):
        v = sel[g * gsz:(g + 1) * gsz]
        rg = row_in_group + float(g * gsz)
        m1 = colmax(v)
        i1 = colmin(jnp.where(v == m1, rg, float(ne)))
        gscores.append(m1 + colmax(jnp.where(rg == i1, neg, v)))
    kept = []
    for g in range(N_GROUPS):
        ahead = jnp.zeros((1, tm), F32)
        for g2 in range(N_GROUPS):
            if g2 == g:
                continue
            beats = (gscores[g2] >= gscores[g]) if g2 < g else (gscores[g2] > gscores[g])
            ahead = ahead + beats.astype(F32)
        keep = jnp.broadcast_to(ahead < float(TOPK_GROUPS), (gsz, tm))
        kept.append(jnp.where(keep, sel[g * gsz:(g + 1) * gsz], neg))
    cur = jnp.concatenate(kept, axis=0)

    picks, eidx, wsel = [], [], []
    onehot_sum = jnp.zeros((ne, tm), F32)
    for k in range(TOP_K):
        ik = colmin(jnp.where(cur == colmax(cur), row, float(ne)))
        hit = row == ik
        wsel.append(jnp.sum(jnp.where(hit, s, 0.0), axis=0, keepdims=True))
        cur = jnp.where(hit, neg, cur)
        onehot_sum = onehot_sum + hit.astype(F32)
        picks.append(hit)
        eidx.append(ik)
    wsel = jnp.concatenate(wsel, axis=0)
    gate_ref[...] = wsel / jnp.sum(wsel, axis=0, keepdims=True) * ROUTED_SCALE
    eidx_ref[...] = jnp.concatenate(eidx, axis=0).astype(jnp.int32)

    ri = lax.broadcasted_iota(jnp.int32, (tm, tm), 0)
    ci = lax.broadcasted_iota(jnp.int32, (tm, tm), 1)
    earlier = (ri < ci).astype(BF16)
    ohb = onehot_sum.astype(BF16)
    counts = jnp.dot(ohb, earlier, preferred_element_type=F32) + jnp.concatenate([run_scr[...]] * (tm // LANES), axis=1)
    rank = [jnp.sum(jnp.where(picks[k], counts, 0.0), axis=0, keepdims=True) for k in range(TOP_K)]
    rank_ref[...] = jnp.concatenate(rank, axis=0).astype(jnp.int32)
    run_scr[...] = run_scr[...] + jnp.dot(ohb, jnp.ones((tm, LANES), BF16), preferred_element_type=F32)
    cnt_ref[...] = run_scr[...].astype(jnp.int32)


def _router_t(x, w_hi, w_lo, bias_b, *, tm):
    t, d = x.shape
    ne = w_hi.shape[0]
    out_col = lambda: pl.BlockSpec((TOP_K, tm), lambda i: (0, i))
    return pl.pallas_call(
        functools.partial(_router_t_kernel, tm=tm, ne=ne),
        grid=(t // tm,),
        in_specs=[pl.BlockSpec((tm, d), lambda i: (i, 0)), _const_spec(w_hi.shape), _const_spec(w_lo.shape),
                  _const_spec(bias_b.shape)],
        out_specs=[out_col(), out_col(), out_col(), _const_spec((ne, LANES))],
        out_shape=[jax.ShapeDtypeStruct((TOP_K, t), jnp.int32),
                   jax.ShapeDtypeStruct((TOP_K, t), F32),
                   jax.ShapeDtypeStruct((TOP_K, t), jnp.int32),
                   jax.ShapeDtypeStruct((ne, LANES), jnp.int32)],
        scratch_shapes=[pltpu.VMEM((ne, LANES), F32)],
        compiler_params=_params("arbitrary"),
        name="router",
    )(x, w_hi, w_lo, bias_b)


def _dest_t_kernel(eidx_ref, rank_ref, start_ref, dest_ref, *, tm, ne):
    row = lax.broadcasted_iota(jnp.int32, (ne, tm), 0)
    starts = start_ref[...].astype(F32)
    eidx = eidx_ref[...]
    dest = [jnp.sum(jnp.where(row == eidx[k:k + 1, :], starts, 0.0), axis=0, keepdims=True) for k in range(TOP_K)]
    dest_ref[...] = jnp.concatenate(dest, axis=0).astype(jnp.int32) + rank_ref[...]


def _dest_t(eidx, rank, starts_b, *, tm):
    t = eidx.shape[1]
    ne = starts_b.shape[0]
    col = lambda: pl.BlockSpec((TOP_K, tm), lambda i: (0, i))
    return pl.pallas_call(
        functools.partial(_dest_t_kernel, tm=tm, ne=ne),
        grid=(t // tm,),
        in_specs=[col(), col(), _const_spec(starts_b.shape)],
        out_specs=col(),
        out_shape=jax.ShapeDtypeStruct((TOP_K, t), jnp.int32),
        compiler_params=_params("arbitrary"),
        name="dest",
    )(eidx, rank, starts_b)


TOKEN_DTYPE = F32


def _to_token_tiles(rows):
    chunks = jnp.stack([rows[:, c * LANES:(c + 1) * LANES] for c in range(SUBLANES)], axis=0)
    return jnp.swapaxes(chunks, 0, 1).astype(TOKEN_DTYPE)


def _chunks_of_token_tiles(tiles):
    chunks = jnp.swapaxes(tiles.astype(F32), 0, 1)
    return [chunks[c] for c in range(SUBLANES)]


def _token_copy(src_ref, src_row, dst_ref, dst_row, sem):
    return pltpu.make_async_copy(src_ref.at[pl.ds(src_row, 1)], dst_ref.at[pl.ds(dst_row, 1)], sem)


def _dispatch_kernel(dest_ref, x_ref, xs_ref, xt, sem, *, tt):
    xt[...] = _to_token_tiles(x_ref[...])

    def start(tp, c):
        for u in range(ISSUE_UNROLL):
            t = tp * ISSUE_UNROLL + u
            for k in range(TOP_K):
                _token_copy(xt, t, xs_ref, dest_ref[t * TOP_K + k], sem).start(priority=k % 2)
        return c

    lax.fori_loop(0, tt // ISSUE_UNROLL, start, 0)
    for _ in range(TOP_K):
        pltpu.make_async_copy(xt, xs_ref.at[pl.ds(0, tt)], sem).wait()


def _dispatch(dest_flat, x, n_slots, *, tt):
    t, d = x.shape
    assert d == SUBLANES * LANES
    return pl.pallas_call(
        functools.partial(_dispatch_kernel, tt=tt),
        grid=(t // tt,),
        in_specs=[pl.BlockSpec((tt * TOP_K,), lambda i: (i,), memory_space=pltpu.SMEM),
                  pl.BlockSpec((tt, d), lambda i: (i, 0))],
        out_specs=pl.BlockSpec(memory_space=pl.ANY),
        out_shape=jax.ShapeDtypeStruct((n_slots, SUBLANES, LANES), TOKEN_DTYPE),
        scratch_shapes=[pltpu.VMEM((tt, SUBLANES, LANES), TOKEN_DTYPE), pltpu.SemaphoreType.DMA(())],
        compiler_params=_params("arbitrary"),
        name="dispatch",
    )(dest_flat, x)


EXPERT_SLOTS = 16
EXPERT_STEP_BLOCKS = 4
EXPERT_AHEAD = 8


def _expert_kernel(base_ref, nblk_ref, cnt_ref, nu_ref, xs_ref, wg_ref, wu_ref, wd_ref, ys_ref,
                   xbuf, ybuf, wgb, wub, wdb, sem_in, sem_out):
    e = pl.program_id(0)
    n_used = nu_ref[0]
    nb = nblk_ref[e]
    base = base_ref[e]
    cnt = cnt_ref[e]
    ahead = EXPERT_AHEAD

    def x_copy(blk, slot):
        return pltpu.make_async_copy(xs_ref.at[pl.ds(blk * MOE_ROWS, MOE_ROWS)], xbuf.at[slot], sem_in.at[slot])

    def y_copy(blk, slot):
        return pltpu.make_async_copy(ybuf.at[slot], ys_ref.at[pl.ds(blk * MOE_ROWS, MOE_ROWS)], sem_out.at[slot])

    @pl.when(e == 0)
    def _():
        for first in range(ahead):
            @pl.when(first < n_used)
            def _():
                x_copy(first, first).start()

    @pl.when(nb > 0)
    def _():
        wgb[...] = wg_ref[0].astype(BF16)
        wub[...] = wu_ref[0].astype(BF16)
        wdb[...] = wd_ref[0].astype(BF16)

    def step(jb, chains):
        nblocks = sum(chains)
        blks = [base + jb + i for i in range(nblocks)]
        slots = [jnp.bitwise_and(b, EXPERT_SLOTS - 1) for b in blks]
        for b, s in zip(blks, slots):
            x_copy(b, s).wait()
        for b in blks:
            @pl.when(b + ahead < n_used)
            def _():
                x_copy(b + ahead, jnp.bitwise_and(b + ahead, EXPERT_SLOTS - 1)).start()
        for b, s in zip(blks, slots):
            @pl.when(b >= EXPERT_SLOTS)
            def _():
                y_copy(b - EXPERT_SLOTS, s).wait()

        first = 0
        for n in chains:
            x = jnp.concatenate([jnp.concatenate(_chunks_of_token_tiles(xbuf[s]), axis=1)
                                 for s in slots[first:first + n]], axis=0)
            rows = lax.broadcasted_iota(jnp.int32, x.shape, 0)
            xb = jnp.where(rows < cnt - (jb + first) * MOE_ROWS, x, 0.0).astype(BF16)
            g = jnp.dot(xb, wgb[...], preferred_element_type=F32)
            u = jnp.dot(xb, wub[...], preferred_element_type=F32)
            hmid = (g * _sigmoid(g)) * u
            y = jnp.dot(hmid.astype(BF16), wdb[...], preferred_element_type=F32)
            for i in range(n):
                b, s = blks[first + i], slots[first + i]
                ybuf[s] = _to_token_tiles(y[i * MOE_ROWS:(i + 1) * MOE_ROWS])
                y_copy(b, s).start()
            first += n

    def quad(jq, c):
        step(4 * jq, (2, 2))
        return c

    nquad = lax.shift_right_logical(nb, 2)
    lax.fori_loop(0, nquad, quad, 0)
    rem = jnp.bitwise_and(nb, 3)
    for r, chains in ((1, (1,)), (2, (2,)), (3, (2, 1))):
        @pl.when(rem == r)
        def _():
            step(4 * nquad, chains)

    @pl.when(e == pl.num_programs(0) - 1)
    def _():
        for back in range(EXPERT_SLOTS):
            @pl.when(n_used > back)
            def _():
                last = n_used - 1 - back
                y_copy(last, jnp.bitwise_and(last, EXPERT_SLOTS - 1)).wait()


def _experts(base, nblk, cnt, n_used, xs, wg, wu, wd):
    p = xs.shape[0]
    ne, d, de = wg.shape
    assert EXPERT_SLOTS & (EXPERT_SLOTS - 1) == 0 and EXPERT_AHEAD <= EXPERT_SLOTS - EXPERT_STEP_BLOCKS
    assert d == SUBLANES * LANES
    ring = pltpu.VMEM((EXPERT_SLOTS, MOE_ROWS, SUBLANES, LANES), TOKEN_DTYPE)
    grid_spec = pltpu.PrefetchScalarGridSpec(
        num_scalar_prefetch=4,
        grid=(ne,),
        in_specs=[pl.BlockSpec(memory_space=pl.ANY),
                  pl.BlockSpec((1, d, de), lambda e, *_: (e, 0, 0)),
                  pl.BlockSpec((1, d, de), lambda e, *_: (e, 0, 0)),
                  pl.BlockSpec((1, de, d), lambda e, *_: (e, 0, 0))],
        out_specs=pl.BlockSpec(memory_space=pl.ANY),
        scratch_shapes=[ring, ring,
                        pltpu.VMEM((d, de), BF16), pltpu.VMEM((d, de), BF16), pltpu.VMEM((de, d), BF16),
                        pltpu.SemaphoreType.DMA((EXPERT_SLOTS,)), pltpu.SemaphoreType.DMA((EXPERT_SLOTS,))],
    )
    return pl.pallas_call(
        _expert_kernel,
        grid_spec=grid_spec,
        out_shape=jax.ShapeDtypeStruct((p, SUBLANES, LANES), TOKEN_DTYPE),
        compiler_params=_params("arbitrary"),
        name="experts",
    )(base, nblk, cnt, n_used, xs, wg, wu, wd)


def _combine_kernel(dest_ref, dest_next_ref, x_ref, gate_ref, ys_ref, wsg_ref, wsu_ref, wsd_ref, g_ref, b_ref,
                    o_head_ref, o_tail_ref, ybuf, sem, *, tt, alpha, head_steps):
    i = pl.program_id(0)
    slot = jnp.bitwise_and(i, 1)

    def gather(dref, s):
        def start(tp, c):
            for u in range(ISSUE_UNROLL):
                t = tp * ISSUE_UNROLL + u
                for k in range(TOP_K):
                    _token_copy(ys_ref, dref[t * TOP_K + k], ybuf.at[s], k * tt + t, sem.at[s]).start(priority=k % 2)
            return c
        lax.fori_loop(0, tt // ISSUE_UNROLL, start, 0)

    @pl.when(i == 0)
    def _():
        gather(dest_ref, 0)

    def prefetch_group(k):
        nxt = 1 - slot
        for t in range(tt):
            _token_copy(ys_ref, dest_next_ref[t * TOP_K + k], ybuf.at[nxt], k * tt + t, sem.at[nxt]).start(priority=t % 2)

    x = x_ref[...]
    xb = x.astype(BF16)
    sg = jnp.dot(xb, wsg_ref[...], preferred_element_type=F32)
    su = jnp.dot(xb, wsu_ref[...], preferred_element_type=F32)
    acc = jnp.dot(((sg * _sigmoid(sg)) * su).astype(BF16), wsd_ref[...], preferred_element_type=F32)
    pltpu.make_async_copy(ys_ref.at[pl.ds(0, tt * TOP_K)], ybuf.at[slot], sem.at[slot]).wait()
    gate = gate_ref[...]
    routed = None
    for k in range(TOP_K):
        prefetch_group(k)
        gk = jnp.broadcast_to(gate[:, k:k + 1], (tt, LANES))
        part = [gk * ch for ch in _chunks_of_token_tiles(ybuf[slot, k * tt:(k + 1) * tt])]
        routed = part if routed is None else [r + p for r, p in zip(routed, part)]
    acc = acc + jnp.concatenate(routed, axis=1)
    out = _layer_norm(alpha * x + acc, g_ref[...], b_ref[...])

    @pl.when(i < head_steps)
    def _():
        o_head_ref[...] = out

    @pl.when(i >= head_steps)
    def _():
        o_tail_ref[...] = out

    @pl.when(i == pl.num_programs(0) - 1)
    def _():
        pltpu.make_async_copy(ys_ref.at[pl.ds(0, tt * TOP_K)], ybuf.at[1 - slot], sem.at[1 - slot]).wait()


def _combine(dest_flat, x, gate, ys, wsg, wsu, wsd, g, b, *, tt, alpha, head_rows):
    t, d = x.shape
    nsteps = t // tt
    head_steps = head_rows // tt
    assert head_rows % tt == 0 and 0 < head_steps < nsteps
    return pl.pallas_call(
        functools.partial(_combine_kernel, tt=tt, alpha=alpha, head_steps=head_steps),
        grid=(nsteps,),
        in_specs=[pl.BlockSpec((tt * TOP_K,), lambda i: (i,), memory_space=pltpu.SMEM),
                  pl.BlockSpec((tt * TOP_K,), lambda i: (jnp.minimum(i + 1, nsteps - 1),),
                               memory_space=pltpu.SMEM),
                  pl.BlockSpec((tt, d), lambda i: (i, 0)),
                  pl.BlockSpec((tt, TOP_K), lambda i: (i, 0)),
                  pl.BlockSpec(memory_space=pl.ANY),
                  _const_spec(wsg.shape), _const_spec(wsu.shape), _const_spec(wsd.shape),
                  _const_spec(g.shape), _const_spec(b.shape)],
        out_specs=[pl.BlockSpec((tt, d), lambda i: (jnp.minimum(i, head_steps - 1), 0)),
                   pl.BlockSpec((tt, d), lambda i: (jnp.maximum(i - head_steps, 0), 0))],
        out_shape=[jax.ShapeDtypeStruct((head_rows, d), F32), jax.ShapeDtypeStruct((t - head_rows, d), F32)],
        scratch_shapes=[pltpu.VMEM((2, tt * TOP_K, SUBLANES, LANES), TOKEN_DTYPE), pltpu.SemaphoreType.DMA((2,))],
        compiler_params=_params("arbitrary"),
        name="combine_ln2",
    )(dest_flat, dest_flat, x, gate, ys, wsg, wsu, wsd, g, b)


def _block_diag_tiles(w):
    n, s, _ = w.shape
    per = MXU_DIM // s
    w = w.reshape(n // per, per, s, s)
    eye = jnp.eye(per, dtype=w.dtype)
    return jnp.einsum('tpij,pq->tpiqj', w, eye).reshape(n // per, MXU_DIM, MXU_DIM)


def _row(v):
    return v.reshape(1, -1)


def _moe(x1, w_router, router_bias, w_gate, w_up, w_down, ws_gate, ws_up, ws_down, ln_g, ln_b, alpha, head_rows):
    t, d = x1.shape
    ne = w_router.shape[1]
    tm = 256
    w_t = w_router.T
    w_hi = w_t.astype(BF16)
    w_lo = (w_t - w_hi.astype(F32)).astype(BF16)
    bias_b = jnp.broadcast_to(router_bias.reshape(ne, 1), (ne, tm))
    eidx_t, gate_t, rank_t, counts = _router_t(x1, w_hi, w_lo, bias_b, tm=tm)
    counts = counts[:, 0]
    nblk = (counts + MOE_ROWS - 1) // MOE_ROWS
    blk_end = jnp.cumsum(nblk)
    blk_base = (blk_end - nblk).astype(jnp.int32)
    n_used = blk_end[-1:].astype(jnp.int32)
    n_slots = ((t * TOP_K) // MOE_ROWS + ne) * MOE_ROWS

    starts_b = jnp.broadcast_to((blk_base * MOE_ROWS).reshape(ne, 1), (ne, tm))
    dest_flat = _dest_t(eidx_t, rank_t, starts_b, tm=tm).T.reshape(-1)
    xs = _dispatch(dest_flat, x1, n_slots, tt=256)
    ys = _experts(blk_base, nblk.astype(jnp.int32), counts, n_used, xs, w_gate, w_up, w_down)
    return _combine(dest_flat, x1, gate_t.T, ys, ws_gate.astype(BF16), ws_up.astype(BF16),
                    ws_down.astype(BF16), _row(ln_g), _row(ln_b), tt=128, alpha=alpha, head_rows=head_rows)


def _pad_time(a, lp):
    return jnp.pad(a, ((0, 0), (0, lp - a.shape[1]), (0, 0)))


def kernel(x_prompt, x_sample, state_rglru_h, state_conv, state_gla, meta_tokens, w_in, conv_w, conv_b,
           rg_wa, rg_ba, rg_wi, rg_bi, rg_lambda, gla_wf2, gla_bf, gla_norm_g, w_out, ln1_g, ln1_b,
           w_router, router_bias, w_gate, w_up, w_down, ws_gate, ws_up, ws_down, ln2_g, ln2_b):
    bp, lp, d = x_prompt.shape
    bs, ls, _ = x_sample.shape
    depth = w_in.shape[0]
    nh, dk, dv = state_gla.shape[2:]
    qk, vw = nh * dk, nh * dv
    lowrank = gla_wf2.shape[1]
    alpha = (2.0 * depth) ** 0.25
    ls_pad = SUBLANES
    assert depth == 1 and ls <= ls_pad and lp % 256 == 0 and N_META % SUBLANES == 0

    xp = x_prompt.reshape(bp * lp, d)
    xs = x_sample.reshape(bs * ls, d)
    xm = meta_tokens.astype(F32)

    hp_l, cp_l, sp_l, hs_l, cs_l, ss_l = [], [], [], [], [], []
    for l in range(depth):
        o1, o2 = 2 * d, 2 * d + 2 * qk + 2 * vw
        wa = w_in[l][:, :o1].astype(BF16)
        wb = jnp.pad(w_in[l][:, o1:o2 + lowrank], ((0, 0), (0, LANES - lowrank))).astype(BF16)
        wm = w_in[l][:, o2 + lowrank:].astype(BF16)
        wf = jnp.pad(gla_wf2[l], ((0, LANES - lowrank), (0, 0))).astype(BF16)
        wa4 = _block_diag_tiles(rg_wa[l]).astype(BF16)
        wi4 = _block_diag_tiles(rg_wi[l]).astype(BF16)
        mixer_w = (conv_w[l], _row(conv_b[l]), wa4, _row(rg_ba[l]), wi4, _row(rg_bi[l]), _row(rg_lambda[l]))
        gla_w = (wf, _row(gla_bf[l]), _row(gla_norm_g[l]))

        ua, ub, _ = _inproj(xm, wa, wb, wm, tm=N_META)
        _, h_m, c_m = _rglru(ua[None], jnp.zeros((1, CONV_W - 1, d), F32), jnp.zeros((1, 1, d), F32),
                             *mixer_w, bb=1, tb=N_META, l_real=N_META)
        _, s_m = _gla(ub[None], jnp.zeros((1, nh, dk, dv), F32), *gla_w,
                      bb=1, tb=N_META, chunk=N_META, l_real=N_META)

        ua, ub, um = _inproj(xp, wa, wb, wm, tm=256, um_dtype=BF16)
        ya, h_p, c_p = _rglru(ua.reshape(bp, lp, -1), jnp.broadcast_to(c_m, (bp,) + c_m.shape[1:]),
                              jnp.broadcast_to(h_m, (bp,) + h_m.shape[1:]), *mixer_w,
                              bb=1, tb=256, l_real=lp, y_dtype=BF16)
        yb, s_p = _gla(ub.reshape(bp, lp, -1), jnp.broadcast_to(s_m, (bp,) + s_m.shape[1:]), *gla_w,
                       bb=1, tb=256, chunk=GLA_CHUNK, l_real=lp, y_dtype=BF16)
        n_tok = bp * lp + bs * ls
        x1 = _mix(ya.reshape(bp * lp, d), yb.reshape(bp * lp, d), um, xp, w_out[l].astype(BF16),
                  _row(ln1_g[l]), _row(ln1_b[l]), alpha=alpha, tm=512, total_rows=n_tok)

        xs_pad = _pad_time(xs.reshape(bs, ls, d), ls_pad).reshape(bs * ls_pad, d)
        ua, ub, um = _inproj(xs_pad, wa, wb, wm, tm=256)
        ya, h_s, c_s = _rglru(ua.reshape(bs, ls_pad, -1), state_conv[l].astype(F32),
                              state_rglru_h[l].astype(F32)[:, None, :], *mixer_w,
                              bb=32, tb=ls_pad, l_real=ls)
        yb, s_s = _gla(ub.reshape(bs, ls_pad, -1), state_gla[l].astype(F32), *gla_w,
                       bb=8, tb=ls_pad, chunk=ls_pad, l_real=ls)
        unpad = lambda a: a.reshape(bs, ls_pad, -1)[:, :ls].reshape(bs * ls, -1)
        x1 = _mix(unpad(ya), unpad(yb), unpad(um), xs, w_out[l].astype(BF16),
                  _row(ln1_g[l]), _row(ln1_b[l]), alpha=alpha, tm=512, total_rows=n_tok,
                  row_offset=bp * lp, into=x1)

        xp, xs = _moe(x1, w_router[l], router_bias[l], w_gate[l], w_up[l], w_down[l],
                      ws_gate[l], ws_up[l], ws_down[l], ln2_g[l], ln2_b[l], alpha, bp * lp)

        hp_l.append(h_p[:, 0]); cp_l.append(c_p); sp_l.append(s_p)
        hs_l.append(h_s[:, 0]); cs_l.append(c_s); ss_l.append(s_s)

    y_prompt = xp.reshape(bp, lp, d)
    y_sample = xs.reshape(bs, ls, d)
    return (y_prompt, y_sample,
            jnp.stack(hp_l).astype(state_rglru_h.dtype), jnp.stack(cp_l).astype(state_conv.dtype),
            jnp.stack(sp_l).astype(state_gla.dtype),
            jnp.stack(hs_l).astype(state_rglru_h.dtype), jnp.stack(cs_l).astype(state_conv.dtype),
            jnp.stack(ss_l).astype(state_gla.dtype))
```

```python
import functools

import jax
import jax.numpy as jnp
from jax import lax
from jax.experimental import pallas as pl
from jax.experimental.pallas import tpu as pltpu

F32 = jnp.float32
BF16 = jnp.bfloat16
HIGHEST = lax.Precision.HIGHEST

N_META = 16
CONV_W = 4
RG_C = 8.0
RNN_BLOCKS = 16
GLA_HEADS = 4
GLA_TAU = 16.0
GLA_CHUNK = 64
N_GROUPS = 8
TOPK_GROUPS = 4
TOP_K = 8
ROUTED_SCALE = 2.5
LN_EPS = 1e-5
RMS_EPS = 1e-6

LANES = 128
SUBLANES = 8
MXU_DIM = 256
VMEM_LIMIT_BYTES = 56 * 1024 * 1024

MOE_ROWS = 128
ISSUE_UNROLL = 4


def _params(*sem):
    return pltpu.CompilerParams(dimension_semantics=sem, vmem_limit_bytes=VMEM_LIMIT_BYTES)


def _sigmoid(x):
    return jax.nn.sigmoid(x)


def _softplus(z):
    return jnp.maximum(z, 0.0) + jnp.log1p(jnp.exp(-jnp.abs(z)))


def _layer_norm(z, g, b):
    mu = jnp.mean(z, axis=-1, keepdims=True)
    zc = z - mu
    var = jnp.mean(zc * zc, axis=-1, keepdims=True)
    return zc * lax.rsqrt(var + LN_EPS) * g + b


def _const_spec(shape):
    nd = len(shape)
    return pl.BlockSpec(shape, lambda *_: (0,) * nd)


def _inproj_kernel(x_ref, wa_ref, wb_ref, wm_ref, ua_ref, ub_ref, um_ref):
    xb = x_ref[...].astype(BF16)
    ua_ref[...] = jnp.dot(xb, wa_ref[...], preferred_element_type=F32)
    ub_ref[...] = jnp.dot(xb, wb_ref[...], preferred_element_type=F32)
    um_ref[...] = jnp.dot(xb, wm_ref[...], preferred_element_type=F32).astype(um_ref.dtype)


def _inproj(x2d, wa, wb, wm, tm, um_dtype=F32):
    m, d = x2d.shape
    na, nb, nm = wa.shape[1], wb.shape[1], wm.shape[1]
    return pl.pallas_call(
        _inproj_kernel,
        grid=(m // tm,),
        in_specs=[pl.BlockSpec((tm, d), lambda i: (i, 0)),
                  _const_spec(wa.shape), _const_spec(wb.shape), _const_spec(wm.shape)],
        out_specs=[pl.BlockSpec((tm, na), lambda i: (i, 0)),
                   pl.BlockSpec((tm, nb), lambda i: (i, 0)),
                   pl.BlockSpec((tm, nm), lambda i: (i, 0))],
        out_shape=[jax.ShapeDtypeStruct((m, na), F32),
                   jax.ShapeDtypeStruct((m, nb), F32),
                   jax.ShapeDtypeStruct((m, nm), um_dtype)],
        compiler_params=_params("arbitrary"),
        name="inproj",
    )(x2d, wa, wb, wm)


def _rglru_kernel(ua_ref, cbuf_ref, h0_ref, cw_ref, cb_ref, wa_ref, ba_ref, wi_ref, bi_ref, lam_ref,
                  ya_ref, hlast_ref, cnew_ref, ext_scr, h_scr, *, bb, tb, l_last, nblk, d):
    j = pl.program_id(1)
    ntail = CONV_W - 1
    rows = bb * tb

    @pl.when(j == 0)
    def _():
        h_scr[...] = h0_ref[...]
        ext_scr[:, SUBLANES - ntail:SUBLANES, :] = cbuf_ref[...]

    xr = ua_ref[:, :, :d]
    gr = ua_ref[:, :, d:].reshape(rows, d)
    ext_scr[:, SUBLANES:, :] = xr
    xc = cb_ref[...] + cw_ref[ntail:ntail + 1, :] * xr
    for s in range(1, CONV_W):
        xc = xc + cw_ref[ntail - s:ntail - s + 1, :] * ext_scr[:, SUBLANES - s:SUBLANES - s + tb, :]
    xc = xc.reshape(rows, d)

    xcb = xc.astype(BF16)
    nt = d // MXU_DIM
    ra = jnp.concatenate(
        [jnp.dot(xcb[:, q * MXU_DIM:(q + 1) * MXU_DIM], wa_ref[q], preferred_element_type=F32)
         for q in range(nt)], axis=-1)
    ia = jnp.concatenate(
        [jnp.dot(xcb[:, q * MXU_DIM:(q + 1) * MXU_DIM], wi_ref[q], preferred_element_type=F32)
         for q in range(nt)], axis=-1)
    r = _sigmoid(ra + ba_ref[...])
    ig = _sigmoid(ia + bi_ref[...])
    log_a = (-RG_C) * r * _softplus(-lam_ref[...])
    a = jnp.exp(log_a)
    z = -jnp.tanh(log_a) * (a * a + 1.0)
    mult = jnp.where(z > 0.0, z * lax.rsqrt(z), 0.0)
    b = mult * (ig * xc)

    t_in = jnp.bitwise_and(lax.broadcasted_iota(jnp.int32, (rows, d), 0), SUBLANES - 1)
    s = 1
    while s < SUBLANES:
        a_sh = jnp.where(t_in >= s, pltpu.roll(a, s, 0), 1.0)
        b_sh = jnp.where(t_in >= s, pltpu.roll(b, s, 0), 0.0)
        b = b + a * b_sh
        a = a * a_sh
        s *= 2
    a = a.reshape(bb, tb, d)
    b = b.reshape(bb, tb, d)
    carry = h_scr[...]
    tiles = []
    for i in range(tb // SUBLANES):
        rs = slice(i * SUBLANES, (i + 1) * SUBLANES)
        hi = b[:, rs, :] + a[:, rs, :] * carry
        carry = hi[:, SUBLANES - 1:SUBLANES, :]
        tiles.append(hi)
    h = jnp.concatenate(tiles, axis=1)
    h_scr[...] = carry
    ya_ref[...] = (h * jax.nn.gelu(gr, approximate=True).reshape(bb, tb, d)).astype(ya_ref.dtype)

    @pl.when(j == nblk - 1)
    def _():
        hlast_ref[...] = h[:, l_last - 1:l_last, :]
        cnew_ref[...] = ext_scr[:, SUBLANES + l_last - ntail:SUBLANES + l_last, :]

    tail = ext_scr[:, tb + SUBLANES - ntail:tb + SUBLANES, :]
    ext_scr[:, SUBLANES - ntail:SUBLANES, :] = tail


def _rglru(ua, cbuf, h0, cw, cb, wa4, ba, wi4, bi, lam, *, bb, tb, l_real, y_dtype=F32):
    bsz, lp, n2 = ua.shape
    d = n2 // 2
    nblk = lp // tb
    l_last = l_real - (nblk - 1) * tb
    assert tb & (tb - 1) == 0 and bsz % bb == 0
    kern = functools.partial(_rglru_kernel, bb=bb, tb=tb, l_last=l_last, nblk=nblk, d=d)
    return pl.pallas_call(
        kern,
        grid=(bsz // bb, nblk),
        in_specs=[pl.BlockSpec((bb, tb, n2), lambda b, j: (b, j, 0)),
                  pl.BlockSpec((bb, CONV_W - 1, d), lambda b, j: (b, 0, 0)),
                  pl.BlockSpec((bb, 1, d), lambda b, j: (b, 0, 0)),
                  _const_spec(cw.shape), _const_spec(cb.shape),
                  _const_spec(wa4.shape), _const_spec(ba.shape),
                  _const_spec(wi4.shape), _const_spec(bi.shape), _const_spec(lam.shape)],
        out_specs=[pl.BlockSpec((bb, tb, d), lambda b, j: (b, j, 0)),
                   pl.BlockSpec((bb, 1, d), lambda b, j: (b, 0, 0)),
                   pl.BlockSpec((bb, CONV_W - 1, d), lambda b, j: (b, 0, 0))],
        out_shape=[jax.ShapeDtypeStruct((bsz, lp, d), y_dtype),
                   jax.ShapeDtypeStruct((bsz, 1, d), F32),
                   jax.ShapeDtypeStruct((bsz, CONV_W - 1, d), F32)],
        scratch_shapes=[pltpu.VMEM((bb, tb + SUBLANES, d), F32), pltpu.VMEM((bb, 1, d), F32)],
        compiler_params=_params("arbitrary", "arbitrary"),
        name="rglru",
    )(ua, cbuf, h0, cw, cb, wa4, ba, wi4, bi, lam)


def _gla_kernel(ub_ref, s0_ref, wf_ref, bf_ref, g_ref, yb_ref, sout_ref, s_scr,
                *, bb, tb, chunk, l_real, nblk, dk, dv):
    j = pl.program_id(1)
    nh = GLA_HEADS
    qk = nh * dk
    vw = nh * dv

    @pl.when(j == 0)
    def _():
        s_scr[...] = s0_ref[...]

    off_k, off_v, off_g, off_f = qk, 2 * qk, 2 * qk + vw, 2 * qk + 2 * vw
    merge = bb > 1 and nblk == 1 and tb == chunk
    nrow = bb * tb if merge else tb
    nc = nrow // chunk
    shift = chunk.bit_length() - 1
    ri = lax.broadcasted_iota(jnp.int32, (nrow, nrow), 0)
    ci = lax.broadcasted_iota(jnp.int32, (nrow, nrow), 1)
    keep = (lax.shift_right_logical(ri, shift) == lax.shift_right_logical(ci, shift)) & (ri >= ci)
    tril = keep.astype(F32)
    row_chunk = lax.shift_right_logical(lax.broadcasted_iota(jnp.int32, (nrow, dk), 0), shift)
    mid = chunk // 2
    scale = dk ** -0.5

    def by_chunk(a):
        if nc == 1:
            return a
        return jnp.concatenate([jnp.where(row_chunk == c, a, 0.0) for c in range(nc)], axis=1)

    def per_chunk_row(a, r):
        return jnp.concatenate([jnp.broadcast_to(a[c * chunk + r:c * chunk + r + 1, :], (chunk, a.shape[1]))
                                for c in range(nc)], axis=0)

    for bi in range(1 if merge else bb):
        def cols(c0, c1, bi=bi):
            if merge:
                return ub_ref[:, :, c0:c1].reshape(nrow, c1 - c0)
            return ub_ref[bi, :, c0:c1]

        fl = cols(off_f, off_f + LANES).astype(BF16)
        logit = jnp.dot(fl, wf_ref[...], preferred_element_type=F32) + bf_ref[...]
        log_f = (jnp.minimum(logit, 0.0) - jnp.log1p(jnp.exp(-jnp.abs(logit)))) * (1.0 / GLA_TAU)
        if l_real < tb * nblk:
            rows = lax.broadcasted_iota(jnp.int32, log_f.shape, 0)
            rows = (jnp.bitwise_and(rows, tb - 1) if merge else rows) + j * tb
            log_f = jnp.where(rows < l_real, log_f, 0.0)
        bcum = jnp.dot(tril, log_f, precision=HIGHEST, preferred_element_type=F32)
        ref = per_chunk_row(bcum, mid)
        last = per_chunk_row(bcum, chunk - 1)
        e_inter = jnp.exp(bcum)
        e_q = jnp.exp(bcum - ref)
        e_k = jnp.exp(ref - bcum)
        e_dec = jnp.exp(last - bcum)

        for h in range(nh):
            hk = slice(h * dk, (h + 1) * dk)
            q = cols(h * dk, (h + 1) * dk) * scale
            k = cols(off_k + h * dk, off_k + (h + 1) * dk)
            vb = cols(off_v + h * dv, off_v + (h + 1) * dv).astype(BF16)
            go = cols(off_g + h * dv, off_g + (h + 1) * dv)
            att = lax.dot_general((q * e_q[:, hk]).astype(BF16), (k * e_k[:, hk]).astype(BF16),
                                  (((1,), (1,)), ((), ())), preferred_element_type=F32)
            att = jnp.where(keep, att, 0.0)
            o = jnp.dot(att.astype(BF16), vb, preferred_element_type=F32)
            kd = by_chunk(k * e_dec[:, hk])
            kv = lax.dot_general(kd.astype(BF16), vb, (((0,), (0,)), ((), ())),
                                 preferred_element_type=F32)
            states = [s_scr[c, h] for c in range(nc)] if merge else [s_scr[bi, h]]
            for c in range(nc):
                drow = jnp.exp(bcum[c * chunk + chunk - 1:c * chunk + chunk, hk])
                dec = jnp.transpose(jnp.broadcast_to(drow, (dk, dk)))
                dec = jnp.concatenate([dec] * (dv // dk), axis=1)
                s_next = dec * states[c] + kv[c * dk:(c + 1) * dk]
                if merge:
                    s_scr[c, h] = s_next
                else:
                    states.append(s_next)
            if not merge:
                s_scr[bi, h] = states[nc]
            qi = by_chunk(q * e_inter[:, hk])
            s_in = jnp.concatenate(states[:nc], axis=0).astype(BF16)
            o = o + jnp.dot(qi.astype(BF16), s_in, preferred_element_type=F32)
            o = o * lax.rsqrt(jnp.mean(o * o, axis=-1, keepdims=True) + RMS_EPS) * g_ref[...]
            y = (o * (go * _sigmoid(go))).astype(yb_ref.dtype)
            if merge:
                yb_ref[:, :, h * dv:(h + 1) * dv] = y.reshape(bb, tb, dv)
            else:
                yb_ref[bi, :, h * dv:(h + 1) * dv] = y

    @pl.when(j == nblk - 1)
    def _():
        sout_ref[...] = s_scr[...]


def _gla(ub, s0, wf, bf, g, *, bb, tb, chunk, l_real, y_dtype=F32):
    bsz, lp, nb = ub.shape
    _, nh, dk, dv = s0.shape
    nblk = lp // tb
    assert bsz % bb == 0 and dv % dk == 0 and chunk & (chunk - 1) == 0 and tb % chunk == 0
    kern = functools.partial(_gla_kernel, bb=bb, tb=tb, chunk=chunk, l_real=l_real, nblk=nblk, dk=dk, dv=dv)
    return pl.pallas_call(
        kern,
        grid=(bsz // bb, nblk),
        in_specs=[pl.BlockSpec((bb, tb, nb), lambda b, j: (b, j, 0)),
                  pl.BlockSpec((bb, nh, dk, dv), lambda b, j: (b, 0, 0, 0)),
                  _const_spec(wf.shape), _const_spec(bf.shape), _const_spec(g.shape)],
        out_specs=[pl.BlockSpec((bb, tb, nh * dv), lambda b, j: (b, j, 0)),
                   pl.BlockSpec((bb, nh, dk, dv), lambda b, j: (b, 0, 0, 0))],
        out_shape=[jax.ShapeDtypeStruct((bsz, lp, nh * dv), y_dtype),
                   jax.ShapeDtypeStruct((bsz, nh, dk, dv), F32)],
        scratch_shapes=[pltpu.VMEM((bb, nh, dk, dv), F32)],
        compiler_params=_params("arbitrary", "arbitrary"),
        name="gla",
    )(ub, s0, wf, bf, g)


def _mix_kernel(ya_ref, yb_ref, um_ref, x_ref, wo_ref, g_ref, b_ref, *rest, alpha, d):
    o_ref = rest[-1]
    um = um_ref[...].astype(F32)
    merged = (_sigmoid(um[:, :d]) * ya_ref[...].astype(F32)
              + _sigmoid(um[:, d:]) * yb_ref[...].astype(F32))
    mix = jnp.dot(merged.astype(BF16), wo_ref[...], preferred_element_type=F32)
    o_ref[...] = _layer_norm(alpha * x_ref[...] + mix, g_ref[...], b_ref[...])


def _mix(ya, yb, um, x, wo, g, b, *, alpha, tm, total_rows, row_offset=0, into=None):
    m, d = x.shape
    assert m % tm == 0 and row_offset % tm == 0
    row = lambda w: pl.BlockSpec((tm, w), lambda i: (i, 0))
    in_specs = [row(d), row(d), row(2 * d), row(d),
                _const_spec(wo.shape), _const_spec(g.shape), _const_spec(b.shape)]
    args = (ya, yb, um, x, wo, g, b)
    aliases = {}
    if into is not None:
        in_specs.append(pl.BlockSpec(memory_space=pl.ANY))
        aliases = {len(args): 0}
        args = args + (into,)
    return pl.pallas_call(
        functools.partial(_mix_kernel, alpha=alpha, d=d),
        grid=(m // tm,),
        in_specs=in_specs,
        out_specs=pl.BlockSpec((tm, d), lambda i: (i + row_offset // tm, 0)),
        out_shape=jax.ShapeDtypeStruct((total_rows, d), F32),
        input_output_aliases=aliases,
        compiler_params=_params("arbitrary"),
        name="mix_ln1",
    )(*args)


def _router_kernel(x_ref, wr_ref, bias_ref, eidx_ref, gate_ref, rank_ref, cnt_ref, run_scr,
                   *, tm, ne):
    i = pl.program_id(0)

    @pl.when(i == 0)
    def _():
        run_scr[...] = jnp.zeros_like(run_scr)

    logits = jnp.dot(x_ref[...], wr_ref[...], precision=HIGHEST, preferred_element_type=F32)
    s = _sigmoid(logits)
    sel = s + bias_ref[...]
    lane = lax.broadcasted_iota(jnp.int32, (tm, ne), 1)
    lane_f = lane.astype(F32)
    gsz = ne // N_GROUPS
    neg = -jnp.inf

    gscores = []
    for g in range(N_GROUPS):
        ing = (lane >= g * gsz) & (lane < (g + 1) * gsz)
        v = jnp.where(ing, sel, neg)
        m1 = jnp.max(v, axis=1, keepdims=True)
        i1 = jnp.min(jnp.where(v == m1, lane_f, float(ne)), axis=1, keepdims=True)
        m2 = jnp.max(jnp.where(lane_f == i1, neg, v), axis=1, keepdims=True)
        gscores.append(m1 + m2)
    cur = jnp.full((tm, ne), neg, F32)
    for g in range(N_GROUPS):
        ahead = jnp.zeros((tm, 1), F32)
        for g2 in range(N_GROUPS):
            if g2 == g:
                continue
            beats = (gscores[g2] >= gscores[g]) if g2 < g else (gscores[g2] > gscores[g])
            ahead = ahead + beats.astype(F32)
        ing = (lane >= g * gsz) & (lane < (g + 1) * gsz)
        cur = jnp.where(ing & (ahead < float(TOPK_GROUPS)), sel, cur)

    col = lax.broadcasted_iota(jnp.int32, (tm, TOP_K), 1)
    eidx = jnp.zeros((tm, TOP_K), F32)
    wsel = jnp.zeros((tm, TOP_K), F32)
    onehot_sum = jnp.zeros((tm, ne), F32)
    picks = []
    for k in range(TOP_K):
        m = jnp.max(cur, axis=1, keepdims=True)
        ik = jnp.min(jnp.where(cur == m, lane_f, float(ne)), axis=1, keepdims=True)
        hit = lane_f == ik
        wk = jnp.sum(jnp.where(hit, s, 0.0), axis=1, keepdims=True)
        cur = jnp.where(hit, neg, cur)
        onehot_sum = onehot_sum + hit.astype(F32)
        picks.append(hit)
        eidx = jnp.where(col == k, ik, eidx)
        wsel = jnp.where(col == k, wk, wsel)
    gate_ref[...] = wsel / jnp.sum(wsel, axis=1, keepdims=True) * ROUTED_SCALE
    eidx_ref[...] = eidx.astype(jnp.int32)

    ri = lax.broadcasted_iota(jnp.int32, (tm, tm), 0)
    ci = lax.broadcasted_iota(jnp.int32, (tm, tm), 1)
    before = (ci < ri).astype(BF16)
    counts = jnp.dot(before, onehot_sum.astype(BF16), preferred_element_type=F32) + run_scr[...]
    rank = jnp.zeros((tm, TOP_K), F32)
    for k in range(TOP_K):
        rk = jnp.sum(jnp.where(picks[k], counts, 0.0), axis=1, keepdims=True)
        rank = jnp.where(col == k, rk, rank)
    rank_ref[...] = rank.astype(jnp.int32)
    run_scr[...] = run_scr[...] + jnp.sum(onehot_sum, axis=0, keepdims=True)
    cnt_ref[...] = run_scr[...].astype(jnp.int32)


def _router(x, wr, bias, *, tm):
    t, d = x.shape
    ne = wr.shape[1]
    out_row = lambda: pl.BlockSpec((tm, TOP_K), lambda i: (i, 0))
    return pl.pallas_call(
        functools.partial(_router_kernel, tm=tm, ne=ne),
        grid=(t // tm,),
        in_specs=[pl.BlockSpec((tm, d), lambda i: (i, 0)), _const_spec(wr.shape), _const_spec(bias.shape)],
        out_specs=[out_row(), out_row(), out_row(), _const_spec((1, ne))],
        out_shape=[jax.ShapeDtypeStruct((t, TOP_K), jnp.int32),
                   jax.ShapeDtypeStruct((t, TOP_K), F32),
                   jax.ShapeDtypeStruct((t, TOP_K), jnp.int32),
                   jax.ShapeDtypeStruct((1, ne), jnp.int32)],
        scratch_shapes=[pltpu.VMEM((1, ne), F32)],
        compiler_params=_params("arbitrary"),
        name="router",
    )(x, wr, bias)


def _dest_kernel(eidx_ref, rank_ref, start_ref, dest_ref, *, tm, ne):
    lane = lax.broadcasted_iota(jnp.int32, (tm, ne), 1)
    col = lax.broadcasted_iota(jnp.int32, (tm, TOP_K), 1)
    starts = start_ref[...].astype(F32)
    eidx = eidx_ref[...]
    dest = jnp.zeros((tm, TOP_K), F32)
    for k in range(TOP_K):
        hit = lane == eidx[:, k:k + 1]
        sk = jnp.sum(jnp.where(hit, starts, 0.0), axis=1, keepdims=True)
        dest = jnp.where(col == k, sk, dest)
    dest_ref[...] = dest.astype(jnp.int32) + rank_ref[...]


def _dest(eidx, rank, starts, *, tm):
    t = eidx.shape[0]
    ne = starts.shape[1]
    row = lambda: pl.BlockSpec((tm, TOP_K), lambda i: (i, 0))
    return pl.pallas_call(
        functools.partial(_dest_kernel, tm=tm, ne=ne),
        grid=(t // tm,),
        in_specs=[row(), row(), _const_spec((1, ne))],
        out_specs=row(),
        out_shape=jax.ShapeDtypeStruct((t, TOP_K), jnp.int32),
        compiler_params=_params("arbitrary"),
        name="dest",
    )(eidx, rank, starts)


def _router_t_kernel(x_ref, whi_ref, wlo_ref, bias_ref, eidx_ref, gate_ref, rank_ref, cnt_ref, run_scr,
                     *, tm, ne):
    i = pl.program_id(0)

    @pl.when(i == 0)
    def _():
        run_scr[...] = jnp.zeros_like(run_scr)

    x = x_ref[...]
    x_hi = x.astype(BF16)
    x_lo = (x - x_hi.astype(F32)).astype(BF16)
    nt = (((1,), (1,)), ((), ()))
    logits = (lax.dot_general(whi_ref[...], x_hi, nt, preferred_element_type=F32)
              + lax.dot_general(whi_ref[...], x_lo, nt, preferred_element_type=F32)
              + lax.dot_general(wlo_ref[...], x_hi, nt, preferred_element_type=F32))
    s = _sigmoid(logits)
    sel = s + bias_ref[...]
    row = lax.broadcasted_iota(jnp.int32, (ne, tm), 0).astype(F32)
    gsz = ne // N_GROUPS
    neg = -jnp.inf

    def colmax(a):
        return jnp.max(a, axis=0, keepdims=True)

    def colmin(a):
        return jnp.min(a, axis=0, keepdims=True)

    gscores = []
    row_in_group = lax.broadcasted_iota(jnp.int32, (gsz, tm), 0).astype(F32)
    for g in range(N_GROUPS):
        v = sel[g * gsz:(g + 1) * gsz]
        rg = row_in_group + float(g * gsz)
        m1 = colmax(v)
        i1 = colmin(jnp.where(v == m1, rg, float(ne)))
        gscores.append(m1 + colmax(jnp.where(rg == i1, neg, v)))
    kept = []
    for g in range(N_GROUPS):
        ahead = jnp.zeros((1, tm), F32)
        for g2 in range(N_GROUPS):
            if g2 == g:
                continue
            beats = (gscores[g2] >= gscores[g]) if g2 < g else (gscores[g2] > gscores[g])
            ahead = ahead + beats.astype(F32)
        keep = jnp.broadcast_to(ahead < float(TOPK_GROUPS), (gsz, tm))
        kept.append(jnp.where(keep, sel[g * gsz:(g + 1) * gsz], neg))
    cur = jnp.concatenate(kept, axis=0)

    picks, eidx, wsel = [], [], []
    onehot_sum = jnp.zeros((ne, tm), F32)
    for k in range(TOP_K):
        ik = colmin(jnp.where(cur == colmax(cur), row, float(ne)))
        hit = row == ik
        wsel.append(jnp.sum(jnp.where(hit, s, 0.0), axis=0, keepdims=True))
        cur = jnp.where(hit, neg, cur)
        onehot_sum = onehot_sum + hit.astype(F32)
        picks.append(hit)
        eidx.append(ik)
    wsel = jnp.concatenate(wsel, axis=0)
    gate_ref[...] = wsel / jnp.sum(wsel, axis=0, keepdims=True) * ROUTED_SCALE
    eidx_ref[...] = jnp.concatenate(eidx, axis=0).astype(jnp.int32)

    ri = lax.broadcasted_iota(jnp.int32, (tm, tm), 0)
    ci = lax.broadcasted_iota(jnp.int32, (tm, tm), 1)
    earlier = (ri < ci).astype(BF16)
    ohb = onehot_sum.astype(BF16)
    counts = jnp.dot(ohb, earlier, preferred_element_type=F32) + jnp.concatenate([run_scr[...]] * (tm // LANES), axis=1)
    rank = [jnp.sum(jnp.where(picks[k], counts, 0.0), axis=0, keepdims=True) for k in range(TOP_K)]
    rank_ref[...] = jnp.concatenate(rank, axis=0).astype(jnp.int32)
    run_scr[...] = run_scr[...] + jnp.dot(ohb, jnp.ones((tm, LANES), BF16), preferred_element_type=F32)
    cnt_ref[...] = run_scr[...].astype(jnp.int32)


def _router_t(x, w_hi, w_lo, bias_b, *, tm):
    t, d = x.shape
    ne = w_hi.shape[0]
    out_col = lambda: pl.BlockSpec((TOP_K, tm), lambda i: (0, i))
    return pl.pallas_call(
        functools.partial(_router_t_kernel, tm=tm, ne=ne),
        grid=(t // tm,),
        in_specs=[pl.BlockSpec((tm, d), lambda i: (i, 0)), _const_spec(w_hi.shape), _const_spec(w_lo.shape),
                  _const_spec(bias_b.shape)],
        out_specs=[out_col(), out_col(), out_col(), _const_spec((ne, LANES))],
        out_shape=[jax.ShapeDtypeStruct((TOP_K, t), jnp.int32),
                   jax.ShapeDtypeStruct((TOP_K, t), F32),
                   jax.ShapeDtypeStruct((TOP_K, t), jnp.int32),
                   jax.ShapeDtypeStruct((ne, LANES), jnp.int32)],
        scratch_shapes=[pltpu.VMEM((ne, LANES), F32)],
        compiler_params=_params("arbitrary"),
        name="router",
    )(x, w_hi, w_lo, bias_b)


def _dest_t_kernel(eidx_ref, rank_ref, start_ref, dest_ref, *, tm, ne):
    row = lax.broadcasted_iota(jnp.int32, (ne, tm), 0)
    starts = start_ref[...].astype(F32)
    eidx = eidx_ref[...]
    dest = [jnp.sum(jnp.where(row == eidx[k:k + 1, :], starts, 0.0), axis=0, keepdims=True) for k in range(TOP_K)]
    dest_ref[...] = jnp.concatenate(dest, axis=0).astype(jnp.int32) + rank_ref[...]


def _dest_t(eidx, rank, starts_b, *, tm):
    t = eidx.shape[1]
    ne = starts_b.shape[0]
    col = lambda: pl.BlockSpec((TOP_K, tm), lambda i: (0, i))
    return pl.pallas_call(
        functools.partial(_dest_t_kernel, tm=tm, ne=ne),
        grid=(t // tm,),
        in_specs=[col(), col(), _const_spec(starts_b.shape)],
        out_specs=col(),
        out_shape=jax.ShapeDtypeStruct((TOP_K, t), jnp.int32),
        compiler_params=_params("arbitrary"),
        name="dest",
    )(eidx, rank, starts_b)


TOKEN_DTYPE = BF16


def _to_token_tiles(rows):
    chunks = jnp.stack([rows[:, c * LANES:(c + 1) * LANES] for c in range(SUBLANES)], axis=0)
    return jnp.swapaxes(chunks, 0, 1).astype(TOKEN_DTYPE)


def _chunks_of_token_tiles(tiles):
    chunks = jnp.swapaxes(tiles.astype(F32), 0, 1)
    return [chunks[c] for c in range(SUBLANES)]


def _token_copy(src_ref, src_row, dst_ref, dst_row, sem):
    return pltpu.make_async_copy(src_ref.at[pl.ds(src_row, 1)], dst_ref.at[pl.ds(dst_row, 1)], sem)


def _dispatch_kernel(dest_ref, x_ref, xs_ref, xt, sem, *, tt):
    xt[...] = _to_token_tiles(x_ref[...])

    def start(tp, c):
        for u in range(ISSUE_UNROLL):
            t = tp * ISSUE_UNROLL + u
            for k in range(TOP_K):
                _token_copy(xt, t, xs_ref, dest_ref[t * TOP_K + k], sem).start(priority=k % 2)
        return c

    lax.fori_loop(0, tt // ISSUE_UNROLL, start, 0)
    for _ in range(TOP_K):
        pltpu.make_async_copy(xt, xs_ref.at[pl.ds(0, tt)], sem).wait()


def _dispatch(dest_flat, x, n_slots, *, tt):
    t, d = x.shape
    assert d == SUBLANES * LANES
    return pl.pallas_call(
        functools.partial(_dispatch_kernel, tt=tt),
        grid=(t // tt,),
        in_specs=[pl.BlockSpec((tt * TOP_K,), lambda i: (i,), memory_space=pltpu.SMEM),
                  pl.BlockSpec((tt, d), lambda i: (i, 0))],
        out_specs=pl.BlockSpec(memory_space=pl.ANY),
        out_shape=jax.ShapeDtypeStruct((n_slots, SUBLANES, LANES), TOKEN_DTYPE),
        scratch_shapes=[pltpu.VMEM((tt, SUBLANES, LANES), TOKEN_DTYPE), pltpu.SemaphoreType.DMA(())],
        compiler_params=_params("arbitrary"),
        name="dispatch",
    )(dest_flat, x)


EXPERT_SLOTS = 16
EXPERT_STEP_BLOCKS = 4
EXPERT_AHEAD = 8


def _expert_kernel(base_ref, nblk_ref, cnt_ref, nu_ref, xs_ref, wg_ref, wu_ref, wd_ref, ys_ref,
                   xbuf, ybuf, wgb, wub, wdb, sem_in, sem_out):
    e = pl.program_id(0)
    n_used = nu_ref[0]
    nb = nblk_ref[e]
    base = base_ref[e]
    cnt = cnt_ref[e]
    ahead = EXPERT_AHEAD

    def x_copy(blk, slot):
        return pltpu.make_async_copy(xs_ref.at[pl.ds(blk * MOE_ROWS, MOE_ROWS)], xbuf.at[slot], sem_in.at[slot])

    def y_copy(blk, slot):
        return pltpu.make_async_copy(ybuf.at[slot], ys_ref.at[pl.ds(blk * MOE_ROWS, MOE_ROWS)], sem_out.at[slot])

    @pl.when(e == 0)
    def _():
        for first in range(ahead):
            @pl.when(first < n_used)
            def _():
                x_copy(first, first).start()

    @pl.when(nb > 0)
    def _():
        wgb[...] = wg_ref[0].astype(BF16)
        wub[...] = wu_ref[0].astype(BF16)
        wdb[...] = wd_ref[0].astype(BF16)

    def step(jb, chains):
        nblocks = sum(chains)
        blks = [base + jb + i for i in range(nblocks)]
        slots = [jnp.bitwise_and(b, EXPERT_SLOTS - 1) for b in blks]
        for b, s in zip(blks, slots):
            x_copy(b, s).wait()
        for b in blks:
            @pl.when(b + ahead < n_used)
            def _():
                x_copy(b + ahead, jnp.bitwise_and(b + ahead, EXPERT_SLOTS - 1)).start()
        for b, s in zip(blks, slots):
            @pl.when(b >= EXPERT_SLOTS)
            def _():
                y_copy(b - EXPERT_SLOTS, s).wait()

        first = 0
        for n in chains:
            x = jnp.concatenate([jnp.concatenate(_chunks_of_token_tiles(xbuf[s]), axis=1)
                                 for s in slots[first:first + n]], axis=0)
            rows = lax.broadcasted_iota(jnp.int32, x.shape, 0)
            xb = jnp.where(rows < cnt - (jb + first) * MOE_ROWS, x, 0.0).astype(BF16)
            g = jnp.dot(xb, wgb[...], preferred_element_type=F32)
            u = jnp.dot(xb, wub[...], preferred_element_type=F32)
            hmid = (g * _sigmoid(g)) * u
            y = jnp.dot(hmid.astype(BF16), wdb[...], preferred_element_type=F32)
            for i in range(n):
                b, s = blks[first + i], slots[first + i]
                ybuf[s] = _to_token_tiles(y[i * MOE_ROWS:(i + 1) * MOE_ROWS])
                y_copy(b, s).start()
            first += n

    def quad(jq, c):
        step(4 * jq, (2, 2))
        return c

    nquad = lax.shift_right_logical(nb, 2)
    lax.fori_loop(0, nquad, quad, 0)
    rem = jnp.bitwise_and(nb, 3)
    for r, chains in ((1, (1,)), (2, (2,)), (3, (2, 1))):
        @pl.when(rem == r)
        def _():
            step(4 * nquad, chains)

    @pl.when(e == pl.num_programs(0) - 1)
    def _():
        for back in range(EXPERT_SLOTS):
            @pl.when(n_used > back)
            def _():
                last = n_used - 1 - back
                y_copy(last, jnp.bitwise_and(last, EXPERT_SLOTS - 1)).wait()


def _experts(base, nblk, cnt, n_used, xs, wg, wu, wd):
    p = xs.shape[0]
    ne, d, de = wg.shape
    assert EXPERT_SLOTS & (EXPERT_SLOTS - 1) == 0 and EXPERT_AHEAD <= EXPERT_SLOTS - EXPERT_STEP_BLOCKS
    assert d == SUBLANES * LANES
    ring = pltpu.VMEM((EXPERT_SLOTS, MOE_ROWS, SUBLANES, LANES), TOKEN_DTYPE)
    grid_spec = pltpu.PrefetchScalarGridSpec(
        num_scalar_prefetch=4,
        grid=(ne,),
        in_specs=[pl.BlockSpec(memory_space=pl.ANY),
                  pl.BlockSpec((1, d, de), lambda e, *_: (e, 0, 0)),
                  pl.BlockSpec((1, d, de), lambda e, *_: (e, 0, 0)),
                  pl.BlockSpec((1, de, d), lambda e, *_: (e, 0, 0))],
        out_specs=pl.BlockSpec(memory_space=pl.ANY),
        scratch_shapes=[ring, ring,
                        pltpu.VMEM((d, de), BF16), pltpu.VMEM((d, de), BF16), pltpu.VMEM((de, d), BF16),
                        pltpu.SemaphoreType.DMA((EXPERT_SLOTS,)), pltpu.SemaphoreType.DMA((EXPERT_SLOTS,))],
    )
    return pl.pallas_call(
        _expert_kernel,
        grid_spec=grid_spec,
        out_shape=jax.ShapeDtypeStruct((p, SUBLANES, LANES), TOKEN_DTYPE),
        compiler_params=_params("arbitrary"),
        name="experts",
    )(base, nblk, cnt, n_used, xs, wg, wu, wd)


def _combine_kernel(dest_ref, dest_next_ref, x_ref, gate_ref, ys_ref, wsg_ref, wsu_ref, wsd_ref, g_ref, b_ref,
                    o_head_ref, o_tail_ref, ybuf, sem, *, tt, alpha, head_steps):
    i = pl.program_id(0)
    slot = jnp.bitwise_and(i, 1)

    def gather(dref, s):
        def start(tp, c):
            for u in range(ISSUE_UNROLL):
                t = tp * ISSUE_UNROLL + u
                for k in range(TOP_K):
                    _token_copy(ys_ref, dref[t * TOP_K + k], ybuf.at[s], k * tt + t, sem.at[s]).start(priority=k % 2)
            return c
        lax.fori_loop(0, tt // ISSUE_UNROLL, start, 0)

    @pl.when(i == 0)
    def _():
        gather(dest_ref, 0)

    def prefetch_group(k):
        nxt = 1 - slot
        for t in range(tt):
            _token_copy(ys_ref, dest_next_ref[t * TOP_K + k], ybuf.at[nxt], k * tt + t, sem.at[nxt]).start(priority=t % 2)

    x = x_ref[...]
    xb = x.astype(BF16)
    sg = jnp.dot(xb, wsg_ref[...], preferred_element_type=F32)
    su = jnp.dot(xb, wsu_ref[...], preferred_element_type=F32)
    acc = jnp.dot(((sg * _sigmoid(sg)) * su).astype(BF16), wsd_ref[...], preferred_element_type=F32)
    pltpu.make_async_copy(ys_ref.at[pl.ds(0, tt * TOP_K)], ybuf.at[slot], sem.at[slot]).wait()
    gate = gate_ref[...]
    routed = None
    for k in range(TOP_K):
        prefetch_group(k)
        gk = jnp.broadcast_to(gate[:, k:k + 1], (tt, LANES))
        part = [gk * ch for ch in _chunks_of_token_tiles(ybuf[slot, k * tt:(k + 1) * tt])]
        routed = part if routed is None else [r + p for r, p in zip(routed, part)]
    acc = acc + jnp.concatenate(routed, axis=1)
    out = _layer_norm(alpha * x + acc, g_ref[...], b_ref[...])

    @pl.when(i < head_steps)
    def _():
        o_head_ref[...] = out

    @pl.when(i >= head_steps)
    def _():
        o_tail_ref[...] = out

    @pl.when(i == pl.num_programs(0) - 1)
    def _():
        pltpu.make_async_copy(ys_ref.at[pl.ds(0, tt * TOP_K)], ybuf.at[1 - slot], sem.at[1 - slot]).wait()


def _combine(dest_flat, x, gate, ys, wsg, wsu, wsd, g, b, *, tt, alpha, head_rows):
    t, d = x.shape
    nsteps = t // tt
    head_steps = head_rows // tt
    assert head_rows % tt == 0 and 0 < head_steps < nsteps
    return pl.pallas_call(
        functools.partial(_combine_kernel, tt=tt, alpha=alpha, head_steps=head_steps),
        grid=(nsteps,),
        in_specs=[pl.BlockSpec((tt * TOP_K,), lambda i: (i,), memory_space=pltpu.SMEM),
                  pl.BlockSpec((tt * TOP_K,), lambda i: (jnp.minimum(i + 1, nsteps - 1),),
                               memory_space=pltpu.SMEM),
                  pl.BlockSpec((tt, d), lambda i: (i, 0)),
                  pl.BlockSpec((tt, TOP_K), lambda i: (i, 0)),
                  pl.BlockSpec(memory_space=pl.ANY),
                  _const_spec(wsg.shape), _const_spec(wsu.shape), _const_spec(wsd.shape),
                  _const_spec(g.shape), _const_spec(b.shape)],
        out_specs=[pl.BlockSpec((tt, d), lambda i: (jnp.minimum(i, head_steps - 1), 0)),
                   pl.BlockSpec((tt, d), lambda i: (jnp.maximum(i - head_steps, 0), 0))],
        out_shape=[jax.ShapeDtypeStruct((head_rows, d), F32), jax.ShapeDtypeStruct((t - head_rows, d), F32)],
        scratch_shapes=[pltpu.VMEM((2, tt * TOP_K, SUBLANES, LANES), TOKEN_DTYPE), pltpu.SemaphoreType.DMA((2,))],
        compiler_params=_params("arbitrary"),
        name="combine_ln2",
    )(dest_flat, dest_flat, x, gate, ys, wsg, wsu, wsd, g, b)


def _block_diag_tiles(w):
    n, s, _ = w.shape
    per = MXU_DIM // s
    w = w.reshape(n // per, per, s, s)
    eye = jnp.eye(per, dtype=w.dtype)
    return jnp.einsum('tpij,pq->tpiqj', w, eye).reshape(n // per, MXU_DIM, MXU_DIM)


def _row(v):
    return v.reshape(1, -1)


def _moe(x1, w_router, router_bias, w_gate, w_up, w_down, ws_gate, ws_up, ws_down, ln_g, ln_b, alpha, head_rows):
    t, d = x1.shape
    ne = w_router.shape[1]
    tm = 256
    w_t = w_router.T
    w_hi = w_t.astype(BF16)
    w_lo = (w_t - w_hi.astype(F32)).astype(BF16)
    bias_b = jnp.broadcast_to(router_bias.reshape(ne, 1), (ne, tm))
    eidx_t, gate_t, rank_t, counts = _router_t(x1, w_hi, w_lo, bias_b, tm=tm)
    counts = counts[:, 0]
    nblk = (counts + MOE_ROWS - 1) // MOE_ROWS
    blk_end = jnp.cumsum(nblk)
    blk_base = (blk_end - nblk).astype(jnp.int32)
    n_used = blk_end[-1:].astype(jnp.int32)
    n_slots = ((t * TOP_K) // MOE_ROWS + ne) * MOE_ROWS

    starts_b = jnp.broadcast_to((blk_base * MOE_ROWS).reshape(ne, 1), (ne, tm))
    dest_flat = _dest_t(eidx_t, rank_t, starts_b, tm=tm).T.reshape(-1)
    xs = _dispatch(dest_flat, x1, n_slots, tt=256)
    ys = _experts(blk_base, nblk.astype(jnp.int32), counts, n_used, xs, w_gate, w_up, w_down)
    return _combine(dest_flat, x1, gate_t.T, ys, ws_gate.astype(BF16), ws_up.astype(BF16),
                    ws_down.astype(BF16), _row(ln_g), _row(ln_b), tt=128, alpha=alpha, head_rows=head_rows)


def _pad_time(a, lp):
    return jnp.pad(a, ((0, 0), (0, lp - a.shape[1]), (0, 0)))


def kernel(x_prompt, x_sample, state_rglru_h, state_conv, state_gla, meta_tokens, w_in, conv_w, conv_b,
           rg_wa, rg_ba, rg_wi, rg_bi, rg_lambda, gla_wf2, gla_bf, gla_norm_g, w_out, ln1_g, ln1_b,
           w_router, router_bias, w_gate, w_up, w_down, ws_gate, ws_up, ws_down, ln2_g, ln2_b):
    bp, lp, d = x_prompt.shape
    bs, ls, _ = x_sample.shape
    depth = w_in.shape[0]
    nh, dk, dv = state_gla.shape[2:]
    qk, vw = nh * dk, nh * dv
    lowrank = gla_wf2.shape[1]
    alpha = (2.0 * depth) ** 0.25
    ls_pad = SUBLANES
    assert depth == 1 and ls <= ls_pad and lp % 256 == 0 and N_META % SUBLANES == 0

    xp = x_prompt.reshape(bp * lp, d)
    xs = x_sample.reshape(bs * ls, d)
    xm = meta_tokens.astype(F32)

    hp_l, cp_l, sp_l, hs_l, cs_l, ss_l = [], [], [], [], [], []
    for l in range(depth):
        o1, o2 = 2 * d, 2 * d + 2 * qk + 2 * vw
        wa = w_in[l][:, :o1].astype(BF16)
        wb = jnp.pad(w_in[l][:, o1:o2 + lowrank], ((0, 0), (0, LANES - lowrank))).astype(BF16)
        wm = w_in[l][:, o2 + lowrank:].astype(BF16)
        wf = jnp.pad(gla_wf2[l], ((0, LANES - lowrank), (0, 0))).astype(BF16)
        wa4 = _block_diag_tiles(rg_wa[l]).astype(BF16)
        wi4 = _block_diag_tiles(rg_wi[l]).astype(BF16)
        mixer_w = (conv_w[l], _row(conv_b[l]), wa4, _row(rg_ba[l]), wi4, _row(rg_bi[l]), _row(rg_lambda[l]))
        gla_w = (wf, _row(gla_bf[l]), _row(gla_norm_g[l]))

        ua, ub, _ = _inproj(xm, wa, wb, wm, tm=N_META)
        _, h_m, c_m = _rglru(ua[None], jnp.zeros((1, CONV_W - 1, d), F32), jnp.zeros((1, 1, d), F32),
                             *mixer_w, bb=1, tb=N_META, l_real=N_META)
        _, s_m = _gla(ub[None], jnp.zeros((1, nh, dk, dv), F32), *gla_w,
                      bb=1, tb=N_META, chunk=N_META, l_real=N_META)

        ua, ub, um = _inproj(xp, wa, wb, wm, tm=256, um_dtype=BF16)
        ya, h_p, c_p = _rglru(ua.reshape(bp, lp, -1), jnp.broadcast_to(c_m, (bp,) + c_m.shape[1:]),
                              jnp.broadcast_to(h_m, (bp,) + h_m.shape[1:]), *mixer_w,
                              bb=1, tb=256, l_real=lp, y_dtype=BF16)
        yb, s_p = _gla(ub.reshape(bp, lp, -1), jnp.broadcast_to(s_m, (bp,) + s_m.shape[1:]), *gla_w,
                       bb=1, tb=256, chunk=GLA_CHUNK, l_real=lp, y_dtype=BF16)
        n_tok = bp * lp + bs * ls
        x1 = _mix(ya.reshape(bp * lp, d), yb.reshape(bp * lp, d), um, xp, w_out[l].astype(BF16),
                  _row(ln1_g[l]), _row(ln1_b[l]), alpha=alpha, tm=512, total_rows=n_tok)

        xs_pad = _pad_time(xs.reshape(bs, ls, d), ls_pad).reshape(bs * ls_pad, d)
        ua, ub, um = _inproj(xs_pad, wa, wb, wm, tm=256)
        ya, h_s, c_s = _rglru(ua.reshape(bs, ls_pad, -1), state_conv[l].astype(F32),
                              state_rglru_h[l].astype(F32)[:, None, :], *mixer_w,
                              bb=32, tb=ls_pad, l_real=ls)
        yb, s_s = _gla(ub.reshape(bs, ls_pad, -1), state_gla[l].astype(F32), *gla_w,
                       bb=8, tb=ls_pad, chunk=ls_pad, l_real=ls)
        unpad = lambda a: a.reshape(bs, ls_pad, -1)[:, :ls].reshape(bs * ls, -1)
        x1 = _mix(unpad(ya), unpad(yb), unpad(um), xs, w_out[l].astype(BF16),
                  _row(ln1_g[l]), _row(ln1_b[l]), alpha=alpha, tm=512, total_rows=n_tok,
                  row_offset=bp * lp, into=x1)

        xp, xs = _moe(x1, w_router[l], router_bias[l], w_gate[l], w_up[l], w_down[l],
                      ws_gate[l], ws_up[l], ws_down[l], ln2_g[l], ln2_b[l], alpha, bp * lp)

        hp_l.append(h_p[:, 0]); cp_l.append(c_p); sp_l.append(s_p)
        hs_l.append(h_s[:, 0]); cs_l.append(c_s); ss_l.append(s_s)

    y_prompt = xp.reshape(bp, lp, d)
    y_sample = xs.reshape(bs, ls, d)
    return (y_prompt, y_sample,
            jnp.stack(hp_l).astype(state_rglru_h.dtype), jnp.stack(cp_l).astype(state_conv.dtype),
            jnp.stack(sp_l).astype(state_gla.dtype),
            jnp.stack(hs_l).astype(state_rglru_h.dtype), jnp.stack(cs_l).astype(state_conv.dtype),
            jnp.stack(ss_l).astype(state_gla.dtype))
```

```python
import functools

import jax
import jax.numpy as jnp
from jax import lax
from jax.experimental import pallas as pl
from jax.experimental.pallas import tpu as pltpu

F32 = jnp.float32
BF16 = jnp.bfloat16
HIGHEST = lax.Precision.HIGHEST

N_META = 16
CONV_W = 4
RG_C = 8.0
RNN_BLOCKS = 16
GLA_HEADS = 4
GLA_TAU = 16.0
GLA_CHUNK = 64
N_GROUPS = 8
TOPK_GROUPS = 4
TOP_K = 8
ROUTED_SCALE = 2.5
LN_EPS = 1e-5
RMS_EPS = 1e-6

LANES = 128
SUBLANES = 8
MXU_DIM = 256
VMEM_LIMIT_BYTES = 56 * 1024 * 1024

MOE_ROWS = 128
ISSUE_UNROLL = 4


def _params(*sem):
    return pltpu.CompilerParams(dimension_semantics=sem, vmem_limit_bytes=VMEM_LIMIT_BYTES)


def _sigmoid(x):
    return jax.nn.sigmoid(x)


def _softplus(z):
    return jnp.maximum(z, 0.0) + jnp.log1p(jnp.exp(-jnp.abs(z)))


def _layer_norm(z, g, b):
    mu = jnp.mean(z, axis=-1, keepdims=True)
    zc = z - mu
    var = jnp.mean(zc * zc, axis=-1, keepdims=True)
    return zc * lax.rsqrt(var + LN_EPS) * g + b


def _const_spec(shape):
    nd = len(shape)
    return pl.BlockSpec(shape, lambda *_: (0,) * nd)


def _inproj_kernel(x_ref, wa_ref, wb_ref, wm_ref, ua_ref, ub_ref, um_ref):
    xb = x_ref[...].astype(BF16)
    ua_ref[...] = jnp.dot(xb, wa_ref[...], preferred_element_type=F32)
    ub_ref[...] = jnp.dot(xb, wb_ref[...], preferred_element_type=F32)
    um_ref[...] = jnp.dot(xb, wm_ref[...], preferred_element_type=F32).astype(um_ref.dtype)


def _inproj(x2d, wa, wb, wm, tm, um_dtype=F32):
    m, d = x2d.shape
    na, nb, nm = wa.shape[1], wb.shape[1], wm.shape[1]
    return pl.pallas_call(
        _inproj_kernel,
        grid=(m // tm,),
        in_specs=[pl.BlockSpec((tm, d), lambda i: (i, 0)),
                  _const_spec(wa.shape), _const_spec(wb.shape), _const_spec(wm.shape)],
        out_specs=[pl.BlockSpec((tm, na), lambda i: (i, 0)),
                   pl.BlockSpec((tm, nb), lambda i: (i, 0)),
                   pl.BlockSpec((tm, nm), lambda i: (i, 0))],
        out_shape=[jax.ShapeDtypeStruct((m, na), F32),
                   jax.ShapeDtypeStruct((m, nb), F32),
                   jax.ShapeDtypeStruct((m, nm), um_dtype)],
        compiler_params=_params("arbitrary"),
        name="inproj",
    )(x2d, wa, wb, wm)


def _rglru_kernel(ua_ref, cbuf_ref, h0_ref, cw_ref, cb_ref, wa_ref, ba_ref, wi_ref, bi_ref, lam_ref,
                  ya_ref, hlast_ref, cnew_ref, ext_scr, h_scr, *, bb, tb, l_last, nblk, d):
    j = pl.program_id(1)
    ntail = CONV_W - 1
    rows = bb * tb

    @pl.when(j == 0)
    def _():
        h_scr[...] = h0_ref[...]
        ext_scr[:, SUBLANES - ntail:SUBLANES, :] = cbuf_ref[...]

    xr = ua_ref[:, :, :d]
    gr = ua_ref[:, :, d:].reshape(rows, d)
    ext_scr[:, SUBLANES:, :] = xr
    xc = cb_ref[...] + cw_ref[ntail:ntail + 1, :] * xr
    for s in range(1, CONV_W):
        xc = xc + cw_ref[ntail - s:ntail - s + 1, :] * ext_scr[:, SUBLANES - s:SUBLANES - s + tb, :]
    xc = xc.reshape(rows, d)

    xcb = xc.astype(BF16)
    nt = d // MXU_DIM
    ra = jnp.concatenate(
        [jnp.dot(xcb[:, q * MXU_DIM:(q + 1) * MXU_DIM], wa_ref[q], preferred_element_type=F32)
         for q in range(nt)], axis=-1)
    ia = jnp.concatenate(
        [jnp.dot(xcb[:, q * MXU_DIM:(q + 1) * MXU_DIM], wi_ref[q], preferred_element_type=F32)
         for q in range(nt)], axis=-1)
    r = _sigmoid(ra + ba_ref[...])
    ig = _sigmoid(ia + bi_ref[...])
    log_a = (-RG_C) * r * _softplus(-lam_ref[...])
    a = jnp.exp(log_a)
    z = -jnp.tanh(log_a) * (a * a + 1.0)
    mult = jnp.where(z > 0.0, z * lax.rsqrt(z), 0.0)
    b = mult * (ig * xc)

    t_in = jnp.bitwise_and(lax.broadcasted_iota(jnp.int32, (rows, d), 0), SUBLANES - 1)
    s = 1
    while s < SUBLANES:
        a_sh = jnp.where(t_in >= s, pltpu.roll(a, s, 0), 1.0)
        b_sh = jnp.where(t_in >= s, pltpu.roll(b, s, 0), 0.0)
        b = b + a * b_sh
        a = a * a_sh
        s *= 2
    a = a.reshape(bb, tb, d)
    b = b.reshape(bb, tb, d)
    carry = h_scr[...]
    tiles = []
    for i in range(tb // SUBLANES):
        rs = slice(i * SUBLANES, (i + 1) * SUBLANES)
        hi = b[:, rs, :] + a[:, rs, :] * carry
        carry = hi[:, SUBLANES - 1:SUBLANES, :]
        tiles.append(hi)
    h = jnp.concatenate(tiles, axis=1)
    h_scr[...] = carry
    ya_ref[...] = (h * jax.nn.gelu(gr, approximate=True).reshape(bb, tb, d)).astype(ya_ref.dtype)

    @pl.when(j == nblk - 1)
    def _():
        hlast_ref[...] = h[:, l_last - 1:l_last, :]
        cnew_ref[...] = ext_scr[:, SUBLANES + l_last - ntail:SUBLANES + l_last, :]

    tail = ext_scr[:, tb + SUBLANES - ntail:tb + SUBLANES, :]
    ext_scr[:, SUBLANES - ntail:SUBLANES, :] = tail


def _rglru(ua, cbuf, h0, cw, cb, wa4, ba, wi4, bi, lam, *, bb, tb, l_real, y_dtype=F32):
    bsz, lp, n2 = ua.shape
    d = n2 // 2
    nblk = lp // tb
    l_last = l_real - (nblk - 1) * tb
    assert tb & (tb - 1) == 0 and bsz % bb == 0
    kern = functools.partial(_rglru_kernel, bb=bb, tb=tb, l_last=l_last, nblk=nblk, d=d)
    return pl.pallas_call(
        kern,
        grid=(bsz // bb, nblk),
        in_specs=[pl.BlockSpec((bb, tb, n2), lambda b, j: (b, j, 0)),
                  pl.BlockSpec((bb, CONV_W - 1, d), lambda b, j: (b, 0, 0)),
                  pl.BlockSpec((bb, 1, d), lambda b, j: (b, 0, 0)),
                  _const_spec(cw.shape), _const_spec(cb.shape),
                  _const_spec(wa4.shape), _const_spec(ba.shape),
                  _const_spec(wi4.shape), _const_spec(bi.shape), _const_spec(lam.shape)],
        out_specs=[pl.BlockSpec((bb, tb, d), lambda b, j: (b, j, 0)),
                   pl.BlockSpec((bb, 1, d), lambda b, j: (b, 0, 0)),
                   pl.BlockSpec((bb, CONV_W - 1, d), lambda b, j: (b, 0, 0))],
        out_shape=[jax.ShapeDtypeStruct((bsz, lp, d), y_dtype),
                   jax.ShapeDtypeStruct((bsz, 1, d), F32),
                   jax.ShapeDtypeStruct((bsz, CONV_W - 1, d), F32)],
        scratch_shapes=[pltpu.VMEM((bb, tb + SUBLANES, d), F32), pltpu.VMEM((bb, 1, d), F32)],
        compiler_params=_params("arbitrary", "arbitrary"),
        name="rglru",
    )(ua, cbuf, h0, cw, cb, wa4, ba, wi4, bi, lam)


def _gla_kernel(ub_ref, s0_ref, wf_ref, bf_ref, g_ref, yb_ref, sout_ref, s_scr,
                *, bb, tb, chunk, l_real, nblk, dk, dv):
    j = pl.program_id(1)
    nh = GLA_HEADS
    qk = nh * dk
    vw = nh * dv

    @pl.when(j == 0)
    def _():
        s_scr[...] = s0_ref[...]

    off_k, off_v, off_g, off_f = qk, 2 * qk, 2 * qk + vw, 2 * qk + 2 * vw
    merge = bb > 1 and nblk == 1 and tb == chunk
    nrow = bb * tb if merge else tb
    nc = nrow // chunk
    shift = chunk.bit_length() - 1
    ri = lax.broadcasted_iota(jnp.int32, (nrow, nrow), 0)
    ci = lax.broadcasted_iota(jnp.int32, (nrow, nrow), 1)
    keep = (lax.shift_right_logical(ri, shift) == lax.shift_right_logical(ci, shift)) & (ri >= ci)
    tril = keep.astype(F32)
    row_chunk = lax.shift_right_logical(lax.broadcasted_iota(jnp.int32, (nrow, dk), 0), shift)
    mid = chunk // 2
    scale = dk ** -0.5

    def by_chunk(a):
        if nc == 1:
            return a
        return jnp.concatenate([jnp.where(row_chunk == c, a, 0.0) for c in range(nc)], axis=1)

    def per_chunk_row(a, r):
        return jnp.concatenate([jnp.broadcast_to(a[c * chunk + r:c * chunk + r + 1, :], (chunk, a.shape[1]))
                                for c in range(nc)], axis=0)

    for bi in range(1 if merge else bb):
        def cols(c0, c1, bi=bi):
            if merge:
                return ub_ref[:, :, c0:c1].reshape(nrow, c1 - c0)
            return ub_ref[bi, :, c0:c1]

        fl = cols(off_f, off_f + LANES).astype(BF16)
        logit = jnp.dot(fl, wf_ref[...], preferred_element_type=F32) + bf_ref[...]
        log_f = (jnp.minimum(logit, 0.0) - jnp.log1p(jnp.exp(-jnp.abs(logit)))) * (1.0 / GLA_TAU)
        if l_real < tb * nblk:
            rows = lax.broadcasted_iota(jnp.int32, log_f.shape, 0)
            rows = (jnp.bitwise_and(rows, tb - 1) if merge else rows) + j * tb
            log_f = jnp.where(rows < l_real, log_f, 0.0)
        bcum = jnp.dot(tril, log_f, precision=HIGHEST, preferred_element_type=F32)
        ref = per_chunk_row(bcum, mid)
        last = per_chunk_row(bcum, chunk - 1)
        e_inter = jnp.exp(bcum)
        e_q = jnp.exp(bcum - ref)
        e_k = jnp.exp(ref - bcum)
        e_dec = jnp.exp(last - bcum)

        for h in range(nh):
            hk = slice(h * dk, (h + 1) * dk)
            q = cols(h * dk, (h + 1) * dk) * scale
            k = cols(off_k + h * dk, off_k + (h + 1) * dk)
            vb = cols(off_v + h * dv, off_v + (h + 1) * dv).astype(BF16)
            go = cols(off_g + h * dv, off_g + (h + 1) * dv)
            att = lax.dot_general((q * e_q[:, hk]).astype(BF16), (k * e_k[:, hk]).astype(BF16),
                                  (((1,), (1,)), ((), ())), preferred_element_type=F32)
            att = jnp.where(keep, att, 0.0)
            o = jnp.dot(att.astype(BF16), vb, preferred_element_type=F32)
            kd = by_chunk(k * e_dec[:, hk])
            kv = lax.dot_general(kd.astype(BF16), vb, (((0,), (0,)), ((), ())),
                                 preferred_element_type=F32)
            states = [s_scr[c, h] for c in range(nc)] if merge else [s_scr[bi, h]]
            for c in range(nc):
                drow = jnp.exp(bcum[c * chunk + chunk - 1:c * chunk + chunk, hk])
                dec = jnp.transpose(jnp.broadcast_to(drow, (dk, dk)))
                dec = jnp.concatenate([dec] * (dv // dk), axis=1)
                s_next = dec * states[c] + kv[c * dk:(c + 1) * dk]
                if merge:
                    s_scr[c, h] = s_next
                else:
                    states.append(s_next)
            if not merge:
                s_scr[bi, h] = states[nc]
            qi = by_chunk(q * e_inter[:, hk])
            s_in = jnp.concatenate(states[:nc], axis=0).astype(BF16)
            o = o + jnp.dot(qi.astype(BF16), s_in, preferred_element_type=F32)
            o = o * lax.rsqrt(jnp.mean(o * o, axis=-1, keepdims=True) + RMS_EPS) * g_ref[...]
            y = (o * (go * _sigmoid(go))).astype(yb_ref.dtype)
            if merge:
                yb_ref[:, :, h * dv:(h + 1) * dv] = y.reshape(bb, tb, dv)
            else:
                yb_ref[bi, :, h * dv:(h + 1) * dv] = y

    @pl.when(j == nblk - 1)
    def _():
        sout_ref[...] = s_scr[...]


def _gla(ub, s0, wf, bf, g, *, bb, tb, chunk, l_real, y_dtype=F32):
    bsz, lp, nb = ub.shape
    _, nh, dk, dv = s0.shape
    nblk = lp // tb
    assert bsz % bb == 0 and dv % dk == 0 and chunk & (chunk - 1) == 0 and tb % chunk == 0
    kern = functools.partial(_gla_kernel, bb=bb, tb=tb, chunk=chunk, l_real=l_real, nblk=nblk, dk=dk, dv=dv)
    return pl.pallas_call(
        kern,
        grid=(bsz // bb, nblk),
        in_specs=[pl.BlockSpec((bb, tb, nb), lambda b, j: (b, j, 0)),
                  pl.BlockSpec((bb, nh, dk, dv), lambda b, j: (b, 0, 0, 0)),
                  _const_spec(wf.shape), _const_spec(bf.shape), _const_spec(g.shape)],
        out_specs=[pl.BlockSpec((bb, tb, nh * dv), lambda b, j: (b, j, 0)),
                   pl.BlockSpec((bb, nh, dk, dv), lambda b, j: (b, 0, 0, 0))],
        out_shape=[jax.ShapeDtypeStruct((bsz, lp, nh * dv), y_dtype),
                   jax.ShapeDtypeStruct((bsz, nh, dk, dv), F32)],
        scratch_shapes=[pltpu.VMEM((bb, nh, dk, dv), F32)],
        compiler_params=_params("arbitrary", "arbitrary"),
        name="gla",
    )(ub, s0, wf, bf, g)


def _mix_kernel(ya_ref, yb_ref, um_ref, x_ref, wo_ref, g_ref, b_ref, *rest, alpha, d):
    o_ref = rest[-1]
    um = um_ref[...].astype(F32)
    merged = (_sigmoid(um[:, :d]) * ya_ref[...].astype(F32)
              + _sigmoid(um[:, d:]) * yb_ref[...].astype(F32))
    mix = jnp.dot(merged.astype(BF16), wo_ref[...], preferred_element_type=F32)
    o_ref[...] = _layer_norm(alpha * x_ref[...] + mix, g_ref[...], b_ref[...])


def _mix(ya, yb, um, x, wo, g, b, *, alpha, tm, total_rows, row_offset=0, into=None):
    m, d = x.shape
    assert m % tm == 0 and row_offset % tm == 0
    row = lambda w: pl.BlockSpec((tm, w), lambda i: (i, 0))
    in_specs = [row(d), row(d), row(2 * d), row(d),
                _const_spec(wo.shape), _const_spec(g.shape), _const_spec(b.shape)]
    args = (ya, yb, um, x, wo, g, b)
    aliases = {}
    if into is not None:
        in_specs.append(pl.BlockSpec(memory_space=pl.ANY))
        aliases = {len(args): 0}
        args = args + (into,)
    return pl.pallas_call(
        functools.partial(_mix_kernel, alpha=alpha, d=d),
        grid=(m // tm,),
        in_specs=in_specs,
        out_specs=pl.BlockSpec((tm, d), lambda i: (i + row_offset // tm, 0)),
        out_shape=jax.ShapeDtypeStruct((total_rows, d), F32),
        input_output_aliases=aliases,
        compiler_params=_params("arbitrary"),
        name="mix_ln1",
    )(*args)


def _router_t_kernel(x_ref, whi_ref, wlo_ref, bias_ref, eidx_ref, gate_ref, rank_ref, cnt_ref, run_scr,
                     *, tm, ne):
    i = pl.program_id(0)

    @pl.when(i == 0)
    def _():
        run_scr[...] = jnp.zeros_like(run_scr)

    x = x_ref[...]
    x_hi = x.astype(BF16)
    x_lo = (x - x_hi.astype(F32)).astype(BF16)
    nt = (((1,), (1,)), ((), ()))
    logits = (lax.dot_general(whi_ref[...], x_hi, nt, preferred_element_type=F32)
              + lax.dot_general(whi_ref[...], x_lo, nt, preferred_element_type=F32)
              + lax.dot_general(wlo_ref[...], x_hi, nt, preferred_element_type=F32))
    s = _sigmoid(logits)
    sel = s + bias_ref[...]
    row = lax.broadcasted_iota(jnp.int32, (ne, tm), 0).astype(F32)
    gsz = ne // N_GROUPS
    neg = -jnp.inf

    def colmax(a):
        return jnp.max(a, axis=0, keepdims=True)

    def colmin(a):
        return jnp.min(a, axis=0, keepdims=True)

    gscores = []
    row_in_group = lax.broadcasted_iota(jnp.int32, (gsz, tm), 0).astype(F32)
    for g in range(N_GROUPS):
        v = sel[g * gsz:(g + 1) * gsz]
        rg = row_in_group + float(g * gsz)
        m1 = colmax(v)
        i1 = colmin(jnp.where(v == m1, rg, float(ne)))
        gscores.append(m1 + colmax(jnp.where(rg == i1, neg, v)))
    kept = []
    for g in range(N_GROUPS):
        ahead = jnp.zeros((1, tm), F32)
        for g2 in range(N_GROUPS):
            if g2 == g:
                continue
            beats = (gscores[g2] >= gscores[g]) if g2 < g else (gscores[g2] > gscores[g])
            ahead = ahead + beats.astype(F32)
        keep = jnp.broadcast_to(ahead < float(TOPK_GROUPS), (gsz, tm))
        kept.append(jnp.where(keep, sel[g * gsz:(g + 1) * gsz], neg))
    cur = jnp.concatenate(kept, axis=0)

    picks, eidx, wsel = [], [], []
    onehot_sum = jnp.zeros((ne, tm), F32)
    for k in range(TOP_K):
        ik = colmin(jnp.where(cur == colmax(cur), row, float(ne)))
        hit = row == ik
        wsel.append(jnp.sum(jnp.where(hit, s, 0.0), axis=0, keepdims=True))
        cur = jnp.where(hit, neg, cur)
        onehot_sum = onehot_sum + hit.astype(F32)
        picks.append(hit)
        eidx.append(ik)
    wsel = jnp.concatenate(wsel, axis=0)
    gate_ref[...] = wsel / jnp.sum(wsel, axis=0, keepdims=True) * ROUTED_SCALE
    eidx_ref[...] = jnp.concatenate(eidx, axis=0).astype(jnp.int32)

    ri = lax.broadcasted_iota(jnp.int32, (tm, tm), 0)
    ci = lax.broadcasted_iota(jnp.int32, (tm, tm), 1)
    earlier = (ri < ci).astype(BF16)
    ohb = onehot_sum.astype(BF16)
    counts = jnp.dot(ohb, earlier, preferred_element_type=F32) + jnp.concatenate([run_scr[...]] * (tm // LANES), axis=1)
    rank = [jnp.sum(jnp.where(picks[k], counts, 0.0), axis=0, keepdims=True) for k in range(TOP_K)]
    rank_ref[...] = jnp.concatenate(rank, axis=0).astype(jnp.int32)
    run_scr[...] = run_scr[...] + jnp.dot(ohb, jnp.ones((tm, LANES), BF16), preferred_element_type=F32)
    cnt_ref[...] = run_scr[...].astype(jnp.int32)


def _router_t(x, w_hi, w_lo, bias_b, *, tm):
    t, d = x.shape
    ne = w_hi.shape[0]
    out_col = lambda: pl.BlockSpec((TOP_K, tm), lambda i: (0, i))
    return pl.pallas_call(
        functools.partial(_router_t_kernel, tm=tm, ne=ne),
        grid=(t // tm,),
        in_specs=[pl.BlockSpec((tm, d), lambda i: (i, 0)), _const_spec(w_hi.shape), _const_spec(w_lo.shape),
                  _const_spec(bias_b.shape)],
        out_specs=[out_col(), out_col(), out_col(), _const_spec((ne, LANES))],
        out_shape=[jax.ShapeDtypeStruct((TOP_K, t), jnp.int32),
                   jax.ShapeDtypeStruct((TOP_K, t), F32),
                   jax.ShapeDtypeStruct((TOP_K, t), jnp.int32),
                   jax.ShapeDtypeStruct((ne, LANES), jnp.int32)],
        scratch_shapes=[pltpu.VMEM((ne, LANES), F32)],
        compiler_params=_params("arbitrary"),
        name="router",
    )(x, w_hi, w_lo, bias_b)


def _dest_t_kernel(eidx_ref, rank_ref, start_ref, dest_ref, *, tm, ne):
    row = lax.broadcasted_iota(jnp.int32, (ne, tm), 0)
    starts = start_ref[...].astype(F32)
    eidx = eidx_ref[...]
    dest = [jnp.sum(jnp.where(row == eidx[k:k + 1, :], starts, 0.0), axis=0, keepdims=True) for k in range(TOP_K)]
    dest_ref[...] = jnp.concatenate(dest, axis=0).astype(jnp.int32) + rank_ref[...]


def _dest_t(eidx, rank, starts_b, *, tm):
    t = eidx.shape[1]
    ne = starts_b.shape[0]
    col = lambda: pl.BlockSpec((TOP_K, tm), lambda i: (0, i))
    return pl.pallas_call(
        functools.partial(_dest_t_kernel, tm=tm, ne=ne),
        grid=(t // tm,),
        in_specs=[col(), col(), _const_spec(starts_b.shape)],
        out_specs=col(),
        out_shape=jax.ShapeDtypeStruct((TOP_K, t), jnp.int32),
        compiler_params=_params("arbitrary"),
        name="dest",
    )(eidx, rank, starts_b)


TOKEN_DTYPE = BF16


def _to_token_tiles(rows):
    chunks = jnp.stack([rows[:, c * LANES:(c + 1) * LANES] for c in range(SUBLANES)], axis=0)
    return jnp.swapaxes(chunks, 0, 1).astype(TOKEN_DTYPE)


def _chunks_of_token_tiles(tiles):
    chunks = jnp.swapaxes(tiles.astype(F32), 0, 1)
    return [chunks[c] for c in range(SUBLANES)]


def _token_copy(src_ref, src_row, dst_ref, dst_row, sem):
    return pltpu.make_async_copy(src_ref.at[pl.ds(src_row, 1)], dst_ref.at[pl.ds(dst_row, 1)], sem)


def _dispatch_kernel(dest_ref, x_ref, xs_ref, xt, sem, *, tt):
    xt[...] = _to_token_tiles(x_ref[...])

    def start(tp, c):
        for u in range(ISSUE_UNROLL):
            t = tp * ISSUE_UNROLL + u
            for k in range(TOP_K):
                _token_copy(xt, t, xs_ref, dest_ref[t * TOP_K + k], sem).start(priority=k % 2)
        return c

    lax.fori_loop(0, tt // ISSUE_UNROLL, start, 0)
    for _ in range(TOP_K):
        pltpu.make_async_copy(xt, xs_ref.at[pl.ds(0, tt)], sem).wait()


def _dispatch(dest_flat, x, n_slots, *, tt):
    t, d = x.shape
    assert d == SUBLANES * LANES
    return pl.pallas_call(
        functools.partial(_dispatch_kernel, tt=tt),
        grid=(t // tt,),
        in_specs=[pl.BlockSpec((tt * TOP_K,), lambda i: (i,), memory_space=pltpu.SMEM),
                  pl.BlockSpec((tt, d), lambda i: (i, 0))],
        out_specs=pl.BlockSpec(memory_space=pl.ANY),
        out_shape=jax.ShapeDtypeStruct((n_slots, SUBLANES, LANES), TOKEN_DTYPE),
        scratch_shapes=[pltpu.VMEM((tt, SUBLANES, LANES), TOKEN_DTYPE), pltpu.SemaphoreType.DMA(())],
        compiler_params=_params("arbitrary"),
        name="dispatch",
    )(dest_flat, x)


EXPERT_SLOTS = 16
EXPERT_TAILS = ((1, (1,)), (2, (2,)), (3, (2, 1)), (5, (2, 2, 1)), (6, (2, 2, 2)), (7, (2, 2, 2, 1)))
EXPERT_STEP_BLOCKS = max(r for r, _ in EXPERT_TAILS)
EXPERT_AHEAD = 8


def _expert_kernel(base_ref, nblk_ref, cnt_ref, nu_ref, xs_ref, wg_ref, wu_ref, wd_ref, ys_ref,
                   xbuf, ybuf, wgb, wub, wdb, sem_in, sem_out):
    e = pl.program_id(0)
    n_used = nu_ref[0]
    nb = nblk_ref[e]
    base = base_ref[e]
    cnt = cnt_ref[e]
    ahead = EXPERT_AHEAD

    def x_copy(blk, slot):
        return pltpu.make_async_copy(xs_ref.at[pl.ds(blk * MOE_ROWS, MOE_ROWS)], xbuf.at[slot], sem_in.at[slot])

    def y_copy(blk, slot):
        return pltpu.make_async_copy(ybuf.at[slot], ys_ref.at[pl.ds(blk * MOE_ROWS, MOE_ROWS)], sem_out.at[slot])

    @pl.when(e == 0)
    def _():
        for first in range(ahead):
            @pl.when(first < n_used)
            def _():
                x_copy(first, first).start()

    @pl.when(nb > 0)
    def _():
        wgb[...] = wg_ref[0].astype(BF16)
        wub[...] = wu_ref[0].astype(BF16)
        wdb[...] = wd_ref[0].astype(BF16)

    def step(jb, chains):
        nblocks = sum(chains)
        blks = [base + jb + i for i in range(nblocks)]
        slots = [jnp.bitwise_and(b, EXPERT_SLOTS - 1) for b in blks]
        for b, s in zip(blks, slots):
            x_copy(b, s).wait()
        for b in blks:
            @pl.when(b + ahead < n_used)
            def _():
                x_copy(b + ahead, jnp.bitwise_and(b + ahead, EXPERT_SLOTS - 1)).start()
        for b, s in zip(blks, slots):
            @pl.when(b >= EXPERT_SLOTS)
            def _():
                y_copy(b - EXPERT_SLOTS, s).wait()

        first = 0
        for n in chains:
            x = jnp.concatenate([jnp.concatenate(_chunks_of_token_tiles(xbuf[s]), axis=1)
                                 for s in slots[first:first + n]], axis=0)
            rows = lax.broadcasted_iota(jnp.int32, x.shape, 0)
            xb = jnp.where(rows < cnt - (jb + first) * MOE_ROWS, x, 0.0).astype(BF16)
            g = jnp.dot(xb, wgb[...], preferred_element_type=F32)
            u = jnp.dot(xb, wub[...], preferred_element_type=F32)
            hmid = (g * _sigmoid(g)) * u
            y = jnp.dot(hmid.astype(BF16), wdb[...], preferred_element_type=F32)
            for i in range(n):
                b, s = blks[first + i], slots[first + i]
                ybuf[s] = _to_token_tiles(y[i * MOE_ROWS:(i + 1) * MOE_ROWS])
                y_copy(b, s).start()
            first += n

    def quad(jq, c):
        step(4 * jq, (2, 2))
        return c

    rem = jnp.bitwise_and(nb, 3)
    nfull = lax.shift_right_logical(nb, 2)
    nloop = jnp.where(rem == 0, nfull, jnp.maximum(nfull - 1, 0))
    lax.fori_loop(0, nloop, quad, 0)
    tail = nb - 4 * nloop
    for r, chains in EXPERT_TAILS:
        @pl.when(tail == r)
        def _():
            step(4 * nloop, chains)

    @pl.when(e == pl.num_programs(0) - 1)
    def _():
        for back in range(EXPERT_SLOTS):
            @pl.when(n_used > back)
            def _():
                last = n_used - 1 - back
                y_copy(last, jnp.bitwise_and(last, EXPERT_SLOTS - 1)).wait()


def _experts(base, nblk, cnt, n_used, xs, wg, wu, wd):
    p = xs.shape[0]
    ne, d, de = wg.shape
    assert EXPERT_SLOTS & (EXPERT_SLOTS - 1) == 0 and EXPERT_AHEAD <= EXPERT_SLOTS - EXPERT_STEP_BLOCKS
    assert d == SUBLANES * LANES
    ring = pltpu.VMEM((EXPERT_SLOTS, MOE_ROWS, SUBLANES, LANES), TOKEN_DTYPE)
    grid_spec = pltpu.PrefetchScalarGridSpec(
        num_scalar_prefetch=4,
        grid=(ne,),
        in_specs=[pl.BlockSpec(memory_space=pl.ANY),
                  pl.BlockSpec((1, d, de), lambda e, *_: (e, 0, 0)),
                  pl.BlockSpec((1, d, de), lambda e, *_: (e, 0, 0)),
                  pl.BlockSpec((1, de, d), lambda e, *_: (e, 0, 0))],
        out_specs=pl.BlockSpec(memory_space=pl.ANY),
        scratch_shapes=[ring, ring,
                        pltpu.VMEM((d, de), BF16), pltpu.VMEM((d, de), BF16), pltpu.VMEM((de, d), BF16),
                        pltpu.SemaphoreType.DMA((EXPERT_SLOTS,)), pltpu.SemaphoreType.DMA((EXPERT_SLOTS,))],
    )
    return pl.pallas_call(
        _expert_kernel,
        grid_spec=grid_spec,
        out_shape=jax.ShapeDtypeStruct((p, SUBLANES, LANES), TOKEN_DTYPE),
        compiler_params=_params("arbitrary"),
        name="experts",
    )(base, nblk, cnt, n_used, xs, wg, wu, wd)


def _combine_kernel(dest_ref, dest_next_ref, x_ref, gate_ref, ys_ref, wsg_ref, wsu_ref, wsd_ref, g_ref, b_ref,
                    o_head_ref, o_tail_ref, ybuf, sem, *, tt, alpha, head_steps):
    i = pl.program_id(0)
    slot = jnp.bitwise_and(i, 1)

    def gather(dref, s):
        def start(tp, c):
            for u in range(ISSUE_UNROLL):
                t = tp * ISSUE_UNROLL + u
                for k in range(TOP_K):
                    _token_copy(ys_ref, dref[t * TOP_K + k], ybuf.at[s], k * tt + t, sem.at[s]).start(priority=k % 2)
            return c
        lax.fori_loop(0, tt // ISSUE_UNROLL, start, 0)

    @pl.when(i == 0)
    def _():
        gather(dest_ref, 0)

    def prefetch_group(k):
        nxt = 1 - slot
        for t in range(tt):
            _token_copy(ys_ref, dest_next_ref[t * TOP_K + k], ybuf.at[nxt], k * tt + t, sem.at[nxt]).start(priority=t % 2)

    x = x_ref[...]
    xb = x.astype(BF16)
    sg = jnp.dot(xb, wsg_ref[...], preferred_element_type=F32)
    su = jnp.dot(xb, wsu_ref[...], preferred_element_type=F32)
    acc = jnp.dot(((sg * _sigmoid(sg)) * su).astype(BF16), wsd_ref[...], preferred_element_type=F32)
    pltpu.make_async_copy(ys_ref.at[pl.ds(0, tt * TOP_K)], ybuf.at[slot], sem.at[slot]).wait()
    gate = gate_ref[...]
    routed = None
    for k in range(TOP_K):
        prefetch_group(k)
        gk = jnp.broadcast_to(gate[:, k:k + 1], (tt, LANES))
        part = [gk * ch for ch in _chunks_of_token_tiles(ybuf[slot, k * tt:(k + 1) * tt])]
        routed = part if routed is None else [r + p for r, p in zip(routed, part)]
    acc = acc + jnp.concatenate(routed, axis=1)
    out = _layer_norm(alpha * x + acc, g_ref[...], b_ref[...])

    @pl.when(i < head_steps)
    def _():
        o_head_ref[...] = out

    @pl.when(i >= head_steps)
    def _():
        o_tail_ref[...] = out

    @pl.when(i == pl.num_programs(0) - 1)
    def _():
        pltpu.make_async_copy(ys_ref.at[pl.ds(0, tt * TOP_K)], ybuf.at[1 - slot], sem.at[1 - slot]).wait()


def _combine(dest_flat, x, gate, ys, wsg, wsu, wsd, g, b, *, tt, alpha, head_rows):
    t, d = x.shape
    nsteps = t // tt
    head_steps = head_rows // tt
    assert head_rows % tt == 0 and 0 < head_steps < nsteps
    return pl.pallas_call(
        functools.partial(_combine_kernel, tt=tt, alpha=alpha, head_steps=head_steps),
        grid=(nsteps,),
        in_specs=[pl.BlockSpec((tt * TOP_K,), lambda i: (i,), memory_space=pltpu.SMEM),
                  pl.BlockSpec((tt * TOP_K,), lambda i: (jnp.minimum(i + 1, nsteps - 1),),
                               memory_space=pltpu.SMEM),
                  pl.BlockSpec((tt, d), lambda i: (i, 0)),
                  pl.BlockSpec((tt, TOP_K), lambda i: (i, 0)),
                  pl.BlockSpec(memory_space=pl.ANY),
                  _const_spec(wsg.shape), _const_spec(wsu.shape), _const_spec(wsd.shape),
                  _const_spec(g.shape), _const_spec(b.shape)],
        out_specs=[pl.BlockSpec((tt, d), lambda i: (jnp.minimum(i, head_steps - 1), 0)),
                   pl.BlockSpec((tt, d), lambda i: (jnp.maximum(i - head_steps, 0), 0))],
        out_shape=[jax.ShapeDtypeStruct((head_rows, d), F32), jax.ShapeDtypeStruct((t - head_rows, d), F32)],
        scratch_shapes=[pltpu.VMEM((2, tt * TOP_K, SUBLANES, LANES), TOKEN_DTYPE), pltpu.SemaphoreType.DMA((2,))],
        compiler_params=_params("arbitrary"),
        name="combine_ln2",
    )(dest_flat, dest_flat, x, gate, ys, wsg, wsu, wsd, g, b)


def _block_diag_tiles(w):
    n, s, _ = w.shape
    per = MXU_DIM // s
    w = w.reshape(n // per, per, s, s)
    eye = jnp.eye(per, dtype=w.dtype)
    return jnp.einsum('tpij,pq->tpiqj', w, eye).reshape(n // per, MXU_DIM, MXU_DIM)


def _row(v):
    return v.reshape(1, -1)


def _moe(x1, w_router, router_bias, w_gate, w_up, w_down, ws_gate, ws_up, ws_down, ln_g, ln_b, alpha, head_rows):
    t, d = x1.shape
    ne = w_router.shape[1]
    tm = 256
    w_t = w_router.T
    w_hi = w_t.astype(BF16)
    w_lo = (w_t - w_hi.astype(F32)).astype(BF16)
    bias_b = jnp.broadcast_to(router_bias.reshape(ne, 1), (ne, tm))
    eidx_t, gate_t, rank_t, counts = _router_t(x1, w_hi, w_lo, bias_b, tm=tm)
    counts = counts[:, 0]
    nblk = (counts + MOE_ROWS - 1) // MOE_ROWS
    blk_end = jnp.cumsum(nblk)
    blk_base = (blk_end - nblk).astype(jnp.int32)
    n_used = blk_end[-1:].astype(jnp.int32)
    n_slots = ((t * TOP_K) // MOE_ROWS + ne) * MOE_ROWS

    starts_b = jnp.broadcast_to((blk_base * MOE_ROWS).reshape(ne, 1), (ne, tm))
    dest_flat = _dest_t(eidx_t, rank_t, starts_b, tm=tm).T.reshape(-1)
    xs = _dispatch(dest_flat, x1, n_slots, tt=512)
    ys = _experts(blk_base, nblk.astype(jnp.int32), counts, n_used, xs, w_gate, w_up, w_down)
    return _combine(dest_flat, x1, gate_t.T, ys, ws_gate.astype(BF16), ws_up.astype(BF16),
                    ws_down.astype(BF16), _row(ln_g), _row(ln_b), tt=256, alpha=alpha, head_rows=head_rows)


def _pad_time(a, lp):
    return jnp.pad(a, ((0, 0), (0, lp - a.shape[1]), (0, 0)))


def kernel(x_prompt, x_sample, state_rglru_h, state_conv, state_gla, meta_tokens, w_in, conv_w, conv_b,
           rg_wa, rg_ba, rg_wi, rg_bi, rg_lambda, gla_wf2, gla_bf, gla_norm_g, w_out, ln1_g, ln1_b,
           w_router, router_bias, w_gate, w_up, w_down, ws_gate, ws_up, ws_down, ln2_g, ln2_b):
    bp, lp, d = x_prompt.shape
    bs, ls, _ = x_sample.shape
    depth = w_in.shape[0]
    nh, dk, dv = state_gla.shape[2:]
    qk, vw = nh * dk, nh * dv
    lowrank = gla_wf2.shape[1]
    alpha = (2.0 * depth) ** 0.25
    ls_pad = SUBLANES
    assert depth == 1 and ls <= ls_pad and lp % 256 == 0 and N_META % SUBLANES == 0

    xp = x_prompt.reshape(bp * lp, d)
    xs = x_sample.reshape(bs * ls, d)
    xm = meta_tokens.astype(F32)

    hp_l, cp_l, sp_l, hs_l, cs_l, ss_l = [], [], [], [], [], []
    for l in range(depth):
        o1, o2 = 2 * d, 2 * d + 2 * qk + 2 * vw
        wa = w_in[l][:, :o1].astype(BF16)
        wb = jnp.pad(w_in[l][:, o1:o2 + lowrank], ((0, 0), (0, LANES - lowrank))).astype(BF16)
        wm = w_in[l][:, o2 + lowrank:].astype(BF16)
        wf = jnp.pad(gla_wf2[l], ((0, LANES - lowrank), (0, 0))).astype(BF16)
        wa4 = _block_diag_tiles(rg_wa[l]).astype(BF16)
        wi4 = _block_diag_tiles(rg_wi[l]).astype(BF16)
        mixer_w = (conv_w[l], _row(conv_b[l]), wa4, _row(rg_ba[l]), wi4, _row(rg_bi[l]), _row(rg_lambda[l]))
        gla_w = (wf, _row(gla_bf[l]), _row(gla_norm_g[l]))

        ua, ub, _ = _inproj(xm, wa, wb, wm, tm=N_META)
        _, h_m, c_m = _rglru(ua[None], jnp.zeros((1, CONV_W - 1, d), F32), jnp.zeros((1, 1, d), F32),
                             *mixer_w, bb=1, tb=N_META, l_real=N_META)
        _, s_m = _gla(ub[None], jnp.zeros((1, nh, dk, dv), F32), *gla_w,
                      bb=1, tb=N_META, chunk=N_META, l_real=N_META)

        ua, ub, um = _inproj(xp, wa, wb, wm, tm=256, um_dtype=BF16)
        ya, h_p, c_p = _rglru(ua.reshape(bp, lp, -1), jnp.broadcast_to(c_m, (bp,) + c_m.shape[1:]),
                              jnp.broadcast_to(h_m, (bp,) + h_m.shape[1:]), *mixer_w,
                              bb=1, tb=256, l_real=lp, y_dtype=BF16)
        yb, s_p = _gla(ub.reshape(bp, lp, -1), jnp.broadcast_to(s_m, (bp,) + s_m.shape[1:]), *gla_w,
                       bb=1, tb=256, chunk=GLA_CHUNK, l_real=lp, y_dtype=BF16)
        n_tok = bp * lp + bs * ls
        x1 = _mix(ya.reshape(bp * lp, d), yb.reshape(bp * lp, d), um, xp, w_out[l].astype(BF16),
                  _row(ln1_g[l]), _row(ln1_b[l]), alpha=alpha, tm=512, total_rows=n_tok)

        xs_pad = _pad_time(xs.reshape(bs, ls, d), ls_pad).reshape(bs * ls_pad, d)
        ua, ub, um = _inproj(xs_pad, wa, wb, wm, tm=256)
        ya, h_s, c_s = _rglru(ua.reshape(bs, ls_pad, -1), state_conv[l].astype(F32),
                              state_rglru_h[l].astype(F32)[:, None, :], *mixer_w,
                              bb=32, tb=ls_pad, l_real=ls)
        yb, s_s = _gla(ub.reshape(bs, ls_pad, -1), state_gla[l].astype(F32), *gla_w,
                       bb=8, tb=ls_pad, chunk=ls_pad, l_real=ls)
        unpad = lambda a: a.reshape(bs, ls_pad, -1)[:, :ls].reshape(bs * ls, -1)
        x1 = _mix(unpad(ya), unpad(yb), unpad(um), xs, w_out[l].astype(BF16),
                  _row(ln1_g[l]), _row(ln1_b[l]), alpha=alpha, tm=512, total_rows=n_tok,
                  row_offset=bp * lp, into=x1)

        xp, xs = _moe(x1, w_router[l], router_bias[l], w_gate[l], w_up[l], w_down[l],
                      ws_gate[l], ws_up[l], ws_down[l], ln2_g[l], ln2_b[l], alpha, bp * lp)

        hp_l.append(h_p[:, 0]); cp_l.append(c_p); sp_l.append(s_p)
        hs_l.append(h_s[:, 0]); cs_l.append(c_s); ss_l.append(s_s)

    y_prompt = xp.reshape(bp, lp, d)
    y_sample = xs.reshape(bs, ls, d)
    return (y_prompt, y_sample,
            jnp.stack(hp_l).astype(state_rglru_h.dtype), jnp.stack(cp_l).astype(state_conv.dtype),
            jnp.stack(sp_l).astype(state_gla.dtype),
            jnp.stack(hs_l).astype(state_rglru_h.dtype), jnp.stack(cs_l).astype(state_conv.dtype),
            jnp.stack(ss_l).astype(state_gla.dtype))
```

```python
import functools

import jax
import jax.numpy as jnp
from jax import lax
from jax.experimental import pallas as pl
from jax.experimental.pallas import tpu as pltpu

F32 = jnp.float32
BF16 = jnp.bfloat16
HIGHEST = lax.Precision.HIGHEST

N_META = 16
CONV_W = 4
RG_C = 8.0
RNN_BLOCKS = 16
GLA_HEADS = 4
GLA_TAU = 16.0
GLA_CHUNK = 64
N_GROUPS = 8
TOPK_GROUPS = 4
TOP_K = 8
ROUTED_SCALE = 2.5
LN_EPS = 1e-5
RMS_EPS = 1e-6

LANES = 128
SUBLANES = 8
MXU_DIM = 256
VMEM_LIMIT_BYTES = 56 * 1024 * 1024

MOE_ROWS = 128
ISSUE_UNROLL = 4


def _params(*sem):
    return pltpu.CompilerParams(dimension_semantics=sem, vmem_limit_bytes=VMEM_LIMIT_BYTES)


def _sigmoid(x):
    return jax.nn.sigmoid(x)


def _softplus(z):
    return jnp.maximum(z, 0.0) + jnp.log1p(jnp.exp(-jnp.abs(z)))


def _layer_norm(z, g, b):
    mu = jnp.mean(z, axis=-1, keepdims=True)
    zc = z - mu
    var = jnp.mean(zc * zc, axis=-1, keepdims=True)
    return zc * lax.rsqrt(var + LN_EPS) * g + b


def _const_spec(shape):
    nd = len(shape)
    return pl.BlockSpec(shape, lambda *_: (0,) * nd)


def _resident_spec(shape):
    nd = len(shape)
    return pl.BlockSpec(shape, lambda *_: (0,) * nd, pipeline_mode=pl.Buffered(1))


def _inproj_kernel(x_ref, wa_ref, wb_ref, wm_ref, ua_ref, ub_ref, um_ref):
    xb = x_ref[...].astype(BF16)
    ua_ref[...] = jnp.dot(xb, wa_ref[...], preferred_element_type=F32)
    ub_ref[...] = jnp.dot(xb, wb_ref[...], preferred_element_type=F32)
    um_ref[...] = jnp.dot(xb, wm_ref[...], preferred_element_type=F32).astype(um_ref.dtype)


def _inproj(x2d, wa, wb, wm, tm, um_dtype=F32):
    m, d = x2d.shape
    na, nb, nm = wa.shape[1], wb.shape[1], wm.shape[1]
    return pl.pallas_call(
        _inproj_kernel,
        grid=(m // tm,),
        in_specs=[pl.BlockSpec((tm, d), lambda i: (i, 0)),
                  _resident_spec(wa.shape), _resident_spec(wb.shape), _resident_spec(wm.shape)],
        out_specs=[pl.BlockSpec((tm, na), lambda i: (i, 0)),
                   pl.BlockSpec((tm, nb), lambda i: (i, 0)),
                   pl.BlockSpec((tm, nm), lambda i: (i, 0))],
        out_shape=[jax.ShapeDtypeStruct((m, na), F32),
                   jax.ShapeDtypeStruct((m, nb), F32),
                   jax.ShapeDtypeStruct((m, nm), um_dtype)],
        compiler_params=_params("arbitrary"),
        name="inproj",
    )(x2d, wa, wb, wm)


def _rglru_kernel(ua_ref, cbuf_ref, h0_ref, cw_ref, cb_ref, wa_ref, ba_ref, wi_ref, bi_ref, lam_ref,
                  ya_ref, hlast_ref, cnew_ref, ext_scr, h_scr, *, bb, tb, l_last, nblk, d):
    j = pl.program_id(1)
    ntail = CONV_W - 1
    rows = bb * tb

    @pl.when(j == 0)
    def _():
        h_scr[...] = h0_ref[...]
        ext_scr[:, SUBLANES - ntail:SUBLANES, :] = cbuf_ref[...]

    xr = ua_ref[:, :, :d]
    gr = ua_ref[:, :, d:].reshape(rows, d)
    ext_scr[:, SUBLANES:, :] = xr
    xc = cb_ref[...] + cw_ref[ntail:ntail + 1, :] * xr
    for s in range(1, CONV_W):
        xc = xc + cw_ref[ntail - s:ntail - s + 1, :] * ext_scr[:, SUBLANES - s:SUBLANES - s + tb, :]
    xc = xc.reshape(rows, d)

    xcb = xc.astype(BF16)
    nt = d // MXU_DIM
    ra = jnp.concatenate(
        [jnp.dot(xcb[:, q * MXU_DIM:(q + 1) * MXU_DIM], wa_ref[q], preferred_element_type=F32)
         for q in range(nt)], axis=-1)
    ia = jnp.concatenate(
        [jnp.dot(xcb[:, q * MXU_DIM:(q + 1) * MXU_DIM], wi_ref[q], preferred_element_type=F32)
         for q in range(nt)], axis=-1)
    r = _sigmoid(ra + ba_ref[...])
    ig = _sigmoid(ia + bi_ref[...])
    log_a = (-RG_C) * r * _softplus(-lam_ref[...])
    a = jnp.exp(log_a)
    z = -jnp.tanh(log_a) * (a * a + 1.0)
    mult = jnp.where(z > 0.0, z * lax.rsqrt(z), 0.0)
    b = mult * (ig * xc)

    t_in = jnp.bitwise_and(lax.broadcasted_iota(jnp.int32, (rows, d), 0), SUBLANES - 1)
    s = 1
    while s < SUBLANES:
        a_sh = jnp.where(t_in >= s, pltpu.roll(a, s, 0), 1.0)
        b_sh = jnp.where(t_in >= s, pltpu.roll(b, s, 0), 0.0)
        b = b + a * b_sh
        a = a * a_sh
        s *= 2
    a = a.reshape(bb, tb, d)
    b = b.reshape(bb, tb, d)
    carry = h_scr[...]
    tiles = []
    for i in range(tb // SUBLANES):
        rs = slice(i * SUBLANES, (i + 1) * SUBLANES)
        hi = b[:, rs, :] + a[:, rs, :] * carry
        carry = hi[:, SUBLANES - 1:SUBLANES, :]
        tiles.append(hi)
    h = jnp.concatenate(tiles, axis=1)
    h_scr[...] = carry
    ya_ref[...] = (h * jax.nn.gelu(gr, approximate=True).reshape(bb, tb, d)).astype(ya_ref.dtype)

    @pl.when(j == nblk - 1)
    def _():
        hlast_ref[...] = h[:, l_last - 1:l_last, :]
        cnew_ref[...] = ext_scr[:, SUBLANES + l_last - ntail:SUBLANES + l_last, :]

    tail = ext_scr[:, tb + SUBLANES - ntail:tb + SUBLANES, :]
    ext_scr[:, SUBLANES - ntail:SUBLANES, :] = tail


def _rglru(ua, cbuf, h0, cw, cb, wa4, ba, wi4, bi, lam, *, bb, tb, l_real, y_dtype=F32):
    bsz, lp, n2 = ua.shape
    d = n2 // 2
    nblk = lp // tb
    l_last = l_real - (nblk - 1) * tb
    assert tb & (tb - 1) == 0 and bsz % bb == 0
    kern = functools.partial(_rglru_kernel, bb=bb, tb=tb, l_last=l_last, nblk=nblk, d=d)
    return pl.pallas_call(
        kern,
        grid=(bsz // bb, nblk),
        in_specs=[pl.BlockSpec((bb, tb, n2), lambda b, j: (b, j, 0)),
                  pl.BlockSpec((bb, CONV_W - 1, d), lambda b, j: (b, 0, 0)),
                  pl.BlockSpec((bb, 1, d), lambda b, j: (b, 0, 0)),
                  _const_spec(cw.shape), _const_spec(cb.shape),
                  _const_spec(wa4.shape), _const_spec(ba.shape),
                  _const_spec(wi4.shape), _const_spec(bi.shape), _const_spec(lam.shape)],
        out_specs=[pl.BlockSpec((bb, tb, d), lambda b, j: (b, j, 0)),
                   pl.BlockSpec((bb, 1, d), lambda b, j: (b, 0, 0)),
                   pl.BlockSpec((bb, CONV_W - 1, d), lambda b, j: (b, 0, 0))],
        out_shape=[jax.ShapeDtypeStruct((bsz, lp, d), y_dtype),
                   jax.ShapeDtypeStruct((bsz, 1, d), F32),
                   jax.ShapeDtypeStruct((bsz, CONV_W - 1, d), F32)],
        scratch_shapes=[pltpu.VMEM((bb, tb + SUBLANES, d), F32), pltpu.VMEM((bb, 1, d), F32)],
        compiler_params=_params("arbitrary", "arbitrary"),
        name="rglru",
    )(ua, cbuf, h0, cw, cb, wa4, ba, wi4, bi, lam)


def _gla_kernel(ub_ref, s0_ref, wf_ref, bf_ref, g_ref, yb_ref, sout_ref, s_scr,
                *, bb, tb, chunk, l_real, nblk, dk, dv):
    j = pl.program_id(1)
    nh = GLA_HEADS
    qk = nh * dk
    vw = nh * dv

    @pl.when(j == 0)
    def _():
        s_scr[...] = s0_ref[...]

    off_k, off_v, off_g, off_f = qk, 2 * qk, 2 * qk + vw, 2 * qk + 2 * vw
    merge = bb > 1 and nblk == 1 and tb == chunk
    nrow = bb * tb if merge else tb
    nc = nrow // chunk
    shift = chunk.bit_length() - 1
    ri = lax.broadcasted_iota(jnp.int32, (nrow, nrow), 0)
    ci = lax.broadcasted_iota(jnp.int32, (nrow, nrow), 1)
    keep = (lax.shift_right_logical(ri, shift) == lax.shift_right_logical(ci, shift)) & (ri >= ci)
    tril = keep.astype(F32)
    row_chunk = lax.shift_right_logical(lax.broadcasted_iota(jnp.int32, (nrow, dk), 0), shift)
    mid = chunk // 2
    scale = dk ** -0.5

    def by_chunk(a):
        if nc == 1:
            return a
        return jnp.concatenate([jnp.where(row_chunk == c, a, 0.0) for c in range(nc)], axis=1)

    def per_chunk_row(a, r):
        return jnp.concatenate([jnp.broadcast_to(a[c * chunk + r:c * chunk + r + 1, :], (chunk, a.shape[1]))
                                for c in range(nc)], axis=0)

    for bi in range(1 if merge else bb):
        def cols(c0, c1, bi=bi):
            if merge:
                return ub_ref[:, :, c0:c1].reshape(nrow, c1 - c0)
            return ub_ref[bi, :, c0:c1]

        fl = cols(off_f, off_f + LANES).astype(BF16)
        logit = jnp.dot(fl, wf_ref[...], preferred_element_type=F32) + bf_ref[...]
        log_f = (jnp.minimum(logit, 0.0) - jnp.log1p(jnp.exp(-jnp.abs(logit)))) * (1.0 / GLA_TAU)
        if l_real < tb * nblk:
            rows = lax.broadcasted_iota(jnp.int32, log_f.shape, 0)
            rows = (jnp.bitwise_and(rows, tb - 1) if merge else rows) + j * tb
            log_f = jnp.where(rows < l_real, log_f, 0.0)
        bcum = jnp.dot(tril, log_f, precision=HIGHEST, preferred_element_type=F32)
        ref = per_chunk_row(bcum, mid)
        last = per_chunk_row(bcum, chunk - 1)
        e_inter = jnp.exp(bcum)
        e_q = jnp.exp(bcum - ref)
        e_k = jnp.exp(ref - bcum)
        e_dec = jnp.exp(last - bcum)

        for h in range(nh):
            hk = slice(h * dk, (h + 1) * dk)
            q = cols(h * dk, (h + 1) * dk) * scale
            k = cols(off_k + h * dk, off_k + (h + 1) * dk)
            vb = cols(off_v + h * dv, off_v + (h + 1) * dv).astype(BF16)
            go = cols(off_g + h * dv, off_g + (h + 1) * dv)
            att = lax.dot_general((q * e_q[:, hk]).astype(BF16), (k * e_k[:, hk]).astype(BF16),
                                  (((1,), (1,)), ((), ())), preferred_element_type=F32)
            att = jnp.where(keep, att, 0.0)
            o = jnp.dot(att.astype(BF16), vb, preferred_element_type=F32)
            kd = by_chunk(k * e_dec[:, hk])
            kv = lax.dot_general(kd.astype(BF16), vb, (((0,), (0,)), ((), ())),
                                 preferred_element_type=F32)
            states = [s_scr[c, h] for c in range(nc)] if merge else [s_scr[bi, h]]
            for c in range(nc):
                drow = jnp.exp(bcum[c * chunk + chunk - 1:c * chunk + chunk, hk])
                dec = jnp.transpose(jnp.broadcast_to(drow, (dk, dk)))
                dec = jnp.concatenate([dec] * (dv // dk), axis=1)
                s_next = dec * states[c] + kv[c * dk:(c + 1) * dk]
                if merge:
                    s_scr[c, h] = s_next
                else:
                    states.append(s_next)
            if not merge:
                s_scr[bi, h] = states[nc]
            qi = by_chunk(q * e_inter[:, hk])
            s_in = jnp.concatenate(states[:nc], axis=0).astype(BF16)
            o = o + jnp.dot(qi.astype(BF16), s_in, preferred_element_type=F32)
            o = o * lax.rsqrt(jnp.mean(o * o, axis=-1, keepdims=True) + RMS_EPS) * g_ref[...]
            y = (o * (go * _sigmoid(go))).astype(yb_ref.dtype)
            if merge:
                yb_ref[:, :, h * dv:(h + 1) * dv] = y.reshape(bb, tb, dv)
            else:
                yb_ref[bi, :, h * dv:(h + 1) * dv] = y

    @pl.when(j == nblk - 1)
    def _():
        sout_ref[...] = s_scr[...]


def _gla(ub, s0, wf, bf, g, *, bb, tb, chunk, l_real, y_dtype=F32):
    bsz, lp, nb = ub.shape
    _, nh, dk, dv = s0.shape
    nblk = lp // tb
    assert bsz % bb == 0 and dv % dk == 0 and chunk & (chunk - 1) == 0 and tb % chunk == 0
    kern = functools.partial(_gla_kernel, bb=bb, tb=tb, chunk=chunk, l_real=l_real, nblk=nblk, dk=dk, dv=dv)
    return pl.pallas_call(
        kern,
        grid=(bsz // bb, nblk),
        in_specs=[pl.BlockSpec((bb, tb, nb), lambda b, j: (b, j, 0)),
                  pl.BlockSpec((bb, nh, dk, dv), lambda b, j: (b, 0, 0, 0)),
                  _const_spec(wf.shape), _const_spec(bf.shape), _const_spec(g.shape)],
        out_specs=[pl.BlockSpec((bb, tb, nh * dv), lambda b, j: (b, j, 0)),
                   pl.BlockSpec((bb, nh, dk, dv), lambda b, j: (b, 0, 0, 0))],
        out_shape=[jax.ShapeDtypeStruct((bsz, lp, nh * dv), y_dtype),
                   jax.ShapeDtypeStruct((bsz, nh, dk, dv), F32)],
        scratch_shapes=[pltpu.VMEM((bb, nh, dk, dv), F32)],
        compiler_params=_params("arbitrary", "arbitrary"),
        name="gla",
    )(ub, s0, wf, bf, g)


def _mix_kernel(ya_ref, yb_ref, um_ref, x_ref, wo_ref, g_ref, b_ref, *rest, alpha, d):
    o_ref = rest[-1]
    um = um_ref[...].astype(F32)
    merged = (_sigmoid(um[:, :d]) * ya_ref[...].astype(F32)
              + _sigmoid(um[:, d:]) * yb_ref[...].astype(F32))
    mix = jnp.dot(merged.astype(BF16), wo_ref[...], preferred_element_type=F32)
    o_ref[...] = _layer_norm(alpha * x_ref[...] + mix, g_ref[...], b_ref[...])


def _mix(ya, yb, um, x, wo, g, b, *, alpha, tm, total_rows, row_offset=0, into=None):
    m, d = x.shape
    assert m % tm == 0 and row_offset % tm == 0
    row = lambda w: pl.BlockSpec((tm, w), lambda i: (i, 0))
    in_specs = [row(d), row(d), row(2 * d), row(d),
                _const_spec(wo.shape), _const_spec(g.shape), _const_spec(b.shape)]
    args = (ya, yb, um, x, wo, g, b)
    aliases = {}
    if into is not None:
        in_specs.append(pl.BlockSpec(memory_space=pl.ANY))
        aliases = {len(args): 0}
        args = args + (into,)
    return pl.pallas_call(
        functools.partial(_mix_kernel, alpha=alpha, d=d),
        grid=(m // tm,),
        in_specs=in_specs,
        out_specs=pl.BlockSpec((tm, d), lambda i: (i + row_offset // tm, 0)),
        out_shape=jax.ShapeDtypeStruct((total_rows, d), F32),
        input_output_aliases=aliases,
        compiler_params=_params("arbitrary"),
        name="mix_ln1",
    )(*args)


def _router_t_kernel(x_ref, whi_ref, wlo_ref, bias_ref, eidx_ref, gate_ref, rank_ref, cnt_ref, run_scr,
                     *, tm, ne):
    i = pl.program_id(0)

    @pl.when(i == 0)
    def _():
        run_scr[...] = jnp.zeros_like(run_scr)

    x = x_ref[...]
    x_hi = x.astype(BF16)
    x_lo = (x - x_hi.astype(F32)).astype(BF16)
    nt = (((1,), (1,)), ((), ()))
    logits = (lax.dot_general(whi_ref[...], x_hi, nt, preferred_element_type=F32)
              + lax.dot_general(whi_ref[...], x_lo, nt, preferred_element_type=F32)
              + lax.dot_general(wlo_ref[...], x_hi, nt, preferred_element_type=F32))
    s = _sigmoid(logits)
    sel = s + bias_ref[...]
    row = lax.broadcasted_iota(jnp.int32, (ne, tm), 0).astype(F32)
    gsz = ne // N_GROUPS
    neg = -jnp.inf

    def colmax(a):
        return jnp.max(a, axis=0, keepdims=True)

    def colmin(a):
        return jnp.min(a, axis=0, keepdims=True)

    gscores = []
    row_in_group = lax.broadcasted_iota(jnp.int32, (gsz, tm), 0).astype(F32)
    for g in range(N_GROUPS):
        v = sel[g * gsz:(g + 1) * gsz]
        rg = row_in_group + float(g * gsz)
        m1 = colmax(v)
        i1 = colmin(jnp.where(v == m1, rg, float(ne)))
        gscores.append(m1 + colmax(jnp.where(rg == i1, neg, v)))
    kept = []
    for g in range(N_GROUPS):
        ahead = jnp.zeros((1, tm), F32)
        for g2 in range(N_GROUPS):
            if g2 == g:
                continue
            beats = (gscores[g2] >= gscores[g]) if g2 < g else (gscores[g2] > gscores[g])
            ahead = ahead + beats.astype(F32)
        keep = jnp.broadcast_to(ahead < float(TOPK_GROUPS), (gsz, tm))
        kept.append(jnp.where(keep, sel[g * gsz:(g + 1) * gsz], neg))
    cur = jnp.concatenate(kept, axis=0)

    picks, eidx, wsel = [], [], []
    onehot_sum = jnp.zeros((ne, tm), F32)
    for k in range(TOP_K):
        ik = colmin(jnp.where(cur == colmax(cur), row, float(ne)))
        hit = row == ik
        wsel.append(jnp.sum(jnp.where(hit, s, 0.0), axis=0, keepdims=True))
        cur = jnp.where(hit, neg, cur)
        onehot_sum = onehot_sum + hit.astype(F32)
        picks.append(hit)
        eidx.append(ik)
    wsel = jnp.concatenate(wsel, axis=0)
    gate_ref[...] = wsel / jnp.sum(wsel, axis=0, keepdims=True) * ROUTED_SCALE
    eidx_ref[...] = jnp.concatenate(eidx, axis=0).astype(jnp.int32)

    ri = lax.broadcasted_iota(jnp.int32, (tm, tm), 0)
    ci = lax.broadcasted_iota(jnp.int32, (tm, tm), 1)
    earlier = (ri < ci).astype(BF16)
    ohb = onehot_sum.astype(BF16)
    counts = jnp.dot(ohb, earlier, preferred_element_type=F32) + jnp.concatenate([run_scr[...]] * (tm // LANES), axis=1)
    rank = [jnp.sum(jnp.where(picks[k], counts, 0.0), axis=0, keepdims=True) for k in range(TOP_K)]
    rank_ref[...] = jnp.concatenate(rank, axis=0).astype(jnp.int32)
    run_scr[...] = run_scr[...] + jnp.dot(ohb, jnp.ones((tm, LANES), BF16), preferred_element_type=F32)
    cnt_ref[...] = run_scr[...].astype(jnp.int32)


def _router_t(x, w_hi, w_lo, bias_b, *, tm):
    t, d = x.shape
    ne = w_hi.shape[0]
    out_col = lambda: pl.BlockSpec((TOP_K, tm), lambda i: (0, i))
    return pl.pallas_call(
        functools.partial(_router_t_kernel, tm=tm, ne=ne),
        grid=(t // tm,),
        in_specs=[pl.BlockSpec((tm, d), lambda i: (i, 0)), _const_spec(w_hi.shape), _const_spec(w_lo.shape),
                  _const_spec(bias_b.shape)],
        out_specs=[out_col(), out_col(), out_col(), _const_spec((ne, LANES))],
        out_shape=[jax.ShapeDtypeStruct((TOP_K, t), jnp.int32),
                   jax.ShapeDtypeStruct((TOP_K, t), F32),
                   jax.ShapeDtypeStruct((TOP_K, t), jnp.int32),
                   jax.ShapeDtypeStruct((ne, LANES), jnp.int32)],
        scratch_shapes=[pltpu.VMEM((ne, LANES), F32)],
        compiler_params=_params("arbitrary"),
        name="router",
    )(x, w_hi, w_lo, bias_b)


def _dest_t_kernel(eidx_ref, rank_ref, start_ref, dest_ref, *, tm, ne):
    row = lax.broadcasted_iota(jnp.int32, (ne, tm), 0)
    starts = start_ref[...].astype(F32)
    eidx = eidx_ref[...]
    dest = [jnp.sum(jnp.where(row == eidx[k:k + 1, :], starts, 0.0), axis=0, keepdims=True) for k in range(TOP_K)]
    dest_ref[...] = jnp.concatenate(dest, axis=0).astype(jnp.int32) + rank_ref[...]


def _dest_t(eidx, rank, starts_b, *, tm):
    t = eidx.shape[1]
    ne = starts_b.shape[0]
    col = lambda: pl.BlockSpec((TOP_K, tm), lambda i: (0, i))
    return pl.pallas_call(
        functools.partial(_dest_t_kernel, tm=tm, ne=ne),
        grid=(t // tm,),
        in_specs=[col(), col(), _const_spec(starts_b.shape)],
        out_specs=col(),
        out_shape=jax.ShapeDtypeStruct((TOP_K, t), jnp.int32),
        compiler_params=_params("arbitrary"),
        name="dest",
    )(eidx, rank, starts_b)


TOKEN_DTYPE = BF16


def _to_token_tiles(rows):
    chunks = jnp.stack([rows[:, c * LANES:(c + 1) * LANES] for c in range(SUBLANES)], axis=0)
    return jnp.swapaxes(chunks, 0, 1).astype(TOKEN_DTYPE)


def _chunks_of_token_tiles(tiles):
    chunks = jnp.swapaxes(tiles.astype(F32), 0, 1)
    return [chunks[c] for c in range(SUBLANES)]


def _token_copy(src_ref, src_row, dst_ref, dst_row, sem):
    return pltpu.make_async_copy(src_ref.at[pl.ds(src_row, 1)], dst_ref.at[pl.ds(dst_row, 1)], sem)


def _dispatch_kernel(dest_ref, x_ref, xs_ref, xt, sem, *, tt):
    xt[...] = _to_token_tiles(x_ref[...])

    def start(tp, c):
        for u in range(ISSUE_UNROLL):
            t = tp * ISSUE_UNROLL + u
            for k in range(TOP_K):
                _token_copy(xt, t, xs_ref, dest_ref[t * TOP_K + k], sem).start(priority=k % 2)
        return c

    lax.fori_loop(0, tt // ISSUE_UNROLL, start, 0)
    for _ in range(TOP_K):
        pltpu.make_async_copy(xt, xs_ref.at[pl.ds(0, tt)], sem).wait()


def _dispatch(dest_flat, x, n_slots, *, tt):
    t, d = x.shape
    assert d == SUBLANES * LANES
    return pl.pallas_call(
        functools.partial(_dispatch_kernel, tt=tt),
        grid=(t // tt,),
        in_specs=[pl.BlockSpec((tt * TOP_K,), lambda i: (i,), memory_space=pltpu.SMEM),
                  pl.BlockSpec((tt, d), lambda i: (i, 0))],
        out_specs=pl.BlockSpec(memory_space=pl.ANY),
        out_shape=jax.ShapeDtypeStruct((n_slots, SUBLANES, LANES), TOKEN_DTYPE),
        scratch_shapes=[pltpu.VMEM((tt, SUBLANES, LANES), TOKEN_DTYPE), pltpu.SemaphoreType.DMA(())],
        compiler_params=_params("arbitrary"),
        name="dispatch",
    )(dest_flat, x)


EXPERT_SLOTS = 16
EXPERT_TAILS = ((1, (1,)), (2, (2,)), (3, (2, 1)), (5, (2, 2, 1)), (6, (2, 2, 2)), (7, (2, 2, 2, 1)))
EXPERT_STEP_BLOCKS = max(r for r, _ in EXPERT_TAILS)
EXPERT_AHEAD = 8


def _expert_kernel(base_ref, nblk_ref, cnt_ref, nu_ref, xs_ref, wg_ref, wu_ref, wd_ref, ys_ref,
                   xbuf, ybuf, wgb, wub, wdb, sem_in, sem_out):
    e = pl.program_id(0)
    n_used = nu_ref[0]
    nb = nblk_ref[e]
    base = base_ref[e]
    cnt = cnt_ref[e]
    ahead = EXPERT_AHEAD

    def x_copy(blk, slot):
        return pltpu.make_async_copy(xs_ref.at[pl.ds(blk * MOE_ROWS, MOE_ROWS)], xbuf.at[slot], sem_in.at[slot])

    def y_copy(blk, slot):
        return pltpu.make_async_copy(ybuf.at[slot], ys_ref.at[pl.ds(blk * MOE_ROWS, MOE_ROWS)], sem_out.at[slot])

    @pl.when(e == 0)
    def _():
        for first in range(ahead):
            @pl.when(first < n_used)
            def _():
                x_copy(first, first).start()

    @pl.when(nb > 0)
    def _():
        wgb[...] = wg_ref[0].astype(BF16)
        wub[...] = wu_ref[0].astype(BF16)
        wdb[...] = wd_ref[0].astype(BF16)

    def step(jb, chains):
        nblocks = sum(chains)
        blks = [base + jb + i for i in range(nblocks)]
        slots = [jnp.bitwise_and(b, EXPERT_SLOTS - 1) for b in blks]
        for b, s in zip(blks, slots):
            x_copy(b, s).wait()
        for b in blks:
            @pl.when(b + ahead < n_used)
            def _():
                x_copy(b + ahead, jnp.bitwise_and(b + ahead, EXPERT_SLOTS - 1)).start()
        for b, s in zip(blks, slots):
            @pl.when(b >= EXPERT_SLOTS)
            def _():
                y_copy(b - EXPERT_SLOTS, s).wait()

        first = 0
        for n in chains:
            x = jnp.concatenate([jnp.concatenate(_chunks_of_token_tiles(xbuf[s]), axis=1)
                                 for s in slots[first:first + n]], axis=0)
            rows = lax.broadcasted_iota(jnp.int32, x.shape, 0)
            xb = jnp.where(rows < cnt - (jb + first) * MOE_ROWS, x, 0.0).astype(BF16)
            g = jnp.dot(xb, wgb[...], preferred_element_type=F32)
            u = jnp.dot(xb, wub[...], preferred_element_type=F32)
            hmid = (g * _sigmoid(g)) * u
            y = jnp.dot(hmid.astype(BF16), wdb[...], preferred_element_type=F32)
            for i in range(n):
                b, s = blks[first + i], slots[first + i]
                ybuf[s] = _to_token_tiles(y[i * MOE_ROWS:(i + 1) * MOE_ROWS])
                y_copy(b, s).start()
            first += n

    def quad(jq, c):
        step(4 * jq, (2, 2))
        return c

    rem = jnp.bitwise_and(nb, 3)
    nfull = lax.shift_right_logical(nb, 2)
    nloop = jnp.where(rem == 0, nfull, jnp.maximum(nfull - 1, 0))
    lax.fori_loop(0, nloop, quad, 0)
    tail = nb - 4 * nloop
    for r, chains in EXPERT_TAILS:
        @pl.when(tail == r)
        def _():
            step(4 * nloop, chains)

    @pl.when(e == pl.num_programs(0) - 1)
    def _():
        for back in range(EXPERT_SLOTS):
            @pl.when(n_used > back)
            def _():
                last = n_used - 1 - back
                y_copy(last, jnp.bitwise_and(last, EXPERT_SLOTS - 1)).wait()


def _experts(base, nblk, cnt, n_used, xs, wg, wu, wd):
    p = xs.shape[0]
    ne, d, de = wg.shape
    assert EXPERT_SLOTS & (EXPERT_SLOTS - 1) == 0 and EXPERT_AHEAD <= EXPERT_SLOTS - EXPERT_STEP_BLOCKS
    assert d == SUBLANES * LANES
    ring = pltpu.VMEM((EXPERT_SLOTS, MOE_ROWS, SUBLANES, LANES), TOKEN_DTYPE)
    grid_spec = pltpu.PrefetchScalarGridSpec(
        num_scalar_prefetch=4,
        grid=(ne,),
        in_specs=[pl.BlockSpec(memory_space=pl.ANY),
                  pl.BlockSpec((1, d, de), lambda e, *_: (e, 0, 0)),
                  pl.BlockSpec((1, d, de), lambda e, *_: (e, 0, 0)),
                  pl.BlockSpec((1, de, d), lambda e, *_: (e, 0, 0))],
        out_specs=pl.BlockSpec(memory_space=pl.ANY),
        scratch_shapes=[ring, ring,
                        pltpu.VMEM((d, de), BF16), pltpu.VMEM((d, de), BF16), pltpu.VMEM((de, d), BF16),
                        pltpu.SemaphoreType.DMA((EXPERT_SLOTS,)), pltpu.SemaphoreType.DMA((EXPERT_SLOTS,))],
    )
    return pl.pallas_call(
        _expert_kernel,
        grid_spec=grid_spec,
        out_shape=jax.ShapeDtypeStruct((p, SUBLANES, LANES), TOKEN_DTYPE),
        compiler_params=_params("arbitrary"),
        name="experts",
    )(base, nblk, cnt, n_used, xs, wg, wu, wd)


def _combine_kernel(dest_ref, dest_next_ref, x_ref, gate_ref, ys_ref, wsg_ref, wsu_ref, wsd_ref, g_ref, b_ref,
                    o_head_ref, o_tail_ref, ybuf, sem, *, tt, alpha, head_steps):
    i = pl.program_id(0)
    slot = jnp.bitwise_and(i, 1)

    def gather(dref, s):
        def start(tp, c):
            for u in range(ISSUE_UNROLL):
                t = tp * ISSUE_UNROLL + u
                for k in range(TOP_K):
                    _token_copy(ys_ref, dref[t * TOP_K + k], ybuf.at[s], k * tt + t, sem.at[s]).start(priority=k % 2)
            return c
        lax.fori_loop(0, tt // ISSUE_UNROLL, start, 0)

    @pl.when(i == 0)
    def _():
        gather(dest_ref, 0)

    def prefetch_group(k):
        nxt = 1 - slot
        for t in range(tt):
            _token_copy(ys_ref, dest_next_ref[t * TOP_K + k], ybuf.at[nxt], k * tt + t, sem.at[nxt]).start(priority=t % 2)

    x = x_ref[...]
    xb = x.astype(BF16)
    sg = jnp.dot(xb, wsg_ref[...], preferred_element_type=F32)
    su = jnp.dot(xb, wsu_ref[...], preferred_element_type=F32)
    acc = jnp.dot(((sg * _sigmoid(sg)) * su).astype(BF16), wsd_ref[...], preferred_element_type=F32)
    pltpu.make_async_copy(ys_ref.at[pl.ds(0, tt * TOP_K)], ybuf.at[slot], sem.at[slot]).wait()
    gate = gate_ref[...]
    routed = None
    for k in range(TOP_K):
        prefetch_group(k)
        gk = jnp.broadcast_to(gate[:, k:k + 1], (tt, LANES))
        part = [gk * ch for ch in _chunks_of_token_tiles(ybuf[slot, k * tt:(k + 1) * tt])]
        routed = part if routed is None else [r + p for r, p in zip(routed, part)]
    acc = acc + jnp.concatenate(routed, axis=1)
    out = _layer_norm(alpha * x + acc, g_ref[...], b_ref[...])

    @pl.when(i < head_steps)
    def _():
        o_head_ref[...] = out

    @pl.when(i >= head_steps)
    def _():
        o_tail_ref[...] = out

    @pl.when(i == pl.num_programs(0) - 1)
    def _():
        pltpu.make_async_copy(ys_ref.at[pl.ds(0, tt * TOP_K)], ybuf.at[1 - slot], sem.at[1 - slot]).wait()


def _combine(dest_flat, x, gate, ys, wsg, wsu, wsd, g, b, *, tt, alpha, head_rows):
    t, d = x.shape
    nsteps = t // tt
    head_steps = head_rows // tt
    assert head_rows % tt == 0 and 0 < head_steps < nsteps
    return pl.pallas_call(
        functools.partial(_combine_kernel, tt=tt, alpha=alpha, head_steps=head_steps),
        grid=(nsteps,),
        in_specs=[pl.BlockSpec((tt * TOP_K,), lambda i: (i,), memory_space=pltpu.SMEM),
                  pl.BlockSpec((tt * TOP_K,), lambda i: (jnp.minimum(i + 1, nsteps - 1),),
                               memory_space=pltpu.SMEM),
                  pl.BlockSpec((tt, d), lambda i: (i, 0)),
                  pl.BlockSpec((tt, TOP_K), lambda i: (i, 0)),
                  pl.BlockSpec(memory_space=pl.ANY),
                  _const_spec(wsg.shape), _const_spec(wsu.shape), _const_spec(wsd.shape),
                  _const_spec(g.shape), _const_spec(b.shape)],
        out_specs=[pl.BlockSpec((tt, d), lambda i: (jnp.minimum(i, head_steps - 1), 0)),
                   pl.BlockSpec((tt, d), lambda i: (jnp.maximum(i - head_steps, 0), 0))],
        out_shape=[jax.ShapeDtypeStruct((head_rows, d), F32), jax.ShapeDtypeStruct((t - head_rows, d), F32)],
        scratch_shapes=[pltpu.VMEM((2, tt * TOP_K, SUBLANES, LANES), TOKEN_DTYPE), pltpu.SemaphoreType.DMA((2,))],
        compiler_params=_params("arbitrary"),
        name="combine_ln2",
    )(dest_flat, dest_flat, x, gate, ys, wsg, wsu, wsd, g, b)


def _block_diag_tiles(w):
    n, s, _ = w.shape
    per = MXU_DIM // s
    w = w.reshape(n // per, per, s, s)
    eye = jnp.eye(per, dtype=w.dtype)
    return jnp.einsum('tpij,pq->tpiqj', w, eye).reshape(n // per, MXU_DIM, MXU_DIM)


def _row(v):
    return v.reshape(1, -1)


def _moe(x1, w_router, router_bias, w_gate, w_up, w_down, ws_gate, ws_up, ws_down, ln_g, ln_b, alpha, head_rows):
    t, d = x1.shape
    ne = w_router.shape[1]
    tm = 256
    w_t = w_router.T
    w_hi = w_t.astype(BF16)
    w_lo = (w_t - w_hi.astype(F32)).astype(BF16)
    bias_b = jnp.broadcast_to(router_bias.reshape(ne, 1), (ne, tm))
    eidx_t, gate_t, rank_t, counts = _router_t(x1, w_hi, w_lo, bias_b, tm=tm)
    counts = counts[:, 0]
    nblk = (counts + MOE_ROWS - 1) // MOE_ROWS
    blk_end = jnp.cumsum(nblk)
    blk_base = (blk_end - nblk).astype(jnp.int32)
    n_used = blk_end[-1:].astype(jnp.int32)
    n_slots = ((t * TOP_K) // MOE_ROWS + ne) * MOE_ROWS

    starts_b = jnp.broadcast_to((blk_base * MOE_ROWS).reshape(ne, 1), (ne, tm))
    dest_flat = _dest_t(eidx_t, rank_t, starts_b, tm=tm).T.reshape(-1)
    xs = _dispatch(dest_flat, x1, n_slots, tt=512)
    ys = _experts(blk_base, nblk.astype(jnp.int32), counts, n_used, xs, w_gate, w_up, w_down)
    return _combine(dest_flat, x1, gate_t.T, ys, ws_gate.astype(BF16), ws_up.astype(BF16),
                    ws_down.astype(BF16), _row(ln_g), _row(ln_b), tt=256, alpha=alpha, head_rows=head_rows)


def _pad_time(a, lp):
    return jnp.pad(a, ((0, 0), (0, lp - a.shape[1]), (0, 0)))


def kernel(x_prompt, x_sample, state_rglru_h, state_conv, state_gla, meta_tokens, w_in, conv_w, conv_b,
           rg_wa, rg_ba, rg_wi, rg_bi, rg_lambda, gla_wf2, gla_bf, gla_norm_g, w_out, ln1_g, ln1_b,
           w_router, router_bias, w_gate, w_up, w_down, ws_gate, ws_up, ws_down, ln2_g, ln2_b):
    bp, lp, d = x_prompt.shape
    bs, ls, _ = x_sample.shape
    depth = w_in.shape[0]
    nh, dk, dv = state_gla.shape[2:]
    qk, vw = nh * dk, nh * dv
    lowrank = gla_wf2.shape[1]
    alpha = (2.0 * depth) ** 0.25
    ls_pad = SUBLANES
    assert depth == 1 and ls <= ls_pad and lp % 256 == 0 and N_META % SUBLANES == 0

    xp = x_prompt.reshape(bp * lp, d)
    xs = x_sample.reshape(bs * ls, d)
    xm = meta_tokens.astype(F32)

    hp_l, cp_l, sp_l, hs_l, cs_l, ss_l = [], [], [], [], [], []
    for l in range(depth):
        o1, o2 = 2 * d, 2 * d + 2 * qk + 2 * vw
        wa = w_in[l][:, :o1].astype(BF16)
        wb = jnp.pad(w_in[l][:, o1:o2 + lowrank], ((0, 0), (0, LANES - lowrank))).astype(BF16)
        wm = w_in[l][:, o2 + lowrank:].astype(BF16)
        wf = jnp.pad(gla_wf2[l], ((0, LANES - lowrank), (0, 0))).astype(BF16)
        wa4 = _block_diag_tiles(rg_wa[l]).astype(BF16)
        wi4 = _block_diag_tiles(rg_wi[l]).astype(BF16)
        mixer_w = (conv_w[l], _row(conv_b[l]), wa4, _row(rg_ba[l]), wi4, _row(rg_bi[l]), _row(rg_lambda[l]))
        gla_w = (wf, _row(gla_bf[l]), _row(gla_norm_g[l]))

        ua, ub, _ = _inproj(xm, wa, wb, wm, tm=N_META)
        _, h_m, c_m = _rglru(ua[None], jnp.zeros((1, CONV_W - 1, d), F32), jnp.zeros((1, 1, d), F32),
                             *mixer_w, bb=1, tb=N_META, l_real=N_META)
        _, s_m = _gla(ub[None], jnp.zeros((1, nh, dk, dv), F32), *gla_w,
                      bb=1, tb=N_META, chunk=N_META, l_real=N_META)

        ua, ub, um = _inproj(xp, wa, wb, wm, tm=512, um_dtype=BF16)
        ya, h_p, c_p = _rglru(ua.reshape(bp, lp, -1), jnp.broadcast_to(c_m, (bp,) + c_m.shape[1:]),
                              jnp.broadcast_to(h_m, (bp,) + h_m.shape[1:]), *mixer_w,
                              bb=1, tb=512, l_real=lp, y_dtype=BF16)
        yb, s_p = _gla(ub.reshape(bp, lp, -1), jnp.broadcast_to(s_m, (bp,) + s_m.shape[1:]), *gla_w,
                       bb=1, tb=256, chunk=GLA_CHUNK, l_real=lp, y_dtype=BF16)
        n_tok = bp * lp + bs * ls
        x1 = _mix(ya.reshape(bp * lp, d), yb.reshape(bp * lp, d), um, xp, w_out[l].astype(BF16),
                  _row(ln1_g[l]), _row(ln1_b[l]), alpha=alpha, tm=1024, total_rows=n_tok)

        xs_pad = _pad_time(xs.reshape(bs, ls, d), ls_pad).reshape(bs * ls_pad, d)
        ua, ub, um = _inproj(xs_pad, wa, wb, wm, tm=256)
        ya, h_s, c_s = _rglru(ua.reshape(bs, ls_pad, -1), state_conv[l].astype(F32),
                              state_rglru_h[l].astype(F32)[:, None, :], *mixer_w,
                              bb=32, tb=ls_pad, l_real=ls)
        yb, s_s = _gla(ub.reshape(bs, ls_pad, -1), state_gla[l].astype(F32), *gla_w,
                       bb=8, tb=ls_pad, chunk=ls_pad, l_real=ls)
        unpad = lambda a: a.reshape(bs, ls_pad, -1)[:, :ls].reshape(bs * ls, -1)
        x1 = _mix(unpad(ya), unpad(yb), unpad(um), xs, w_out[l].astype(BF16),
                  _row(ln1_g[l]), _row(ln1_b[l]), alpha=alpha, tm=512, total_rows=n_tok,
                  row_offset=bp * lp, into=x1)

        xp, xs = _moe(x1, w_router[l], router_bias[l], w_gate[l], w_up[l], w_down[l],
                      ws_gate[l], ws_up[l], ws_down[l], ln2_g[l], ln2_b[l], alpha, bp * lp)

        hp_l.append(h_p[:, 0]); cp_l.append(c_p); sp_l.append(s_p)
        hs_l.append(h_s[:, 0]); cs_l.append(c_s); ss_l.append(s_s)

    y_prompt = xp.reshape(bp, lp, d)
    y_sample = xs.reshape(bs, ls, d)
    return (y_prompt, y_sample,
            jnp.stack(hp_l).astype(state_rglru_h.dtype), jnp.stack(cp_l).astype(state_conv.dtype),
            jnp.stack(sp_l).astype(state_gla.dtype),
            jnp.stack(hs_l).astype(state_rglru_h.dtype), jnp.stack(cs_l).astype(state_conv.dtype),
            jnp.stack(ss_l).astype(state_gla.dtype))
```

```python
import functools

import jax
import jax.numpy as jnp
from jax import lax
from jax.experimental import pallas as pl
from jax.experimental.pallas import tpu as pltpu

F32 = jnp.float32
BF16 = jnp.bfloat16
HIGHEST = lax.Precision.HIGHEST

N_META = 16
CONV_W = 4
RG_C = 8.0
RNN_BLOCKS = 16
GLA_HEADS = 4
GLA_TAU = 16.0
GLA_CHUNK = 64
N_GROUPS = 8
TOPK_GROUPS = 4
TOP_K = 8
ROUTED_SCALE = 2.5
LN_EPS = 1e-5
RMS_EPS = 1e-6

LANES = 128
SUBLANES = 8
MXU_DIM = 256
VMEM_LIMIT_BYTES = 56 * 1024 * 1024

MOE_ROWS = 128
ISSUE_UNROLL = 4


def _params(*sem):
    return pltpu.CompilerParams(dimension_semantics=sem, vmem_limit_bytes=VMEM_LIMIT_BYTES)


def _sigmoid(x):
    return jax.nn.sigmoid(x)


def _softplus(z):
    return jnp.maximum(z, 0.0) + jnp.log1p(jnp.exp(-jnp.abs(z)))


def _layer_norm(z, g, b):
    mu = jnp.mean(z, axis=-1, keepdims=True)
    zc = z - mu
    var = jnp.mean(zc * zc, axis=-1, keepdims=True)
    return zc * lax.rsqrt(var + LN_EPS) * g + b


def _const_spec(shape):
    nd = len(shape)
    return pl.BlockSpec(shape, lambda *_: (0,) * nd)


def _resident_spec(shape):
    nd = len(shape)
    return pl.BlockSpec(shape, lambda *_: (0,) * nd, pipeline_mode=pl.Buffered(1))


def _inproj_kernel(x_ref, wa_ref, wb_ref, wm_ref, ua_ref, ub_ref, um_ref):
    xb = x_ref[...].astype(BF16)
    ua_ref[...] = jnp.dot(xb, wa_ref[...], preferred_element_type=F32)
    ub_ref[...] = jnp.dot(xb, wb_ref[...], preferred_element_type=F32)
    um_ref[...] = jnp.dot(xb, wm_ref[...], preferred_element_type=F32).astype(um_ref.dtype)


def _inproj(x2d, wa, wb, wm, tm, um_dtype=F32):
    m, d = x2d.shape
    na, nb, nm = wa.shape[1], wb.shape[1], wm.shape[1]
    return pl.pallas_call(
        _inproj_kernel,
        grid=(m // tm,),
        in_specs=[pl.BlockSpec((tm, d), lambda i: (i, 0)),
                  _resident_spec(wa.shape), _resident_spec(wb.shape), _resident_spec(wm.shape)],
        out_specs=[pl.BlockSpec((tm, na), lambda i: (i, 0)),
                   pl.BlockSpec((tm, nb), lambda i: (i, 0)),
                   pl.BlockSpec((tm, nm), lambda i: (i, 0))],
        out_shape=[jax.ShapeDtypeStruct((m, na), F32),
                   jax.ShapeDtypeStruct((m, nb), F32),
                   jax.ShapeDtypeStruct((m, nm), um_dtype)],
        compiler_params=_params("arbitrary"),
        name="inproj",
    )(x2d, wa, wb, wm)


def _scan_rows(a, b, carry):
    bb, n, d = a.shape
    a = a.reshape(bb * n, d)
    b = b.reshape(bb * n, d)
    t_in = jnp.bitwise_and(lax.broadcasted_iota(jnp.int32, (bb * n, d), 0), SUBLANES - 1)
    s = 1
    while s < SUBLANES:
        a_sh = jnp.where(t_in >= s, pltpu.roll(a, s, 0), 1.0)
        b_sh = jnp.where(t_in >= s, pltpu.roll(b, s, 0), 0.0)
        b = b + a * b_sh
        a = a * a_sh
        s *= 2
    a = a.reshape(bb, n, d)
    b = b.reshape(bb, n, d)
    tiles = []
    for i in range(n // SUBLANES):
        rs = slice(i * SUBLANES, (i + 1) * SUBLANES)
        hi = b[:, rs, :] + a[:, rs, :] * carry
        carry = hi[:, SUBLANES - 1:SUBLANES, :]
        tiles.append(hi)
    return jnp.concatenate(tiles, axis=1), carry


def _rglru_kernel(ua_ref, cbuf_ref, h0_ref, cw_ref, cb_ref, wa_ref, ba_ref, wi_ref, bi_ref, lam_ref,
                  ya_ref, hlast_ref, cnew_ref, ext_scr, h_scr, ab_scr, hfull_scr, *, bb, tb, l_last, nblk, d):
    j = pl.program_id(1)
    ntail = CONV_W - 1
    rows = bb * tb

    @pl.when(j == 0)
    def _():
        h_scr[...] = h0_ref[...]
        ext_scr[:, SUBLANES - ntail:SUBLANES, :] = cbuf_ref[...]

    xr = ua_ref[:, :, :d]
    gr = ua_ref[:, :, d:].reshape(rows, d)
    ext_scr[:, SUBLANES:, :] = xr
    xc = cb_ref[...] + cw_ref[ntail:ntail + 1, :] * xr
    for s in range(1, CONV_W):
        xc = xc + cw_ref[ntail - s:ntail - s + 1, :] * ext_scr[:, SUBLANES - s:SUBLANES - s + tb, :]
    xc = xc.reshape(rows, d)

    xcb = xc.astype(BF16)
    nt = d // MXU_DIM
    ra = jnp.concatenate(
        [jnp.dot(xcb[:, q * MXU_DIM:(q + 1) * MXU_DIM], wa_ref[q], preferred_element_type=F32)
         for q in range(nt)], axis=-1)
    ia = jnp.concatenate(
        [jnp.dot(xcb[:, q * MXU_DIM:(q + 1) * MXU_DIM], wi_ref[q], preferred_element_type=F32)
         for q in range(nt)], axis=-1)
    r = _sigmoid(ra + ba_ref[...])
    ig = _sigmoid(ia + bi_ref[...])
    log_a = (-RG_C) * r * _softplus(-lam_ref[...])
    a = jnp.exp(log_a)
    z = -jnp.tanh(log_a) * (a * a + 1.0)
    mult = jnp.where(z > 0.0, z * lax.rsqrt(z), 0.0)
    b = mult * (ig * xc)

    nt = tb // SUBLANES
    if bb == 1 and nt >= SUBLANES:
        nl = d // LANES
        for c in range(nl):
            ab_scr[0, c] = a[:, c * LANES:(c + 1) * LANES]
            ab_scr[1, c] = b[:, c * LANES:(c + 1) * LANES]

        def tile_rows(which, r):
            return jnp.concatenate([ab_scr[which, c, pl.ds(r, nt, stride=SUBLANES), :] for c in range(nl)], axis=1)

        ps, ls = [], []
        for r in range(SUBLANES):
            ar, br = tile_rows(0, r), tile_rows(1, r)
            ps.append(ar if r == 0 else ar * ps[-1])
            ls.append(br if r == 0 else ar * ls[-1] + br)
        h_tile, carry = _scan_rows(ps[-1][None], ls[-1][None], h_scr[...])
        first = lax.broadcasted_iota(jnp.int32, (nt, d), 0) == 0
        h_in = jnp.where(first, h_scr[0], pltpu.roll(h_tile[0], 1, 0))
        for r in range(SUBLANES):
            hr = ls[r] + ps[r] * h_in
            for c in range(nl):
                hfull_scr[c, pl.ds(r, nt, stride=SUBLANES), :] = hr[:, c * LANES:(c + 1) * LANES]
        h = jnp.concatenate([hfull_scr[c] for c in range(nl)], axis=1).reshape(bb, tb, d)
    else:
        h, carry = _scan_rows(a.reshape(bb, tb, d), b.reshape(bb, tb, d), h_scr[...])
    h_scr[...] = carry
    ya_ref[...] = (h * jax.nn.gelu(gr, approximate=True).reshape(bb, tb, d)).astype(ya_ref.dtype)

    @pl.when(j == nblk - 1)
    def _():
        hlast_ref[...] = h[:, l_last - 1:l_last, :]
        cnew_ref[...] = ext_scr[:, SUBLANES + l_last - ntail:SUBLANES + l_last, :]

    tail = ext_scr[:, tb + SUBLANES - ntail:tb + SUBLANES, :]
    ext_scr[:, SUBLANES - ntail:SUBLANES, :] = tail


def _rglru(ua, cbuf, h0, cw, cb, wa4, ba, wi4, bi, lam, *, bb, tb, l_real, y_dtype=F32):
    bsz, lp, n2 = ua.shape
    d = n2 // 2
    nblk = lp // tb
    l_last = l_real - (nblk - 1) * tb
    assert tb & (tb - 1) == 0 and bsz % bb == 0
    kern = functools.partial(_rglru_kernel, bb=bb, tb=tb, l_last=l_last, nblk=nblk, d=d)
    return pl.pallas_call(
        kern,
        grid=(bsz // bb, nblk),
        in_specs=[pl.BlockSpec((bb, tb, n2), lambda b, j: (b, j, 0)),
                  pl.BlockSpec((bb, CONV_W - 1, d), lambda b, j: (b, 0, 0)),
                  pl.BlockSpec((bb, 1, d), lambda b, j: (b, 0, 0)),
                  _const_spec(cw.shape), _const_spec(cb.shape),
                  _const_spec(wa4.shape), _const_spec(ba.shape),
                  _const_spec(wi4.shape), _const_spec(bi.shape), _const_spec(lam.shape)],
        out_specs=[pl.BlockSpec((bb, tb, d), lambda b, j: (b, j, 0)),
                   pl.BlockSpec((bb, 1, d), lambda b, j: (b, 0, 0)),
                   pl.BlockSpec((bb, CONV_W - 1, d), lambda b, j: (b, 0, 0))],
        out_shape=[jax.ShapeDtypeStruct((bsz, lp, d), y_dtype),
                   jax.ShapeDtypeStruct((bsz, 1, d), F32),
                   jax.ShapeDtypeStruct((bsz, CONV_W - 1, d), F32)],
        scratch_shapes=[pltpu.VMEM((bb, tb + SUBLANES, d), F32), pltpu.VMEM((bb, 1, d), F32),
                        pltpu.VMEM((2, d // LANES, bb * tb, LANES), F32),
                        pltpu.VMEM((d // LANES, bb * tb, LANES), F32)],
        compiler_params=_params("arbitrary", "arbitrary"),
        name="rglru",
    )(ua, cbuf, h0, cw, cb, wa4, ba, wi4, bi, lam)


def _gla_kernel(ub_ref, s0_ref, wf_ref, bf_ref, g_ref, yb_ref, sout_ref, s_scr,
                *, bb, tb, chunk, l_real, nblk, dk, dv):
    j = pl.program_id(1)
    nh = GLA_HEADS
    qk = nh * dk
    vw = nh * dv

    @pl.when(j == 0)
    def _():
        s_scr[...] = s0_ref[...]

    off_k, off_v, off_g, off_f = qk, 2 * qk, 2 * qk + vw, 2 * qk + 2 * vw
    merge = bb > 1 and nblk == 1 and tb == chunk
    nrow = bb * tb if merge else tb
    nc = nrow // chunk
    shift = chunk.bit_length() - 1
    ri = lax.broadcasted_iota(jnp.int32, (nrow, nrow), 0)
    ci = lax.broadcasted_iota(jnp.int32, (nrow, nrow), 1)
    keep = (lax.shift_right_logical(ri, shift) == lax.shift_right_logical(ci, shift)) & (ri >= ci)
    tril = keep.astype(F32)
    row_chunk = lax.shift_right_logical(lax.broadcasted_iota(jnp.int32, (nrow, dk), 0), shift)
    mid = chunk // 2
    scale = dk ** -0.5

    def by_chunk(a):
        if nc == 1:
            return a
        return jnp.concatenate([jnp.where(row_chunk == c, a, 0.0) for c in range(nc)], axis=1)

    def per_chunk_row(a, r):
        return jnp.concatenate([jnp.broadcast_to(a[c * chunk + r:c * chunk + r + 1, :], (chunk, a.shape[1]))
                                for c in range(nc)], axis=0)

    for bi in range(1 if merge else bb):
        def cols(c0, c1, bi=bi):
            if merge:
                return ub_ref[:, :, c0:c1].reshape(nrow, c1 - c0)
            return ub_ref[bi, :, c0:c1]

        fl = cols(off_f, off_f + LANES).astype(BF16)
        logit = jnp.dot(fl, wf_ref[...], preferred_element_type=F32) + bf_ref[...]
        log_f = (jnp.minimum(logit, 0.0) - jnp.log1p(jnp.exp(-jnp.abs(logit)))) * (1.0 / GLA_TAU)
        if l_real < tb * nblk:
            rows = lax.broadcasted_iota(jnp.int32, log_f.shape, 0)
            rows = (jnp.bitwise_and(rows, tb - 1) if merge else rows) + j * tb
            log_f = jnp.where(rows < l_real, log_f, 0.0)
        bcum = jnp.dot(tril, log_f, precision=HIGHEST, preferred_element_type=F32)
        ref = per_chunk_row(bcum, mid)
        last = per_chunk_row(bcum, chunk - 1)
        e_inter = jnp.exp(bcum)
        e_q = jnp.exp(bcum - ref)
        e_k = jnp.exp(ref - bcum)
        e_dec = jnp.exp(last - bcum)

        for h in range(nh):
            hk = slice(h * dk, (h + 1) * dk)
            q = cols(h * dk, (h + 1) * dk) * scale
            k = cols(off_k + h * dk, off_k + (h + 1) * dk)
            vb = cols(off_v + h * dv, off_v + (h + 1) * dv).astype(BF16)
            go = cols(off_g + h * dv, off_g + (h + 1) * dv)
            att = lax.dot_general((q * e_q[:, hk]).astype(BF16), (k * e_k[:, hk]).astype(BF16),
                                  (((1,), (1,)), ((), ())), preferred_element_type=F32)
            att = jnp.where(keep, att, 0.0)
            o = jnp.dot(att.astype(BF16), vb, preferred_element_type=F32)
            kd = by_chunk(k * e_dec[:, hk])
            kv = lax.dot_general(kd.astype(BF16), vb, (((0,), (0,)), ((), ())),
                                 preferred_element_type=F32)
            states = [s_scr[c, h] for c in range(nc)] if merge else [s_scr[bi, h]]
            for c in range(nc):
                drow = jnp.exp(bcum[c * chunk + chunk - 1:c * chunk + chunk, hk])
                dec = jnp.transpose(jnp.broadcast_to(drow, (dk, dk)))
                dec = jnp.concatenate([dec] * (dv // dk), axis=1)
                s_next = dec * states[c] + kv[c * dk:(c + 1) * dk]
                if merge:
                    s_scr[c, h] = s_next
                else:
                    states.append(s_next)
            if not merge:
                s_scr[bi, h] = states[nc]
            qi = by_chunk(q * e_inter[:, hk])
            s_in = jnp.concatenate(states[:nc], axis=0).astype(BF16)
            o = o + jnp.dot(qi.astype(BF16), s_in, preferred_element_type=F32)
            o = o * lax.rsqrt(jnp.mean(o * o, axis=-1, keepdims=True) + RMS_EPS) * g_ref[...]
            y = (o * (go * _sigmoid(go))).astype(yb_ref.dtype)
            if merge:
                yb_ref[:, :, h * dv:(h + 1) * dv] = y.reshape(bb, tb, dv)
            else:
                yb_ref[bi, :, h * dv:(h + 1) * dv] = y

    @pl.when(j == nblk - 1)
    def _():
        sout_ref[...] = s_scr[...]


def _gla(ub, s0, wf, bf, g, *, bb, tb, chunk, l_real, y_dtype=F32):
    bsz, lp, nb = ub.shape
    _, nh, dk, dv = s0.shape
    nblk = lp // tb
    assert bsz % bb == 0 and dv % dk == 0 and chunk & (chunk - 1) == 0 and tb % chunk == 0
    kern = functools.partial(_gla_kernel, bb=bb, tb=tb, chunk=chunk, l_real=l_real, nblk=nblk, dk=dk, dv=dv)
    return pl.pallas_call(
        kern,
        grid=(bsz // bb, nblk),
        in_specs=[pl.BlockSpec((bb, tb, nb), lambda b, j: (b, j, 0)),
                  pl.BlockSpec((bb, nh, dk, dv), lambda b, j: (b, 0, 0, 0)),
                  _const_spec(wf.shape), _const_spec(bf.shape), _const_spec(g.shape)],
        out_specs=[pl.BlockSpec((bb, tb, nh * dv), lambda b, j: (b, j, 0)),
                   pl.BlockSpec((bb, nh, dk, dv), lambda b, j: (b, 0, 0, 0))],
        out_shape=[jax.ShapeDtypeStruct((bsz, lp, nh * dv), y_dtype),
                   jax.ShapeDtypeStruct((bsz, nh, dk, dv), F32)],
        scratch_shapes=[pltpu.VMEM((bb, nh, dk, dv), F32)],
        compiler_params=_params("arbitrary", "arbitrary"),
        name="gla",
    )(ub, s0, wf, bf, g)


def _mix_kernel(ya_ref, yb_ref, um_ref, x_ref, wo_ref, g_ref, b_ref, *rest, alpha, d):
    o_ref = rest[-1]
    um = um_ref[...].astype(F32)
    merged = (_sigmoid(um[:, :d]) * ya_ref[...].astype(F32)
              + _sigmoid(um[:, d:]) * yb_ref[...].astype(F32))
    mix = jnp.dot(merged.astype(BF16), wo_ref[...], preferred_element_type=F32)
    o_ref[...] = _layer_norm(alpha * x_ref[...] + mix, g_ref[...], b_ref[...])


def _mix(ya, yb, um, x, wo, g, b, *, alpha, tm, total_rows, row_offset=0, into=None):
    m, d = x.shape
    assert m % tm == 0 and row_offset % tm == 0
    row = lambda w: pl.BlockSpec((tm, w), lambda i: (i, 0))
    in_specs = [row(d), row(d), row(2 * d), row(d),
                _const_spec(wo.shape), _const_spec(g.shape), _const_spec(b.shape)]
    args = (ya, yb, um, x, wo, g, b)
    aliases = {}
    if into is not None:
        in_specs.append(pl.BlockSpec(memory_space=pl.ANY))
        aliases = {len(args): 0}
        args = args + (into,)
    return pl.pallas_call(
        functools.partial(_mix_kernel, alpha=alpha, d=d),
        grid=(m // tm,),
        in_specs=in_specs,
        out_specs=pl.BlockSpec((tm, d), lambda i: (i + row_offset // tm, 0)),
        out_shape=jax.ShapeDtypeStruct((total_rows, d), F32),
        input_output_aliases=aliases,
        compiler_params=_params("arbitrary"),
        name="mix_ln1",
    )(*args)


def _router_t_kernel(x_ref, whi_ref, wlo_ref, bias_ref, eidx_ref, gate_ref, rank_ref, cnt_ref, run_scr,
                     *, tm, ne):
    i = pl.program_id(0)

    @pl.when(i == 0)
    def _():
        run_scr[...] = jnp.zeros_like(run_scr)

    x = x_ref[...]
    x_hi = x.astype(BF16)
    x_lo = (x - x_hi.astype(F32)).astype(BF16)
    nt = (((1,), (1,)), ((), ()))
    logits = (lax.dot_general(whi_ref[...], x_hi, nt, preferred_element_type=F32)
              + lax.dot_general(whi_ref[...], x_lo, nt, preferred_element_type=F32)
              + lax.dot_general(wlo_ref[...], x_hi, nt, preferred_element_type=F32))
    s = _sigmoid(logits)
    sel = s + bias_ref[...]
    row = lax.broadcasted_iota(jnp.int32, (ne, tm), 0).astype(F32)
    gsz = ne // N_GROUPS
    neg = -jnp.inf

    def colmax(a):
        return jnp.max(a, axis=0, keepdims=True)

    def colmin(a):
        return jnp.min(a, axis=0, keepdims=True)

    gscores = []
    row_in_group = lax.broadcasted_iota(jnp.int32, (gsz, tm), 0).astype(F32)
    for g in range(N_GROUPS):
        v = sel[g * gsz:(g + 1) * gsz]
        rg = row_in_group + float(g * gsz)
        m1 = colmax(v)
        i1 = colmin(jnp.where(v == m1, rg, float(ne)))
        gscores.append(m1 + colmax(jnp.where(rg == i1, neg, v)))
    kept = []
    for g in range(N_GROUPS):
        ahead = jnp.zeros((1, tm), F32)
        for g2 in range(N_GROUPS):
            if g2 == g:
                continue
            beats = (gscores[g2] >= gscores[g]) if g2 < g else (gscores[g2] > gscores[g])
            ahead = ahead + beats.astype(F32)
        keep = jnp.broadcast_to(ahead < float(TOPK_GROUPS), (gsz, tm))
        kept.append(jnp.where(keep, sel[g * gsz:(g + 1) * gsz], neg))
    cur = jnp.concatenate(kept, axis=0)

    picks, eidx, wsel = [], [], []
    onehot_sum = jnp.zeros((ne, tm), F32)
    for k in range(TOP_K):
        ik = colmin(jnp.where(cur == colmax(cur), row, float(ne)))
        hit = row == ik
        wsel.append(jnp.sum(jnp.where(hit, s, 0.0), axis=0, keepdims=True))
        cur = jnp.where(hit, neg, cur)
        onehot_sum = onehot_sum + hit.astype(F32)
        picks.append(hit)
        eidx.append(ik)
    wsel = jnp.concatenate(wsel, axis=0)
    gate_ref[...] = wsel / jnp.sum(wsel, axis=0, keepdims=True) * ROUTED_SCALE
    eidx_ref[...] = jnp.concatenate(eidx, axis=0).astype(jnp.int32)

    ri = lax.broadcasted_iota(jnp.int32, (tm, tm), 0)
    ci = lax.broadcasted_iota(jnp.int32, (tm, tm), 1)
    earlier = (ri < ci).astype(BF16)
    ohb = onehot_sum.astype(BF16)
    counts = jnp.dot(ohb, earlier, preferred_element_type=F32) + jnp.concatenate([run_scr[...]] * (tm // LANES), axis=1)
    rank = [jnp.sum(jnp.where(picks[k], counts, 0.0), axis=0, keepdims=True) for k in range(TOP_K)]
    rank_ref[...] = jnp.concatenate(rank, axis=0).astype(jnp.int32)
    run_scr[...] = run_scr[...] + jnp.dot(ohb, jnp.ones((tm, LANES), BF16), preferred_element_type=F32)
    cnt_ref[...] = run_scr[...].astype(jnp.int32)


def _router_t(x, w_hi, w_lo, bias_b, *, tm):
    t, d = x.shape
    ne = w_hi.shape[0]
    out_col = lambda: pl.BlockSpec((TOP_K, tm), lambda i: (0, i))
    return pl.pallas_call(
        functools.partial(_router_t_kernel, tm=tm, ne=ne),
        grid=(t // tm,),
        in_specs=[pl.BlockSpec((tm, d), lambda i: (i, 0)), _const_spec(w_hi.shape), _const_spec(w_lo.shape),
                  _const_spec(bias_b.shape)],
        out_specs=[out_col(), out_col(), out_col(), _const_spec((ne, LANES))],
        out_shape=[jax.ShapeDtypeStruct((TOP_K, t), jnp.int32),
                   jax.ShapeDtypeStruct((TOP_K, t), F32),
                   jax.ShapeDtypeStruct((TOP_K, t), jnp.int32),
                   jax.ShapeDtypeStruct((ne, LANES), jnp.int32)],
        scratch_shapes=[pltpu.VMEM((ne, LANES), F32)],
        compiler_params=_params("arbitrary"),
        name="router",
    )(x, w_hi, w_lo, bias_b)


def _dest_t_kernel(eidx_ref, rank_ref, start_ref, dest_ref, *, tm, ne):
    row = lax.broadcasted_iota(jnp.int32, (ne, tm), 0)
    starts = start_ref[...].astype(F32)
    eidx = eidx_ref[...]
    dest = [jnp.sum(jnp.where(row == eidx[k:k + 1, :], starts, 0.0), axis=0, keepdims=True) for k in range(TOP_K)]
    dest_ref[...] = jnp.concatenate(dest, axis=0).astype(jnp.int32) + rank_ref[...]


def _dest_t(eidx, rank, starts_b, *, tm):
    t = eidx.shape[1]
    ne = starts_b.shape[0]
    col = lambda: pl.BlockSpec((TOP_K, tm), lambda i: (0, i))
    return pl.pallas_call(
        functools.partial(_dest_t_kernel, tm=tm, ne=ne),
        grid=(t // tm,),
        in_specs=[col(), col(), _const_spec(starts_b.shape)],
        out_specs=col(),
        out_shape=jax.ShapeDtypeStruct((TOP_K, t), jnp.int32),
        compiler_params=_params("arbitrary"),
        name="dest",
    )(eidx, rank, starts_b)


TOKEN_DTYPE = BF16


def _to_token_tiles(rows):
    chunks = jnp.stack([rows[:, c * LANES:(c + 1) * LANES] for c in range(SUBLANES)], axis=0)
    return jnp.swapaxes(chunks, 0, 1).astype(TOKEN_DTYPE)


def _chunks_of_token_tiles(tiles):
    chunks = jnp.swapaxes(tiles.astype(F32), 0, 1)
    return [chunks[c] for c in range(SUBLANES)]


def _token_copy(src_ref, src_row, dst_ref, dst_row, sem):
    return pltpu.make_async_copy(src_ref.at[pl.ds(src_row, 1)], dst_ref.at[pl.ds(dst_row, 1)], sem)


def _dispatch_kernel(dest_ref, x_ref, xs_ref, xt, sem, *, tt):
    xt[...] = _to_token_tiles(x_ref[...])

    def start(tp, c):
        for u in range(ISSUE_UNROLL):
            t = tp * ISSUE_UNROLL + u
            for k in range(TOP_K):
                _token_copy(xt, t, xs_ref, dest_ref[t * TOP_K + k], sem).start(priority=k % 2)
        return c

    lax.fori_loop(0, tt // ISSUE_UNROLL, start, 0)
    for _ in range(TOP_K):
        pltpu.make_async_copy(xt, xs_ref.at[pl.ds(0, tt)], sem).wait()


def _dispatch(dest_flat, x, n_slots, *, tt):
    t, d = x.shape
    assert d == SUBLANES * LANES
    return pl.pallas_call(
        functools.partial(_dispatch_kernel, tt=tt),
        grid=(t // tt,),
        in_specs=[pl.BlockSpec((tt * TOP_K,), lambda i: (i,), memory_space=pltpu.SMEM),
                  pl.BlockSpec((tt, d), lambda i: (i, 0))],
        out_specs=pl.BlockSpec(memory_space=pl.ANY),
        out_shape=jax.ShapeDtypeStruct((n_slots, SUBLANES, LANES), TOKEN_DTYPE),
        scratch_shapes=[pltpu.VMEM((tt, SUBLANES, LANES), TOKEN_DTYPE), pltpu.SemaphoreType.DMA(())],
        compiler_params=_params("arbitrary"),
        name="dispatch",
    )(dest_flat, x)


EXPERT_SLOTS = 16
EXPERT_TAILS = ((1, (1,)), (2, (2,)), (3, (2, 1)), (5, (2, 2, 1)), (6, (2, 2, 2)), (7, (2, 2, 2, 1)))
EXPERT_STEP_BLOCKS = max(r for r, _ in EXPERT_TAILS)
EXPERT_AHEAD = 8


def _expert_kernel(base_ref, nblk_ref, cnt_ref, nu_ref, xs_ref, wg_ref, wu_ref, wd_ref, ys_ref,
                   xbuf, ybuf, wgb, wub, wdb, sem_in, sem_out):
    e = pl.program_id(0)
    n_used = nu_ref[0]
    nb = nblk_ref[e]
    base = base_ref[e]
    cnt = cnt_ref[e]
    ahead = EXPERT_AHEAD

    def x_copy(blk, slot):
        return pltpu.make_async_copy(xs_ref.at[pl.ds(blk * MOE_ROWS, MOE_ROWS)], xbuf.at[slot], sem_in.at[slot])

    def y_copy(blk, slot):
        return pltpu.make_async_copy(ybuf.at[slot], ys_ref.at[pl.ds(blk * MOE_ROWS, MOE_ROWS)], sem_out.at[slot])

    @pl.when(e == 0)
    def _():
        for first in range(ahead):
            @pl.when(first < n_used)
            def _():
                x_copy(first, first).start()

    @pl.when(nb > 0)
    def _():
        wgb[...] = wg_ref[0].astype(BF16)
        wub[...] = wu_ref[0].astype(BF16)
        wdb[...] = wd_ref[0].astype(BF16)

    def step(jb, chains):
        nblocks = sum(chains)
        blks = [base + jb + i for i in range(nblocks)]
        slots = [jnp.bitwise_and(b, EXPERT_SLOTS - 1) for b in blks]
        for b, s in zip(blks, slots):
            x_copy(b, s).wait()
        for b in blks:
            @pl.when(b + ahead < n_used)
            def _():
                x_copy(b + ahead, jnp.bitwise_and(b + ahead, EXPERT_SLOTS - 1)).start()
        for b, s in zip(blks, slots):
            @pl.when(b >= EXPERT_SLOTS)
            def _():
                y_copy(b - EXPERT_SLOTS, s).wait()

        first = 0
        for n in chains:
            x = jnp.concatenate([jnp.concatenate(_chunks_of_token_tiles(xbuf[s]), axis=1)
                                 for s in slots[first:first + n]], axis=0)
            rows = lax.broadcasted_iota(jnp.int32, x.shape, 0)
            xb = jnp.where(rows < cnt - (jb + first) * MOE_ROWS, x, 0.0).astype(BF16)
            g = jnp.dot(xb, wgb[...], preferred_element_type=F32)
            u = jnp.dot(xb, wub[...], preferred_element_type=F32)
            hmid = (g * _sigmoid(g)) * u
            y = jnp.dot(hmid.astype(BF16), wdb[...], preferred_element_type=F32)
            for i in range(n):
                b, s = blks[first + i], slots[first + i]
                ybuf[s] = _to_token_tiles(y[i * MOE_ROWS:(i + 1) * MOE_ROWS])
                y_copy(b, s).start()
            first += n

    def quad(jq, c):
        step(4 * jq, (2, 2))
        return c

    rem = jnp.bitwise_and(nb, 3)
    nfull = lax.shift_right_logical(nb, 2)
    nloop = jnp.where(rem == 0, nfull, jnp.maximum(nfull - 1, 0))
    lax.fori_loop(0, nloop, quad, 0)
    tail = nb - 4 * nloop
    for r, chains in EXPERT_TAILS:
        @pl.when(tail == r)
        def _():
            step(4 * nloop, chains)

    @pl.when(e == pl.num_programs(0) - 1)
    def _():
        for back in range(EXPERT_SLOTS):
            @pl.when(n_used > back)
            def _():
                last = n_used - 1 - back
                y_copy(last, jnp.bitwise_and(last, EXPERT_SLOTS - 1)).wait()


def _experts(base, nblk, cnt, n_used, xs, wg, wu, wd):
    p = xs.shape[0]
    ne, d, de = wg.shape
    assert EXPERT_SLOTS & (EXPERT_SLOTS - 1) == 0 and EXPERT_AHEAD <= EXPERT_SLOTS - EXPERT_STEP_BLOCKS
    assert d == SUBLANES * LANES
    ring = pltpu.VMEM((EXPERT_SLOTS, MOE_ROWS, SUBLANES, LANES), TOKEN_DTYPE)
    grid_spec = pltpu.PrefetchScalarGridSpec(
        num_scalar_prefetch=4,
        grid=(ne,),
        in_specs=[pl.BlockSpec(memory_space=pl.ANY),
                  pl.BlockSpec((1, d, de), lambda e, *_: (e, 0, 0)),
                  pl.BlockSpec((1, d, de), lambda e, *_: (e, 0, 0)),
                  pl.BlockSpec((1, de, d), lambda e, *_: (e, 0, 0))],
        out_specs=pl.BlockSpec(memory_space=pl.ANY),
        scratch_shapes=[ring, ring,
                        pltpu.VMEM((d, de), BF16), pltpu.VMEM((d, de), BF16), pltpu.VMEM((de, d), BF16),
                        pltpu.SemaphoreType.DMA((EXPERT_SLOTS,)), pltpu.SemaphoreType.DMA((EXPERT_SLOTS,))],
    )
    return pl.pallas_call(
        _expert_kernel,
        grid_spec=grid_spec,
        out_shape=jax.ShapeDtypeStruct((p, SUBLANES, LANES), TOKEN_DTYPE),
        compiler_params=_params("arbitrary"),
        name="experts",
    )(base, nblk, cnt, n_used, xs, wg, wu, wd)


def _combine_kernel(dest_ref, dest_next_ref, x_ref, gate_ref, ys_ref, wsg_ref, wsu_ref, wsd_ref, g_ref, b_ref,
                    o_head_ref, o_tail_ref, ybuf, sem, *, tt, alpha, head_steps):
    i = pl.program_id(0)
    slot = jnp.bitwise_and(i, 1)

    def gather(dref, s):
        def start(tp, c):
            for u in range(ISSUE_UNROLL):
                t = tp * ISSUE_UNROLL + u
                for k in range(TOP_K):
                    _token_copy(ys_ref, dref[t * TOP_K + k], ybuf.at[s], k * tt + t, sem.at[s]).start(priority=k % 2)
            return c
        lax.fori_loop(0, tt // ISSUE_UNROLL, start, 0)

    @pl.when(i == 0)
    def _():
        gather(dest_ref, 0)

    def prefetch_group(k):
        nxt = 1 - slot
        for t in range(tt):
            _token_copy(ys_ref, dest_next_ref[t * TOP_K + k], ybuf.at[nxt], k * tt + t, sem.at[nxt]).start(priority=t % 2)

    x = x_ref[...]
    xb = x.astype(BF16)
    sg = jnp.dot(xb, wsg_ref[...], preferred_element_type=F32)
    su = jnp.dot(xb, wsu_ref[...], preferred_element_type=F32)
    acc = jnp.dot(((sg * _sigmoid(sg)) * su).astype(BF16), wsd_ref[...], preferred_element_type=F32)
    pltpu.make_async_copy(ys_ref.at[pl.ds(0, tt * TOP_K)], ybuf.at[slot], sem.at[slot]).wait()
    gate = gate_ref[...]
    routed = None
    for k in range(TOP_K):
        prefetch_group(k)
        gk = jnp.broadcast_to(gate[:, k:k + 1], (tt, LANES))
        part = [gk * ch for ch in _chunks_of_token_tiles(ybuf[slot, k * tt:(k + 1) * tt])]
        routed = part if routed is None else [r + p for r, p in zip(routed, part)]
    acc = acc + jnp.concatenate(routed, axis=1)
    out = _layer_norm(alpha * x + acc, g_ref[...], b_ref[...])

    @pl.when(i < head_steps)
    def _():
        o_head_ref[...] = out

    @pl.when(i >= head_steps)
    def _():
        o_tail_ref[...] = out

    @pl.when(i == pl.num_programs(0) - 1)
    def _():
        pltpu.make_async_copy(ys_ref.at[pl.ds(0, tt * TOP_K)], ybuf.at[1 - slot], sem.at[1 - slot]).wait()


def _combine(dest_flat, x, gate, ys, wsg, wsu, wsd, g, b, *, tt, alpha, head_rows):
    t, d = x.shape
    nsteps = t // tt
    head_steps = head_rows // tt
    assert head_rows % tt == 0 and 0 < head_steps < nsteps
    return pl.pallas_call(
        functools.partial(_combine_kernel, tt=tt, alpha=alpha, head_steps=head_steps),
        grid=(nsteps,),
        in_specs=[pl.BlockSpec((tt * TOP_K,), lambda i: (i,), memory_space=pltpu.SMEM),
                  pl.BlockSpec((tt * TOP_K,), lambda i: (jnp.minimum(i + 1, nsteps - 1),),
                               memory_space=pltpu.SMEM),
                  pl.BlockSpec((tt, d), lambda i: (i, 0)),
                  pl.BlockSpec((tt, TOP_K), lambda i: (i, 0)),
                  pl.BlockSpec(memory_space=pl.ANY),
                  _const_spec(wsg.shape), _const_spec(wsu.shape), _const_spec(wsd.shape),
                  _const_spec(g.shape), _const_spec(b.shape)],
        out_specs=[pl.BlockSpec((tt, d), lambda i: (jnp.minimum(i, head_steps - 1), 0)),
                   pl.BlockSpec((tt, d), lambda i: (jnp.maximum(i - head_steps, 0), 0))],
        out_shape=[jax.ShapeDtypeStruct((head_rows, d), F32), jax.ShapeDtypeStruct((t - head_rows, d), F32)],
        scratch_shapes=[pltpu.VMEM((2, tt * TOP_K, SUBLANES, LANES), TOKEN_DTYPE), pltpu.SemaphoreType.DMA((2,))],
        compiler_params=_params("arbitrary"),
        name="combine_ln2",
    )(dest_flat, dest_flat, x, gate, ys, wsg, wsu, wsd, g, b)


def _block_diag_tiles(w):
    n, s, _ = w.shape
    per = MXU_DIM // s
    w = w.reshape(n // per, per, s, s)
    eye = jnp.eye(per, dtype=w.dtype)
    return jnp.einsum('tpij,pq->tpiqj', w, eye).reshape(n // per, MXU_DIM, MXU_DIM)


def _row(v):
    return v.reshape(1, -1)


def _moe(x1, w_router, router_bias, w_gate, w_up, w_down, ws_gate, ws_up, ws_down, ln_g, ln_b, alpha, head_rows):
    t, d = x1.shape
    ne = w_router.shape[1]
    tm = 512
    w_t = w_router.T
    w_hi = w_t.astype(BF16)
    w_lo = (w_t - w_hi.astype(F32)).astype(BF16)
    bias_b = jnp.broadcast_to(router_bias.reshape(ne, 1), (ne, tm))
    eidx_t, gate_t, rank_t, counts = _router_t(x1, w_hi, w_lo, bias_b, tm=tm)
    counts = counts[:, 0]
    nblk = (counts + MOE_ROWS - 1) // MOE_ROWS
    blk_end = jnp.cumsum(nblk)
    blk_base = (blk_end - nblk).astype(jnp.int32)
    n_used = blk_end[-1:].astype(jnp.int32)
    n_slots = ((t * TOP_K) // MOE_ROWS + ne) * MOE_ROWS

    starts_b = jnp.broadcast_to((blk_base * MOE_ROWS).reshape(ne, 1), (ne, tm))
    dest_flat = _dest_t(eidx_t, rank_t, starts_b, tm=tm).T.reshape(-1)
    xs = _dispatch(dest_flat, x1, n_slots, tt=512)
    ys = _experts(blk_base, nblk.astype(jnp.int32), counts, n_used, xs, w_gate, w_up, w_down)
    return _combine(dest_flat, x1, gate_t.T, ys, ws_gate.astype(BF16), ws_up.astype(BF16),
                    ws_down.astype(BF16), _row(ln_g), _row(ln_b), tt=256, alpha=alpha, head_rows=head_rows)


def _pad_time(a, lp):
    return jnp.pad(a, ((0, 0), (0, lp - a.shape[1]), (0, 0)))


def kernel(x_prompt, x_sample, state_rglru_h, state_conv, state_gla, meta_tokens, w_in, conv_w, conv_b,
           rg_wa, rg_ba, rg_wi, rg_bi, rg_lambda, gla_wf2, gla_bf, gla_norm_g, w_out, ln1_g, ln1_b,
           w_router, router_bias, w_gate, w_up, w_down, ws_gate, ws_up, ws_down, ln2_g, ln2_b):
    bp, lp, d = x_prompt.shape
    bs, ls, _ = x_sample.shape
    depth = w_in.shape[0]
    nh, dk, dv = state_gla.shape[2:]
    qk, vw = nh * dk, nh * dv
    lowrank = gla_wf2.shape[1]
    alpha = (2.0 * depth) ** 0.25
    ls_pad = SUBLANES
    assert depth == 1 and ls <= ls_pad and lp % 256 == 0 and N_META % SUBLANES == 0

    xp = x_prompt.reshape(bp * lp, d)
    xs = x_sample.reshape(bs * ls, d)
    xm = meta_tokens.astype(F32)

    hp_l, cp_l, sp_l, hs_l, cs_l, ss_l = [], [], [], [], [], []
    for l in range(depth):
        o1, o2 = 2 * d, 2 * d + 2 * qk + 2 * vw
        wa = w_in[l][:, :o1].astype(BF16)
        wb = jnp.pad(w_in[l][:, o1:o2 + lowrank], ((0, 0), (0, LANES - lowrank))).astype(BF16)
        wm = w_in[l][:, o2 + lowrank:].astype(BF16)
        wf = jnp.pad(gla_wf2[l], ((0, LANES - lowrank), (0, 0))).astype(BF16)
        wa4 = _block_diag_tiles(rg_wa[l]).astype(BF16)
        wi4 = _block_diag_tiles(rg_wi[l]).astype(BF16)
        mixer_w = (conv_w[l], _row(conv_b[l]), wa4, _row(rg_ba[l]), wi4, _row(rg_bi[l]), _row(rg_lambda[l]))
        gla_w = (wf, _row(gla_bf[l]), _row(gla_norm_g[l]))

        ua, ub, _ = _inproj(xm, wa, wb, wm, tm=N_META)
        _, h_m, c_m = _rglru(ua[None], jnp.zeros((1, CONV_W - 1, d), F32), jnp.zeros((1, 1, d), F32),
                             *mixer_w, bb=1, tb=N_META, l_real=N_META)
        _, s_m = _gla(ub[None], jnp.zeros((1, nh, dk, dv), F32), *gla_w,
                      bb=1, tb=N_META, chunk=N_META, l_real=N_META)

        ua, ub, um = _inproj(xp, wa, wb, wm, tm=512, um_dtype=BF16)
        ya, h_p, c_p = _rglru(ua.reshape(bp, lp, -1), jnp.broadcast_to(c_m, (bp,) + c_m.shape[1:]),
                              jnp.broadcast_to(h_m, (bp,) + h_m.shape[1:]), *mixer_w,
                              bb=1, tb=512, l_real=lp, y_dtype=BF16)
        yb, s_p = _gla(ub.reshape(bp, lp, -1), jnp.broadcast_to(s_m, (bp,) + s_m.shape[1:]), *gla_w,
                       bb=1, tb=256, chunk=GLA_CHUNK, l_real=lp, y_dtype=BF16)
        n_tok = bp * lp + bs * ls
        x1 = _mix(ya.reshape(bp * lp, d), yb.reshape(bp * lp, d), um, xp, w_out[l].astype(BF16),
                  _row(ln1_g[l]), _row(ln1_b[l]), alpha=alpha, tm=1024, total_rows=n_tok)

        xs_pad = _pad_time(xs.reshape(bs, ls, d), ls_pad).reshape(bs * ls_pad, d)
        ua, ub, um = _inproj(xs_pad, wa, wb, wm, tm=256)
        ya, h_s, c_s = _rglru(ua.reshape(bs, ls_pad, -1), state_conv[l].astype(F32),
                              state_rglru_h[l].astype(F32)[:, None, :], *mixer_w,
                              bb=32, tb=ls_pad, l_real=ls)
        yb, s_s = _gla(ub.reshape(bs, ls_pad, -1), state_gla[l].astype(F32), *gla_w,
                       bb=8, tb=ls_pad, chunk=ls_pad, l_real=ls)
        unpad = lambda a: a.reshape(bs, ls_pad, -1)[:, :ls].reshape(bs * ls, -1)
        x1 = _mix(unpad(ya), unpad(yb), unpad(um), xs, w_out[l].astype(BF16),
                  _row(ln1_g[l]), _row(ln1_b[l]), alpha=alpha, tm=512, total_rows=n_tok,
                  row_offset=bp * lp, into=x1)

        xp, xs = _moe(x1, w_router[l], router_bias[l], w_gate[l], w_up[l], w_down[l],
                      ws_gate[l], ws_up[l], ws_down[l], ln2_g[l], ln2_b[l], alpha, bp * lp)

        hp_l.append(h_p[:, 0]); cp_l.append(c_p); sp_l.append(s_p)
        hs_l.append(h_s[:, 0]); cs_l.append(c_s); ss_l.append(s_s)

    y_prompt = xp.reshape(bp, lp, d)
    y_sample = xs.reshape(bs, ls, d)
    return (y_prompt, y_sample,
            jnp.stack(hp_l).astype(state_rglru_h.dtype), jnp.stack(cp_l).astype(state_conv.dtype),
            jnp.stack(sp_l).astype(state_gla.dtype),
            jnp.stack(hs_l).astype(state_rglru_h.dtype), jnp.stack(cs_l).astype(state_conv.dtype),
            jnp.stack(ss_l).astype(state_gla.dtype))
```

```python
import functools

import jax
import jax.numpy as jnp
from jax import lax
from jax.experimental import pallas as pl
from jax.experimental.pallas import tpu as pltpu

F32 = jnp.float32
BF16 = jnp.bfloat16
HIGHEST = lax.Precision.HIGHEST

N_META = 16
CONV_W = 4
RG_C = 8.0
RNN_BLOCKS = 16
GLA_HEADS = 4
GLA_TAU = 16.0
GLA_CHUNK = 64
N_GROUPS = 8
TOPK_GROUPS = 4
TOP_K = 8
ROUTED_SCALE = 2.5
LN_EPS = 1e-5
RMS_EPS = 1e-6

LANES = 128
SUBLANES = 8
MXU_DIM = 256
VMEM_LIMIT_BYTES = 56 * 1024 * 1024

MOE_ROWS = 128
ISSUE_UNROLL = 4


def _params(*sem):
    return pltpu.CompilerParams(dimension_semantics=sem, vmem_limit_bytes=VMEM_LIMIT_BYTES)


def _sigmoid(x):
    return jax.nn.sigmoid(x)


def _softplus(z):
    return jnp.maximum(z, 0.0) + jnp.log1p(jnp.exp(-jnp.abs(z)))


def _layer_norm(z, g, b):
    mu = jnp.mean(z, axis=-1, keepdims=True)
    zc = z - mu
    var = jnp.mean(zc * zc, axis=-1, keepdims=True)
    return zc * lax.rsqrt(var + LN_EPS) * g + b


def _const_spec(shape):
    nd = len(shape)
    return pl.BlockSpec(shape, lambda *_: (0,) * nd)


def _resident_spec(shape):
    nd = len(shape)
    return pl.BlockSpec(shape, lambda *_: (0,) * nd, pipeline_mode=pl.Buffered(1))


def _inproj_kernel(x_ref, wa_ref, wb_ref, wm_ref, ua_ref, ub_ref, um_ref):
    xb = x_ref[...].astype(BF16)
    ua_ref[...] = jnp.dot(xb, wa_ref[...], preferred_element_type=F32)
    ub_ref[...] = jnp.dot(xb, wb_ref[...], preferred_element_type=F32)
    um_ref[...] = jnp.dot(xb, wm_ref[...], preferred_element_type=F32).astype(um_ref.dtype)


def _inproj(x2d, wa, wb, wm, tm, um_dtype=F32):
    m, d = x2d.shape
    na, nb, nm = wa.shape[1], wb.shape[1], wm.shape[1]
    assert m % tm == 0
    return pl.pallas_call(
        _inproj_kernel,
        grid=(m // tm,),
        in_specs=[pl.BlockSpec((tm, d), lambda i: (i, 0)),
                  _resident_spec(wa.shape), _resident_spec(wb.shape), _resident_spec(wm.shape)],
        out_specs=[pl.BlockSpec((tm, na), lambda i: (i, 0)),
                   pl.BlockSpec((tm, nb), lambda i: (i, 0)),
                   pl.BlockSpec((tm, nm), lambda i: (i, 0))],
        out_shape=[jax.ShapeDtypeStruct((m, na), F32),
                   jax.ShapeDtypeStruct((m, nb), F32),
                   jax.ShapeDtypeStruct((m, nm), um_dtype)],
        compiler_params=_params("arbitrary"),
        name="inproj",
    )(x2d, wa, wb, wm)


def _scan_rows(a, b, carry):
    bb, n, d = a.shape
    a = a.reshape(bb * n, d)
    b = b.reshape(bb * n, d)
    t_in = jnp.bitwise_and(lax.broadcasted_iota(jnp.int32, (bb * n, d), 0), SUBLANES - 1)
    s = 1
    while s < SUBLANES:
        a_sh = jnp.where(t_in >= s, pltpu.roll(a, s, 0), 1.0)
        b_sh = jnp.where(t_in >= s, pltpu.roll(b, s, 0), 0.0)
        b = b + a * b_sh
        a = a * a_sh
        s *= 2
    a = a.reshape(bb, n, d)
    b = b.reshape(bb, n, d)
    tiles = []
    for i in range(n // SUBLANES):
        rs = slice(i * SUBLANES, (i + 1) * SUBLANES)
        hi = b[:, rs, :] + a[:, rs, :] * carry
        carry = hi[:, SUBLANES - 1:SUBLANES, :]
        tiles.append(hi)
    return jnp.concatenate(tiles, axis=1), carry


def _rglru_kernel(ua_ref, cbuf_ref, h0_ref, cw_ref, cb_ref, wa_ref, ba_ref, wi_ref, bi_ref, lam_ref,
                  ya_ref, hlast_ref, cnew_ref, ext_scr, h_scr, ab_scr, hfull_scr, *, bb, tb, l_last, nblk, d):
    j = pl.program_id(1)
    ntail = CONV_W - 1
    rows = bb * tb

    @pl.when(j == 0)
    def _():
        h_scr[...] = h0_ref[...]
        ext_scr[:, SUBLANES - ntail:SUBLANES, :] = cbuf_ref[...]

    xr = ua_ref[:, :, :d]
    gr = ua_ref[:, :, d:].reshape(rows, d)
    ext_scr[:, SUBLANES:, :] = xr
    xc = cb_ref[...] + cw_ref[ntail:ntail + 1, :] * xr
    for s in range(1, CONV_W):
        xc = xc + cw_ref[ntail - s:ntail - s + 1, :] * ext_scr[:, SUBLANES - s:SUBLANES - s + tb, :]
    xc = xc.reshape(rows, d)

    xcb = xc.astype(BF16)
    nt = d // MXU_DIM
    ra = jnp.concatenate(
        [jnp.dot(xcb[:, q * MXU_DIM:(q + 1) * MXU_DIM], wa_ref[q], preferred_element_type=F32)
         for q in range(nt)], axis=-1)
    ia = jnp.concatenate(
        [jnp.dot(xcb[:, q * MXU_DIM:(q + 1) * MXU_DIM], wi_ref[q], preferred_element_type=F32)
         for q in range(nt)], axis=-1)
    r = _sigmoid(ra + ba_ref[...])
    ig = _sigmoid(ia + bi_ref[...])
    log_a = (-RG_C) * r * _softplus(-lam_ref[...])
    a = jnp.exp(log_a)
    z = -jnp.tanh(log_a) * (a * a + 1.0)
    mult = jnp.where(z > 0.0, z * lax.rsqrt(z), 0.0)
    b = mult * (ig * xc)

    nt = tb // SUBLANES
    if bb == 1 and nt >= SUBLANES:
        nl = d // LANES
        for c in range(nl):
            ab_scr[0, c] = a[:, c * LANES:(c + 1) * LANES]
            ab_scr[1, c] = b[:, c * LANES:(c + 1) * LANES]

        def tile_rows(which, r):
            return jnp.concatenate([ab_scr[which, c, pl.ds(r, nt, stride=SUBLANES), :] for c in range(nl)], axis=1)

        ps, ls = [], []
        for r in range(SUBLANES):
            ar, br = tile_rows(0, r), tile_rows(1, r)
            ps.append(ar if r == 0 else ar * ps[-1])
            ls.append(br if r == 0 else ar * ls[-1] + br)
        h_tile, carry = _scan_rows(ps[-1][None], ls[-1][None], h_scr[...])
        first = lax.broadcasted_iota(jnp.int32, (nt, d), 0) == 0
        h_in = jnp.where(first, h_scr[0], pltpu.roll(h_tile[0], 1, 0))
        for r in range(SUBLANES):
            hr = ls[r] + ps[r] * h_in
            for c in range(nl):
                hfull_scr[c, pl.ds(r, nt, stride=SUBLANES), :] = hr[:, c * LANES:(c + 1) * LANES]
        h = jnp.concatenate([hfull_scr[c] for c in range(nl)], axis=1).reshape(bb, tb, d)
    else:
        h, carry = _scan_rows(a.reshape(bb, tb, d), b.reshape(bb, tb, d), h_scr[...])
    h_scr[...] = carry
    ya_ref[...] = (h * jax.nn.gelu(gr, approximate=True).reshape(bb, tb, d)).astype(ya_ref.dtype)

    @pl.when(j == nblk - 1)
    def _():
        hlast_ref[...] = h[:, l_last - 1:l_last, :]
        cnew_ref[...] = ext_scr[:, SUBLANES + l_last - ntail:SUBLANES + l_last, :]

    tail = ext_scr[:, tb + SUBLANES - ntail:tb + SUBLANES, :]
    ext_scr[:, SUBLANES - ntail:SUBLANES, :] = tail


def _rglru(ua, cbuf, h0, cw, cb, wa4, ba, wi4, bi, lam, *, bb, tb, l_real, y_dtype=F32):
    bsz, lp, n2 = ua.shape
    d = n2 // 2
    nblk = lp // tb
    l_last = l_real - (nblk - 1) * tb
    assert tb & (tb - 1) == 0 and bsz % bb == 0
    kern = functools.partial(_rglru_kernel, bb=bb, tb=tb, l_last=l_last, nblk=nblk, d=d)
    return pl.pallas_call(
        kern,
        grid=(bsz // bb, nblk),
        in_specs=[pl.BlockSpec((bb, tb, n2), lambda b, j: (b, j, 0)),
                  pl.BlockSpec((bb, CONV_W - 1, d), lambda b, j: (b, 0, 0)),
                  pl.BlockSpec((bb, 1, d), lambda b, j: (b, 0, 0)),
                  _const_spec(cw.shape), _const_spec(cb.shape),
                  _const_spec(wa4.shape), _const_spec(ba.shape),
                  _const_spec(wi4.shape), _const_spec(bi.shape), _const_spec(lam.shape)],
        out_specs=[pl.BlockSpec((bb, tb, d), lambda b, j: (b, j, 0)),
                   pl.BlockSpec((bb, 1, d), lambda b, j: (b, 0, 0)),
                   pl.BlockSpec((bb, CONV_W - 1, d), lambda b, j: (b, 0, 0))],
        out_shape=[jax.ShapeDtypeStruct((bsz, lp, d), y_dtype),
                   jax.ShapeDtypeStruct((bsz, 1, d), F32),
                   jax.ShapeDtypeStruct((bsz, CONV_W - 1, d), F32)],
        scratch_shapes=[pltpu.VMEM((bb, tb + SUBLANES, d), F32), pltpu.VMEM((bb, 1, d), F32),
                        pltpu.VMEM((2, d // LANES, bb * tb, LANES), F32),
                        pltpu.VMEM((d // LANES, bb * tb, LANES), F32)],
        compiler_params=_params("arbitrary", "arbitrary"),
        name="rglru",
    )(ua, cbuf, h0, cw, cb, wa4, ba, wi4, bi, lam)


def _gla_kernel(ub_ref, s0_ref, wf_ref, bf_ref, g_ref, yb_ref, sout_ref, s_scr,
                *, bb, tb, chunk, l_real, nblk, dk, dv):
    j = pl.program_id(1)
    nh = GLA_HEADS
    qk = nh * dk
    vw = nh * dv

    @pl.when(j == 0)
    def _():
        s_scr[...] = s0_ref[...]

    off_k, off_v, off_g, off_f = qk, 2 * qk, 2 * qk + vw, 2 * qk + 2 * vw
    merge = bb > 1 and nblk == 1 and tb == chunk
    nrow = bb * tb if merge else tb
    nc = nrow // chunk
    shift = chunk.bit_length() - 1
    ri = lax.broadcasted_iota(jnp.int32, (nrow, nrow), 0)
    ci = lax.broadcasted_iota(jnp.int32, (nrow, nrow), 1)
    keep = (lax.shift_right_logical(ri, shift) == lax.shift_right_logical(ci, shift)) & (ri >= ci)
    tril = keep.astype(F32)
    row_chunk = lax.shift_right_logical(lax.broadcasted_iota(jnp.int32, (nrow, dk), 0), shift)
    mid = chunk // 2
    scale = dk ** -0.5

    def by_chunk(a):
        if nc == 1:
            return a
        return jnp.concatenate([jnp.where(row_chunk == c, a, 0.0) for c in range(nc)], axis=1)

    def per_chunk_row(a, r):
        return jnp.concatenate([jnp.broadcast_to(a[c * chunk + r:c * chunk + r + 1, :], (chunk, a.shape[1]))
                                for c in range(nc)], axis=0)

    for bi in range(1 if merge else bb):
        def cols(c0, c1, bi=bi):
            if merge:
                return ub_ref[:, :, c0:c1].reshape(nrow, c1 - c0)
            return ub_ref[bi, :, c0:c1]

        fl = cols(off_f, off_f + LANES).astype(BF16)
        logit = jnp.dot(fl, wf_ref[...], preferred_element_type=F32) + bf_ref[...]
        log_f = (jnp.minimum(logit, 0.0) - jnp.log1p(jnp.exp(-jnp.abs(logit)))) * (1.0 / GLA_TAU)
        if l_real < tb * nblk:
            rows = lax.broadcasted_iota(jnp.int32, log_f.shape, 0)
            rows = (jnp.bitwise_and(rows, tb - 1) if merge else rows) + j * tb
            log_f = jnp.where(rows < l_real, log_f, 0.0)
        bcum = jnp.dot(tril, log_f, precision=HIGHEST, preferred_element_type=F32)
        ref = per_chunk_row(bcum, mid)
        last = per_chunk_row(bcum, chunk - 1)
        e_inter = jnp.exp(bcum)
        e_q = jnp.exp(bcum - ref)
        e_k = jnp.exp(ref - bcum)
        e_dec = jnp.exp(last - bcum)

        for h in range(nh):
            hk = slice(h * dk, (h + 1) * dk)
            q = cols(h * dk, (h + 1) * dk) * scale
            k = cols(off_k + h * dk, off_k + (h + 1) * dk)
            vb = cols(off_v + h * dv, off_v + (h + 1) * dv).astype(BF16)
            go = cols(off_g + h * dv, off_g + (h + 1) * dv)
            att = lax.dot_general((q * e_q[:, hk]).astype(BF16), (k * e_k[:, hk]).astype(BF16),
                                  (((1,), (1,)), ((), ())), preferred_element_type=F32)
            att = jnp.where(keep, att, 0.0)
            o = jnp.dot(att.astype(BF16), vb, preferred_element_type=F32)
            kd = by_chunk(k * e_dec[:, hk])
            kv = lax.dot_general(kd.astype(BF16), vb, (((0,), (0,)), ((), ())),
                                 preferred_element_type=F32)
            states = [s_scr[c, h] for c in range(nc)] if merge else [s_scr[bi, h]]
            for c in range(nc):
                drow = jnp.exp(bcum[c * chunk + chunk - 1:c * chunk + chunk, hk])
                dec = jnp.transpose(jnp.broadcast_to(drow, (dk, dk)))
                dec = jnp.concatenate([dec] * (dv // dk), axis=1)
                s_next = dec * states[c] + kv[c * dk:(c + 1) * dk]
                if merge:
                    s_scr[c, h] = s_next
                else:
                    states.append(s_next)
            if not merge:
                s_scr[bi, h] = states[nc]
            qi = by_chunk(q * e_inter[:, hk])
            s_in = jnp.concatenate(states[:nc], axis=0).astype(BF16)
            o = o + jnp.dot(qi.astype(BF16), s_in, preferred_element_type=F32)
            o = o * lax.rsqrt(jnp.mean(o * o, axis=-1, keepdims=True) + RMS_EPS) * g_ref[...]
            y = (o * (go * _sigmoid(go))).astype(yb_ref.dtype)
            if merge:
                yb_ref[:, :, h * dv:(h + 1) * dv] = y.reshape(bb, tb, dv)
            else:
                yb_ref[bi, :, h * dv:(h + 1) * dv] = y

    @pl.when(j == nblk - 1)
    def _():
        sout_ref[...] = s_scr[...]


def _gla(ub, s0, wf, bf, g, *, bb, tb, chunk, l_real, y_dtype=F32):
    bsz, lp, nb = ub.shape
    _, nh, dk, dv = s0.shape
    nblk = lp // tb
    assert bsz % bb == 0 and dv % dk == 0 and chunk & (chunk - 1) == 0 and tb % chunk == 0
    kern = functools.partial(_gla_kernel, bb=bb, tb=tb, chunk=chunk, l_real=l_real, nblk=nblk, dk=dk, dv=dv)
    return pl.pallas_call(
        kern,
        grid=(bsz // bb, nblk),
        in_specs=[pl.BlockSpec((bb, tb, nb), lambda b, j: (b, j, 0)),
                  pl.BlockSpec((bb, nh, dk, dv), lambda b, j: (b, 0, 0, 0)),
                  _const_spec(wf.shape), _const_spec(bf.shape), _const_spec(g.shape)],
        out_specs=[pl.BlockSpec((bb, tb, nh * dv), lambda b, j: (b, j, 0)),
                   pl.BlockSpec((bb, nh, dk, dv), lambda b, j: (b, 0, 0, 0))],
        out_shape=[jax.ShapeDtypeStruct((bsz, lp, nh * dv), y_dtype),
                   jax.ShapeDtypeStruct((bsz, nh, dk, dv), F32)],
        scratch_shapes=[pltpu.VMEM((bb, nh, dk, dv), F32)],
        compiler_params=_params("arbitrary", "arbitrary"),
        name="gla",
    )(ub, s0, wf, bf, g)


def _mix_kernel(ya_ref, yb_ref, um_ref, x_ref, wo_ref, g_ref, b_ref, *rest, alpha, d):
    o_ref = rest[-1]
    um = um_ref[...].astype(F32)
    merged = (_sigmoid(um[:, :d]) * ya_ref[...].astype(F32)
              + _sigmoid(um[:, d:]) * yb_ref[...].astype(F32))
    mix = jnp.dot(merged.astype(BF16), wo_ref[...], preferred_element_type=F32)
    o_ref[...] = _layer_norm(alpha * x_ref[...] + mix, g_ref[...], b_ref[...])


def _mix(ya, yb, um, x, wo, g, b, *, alpha, tm, total_rows, row_offset=0, into=None):
    m, d = x.shape
    assert m % tm == 0 and row_offset % tm == 0
    row = lambda w: pl.BlockSpec((tm, w), lambda i: (i, 0))
    in_specs = [row(d), row(d), row(2 * d), row(d),
                _const_spec(wo.shape), _const_spec(g.shape), _const_spec(b.shape)]
    args = (ya, yb, um, x, wo, g, b)
    aliases = {}
    if into is not None:
        in_specs.append(pl.BlockSpec(memory_space=pl.ANY))
        aliases = {len(args): 0}
        args = args + (into,)
    return pl.pallas_call(
        functools.partial(_mix_kernel, alpha=alpha, d=d),
        grid=(m // tm,),
        in_specs=in_specs,
        out_specs=pl.BlockSpec((tm, d), lambda i: (i + row_offset // tm, 0)),
        out_shape=jax.ShapeDtypeStruct((total_rows, d), F32),
        input_output_aliases=aliases,
        compiler_params=_params("arbitrary"),
        name="mix_ln1",
    )(*args)


def _router_t_kernel(x_ref, whi_ref, wlo_ref, bias_ref, eidx_ref, gate_ref, rank_ref, cnt_ref, run_scr,
                     *, tm, ne):
    i = pl.program_id(0)

    @pl.when(i == 0)
    def _():
        run_scr[...] = jnp.zeros_like(run_scr)

    x = x_ref[...]
    x_hi = x.astype(BF16)
    x_lo = (x - x_hi.astype(F32)).astype(BF16)
    nt = (((1,), (1,)), ((), ()))
    logits = (lax.dot_general(whi_ref[...], x_hi, nt, preferred_element_type=F32)
              + lax.dot_general(whi_ref[...], x_lo, nt, preferred_element_type=F32)
              + lax.dot_general(wlo_ref[...], x_hi, nt, preferred_element_type=F32))
    s = _sigmoid(logits)
    sel = s + bias_ref[...]
    row = lax.broadcasted_iota(jnp.int32, (ne, tm), 0).astype(F32)
    gsz = ne // N_GROUPS
    neg = -jnp.inf

    def colmax(a):
        return jnp.max(a, axis=0, keepdims=True)

    def colmin(a):
        return jnp.min(a, axis=0, keepdims=True)

    gscores = []
    row_in_group = lax.broadcasted_iota(jnp.int32, (gsz, tm), 0).astype(F32)
    for g in range(N_GROUPS):
        v = sel[g * gsz:(g + 1) * gsz]
        rg = row_in_group + float(g * gsz)
        m1 = colmax(v)
        i1 = colmin(jnp.where(v == m1, rg, float(ne)))
        gscores.append(m1 + colmax(jnp.where(rg == i1, neg, v)))
    kept = []
    for g in range(N_GROUPS):
        ahead = jnp.zeros((1, tm), F32)
        for g2 in range(N_GROUPS):
            if g2 == g:
                continue
            beats = (gscores[g2] >= gscores[g]) if g2 < g else (gscores[g2] > gscores[g])
            ahead = ahead + beats.astype(F32)
        keep = jnp.broadcast_to(ahead < float(TOPK_GROUPS), (gsz, tm))
        kept.append(jnp.where(keep, sel[g * gsz:(g + 1) * gsz], neg))
    cur = jnp.concatenate(kept, axis=0)

    picks, eidx, wsel = [], [], []
    onehot_sum = jnp.zeros((ne, tm), F32)
    for k in range(TOP_K):
        ik = colmin(jnp.where(cur == colmax(cur), row, float(ne)))
        hit = row == ik
        wsel.append(jnp.sum(jnp.where(hit, s, 0.0), axis=0, keepdims=True))
        cur = jnp.where(hit, neg, cur)
        onehot_sum = onehot_sum + hit.astype(F32)
        picks.append(hit)
        eidx.append(ik)
    wsel = jnp.concatenate(wsel, axis=0)
    gate_ref[...] = wsel / jnp.sum(wsel, axis=0, keepdims=True) * ROUTED_SCALE
    eidx_ref[...] = jnp.concatenate(eidx, axis=0).astype(jnp.int32)

    ri = lax.broadcasted_iota(jnp.int32, (tm, tm), 0)
    ci = lax.broadcasted_iota(jnp.int32, (tm, tm), 1)
    earlier = (ri < ci).astype(BF16)
    ohb = onehot_sum.astype(BF16)
    counts = jnp.dot(ohb, earlier, preferred_element_type=F32) + jnp.concatenate([run_scr[...]] * (tm // LANES), axis=1)
    rank = [jnp.sum(jnp.where(picks[k], counts, 0.0), axis=0, keepdims=True) for k in range(TOP_K)]
    rank_ref[...] = jnp.concatenate(rank, axis=0).astype(jnp.int32)
    run_scr[...] = run_scr[...] + jnp.dot(ohb, jnp.ones((tm, LANES), BF16), preferred_element_type=F32)
    cnt_ref[...] = run_scr[...].astype(jnp.int32)


def _router_t(x, w_hi, w_lo, bias_b, *, tm):
    t, d = x.shape
    ne = w_hi.shape[0]
    assert t % tm == 0 and tm % LANES == 0 and ne % N_GROUPS == 0
    out_col = lambda: pl.BlockSpec((TOP_K, tm), lambda i: (0, i))
    return pl.pallas_call(
        functools.partial(_router_t_kernel, tm=tm, ne=ne),
        grid=(t // tm,),
        in_specs=[pl.BlockSpec((tm, d), lambda i: (i, 0)), _const_spec(w_hi.shape), _const_spec(w_lo.shape),
                  _const_spec(bias_b.shape)],
        out_specs=[out_col(), out_col(), out_col(), _const_spec((ne, LANES))],
        out_shape=[jax.ShapeDtypeStruct((TOP_K, t), jnp.int32),
                   jax.ShapeDtypeStruct((TOP_K, t), F32),
                   jax.ShapeDtypeStruct((TOP_K, t), jnp.int32),
                   jax.ShapeDtypeStruct((ne, LANES), jnp.int32)],
        scratch_shapes=[pltpu.VMEM((ne, LANES), F32)],
        compiler_params=_params("arbitrary"),
        name="router",
    )(x, w_hi, w_lo, bias_b)


def _dest_t_kernel(eidx_ref, rank_ref, start_ref, dest_ref, *, tm, ne):
    row = lax.broadcasted_iota(jnp.int32, (ne, tm), 0)
    starts = start_ref[...].astype(F32)
    eidx = eidx_ref[...]
    dest = [jnp.sum(jnp.where(row == eidx[k:k + 1, :], starts, 0.0), axis=0, keepdims=True) for k in range(TOP_K)]
    dest_ref[...] = jnp.concatenate(dest, axis=0).astype(jnp.int32) + rank_ref[...]


def _dest_t(eidx, rank, starts_b, *, tm):
    t = eidx.shape[1]
    ne = starts_b.shape[0]
    assert t % tm == 0
    col = lambda: pl.BlockSpec((TOP_K, tm), lambda i: (0, i))
    return pl.pallas_call(
        functools.partial(_dest_t_kernel, tm=tm, ne=ne),
        grid=(t // tm,),
        in_specs=[col(), col(), _const_spec(starts_b.shape)],
        out_specs=col(),
        out_shape=jax.ShapeDtypeStruct((TOP_K, t), jnp.int32),
        compiler_params=_params("arbitrary"),
        name="dest",
    )(eidx, rank, starts_b)


TOKEN_DTYPE = BF16


def _to_token_tiles(rows):
    chunks = jnp.stack([rows[:, c * LANES:(c + 1) * LANES] for c in range(SUBLANES)], axis=0)
    return jnp.swapaxes(chunks, 0, 1).astype(TOKEN_DTYPE)


def _chunks_of_token_tiles(tiles):
    chunks = jnp.swapaxes(tiles.astype(F32), 0, 1)
    return [chunks[c] for c in range(SUBLANES)]


def _token_copy(src_ref, src_row, dst_ref, dst_row, sem):
    return pltpu.make_async_copy(src_ref.at[pl.ds(src_row, 1)], dst_ref.at[pl.ds(dst_row, 1)], sem)


def _dispatch_kernel(dest_ref, x_ref, xs_ref, xt, sem, *, tt):
    xt[...] = _to_token_tiles(x_ref[...])

    def start(tp, c):
        for u in range(ISSUE_UNROLL):
            t = tp * ISSUE_UNROLL + u
            for k in range(TOP_K):
                _token_copy(xt, t, xs_ref, dest_ref[t * TOP_K + k], sem).start(priority=k % 2)
        return c

    lax.fori_loop(0, tt // ISSUE_UNROLL, start, 0)
    for _ in range(TOP_K):
        pltpu.make_async_copy(xt, xs_ref.at[pl.ds(0, tt)], sem).wait()


def _dispatch(dest_flat, x, n_slots, *, tt):
    t, d = x.shape
    assert d == SUBLANES * LANES and t % tt == 0 and tt % ISSUE_UNROLL == 0
    return pl.pallas_call(
        functools.partial(_dispatch_kernel, tt=tt),
        grid=(t // tt,),
        in_specs=[pl.BlockSpec((tt * TOP_K,), lambda i: (i,), memory_space=pltpu.SMEM),
                  pl.BlockSpec((tt, d), lambda i: (i, 0))],
        out_specs=pl.BlockSpec(memory_space=pl.ANY),
        out_shape=jax.ShapeDtypeStruct((n_slots, SUBLANES, LANES), TOKEN_DTYPE),
        scratch_shapes=[pltpu.VMEM((tt, SUBLANES, LANES), TOKEN_DTYPE), pltpu.SemaphoreType.DMA(())],
        compiler_params=_params("arbitrary"),
        name="dispatch",
    )(dest_flat, x)


EXPERT_SLOTS = 16
EXPERT_TAILS = ((1, (1,)), (2, (2,)), (3, (2, 1)), (5, (2, 2, 1)), (6, (2, 2, 2)), (7, (2, 2, 2, 1)))
EXPERT_STEP_BLOCKS = max(r for r, _ in EXPERT_TAILS)
EXPERT_AHEAD = 8


def _expert_kernel(base_ref, nblk_ref, cnt_ref, nu_ref, xs_ref, wg_ref, wu_ref, wd_ref, ys_ref,
                   xbuf, ybuf, wgb, wub, wdb, sem_in, sem_out):
    e = pl.program_id(0)
    n_used = nu_ref[0]
    nb = nblk_ref[e]
    base = base_ref[e]
    cnt = cnt_ref[e]
    ahead = EXPERT_AHEAD

    def x_copy(blk, slot):
        return pltpu.make_async_copy(xs_ref.at[pl.ds(blk * MOE_ROWS, MOE_ROWS)], xbuf.at[slot], sem_in.at[slot])

    def y_copy(blk, slot):
        return pltpu.make_async_copy(ybuf.at[slot], ys_ref.at[pl.ds(blk * MOE_ROWS, MOE_ROWS)], sem_out.at[slot])

    @pl.when(e == 0)
    def _():
        for first in range(ahead):
            @pl.when(first < n_used)
            def _():
                x_copy(first, first).start()

    @pl.when(nb > 0)
    def _():
        wgb[...] = wg_ref[0].astype(BF16)
        wub[...] = wu_ref[0].astype(BF16)
        wdb[...] = wd_ref[0].astype(BF16)

    def step(jb, chains):
        nblocks = sum(chains)
        blks = [base + jb + i for i in range(nblocks)]
        slots = [jnp.bitwise_and(b, EXPERT_SLOTS - 1) for b in blks]
        for b, s in zip(blks, slots):
            x_copy(b, s).wait()
        for b in blks:
            @pl.when(b + ahead < n_used)
            def _():
                x_copy(b + ahead, jnp.bitwise_and(b + ahead, EXPERT_SLOTS - 1)).start()
        for b, s in zip(blks, slots):
            @pl.when(b >= EXPERT_SLOTS)
            def _():
                y_copy(b - EXPERT_SLOTS, s).wait()

        first = 0
        for n in chains:
            x = jnp.concatenate([jnp.concatenate(_chunks_of_token_tiles(xbuf[s]), axis=1)
                                 for s in slots[first:first + n]], axis=0)
            rows = lax.broadcasted_iota(jnp.int32, x.shape, 0)
            xb = jnp.where(rows < cnt - (jb + first) * MOE_ROWS, x, 0.0).astype(BF16)
            g = jnp.dot(xb, wgb[...], preferred_element_type=F32)
            u = jnp.dot(xb, wub[...], preferred_element_type=F32)
            hmid = (g * _sigmoid(g)) * u
            y = jnp.dot(hmid.astype(BF16), wdb[...], preferred_element_type=F32)
            for i in range(n):
                b, s = blks[first + i], slots[first + i]
                ybuf[s] = _to_token_tiles(y[i * MOE_ROWS:(i + 1) * MOE_ROWS])
                y_copy(b, s).start()
            first += n

    def quad(jq, c):
        step(4 * jq, (2, 2))
        return c

    rem = jnp.bitwise_and(nb, 3)
    nfull = lax.shift_right_logical(nb, 2)
    nloop = jnp.where(rem == 0, nfull, jnp.maximum(nfull - 1, 0))
    lax.fori_loop(0, nloop, quad, 0)
    tail = nb - 4 * nloop
    for r, chains in EXPERT_TAILS:
        @pl.when(tail == r)
        def _():
            step(4 * nloop, chains)

    @pl.when(e == pl.num_programs(0) - 1)
    def _():
        for back in range(EXPERT_SLOTS):
            @pl.when(n_used > back)
            def _():
                last = n_used - 1 - back
                y_copy(last, jnp.bitwise_and(last, EXPERT_SLOTS - 1)).wait()


def _experts(base, nblk, cnt, n_used, xs, wg, wu, wd):
    p = xs.shape[0]
    ne, d, de = wg.shape
    assert EXPERT_SLOTS & (EXPERT_SLOTS - 1) == 0 and EXPERT_AHEAD <= EXPERT_SLOTS - EXPERT_STEP_BLOCKS
    assert d == SUBLANES * LANES
    ring = pltpu.VMEM((EXPERT_SLOTS, MOE_ROWS, SUBLANES, LANES), TOKEN_DTYPE)
    grid_spec = pltpu.PrefetchScalarGridSpec(
        num_scalar_prefetch=4,
        grid=(ne,),
        in_specs=[pl.BlockSpec(memory_space=pl.ANY),
                  pl.BlockSpec((1, d, de), lambda e, *_: (e, 0, 0)),
                  pl.BlockSpec((1, d, de), lambda e, *_: (e, 0, 0)),
                  pl.BlockSpec((1, de, d), lambda e, *_: (e, 0, 0))],
        out_specs=pl.BlockSpec(memory_space=pl.ANY),
        scratch_shapes=[ring, ring,
                        pltpu.VMEM((d, de), BF16), pltpu.VMEM((d, de), BF16), pltpu.VMEM((de, d), BF16),
                        pltpu.SemaphoreType.DMA((EXPERT_SLOTS,)), pltpu.SemaphoreType.DMA((EXPERT_SLOTS,))],
    )
    return pl.pallas_call(
        _expert_kernel,
        grid_spec=grid_spec,
        out_shape=jax.ShapeDtypeStruct((p, SUBLANES, LANES), TOKEN_DTYPE),
        compiler_params=_params("arbitrary"),
        name="experts",
    )(base, nblk, cnt, n_used, xs, wg, wu, wd)


def _combine_kernel(dest_ref, dest_next_ref, x_ref, gate_ref, ys_ref, wsg_ref, wsu_ref, wsd_ref, g_ref, b_ref,
                    o_head_ref, o_tail_ref, ybuf, sem, *, tt, alpha, head_steps):
    i = pl.program_id(0)
    slot = jnp.bitwise_and(i, 1)

    def gather(dref, s):
        def start(tp, c):
            for u in range(ISSUE_UNROLL):
                t = tp * ISSUE_UNROLL + u
                for k in range(TOP_K):
                    _token_copy(ys_ref, dref[t * TOP_K + k], ybuf.at[s], k * tt + t, sem.at[s]).start(priority=k % 2)
            return c
        lax.fori_loop(0, tt // ISSUE_UNROLL, start, 0)

    @pl.when(i == 0)
    def _():
        gather(dest_ref, 0)

    def prefetch_group(k):
        nxt = 1 - slot
        for t in range(tt):
            _token_copy(ys_ref, dest_next_ref[t * TOP_K + k], ybuf.at[nxt], k * tt + t, sem.at[nxt]).start(priority=t % 2)

    x = x_ref[...]
    xb = x.astype(BF16)
    sg = jnp.dot(xb, wsg_ref[...], preferred_element_type=F32)
    su = jnp.dot(xb, wsu_ref[...], preferred_element_type=F32)
    acc = jnp.dot(((sg * _sigmoid(sg)) * su).astype(BF16), wsd_ref[...], preferred_element_type=F32)
    pltpu.make_async_copy(ys_ref.at[pl.ds(0, tt * TOP_K)], ybuf.at[slot], sem.at[slot]).wait()
    gate = gate_ref[...]
    routed = None
    for k in range(TOP_K):
        prefetch_group(k)
        gk = jnp.broadcast_to(gate[:, k:k + 1], (tt, LANES))
        part = [gk * ch for ch in _chunks_of_token_tiles(ybuf[slot, k * tt:(k + 1) * tt])]
        routed = part if routed is None else [r + p for r, p in zip(routed, part)]
    acc = acc + jnp.concatenate(routed, axis=1)
    out = _layer_norm(alpha * x + acc, g_ref[...], b_ref[...])

    @pl.when(i < head_steps)
    def _():
        o_head_ref[...] = out

    @pl.when(i >= head_steps)
    def _():
        o_tail_ref[...] = out

    @pl.when(i == pl.num_programs(0) - 1)
    def _():
        pltpu.make_async_copy(ys_ref.at[pl.ds(0, tt * TOP_K)], ybuf.at[1 - slot], sem.at[1 - slot]).wait()


def _combine(dest_flat, x, gate, ys, wsg, wsu, wsd, g, b, *, tt, alpha, head_rows):
    t, d = x.shape
    nsteps = t // tt
    head_steps = head_rows // tt
    assert t % tt == 0 and head_rows % tt == 0 and 0 < head_steps < nsteps
    return pl.pallas_call(
        functools.partial(_combine_kernel, tt=tt, alpha=alpha, head_steps=head_steps),
        grid=(nsteps,),
        in_specs=[pl.BlockSpec((tt * TOP_K,), lambda i: (i,), memory_space=pltpu.SMEM),
                  pl.BlockSpec((tt * TOP_K,), lambda i: (jnp.minimum(i + 1, nsteps - 1),),
                               memory_space=pltpu.SMEM),
                  pl.BlockSpec((tt, d), lambda i: (i, 0)),
                  pl.BlockSpec((tt, TOP_K), lambda i: (i, 0)),
                  pl.BlockSpec(memory_space=pl.ANY),
                  _const_spec(wsg.shape), _const_spec(wsu.shape), _const_spec(wsd.shape),
                  _const_spec(g.shape), _const_spec(b.shape)],
        out_specs=[pl.BlockSpec((tt, d), lambda i: (jnp.minimum(i, head_steps - 1), 0)),
                   pl.BlockSpec((tt, d), lambda i: (jnp.maximum(i - head_steps, 0), 0))],
        out_shape=[jax.ShapeDtypeStruct((head_rows, d), F32), jax.ShapeDtypeStruct((t - head_rows, d), F32)],
        scratch_shapes=[pltpu.VMEM((2, tt * TOP_K, SUBLANES, LANES), TOKEN_DTYPE), pltpu.SemaphoreType.DMA((2,))],
        compiler_params=_params("arbitrary"),
        name="combine_ln2",
    )(dest_flat, dest_flat, x, gate, ys, wsg, wsu, wsd, g, b)


def _block_diag_tiles(w):
    n, s, _ = w.shape
    per = MXU_DIM // s
    w = w.reshape(n // per, per, s, s)
    eye = jnp.eye(per, dtype=w.dtype)
    return jnp.einsum('tpij,pq->tpiqj', w, eye).reshape(n // per, MXU_DIM, MXU_DIM)


def _row(v):
    return v.reshape(1, -1)


def _moe(x1, w_router, router_bias, w_gate, w_up, w_down, ws_gate, ws_up, ws_down, ln_g, ln_b, alpha, head_rows):
    t, d = x1.shape
    ne = w_router.shape[1]
    tm = 512
    w_t = w_router.T
    w_hi = w_t.astype(BF16)
    w_lo = (w_t - w_hi.astype(F32)).astype(BF16)
    bias_b = jnp.broadcast_to(router_bias.reshape(ne, 1), (ne, tm))
    eidx_t, gate_t, rank_t, counts = _router_t(x1, w_hi, w_lo, bias_b, tm=tm)
    counts = counts[:, 0]
    nblk = (counts + MOE_ROWS - 1) // MOE_ROWS
    blk_end = jnp.cumsum(nblk)
    blk_base = (blk_end - nblk).astype(jnp.int32)
    n_used = blk_end[-1:].astype(jnp.int32)
    n_slots = ((t * TOP_K) // MOE_ROWS + ne) * MOE_ROWS

    starts_b = jnp.broadcast_to((blk_base * MOE_ROWS).reshape(ne, 1), (ne, tm))
    dest_flat = _dest_t(eidx_t, rank_t, starts_b, tm=tm).T.reshape(-1)
    xs = _dispatch(dest_flat, x1, n_slots, tt=512)
    ys = _experts(blk_base, nblk.astype(jnp.int32), counts, n_used, xs, w_gate, w_up, w_down)
    return _combine(dest_flat, x1, gate_t.T, ys, ws_gate.astype(BF16), ws_up.astype(BF16),
                    ws_down.astype(BF16), _row(ln_g), _row(ln_b), tt=512, alpha=alpha, head_rows=head_rows)


def _pad_time(a, lp):
    return jnp.pad(a, ((0, 0), (0, lp - a.shape[1]), (0, 0)))


def kernel(x_prompt, x_sample, state_rglru_h, state_conv, state_gla, meta_tokens, w_in, conv_w, conv_b,
           rg_wa, rg_ba, rg_wi, rg_bi, rg_lambda, gla_wf2, gla_bf, gla_norm_g, w_out, ln1_g, ln1_b,
           w_router, router_bias, w_gate, w_up, w_down, ws_gate, ws_up, ws_down, ln2_g, ln2_b):
    bp, lp, d = x_prompt.shape
    bs, ls, _ = x_sample.shape
    depth = w_in.shape[0]
    nh, dk, dv = state_gla.shape[2:]
    qk, vw = nh * dk, nh * dv
    lowrank = gla_wf2.shape[1]
    alpha = (2.0 * depth) ** 0.25
    ls_pad = SUBLANES
    assert depth == 1 and ls <= ls_pad and lp % 256 == 0 and N_META % SUBLANES == 0

    xp = x_prompt.reshape(bp * lp, d)
    xs = x_sample.reshape(bs * ls, d)
    xm = meta_tokens.astype(F32)

    hp_l, cp_l, sp_l, hs_l, cs_l, ss_l = [], [], [], [], [], []
    for l in range(depth):
        o1, o2 = 2 * d, 2 * d + 2 * qk + 2 * vw
        wa = w_in[l][:, :o1].astype(BF16)
        wb = jnp.pad(w_in[l][:, o1:o2 + lowrank], ((0, 0), (0, LANES - lowrank))).astype(BF16)
        wm = w_in[l][:, o2 + lowrank:].astype(BF16)
        wf = jnp.pad(gla_wf2[l], ((0, LANES - lowrank), (0, 0))).astype(BF16)
        wa4 = _block_diag_tiles(rg_wa[l]).astype(BF16)
        wi4 = _block_diag_tiles(rg_wi[l]).astype(BF16)
        mixer_w = (conv_w[l], _row(conv_b[l]), wa4, _row(rg_ba[l]), wi4, _row(rg_bi[l]), _row(rg_lambda[l]))
        gla_w = (wf, _row(gla_bf[l]), _row(gla_norm_g[l]))

        ua, ub, _ = _inproj(xm, wa, wb, wm, tm=N_META)
        _, h_m, c_m = _rglru(ua[None], jnp.zeros((1, CONV_W - 1, d), F32), jnp.zeros((1, 1, d), F32),
                             *mixer_w, bb=1, tb=N_META, l_real=N_META)
        _, s_m = _gla(ub[None], jnp.zeros((1, nh, dk, dv), F32), *gla_w,
                      bb=1, tb=N_META, chunk=N_META, l_real=N_META)

        ua, ub, um = _inproj(xp, wa, wb, wm, tm=512, um_dtype=BF16)
        ya, h_p, c_p = _rglru(ua.reshape(bp, lp, -1), jnp.broadcast_to(c_m, (bp,) + c_m.shape[1:]),
                              jnp.broadcast_to(h_m, (bp,) + h_m.shape[1:]), *mixer_w,
                              bb=1, tb=512, l_real=lp, y_dtype=BF16)
        yb, s_p = _gla(ub.reshape(bp, lp, -1), jnp.broadcast_to(s_m, (bp,) + s_m.shape[1:]), *gla_w,
                       bb=1, tb=256, chunk=GLA_CHUNK, l_real=lp, y_dtype=BF16)
        n_tok = bp * lp + bs * ls
        x1 = _mix(ya.reshape(bp * lp, d), yb.reshape(bp * lp, d), um, xp, w_out[l].astype(BF16),
                  _row(ln1_g[l]), _row(ln1_b[l]), alpha=alpha, tm=1024, total_rows=n_tok)

        xs_pad = _pad_time(xs.reshape(bs, ls, d), ls_pad).reshape(bs * ls_pad, d)
        ua, ub, um = _inproj(xs_pad, wa, wb, wm, tm=256)
        ya, h_s, c_s = _rglru(ua.reshape(bs, ls_pad, -1), state_conv[l].astype(F32),
                              state_rglru_h[l].astype(F32)[:, None, :], *mixer_w,
                              bb=32, tb=ls_pad, l_real=ls)
        yb, s_s = _gla(ub.reshape(bs, ls_pad, -1), state_gla[l].astype(F32), *gla_w,
                       bb=8, tb=ls_pad, chunk=ls_pad, l_real=ls)
        unpad = lambda a: a.reshape(bs, ls_pad, -1)[:, :ls].reshape(bs * ls, -1)
        x1 = _mix(unpad(ya), unpad(yb), unpad(um), xs, w_out[l].astype(BF16),
                  _row(ln1_g[l]), _row(ln1_b[l]), alpha=alpha, tm=512, total_rows=n_tok,
                  row_offset=bp * lp, into=x1)

        xp, xs = _moe(x1, w_router[l], router_bias[l], w_gate[l], w_up[l], w_down[l],
                      ws_gate[l], ws_up[l], ws_down[l], ln2_g[l], ln2_b[l], alpha, bp * lp)

        hp_l.append(h_p[:, 0]); cp_l.append(c_p); sp_l.append(s_p)
        hs_l.append(h_s[:, 0]); cs_l.append(c_s); ss_l.append(s_s)

    y_prompt = xp.reshape(bp, lp, d)
    y_sample = xs.reshape(bs, ls, d)
    return (y_prompt, y_sample,
            jnp.stack(hp_l).astype(state_rglru_h.dtype), jnp.stack(cp_l).astype(state_conv.dtype),
            jnp.stack(sp_l).astype(state_gla.dtype),
            jnp.stack(hs_l).astype(state_rglru_h.dtype), jnp.stack(cs_l).astype(state_conv.dtype),
            jnp.stack(ss_l).astype(state_gla.dtype))
```
